```python
import jax
import jax.numpy as jnp
from jax import lax
import numpy as np

D_MODEL = 1024
BATCH = 4
SEQ = 4096
DEPTH = 4
DEC_BATCH = 128
DEC_SEQ = 4
PAST_LEN = 8192
PAGE_SIZE = 128

MIX_WIDTH = D_MODEL
HEAD_DIM = 64
ATTN_WIDTH = D_MODEL // 2
N_HEADS = ATTN_WIDTH // HEAD_DIM
N_KV_HEADS = 2
GQA_GROUP = N_HEADS // N_KV_HEADS
KV_WIDTH = N_KV_HEADS * HEAD_DIM
WINDOW = 128
BLOCK_Q = 128
RWKV_WIDTH = D_MODEL // 4
RWKV_HEAD = 64
RWKV_HEADS = RWKV_WIDTH // RWKV_HEAD
LORA_W = 64
LORA_A = 64
LORA_G = 128
RWKV_PROJ = 3 * RWKV_WIDTH + LORA_W + LORA_A + LORA_G
LRU_WIDTH = D_MODEL // 4
LRU_BLOCKS = 4
LRU_BLOCK = LRU_WIDTH // LRU_BLOCKS
CONV_W = 4
LRU_C = 8.0
IN_COLS = ATTN_WIDTH + 2 * KV_WIDTH + RWKV_PROJ + 2 * LRU_WIDTH
IN_SPLITS = (ATTN_WIDTH, ATTN_WIDTH + KV_WIDTH, ATTN_WIDTH + 2 * KV_WIDTH,
             ATTN_WIDTH + 2 * KV_WIDTH + RWKV_PROJ, ATTN_WIDTH + 2 * KV_WIDTH + RWKV_PROJ + LRU_WIDTH)
RWKV_SPLITS = (RWKV_WIDTH, 2 * RWKV_WIDTH, 3 * RWKV_WIDTH, 3 * RWKV_WIDTH + LORA_W,
               3 * RWKV_WIDTH + LORA_W + LORA_A)
D_FF = -(-(8 * D_MODEL) // (3 * 256)) * 256
PLE_DIM = 256
RMS_EPS = 1e-6
GN_EPS = 64e-5

kernel_name = 'hymba_swa_rwkv7_rglru_step'


def rmsnorm(x, g):
    xf = x.astype(jnp.float32)
    y = xf * lax.rsqrt(jnp.mean(xf * xf, -1, keepdims=True) + RMS_EPS)
    return (y * g.astype(jnp.float32)).astype(x.dtype)


def _sink_probs(s, sink, mask):
    s = jnp.where(mask, s, -jnp.inf)
    m = jnp.maximum(jnp.max(s, -1, keepdims=True), sink)
    e = jnp.exp(s - m)
    return e / (jnp.sum(e, -1, keepdims=True) + jnp.exp(sink - m))


def swa_prompt(q, k, v, sinks):
    b, t = q.shape[:2]
    nb = t // BLOCK_Q
    qb = q.reshape(b, nb, BLOCK_Q, N_KV_HEADS, GQA_GROUP, HEAD_DIM)
    kb = k.reshape(b, nb, BLOCK_Q, N_KV_HEADS, HEAD_DIM)
    vb = v.reshape(b, nb, BLOCK_Q, N_KV_HEADS, HEAD_DIM)
    pad = ((0, 0), (1, 0), (0, 0), (0, 0), (0, 0))
    k_ext = jnp.concatenate([jnp.pad(kb[:, :-1], pad), kb], axis=2)
    v_ext = jnp.concatenate([jnp.pad(vb[:, :-1], pad), vb], axis=2)
    s = jnp.einsum('bnqhgd,bnkhd->bhgnqk', qb, k_ext,
                   preferred_element_type=jnp.float32) * (HEAD_DIM ** -0.5)
    qi = jnp.arange(BLOCK_Q)[:, None] + BLOCK_Q
    kj = jnp.arange(2 * BLOCK_Q)[None, :]
    d = qi - kj
    blk = jnp.arange(nb)[:, None, None]
    mask = (d >= 0) & (d <= WINDOW) & ((blk > 0) | (kj >= BLOCK_Q))
    sink = sinks.astype(jnp.float32).reshape(N_KV_HEADS, GQA_GROUP)[None, :, :, None, None, None]
    p = _sink_probs(s, sink, mask)
    o = jnp.einsum('bhgnqk,bnkhd->bnqhgd', p.astype(v.dtype), v_ext)
    return o.reshape(b, t, ATTN_WIDTH)


def swa_sample(q, k, v, ck, cv, sinks):
    b, tn = q.shape[:2]
    k_all = jnp.concatenate([ck.astype(k.dtype), k], axis=1)
    v_all = jnp.concatenate([cv.astype(v.dtype), v], axis=1)
    qg = q.reshape(b, tn, N_KV_HEADS, GQA_GROUP, HEAD_DIM)
    s = jnp.einsum('bqhgd,bkhd->bhgqk', qg, k_all,
                   preferred_element_type=jnp.float32) * (HEAD_DIM ** -0.5)
    d = (jnp.arange(tn)[:, None] + WINDOW) - jnp.arange(WINDOW + tn)[None, :]
    mask = (d >= 0) & (d <= WINDOW)
    sink = sinks.astype(jnp.float32).reshape(N_KV_HEADS, GQA_GROUP)[None, :, :, None, None]
    p = _sink_probs(s, sink, mask)
    o = jnp.einsum('bhgqk,bkhd->bqhgd', p.astype(v.dtype), v_all)
    return o.reshape(b, tn, ATTN_WIDTH), k_all[:, tn:], v_all[:, tn:]


def rwkv7_mix(proj, shift0, wkv0, lw):
    b, t = proj.shape[:2]
    f32 = jnp.float32
    prev = jnp.concatenate([shift0[:, None].astype(proj.dtype), proj[:, :-1]], axis=1)
    xs = proj + (prev - proj) * lw['rwkv_mu']
    r, k, v, xw, xa, xg = jnp.split(xs, RWKV_SPLITS, axis=-1)
    w_log = -jax.nn.softplus(-(lw['rwkv_w0'] + jnp.tanh(xw) @ lw['rwkv_w_up']).astype(f32)) - 0.5
    decay = jnp.exp(-jnp.exp(w_log))
    a = jax.nn.sigmoid((lw['rwkv_a0'] + xa @ lw['rwkv_a_up']).astype(f32))
    g = (jax.nn.sigmoid(xg) @ lw['rwkv_g_up']).astype(f32)

    def heads(z):
        return z.astype(f32).reshape(b, t, RWKV_HEADS, RWKV_HEAD)

    kk = heads(k * lw['rwkv_k_k'])
    kk = kk * lax.rsqrt(jnp.maximum(jnp.sum(kk * kk, -1, keepdims=True), 1e-24))
    k2 = k.astype(f32) * (1.0 + (a - 1.0) * lw['rwkv_k_a'].astype(f32))
    r_h, k_h, v_h, w_h, a_h = heads(r), heads(k2), heads(v), heads(decay), heads(a)

    def step(S, inp):
        r_t, w_t, k_t, v_t, kk_t, b_t = inp
        S = (S * w_t[:, :, None, :]
             - jnp.einsum('bhvk,bhk->bhv', S, kk_t)[..., None] * b_t[:, :, None, :]
             + v_t[..., None] * k_t[:, :, None, :])
        return S, jnp.einsum('bhvk,bhk->bhv', S, r_t)

    seq = tuple(jnp.moveaxis(z, 1, 0) for z in (r_h, w_h, k_h, v_h, kk, kk * a_h))
    wkv, y = lax.scan(step, wkv0.astype(f32), seq)
    y = jnp.moveaxis(y, 0, 1)
    mean = jnp.mean(y, -1, keepdims=True)
    var = jnp.mean(jnp.square(y - mean), -1, keepdims=True)
    y = (y - mean) * lax.rsqrt(var + GN_EPS)
    y = y.reshape(b, t, RWKV_WIDTH) * lw['rwkv_ln_w'] + lw['rwkv_ln_b']
    bonus = jnp.sum(r_h * k_h * lw['rwkv_r_k'].astype(f32), -1, keepdims=True) * v_h
    y = y + bonus.reshape(b, t, RWKV_WIDTH)
    return (y * g).astype(proj.dtype), proj[:, -1], wkv


def rglru_mix(xb, gb, conv0, h0, lw):
    b, t = xb.shape[:2]
    f32 = jnp.float32
    ext = jnp.concatenate([conv0.astype(xb.dtype), xb], axis=1)
    xc = lw['lru_conv_b'] + sum(ext[:, j:j + t] * lw['lru_conv_w'][j] for j in range(CONV_W))
    xh = xc.reshape(b, t, LRU_BLOCKS, LRU_BLOCK)
    r = jax.nn.sigmoid((jnp.einsum('bthi,hij->bthj', xh, lw['lru_w_a']).reshape(b, t, LRU_WIDTH)
                        + lw['lru_b_a']).astype(f32))
    i = jax.nn.sigmoid((jnp.einsum('bthi,hij->bthj', xh, lw['lru_w_i']).reshape(b, t, LRU_WIDTH)
                        + lw['lru_b_i']).astype(f32))
    log_a = LRU_C * r * jax.nn.log_sigmoid(lw['lru_L'].astype(f32))
    a = jnp.exp(log_a)
    u = jnp.sqrt(-jnp.expm1(2.0 * log_a)) * (i * xc.astype(f32))
    u = u.at[:, 0].add(a[:, 0] * h0.astype(f32))

    def combine(c1, c2):
        a1, b1 = c1
        a2, b2 = c2
        return a1 * a2, a2 * b1 + b2

    _, h = lax.associative_scan(combine, (a, u), axis=1)
    out = (h * jax.nn.gelu(gb.astype(f32))).astype(xb.dtype)
    return out, ext[:, -(CONV_W - 1):], h[:, -1]


def _layer(x, p, st, lw, sample):
    b, t = x.shape[:2]
    ck, cv, sh0, wkv0, conv0, h0 = st
    h = rmsnorm(x, lw['norm_mix_pre'])
    proj = h @ lw['w_in'] + lw['b_in']
    q, k, v, pb, xb, gb = jnp.split(proj, IN_SPLITS, axis=-1)
    q = q.reshape(b, t, N_HEADS, HEAD_DIM)
    k = k.reshape(b, t, N_KV_HEADS, HEAD_DIM)
    v = v.reshape(b, t, N_KV_HEADS, HEAD_DIM)
    if sample:
        o_a, nk, nv = swa_sample(q, k, v, ck, cv, lw['attn_sinks'])
    else:
        o_a = swa_prompt(q, k, v, lw['attn_sinks'])
        nk, nv = k[:, -WINDOW:], v[:, -WINDOW:]
    o_b, nsh, nwkv = rwkv7_mix(pb, sh0, wkv0, lw)
    o_c, nconv, nh = rglru_mix(xb, gb, conv0, h0, lw)
    mix = jnp.concatenate([o_a, o_b, o_c], axis=-1) @ lw['w_out'] + lw['b_out']
    x = x + rmsnorm(mix, lw['norm_mix_post'])
    f = rmsnorm(x, lw['norm_ffn_pre'])
    f = (jax.nn.silu(f @ lw['ffn_w_gate']) * (f @ lw['ffn_w_up'])) @ lw['ffn_w_down']
    x = x + rmsnorm(f, lw['norm_ffn_post'])
    x = x + jax.nn.sigmoid(x @ lw['ple_gate_w']) * (p @ lw['ple_w'])
    return x, (nk, nv, nsh, nwkv, nconv, nh)


def setup_inputs(seed: int = 0) -> dict:
    key = jax.random.key(seed)
    ks = jax.random.split(key, 64)
    cnt = iter(range(64))

    def nrm(shape, scale):
        return jax.random.normal(ks[next(cnt)], shape, jnp.float32) * scale

    def unif(shape, lo, hi):
        return jax.random.uniform(ks[next(cnt)], shape, jnp.float32, lo, hi)

    def gain(shape):
        return 1.0 + nrm(shape, 0.05)

    L = DEPTH
    inp = {}
    inp['x_prompt'] = nrm((BATCH, SEQ, D_MODEL), 1.0)
    inp['x_sample'] = nrm((DEC_BATCH, DEC_SEQ, D_MODEL), 1.0)
    inp['cache_k'] = nrm((L, DEC_BATCH, WINDOW, N_KV_HEADS, HEAD_DIM), 1.0)
    inp['cache_v'] = nrm((L, DEC_BATCH, WINDOW, N_KV_HEADS, HEAD_DIM), 1.0)
    inp['state_shift'] = nrm((L, DEC_BATCH, RWKV_PROJ), 1.0)
    inp['state_wkv'] = nrm((L, DEC_BATCH, RWKV_HEADS, RWKV_HEAD, RWKV_HEAD), 0.3)
    inp['state_conv'] = nrm((L, DEC_BATCH, CONV_W - 1, LRU_WIDTH), 1.0)
    inp['state_lru'] = nrm((L, DEC_BATCH, LRU_WIDTH), 0.5)
    inp['p_prompt'] = nrm((L, BATCH, SEQ, PLE_DIM), 1.0)
    inp['p_sample'] = nrm((L, DEC_BATCH, DEC_SEQ, PLE_DIM), 1.0)
    inp['norm_mix_pre'] = gain((L, D_MODEL))
    inp['norm_mix_post'] = gain((L, D_MODEL))
    inp['norm_ffn_pre'] = gain((L, D_MODEL))
    inp['norm_ffn_post'] = gain((L, D_MODEL))
    inp['w_in'] = nrm((L, D_MODEL, IN_COLS), D_MODEL ** -0.5)
    inp['b_in'] = nrm((L, IN_COLS), 0.02)
    inp['attn_sinks'] = nrm((L, N_HEADS), 1.0)
    inp['rwkv_mu'] = unif((L, RWKV_PROJ), 0.0, 1.0)
    inp['rwkv_w0'] = nrm((L, RWKV_WIDTH), 1.0)
    inp['rwkv_w_up'] = nrm((L, LORA_W, RWKV_WIDTH), 0.1)
    inp['rwkv_a0'] = nrm((L, RWKV_WIDTH), 0.5)
    inp['rwkv_a_up'] = nrm((L, LORA_A, RWKV_WIDTH), 0.1)
    inp['rwkv_g_up'] = nrm((L, LORA_G, RWKV_WIDTH), LORA_G ** -0.5)
    inp['rwkv_k_k'] = 0.85 + nrm((L, RWKV_WIDTH), 0.05)
    inp['rwkv_k_a'] = gain((L, RWKV_WIDTH))
    inp['rwkv_r_k'] = nrm((L, RWKV_HEADS, RWKV_HEAD), 0.1)
    inp['rwkv_ln_w'] = gain((L, RWKV_WIDTH))
    inp['rwkv_ln_b'] = nrm((L, RWKV_WIDTH), 0.02)
    inp['lru_conv_w'] = nrm((L, CONV_W, LRU_WIDTH), CONV_W ** -0.5)
    inp['lru_conv_b'] = nrm((L, LRU_WIDTH), 0.02)
    inp['lru_w_a'] = nrm((L, LRU_BLOCKS, LRU_BLOCK, LRU_BLOCK), LRU_BLOCK ** -0.5)
    inp['lru_b_a'] = nrm((L, LRU_WIDTH), 0.02)
    inp['lru_w_i'] = nrm((L, LRU_BLOCKS, LRU_BLOCK, LRU_BLOCK), LRU_BLOCK ** -0.5)
    inp['lru_b_i'] = nrm((L, LRU_WIDTH), 0.02)
    inp['lru_L'] = unif((L, LRU_WIDTH), 4.3, 9.0)
    inp['w_out'] = nrm((L, MIX_WIDTH, D_MODEL), MIX_WIDTH ** -0.5)
    inp['b_out'] = nrm((L, D_MODEL), 0.02)
    inp['ffn_w_gate'] = nrm((L, D_MODEL, D_FF), D_MODEL ** -0.5)
    inp['ffn_w_up'] = nrm((L, D_MODEL, D_FF), D_MODEL ** -0.5)
    inp['ffn_w_down'] = nrm((L, D_FF, D_MODEL), D_FF ** -0.5)
    inp['ple_w'] = nrm((L, PLE_DIM, D_MODEL), PLE_DIM ** -0.5)
    inp['ple_gate_w'] = nrm((L, D_MODEL, D_MODEL), D_MODEL ** -0.5)
    return inp


def reference(x_prompt, x_sample, cache_k, cache_v, state_shift, state_wkv, state_conv, state_lru,
              p_prompt, p_sample, norm_mix_pre, norm_mix_post, norm_ffn_pre, norm_ffn_post,
              w_in, b_in, attn_sinks, rwkv_mu, rwkv_w0, rwkv_w_up, rwkv_a0, rwkv_a_up, rwkv_g_up,
              rwkv_k_k, rwkv_k_a, rwkv_r_k, rwkv_ln_w, rwkv_ln_b, lru_conv_w, lru_conv_b,
              lru_w_a, lru_b_a, lru_w_i, lru_b_i, lru_L, w_out, b_out, ffn_w_gate, ffn_w_up,
              ffn_w_down, ple_w, ple_gate_w):
    f32 = jnp.float32
    bp = x_prompt.shape[0]
    xp, xs = x_prompt, x_sample
    new_p, new_s = [], []
    for i in range(DEPTH):
        lw = dict(norm_mix_pre=norm_mix_pre[i], norm_mix_post=norm_mix_post[i],
                  norm_ffn_pre=norm_ffn_pre[i], norm_ffn_post=norm_ffn_post[i],
                  w_in=w_in[i], b_in=b_in[i], attn_sinks=attn_sinks[i],
                  rwkv_mu=rwkv_mu[i], rwkv_w0=rwkv_w0[i], rwkv_w_up=rwkv_w_up[i],
                  rwkv_a0=rwkv_a0[i], rwkv_a_up=rwkv_a_up[i], rwkv_g_up=rwkv_g_up[i],
                  rwkv_k_k=rwkv_k_k[i], rwkv_k_a=rwkv_k_a[i], rwkv_r_k=rwkv_r_k[i],
                  rwkv_ln_w=rwkv_ln_w[i], rwkv_ln_b=rwkv_ln_b[i],
                  lru_conv_w=lru_conv_w[i], lru_conv_b=lru_conv_b[i],
                  lru_w_a=lru_w_a[i], lru_b_a=lru_b_a[i], lru_w_i=lru_w_i[i], lru_b_i=lru_b_i[i],
                  lru_L=lru_L[i], w_out=w_out[i], b_out=b_out[i],
                  ffn_w_gate=ffn_w_gate[i], ffn_w_up=ffn_w_up[i], ffn_w_down=ffn_w_down[i],
                  ple_w=ple_w[i], ple_gate_w=ple_gate_w[i])
        st_p = (None, None,
                jnp.zeros((bp, RWKV_PROJ), x_prompt.dtype),
                jnp.zeros((bp, RWKV_HEADS, RWKV_HEAD, RWKV_HEAD), f32),
                jnp.zeros((bp, CONV_W - 1, LRU_WIDTH), x_prompt.dtype),
                jnp.zeros((bp, LRU_WIDTH), f32))
        xp, sp = _layer(xp, p_prompt[i], st_p, lw, False)
        st_s = (cache_k[i], cache_v[i], state_shift[i], state_wkv[i], state_conv[i], state_lru[i])
        xs, ss = _layer(xs, p_sample[i], st_s, lw, True)
        new_p.append(sp)
        new_s.append(ss)

    def stk(lst, j):
        return jnp.stack([s[j] for s in lst], axis=0)

    return (xp, xs,
            stk(new_p, 0), stk(new_p, 1), stk(new_p, 2), stk(new_p, 3), stk(new_p, 4), stk(new_p, 5),
            stk(new_s, 0), stk(new_s, 1), stk(new_s, 2), stk(new_s, 3), stk(new_s, 4), stk(new_s, 5))
```

```python
import functools

import jax
import jax.numpy as jnp
from jax import lax
from jax.experimental import pallas as pl
from jax.experimental.pallas import tpu as pltpu

F32 = jnp.float32
BF16 = jnp.bfloat16

D_MODEL = 1024
DEPTH = 4
HEAD_DIM = 64
ATTN_WIDTH = 512
N_HEADS = 8
N_KV_HEADS = 2
GQA_GROUP = 4
KV_WIDTH = 128
WINDOW = 128
RWKV_WIDTH = 256
RWKV_HEADS = 4
RWKV_HEAD = 64
RWKV_PROJ = 1024
LRU_WIDTH = 256
CONV_W = 4
LRU_C = 8.0
D_FF = 2816
PLE_DIM = 256
RMS_EPS = 1e-6
GN_EPS = 64e-5
IN_COLS = 2304

ROW_TILE = 512
FF_CHUNK = 1408
RWKV_CHUNK = 64
LRU_TILE = 512
SAMPLE_PAD = 8
VMEM_LIMIT = 56 * 1024 * 1024


def _cparams(sem):
    return pltpu.CompilerParams(dimension_semantics=sem, vmem_limit_bytes=VMEM_LIMIT)


def _bdot(a, b):
    return jnp.dot(a.astype(BF16), b.astype(BF16), preferred_element_type=F32)


def _split(x):
    hi = x.astype(BF16)
    lo = (x - hi.astype(F32)).astype(BF16)
    return hi, lo


def _dot3(a, b, dims=None):
    ah, al = _split(a)
    bh, bl = _split(b)
    if dims is None:
        d = lambda x, y: jnp.dot(x, y, preferred_element_type=F32)
    else:
        d = lambda x, y: lax.dot_general(x, y, dims, preferred_element_type=F32)
    return d(ah, bh) + (d(ah, bl) + d(al, bh))


def _dot_exact_rhs(a, b_bf16, passes=3):
    acc = None
    rem = a
    for _ in range(passes):
        piece = rem.astype(BF16)
        rem = rem - piece.astype(F32)
        t = jnp.dot(piece, b_bf16, preferred_element_type=F32)
        acc = t if acc is None else acc + t
    return acc


def _rmsnorm(x, g):
    ms = jnp.mean(x * x, axis=-1, keepdims=True)
    return x * lax.rsqrt(ms + RMS_EPS) * g


def _softplus(x):
    return jnp.maximum(x, 0.0) + jnp.log1p(jnp.exp(-jnp.abs(x)))


def _sigmoid(x):
    return 1.0 / (1.0 + jnp.exp(-x))


def _in_kernel(x_ref, g_ref, w_ref, b_ref, q_ref, kv_ref, rw_ref, lr_ref):
    h = _rmsnorm(x_ref[...], g_ref[...]).astype(BF16)
    for ref, lo, hi in ((q_ref, 0, 512), (kv_ref, 512, 768), (rw_ref, 768, 1792), (lr_ref, 1792, 2304)):
        ref[...] = jnp.dot(h, w_ref[:, lo:hi], preferred_element_type=F32) + b_ref[:, lo:hi]


def _in_proj(x, g, w, b):
    m = x.shape[0]
    tm = min(ROW_TILE, m)
    row = lambda w_: pl.BlockSpec((tm, w_), lambda i: (i, 0))
    const = lambda s: pl.BlockSpec(s, lambda i: (0, 0))
    return pl.pallas_call(
        _in_kernel,
        grid=(m // tm,),
        in_specs=[row(D_MODEL), const((1, D_MODEL)), const((D_MODEL, IN_COLS)), const((1, IN_COLS))],
        out_specs=[row(512), row(256), row(1024), row(512)],
        out_shape=[jax.ShapeDtypeStruct((m, n), F32) for n in (512, 256, 1024, 512)],
        compiler_params=_cparams(("arbitrary",)),
        name="in_proj",
    )(x, g, w, b)


def _attn_prompt_kernel(sink_ref, q_ref, kvp_ref, kvc_ref, o_ref):
    j = pl.program_id(1)
    q = q_ref[...] * (HEAD_DIM ** -0.5)
    kvp = kvp_ref[...]
    kvc = kvc_ref[...]
    qi = lax.broadcasted_iota(jnp.int32, (WINDOW, 2 * WINDOW), 0) + WINDOW
    kj = lax.broadcasted_iota(jnp.int32, (WINDOW, 2 * WINDOW), 1)
    d = qi - kj
    mask = (d >= 0) & (d <= WINDOW) & ((j > 0) | (kj >= WINDOW))
    outs = []
    for g in range(N_KV_HEADS):
        ks = slice(g * HEAD_DIM, (g + 1) * HEAD_DIM)
        vs = slice(KV_WIDTH + g * HEAD_DIM, KV_WIDTH + (g + 1) * HEAD_DIM)
        k_ext = jnp.concatenate([kvp[:, ks], kvc[:, ks]], axis=0).astype(BF16)
        v_ext = jnp.concatenate([kvp[:, vs], kvc[:, vs]], axis=0).astype(BF16)
        for hh in range(GQA_GROUP):
            h = g * GQA_GROUP + hh
            qh = q[:, h * HEAD_DIM:(h + 1) * HEAD_DIM].astype(BF16)
            s = lax.dot_general(qh, k_ext, (((1,), (1,)), ((), ())), preferred_element_type=F32)
            s = jnp.where(mask, s, -1e30)
            sink = sink_ref[h]
            m = jnp.maximum(jnp.max(s, axis=-1, keepdims=True), sink)
            e = jnp.exp(s - m)
            den = jnp.sum(e, axis=-1, keepdims=True) + jnp.exp(sink - m)
            p = e / den
            outs.append(jnp.dot(p.astype(BF16), v_ext, preferred_element_type=F32))
    o_ref[...] = jnp.concatenate(outs, axis=-1)


def _attn_prompt(q, kv, sinks):
    b, t, _ = q.shape
    nb = t // WINDOW
    return pl.pallas_call(
        _attn_prompt_kernel,
        grid=(b, nb),
        in_specs=[
            pl.BlockSpec(memory_space=pltpu.SMEM),
            pl.BlockSpec((None, WINDOW, ATTN_WIDTH), lambda i, j: (i, j, 0)),
            pl.BlockSpec((None, WINDOW, 2 * KV_WIDTH), lambda i, j: (i, jnp.maximum(j - 1, 0), 0)),
            pl.BlockSpec((None, WINDOW, 2 * KV_WIDTH), lambda i, j: (i, j, 0)),
        ],
        out_specs=pl.BlockSpec((None, WINDOW, ATTN_WIDTH), lambda i, j: (i, j, 0)),
        out_shape=jax.ShapeDtypeStruct((b, t, ATTN_WIDTH), F32),
        compiler_params=_cparams(("arbitrary", "arbitrary")),
        name="attn_prompt",
    )(sinks, q, kv, kv)


def _attn_sample_kernel(tn, sink_ref, q_ref, kvn_ref, ck_ref, cv_ref, o_ref, nk_ref, nv_ref):
    rows = GQA_GROUP * tn
    ck = ck_ref[...]
    cv = cv_ref[...]
    kvn = kvn_ref[...]
    row = lax.broadcasted_iota(jnp.int32, (1, rows, 1), 1)
    tok = row % tn
    col = lax.broadcasted_iota(jnp.int32, (1, 1, WINDOW), 2)
    cmask = col >= tok
    for g in range(N_KV_HEADS):
        ks = slice(g * HEAD_DIM, (g + 1) * HEAD_DIM)
        vs = slice(KV_WIDTH + g * HEAD_DIM, KV_WIDTH + (g + 1) * HEAD_DIM)
        qg = q_ref[:, g] * (HEAD_DIM ** -0.5)
        qg_b = qg.astype(BF16).astype(F32)
        sc = jnp.einsum('bqd,bkd->bqk', qg.astype(BF16), ck[:, :, ks].astype(BF16),
                        preferred_element_type=F32)
        sc = jnp.where(cmask, sc, -1e30)
        sink = jnp.zeros((1, rows, 1), F32)
        for hh in range(GQA_GROUP):
            sink = jnp.where(row // tn == hh, sink_ref[g * GQA_GROUP + hh], sink)
        m = jnp.maximum(jnp.max(sc, axis=-1, keepdims=True), sink)
        sn = []
        for jn in range(tn):
            kn = kvn[:, jn:jn + 1, ks].astype(BF16).astype(F32)
            s_j = jnp.sum(qg_b * kn, axis=-1, keepdims=True)
            s_j = jnp.where(tok >= jn, s_j, -1e30)
            sn.append(s_j)
            m = jnp.maximum(m, s_j)
        ec = jnp.exp(sc - m)
        den = jnp.sum(ec, axis=-1, keepdims=True) + jnp.exp(sink - m)
        en = [jnp.exp(s_j - m) for s_j in sn]
        for e_j in en:
            den = den + e_j
        inv = 1.0 / den
        o = jnp.einsum('bqk,bkd->bqd', (ec * inv).astype(BF16), cv[:, :, ks].astype(BF16),
                       preferred_element_type=F32)
        for jn in range(tn):
            vn = kvn[:, jn:jn + 1, vs].astype(BF16).astype(F32)
            o = o + (en[jn] * inv).astype(BF16).astype(F32) * vn
        o_ref[:, g] = o
    nk_ref[:, 0:WINDOW - tn, :] = ck[:, tn:WINDOW, :]
    nk_ref[:, WINDOW - tn:WINDOW, :] = kvn[:, :, 0:KV_WIDTH]
    nv_ref[:, 0:WINDOW - tn, :] = cv[:, tn:WINDOW, :]
    nv_ref[:, WINDOW - tn:WINDOW, :] = kvn[:, :, KV_WIDTH:2 * KV_WIDTH]


def _attn_sample(q, kv, ck, cv, sinks):
    b, tn, _ = q.shape
    bb = 8
    rows = GQA_GROUP * tn
    qh = q.reshape(b, tn, N_KV_HEADS, GQA_GROUP, HEAD_DIM).transpose(0, 2, 3, 1, 4)
    qh = qh.reshape(b, N_KV_HEADS, rows, HEAD_DIM)
    o, nk, nv = pl.pallas_call(
        functools.partial(_attn_sample_kernel, tn),
        grid=(b // bb,),
        in_specs=[
            pl.BlockSpec(memory_space=pltpu.SMEM),
            pl.BlockSpec((bb, N_KV_HEADS, rows, HEAD_DIM), lambda i: (i, 0, 0, 0)),
            pl.BlockSpec((bb, tn, 2 * KV_WIDTH), lambda i: (i, 0, 0)),
            pl.BlockSpec((bb, WINDOW, KV_WIDTH), lambda i: (i, 0, 0)),
            pl.BlockSpec((bb, WINDOW, KV_WIDTH), lambda i: (i, 0, 0)),
        ],
        out_specs=[
            pl.BlockSpec((bb, N_KV_HEADS, rows, HEAD_DIM), lambda i: (i, 0, 0, 0)),
            pl.BlockSpec((bb, WINDOW, KV_WIDTH), lambda i: (i, 0, 0)),
            pl.BlockSpec((bb, WINDOW, KV_WIDTH), lambda i: (i, 0, 0)),
        ],
        out_shape=[
            jax.ShapeDtypeStruct((b, N_KV_HEADS, rows, HEAD_DIM), F32),
            jax.ShapeDtypeStruct((b, WINDOW, KV_WIDTH), F32),
            jax.ShapeDtypeStruct((b, WINDOW, KV_WIDTH), F32),
        ],
        compiler_params=_cparams(("arbitrary",)),
        name="attn_sample",
    )(sinks, qh, kv, ck, cv)
    o = o.reshape(b, N_KV_HEADS, GQA_GROUP, tn, HEAD_DIM).transpose(0, 3, 1, 2, 4)
    return o.reshape(b, tn, ATTN_WIDTH), nk, nv


def _tile_rows(x, n):
    return jnp.concatenate([x] * n, axis=0)


def _rwkv_kernel(c, valid, x_ref, sh0_ref, st0_ref, mu_ref, lora_ref, gup_ref, vec_ref, ones_ref,
                 o_ref, st_ref, last_ref, state_ref):
    ci = pl.program_id(1)
    nc = pl.num_programs(1)
    hw = RWKV_WIDTH
    n = RWKV_HEAD
    cw = RWKV_HEADS * c

    @pl.when(ci == 0)
    def _():
        last_ref[...] = sh0_ref[...]
        state_ref[...] = st0_ref[...]

    x = x_ref[...]
    row = lax.broadcasted_iota(jnp.int32, (c, 1), 0)
    prev = jnp.where(row == 0, last_ref[...], pltpu.roll(x, 1, 0))
    last_ref[...] = x[c - 1:c, :]
    xs = x + (prev - x) * mu_ref[...]

    r = xs[:, 0:hw]
    k = xs[:, hw:2 * hw]
    v = xs[:, 2 * hw:3 * hw]
    wa = xs[:, 3 * hw:3 * hw + 128]
    xg = xs[:, 3 * hw + 128:]
    lane128 = lax.broadcasted_iota(jnp.int32, (c, 128), 1)
    lora_in = jnp.where(lane128 < 64, jnp.tanh(wa), wa)
    lora = _bdot(lora_in, lora_ref[...])
    w0, a0, k_k, k_a, r_k, ln_w, ln_b = (vec_ref[i:i + 1, :] for i in range(7))
    w_log = -_softplus(-(w0 + lora[:, 0:hw])) - 0.5
    logw = -jnp.exp(w_log)
    a = _sigmoid(a0 + lora[:, hw:2 * hw])
    g = _bdot(_sigmoid(xg), gup_ref[...])

    ones_blk = ones_ref[...]
    kk = k * k_k
    ss = _dot_exact_rhs(kk * kk, ones_blk, passes=2)
    kk = kk * lax.rsqrt(jnp.maximum(ss, 1e-24))
    k2 = k * (1.0 + (a - 1.0) * k_a)
    bv = kk * a
    if valid < c:
        live = row < valid
        logw = jnp.where(live, logw, 0.0)
        kk = jnp.where(live, kk, 0.0)
        bv = jnp.where(live, bv, 0.0)
        k2 = jnp.where(live, k2, 0.0)
        v = jnp.where(live, v, 0.0)

    tt = lax.broadcasted_iota(jnp.int32, (c, c), 0)
    ts = lax.broadcasted_iota(jnp.int32, (c, c), 1)
    tri = jnp.where(ts <= tt, 1.0, 0.0).astype(BF16)
    pieces = []
    rem = logw
    for _ in range(3):
        p = rem.astype(BF16)
        rem = rem - p.astype(F32)
        pieces.append(jnp.dot(tri, p, preferred_element_type=F32))
    cum = pieces[0] + (pieces[1] + pieces[2])
    e_inc = jnp.exp(cum)
    e_exc = jnp.exp(cum - logw)
    e_inv = jnp.exp(-cum)
    kq = kk * e_exc
    rq = r * e_inc
    kd = k2 * e_inv
    bd = bv * e_inv
    w_end = e_inc[c - 1:c, :]
    kend = kd * w_end
    bend = bd * w_end

    rh = lax.broadcasted_iota(jnp.int32, (cw, hw), 0) // c
    lh = lax.broadcasted_iota(jnp.int32, (cw, hw), 1) // n
    head_rows = rh == lh

    def expand(z):
        return jnp.where(head_rows, _tile_rows(z, RWKV_HEADS), 0.0)

    lhs = jnp.concatenate([kq, rq], axis=0)
    rhs = jnp.concatenate([expand(bd), expand(kd)], axis=0)
    nt = (((1,), (1,)), ((), ()))
    prod = _dot3(lhs, rhs, nt)
    t_i = lax.broadcasted_iota(jnp.int32, (c, cw), 0)
    s_i = lax.broadcasted_iota(jnp.int32, (c, cw), 1) % c
    strict = s_i < t_i
    incl = s_i <= t_i
    a_b = jnp.where(strict, prod[0:c, 0:cw], 0.0)
    a_k = jnp.where(strict, prod[0:c, cw:2 * cw], 0.0)
    p_b = jnp.where(incl, prod[c:2 * c, 0:cw], 0.0)
    p_k = jnp.where(incl, prod[c:2 * c, cw:2 * cw], 0.0)

    rb = lax.broadcasted_iota(jnp.int32, (cw, cw), 0) // c
    cb = lax.broadcasted_iota(jnp.int32, (cw, cw), 1) // c
    blk = rb == cb

    def bdiag(z):
        return jnp.where(blk, _tile_rows(z, RWKV_HEADS), 0.0)

    xm = -a_b
    tinv = jnp.where(s_i == t_i, 1.0, 0.0) + xm
    span = 1
    while 2 * span < c:
        xm = _dot3(xm, bdiag(xm))
        tinv = tinv + _dot3(tinv, bdiag(xm))
        span *= 2

    st = state_ref[...]
    rs = lax.broadcasted_iota(jnp.int32, (hw, hw), 0) // n
    cs = lax.broadcasted_iota(jnp.int32, (hw, hw), 1) // n
    sblk = rs == cs
    st_d = jnp.where(sblk, _tile_rows(st, RWKV_HEADS), 0.0)
    v_d = expand(v)
    rhs_u = _dot3(kq, st_d) + _dot3(a_k, v_d)
    u = _dot3(tinv, expand(rhs_u))
    y = _dot3(rq, st_d) + _dot3(p_k, v_d) - _dot3(p_b, expand(u))

    kr = lax.broadcasted_iota(jnp.int32, (n, hw), 0)
    kc_ = lax.broadcasted_iota(jnp.int32, (n, hw), 1) % n
    dm = jnp.where(kr == kc_, w_end, 0.0)
    lhs_s = jnp.concatenate([kend, -bend, dm], axis=0)
    rhs_s = jnp.concatenate([v, u, st], axis=0)
    tn_dims = (((0,), (0,)), ((), ()))
    gm = jnp.where(sblk, _dot3(lhs_s, rhs_s, tn_dims), 0.0)
    st_new = gm[0:n] + gm[n:2 * n] + (gm[2 * n:3 * n] + gm[3 * n:4 * n])
    state_ref[...] = st_new

    @pl.when(ci == nc - 1)
    def _():
        st_ref[...] = st_new

    mean = _dot_exact_rhs(y, ones_blk, passes=2) * (1.0 / n)
    yc = y - mean
    var = _dot_exact_rhs(yc * yc, ones_blk, passes=2) * (1.0 / n)
    yn = yc * lax.rsqrt(var + GN_EPS) * ln_w + ln_b
    bonus = _dot_exact_rhs(r * k2 * r_k, ones_blk, passes=2) * v
    o_ref[...] = (yn + bonus) * g


def _rwkv(x, sh0, st0, mu, lora_w, g_up, vecs, ones_blk, valid):
    b, t, _ = x.shape
    c = min(RWKV_CHUNK, t)
    const = lambda s: pl.BlockSpec(s, lambda i, j: (0,) * len(s))
    return pl.pallas_call(
        functools.partial(_rwkv_kernel, c, valid if valid < c else c),
        grid=(b, t // c),
        in_specs=[
            pl.BlockSpec((None, c, RWKV_PROJ), lambda i, j: (i, j, 0)),
            pl.BlockSpec((None, 1, RWKV_PROJ), lambda i, j: (i, 0, 0)),
            pl.BlockSpec((None, RWKV_HEAD, RWKV_WIDTH), lambda i, j: (i, 0, 0)),
            const((1, RWKV_PROJ)), const((128, 512)), const((128, RWKV_WIDTH)),
            const((8, RWKV_WIDTH)), const((RWKV_WIDTH, RWKV_WIDTH)),
        ],
        out_specs=[
            pl.BlockSpec((None, c, RWKV_WIDTH), lambda i, j: (i, j, 0)),
            pl.BlockSpec((None, RWKV_HEAD, RWKV_WIDTH), lambda i, j: (i, 0, 0)),
        ],
        out_shape=[
            jax.ShapeDtypeStruct((b, t, RWKV_WIDTH), F32),
            jax.ShapeDtypeStruct((b, RWKV_HEAD, RWKV_WIDTH), F32),
        ],
        scratch_shapes=[pltpu.VMEM((1, RWKV_PROJ), F32), pltpu.VMEM((RWKV_HEAD, RWKV_WIDTH), F32)],
        compiler_params=_cparams(("arbitrary", "arbitrary")),
        name="rwkv",
    )(x, sh0, st0, mu, lora_w, g_up, vecs, ones_blk)


def _lru_kernel(tc, valid, x_ref, cv0_ref, h0_ref, cw_ref, wg_ref, vec_ref, o_ref, h_ref, ext_ref, hc_ref):
    ti = pl.program_id(1)
    w = LRU_WIDTH

    @pl.when(ti == 0)
    def _():
        ext_ref[0:8, :] = cv0_ref[...]
        hc_ref[...] = h0_ref[...]

    xb = x_ref[:, 0:w]
    gb = x_ref[:, w:2 * w]
    ext_ref[8:8 + tc, :] = xb
    conv_b, b_a, b_i, lam = (vec_ref[i:i + 1, :] for i in range(4))
    xc = conv_b + xb * cw_ref[CONV_W - 1:CONV_W, :]
    for j in range(CONV_W - 1):
        xc = xc + ext_ref[pl.ds(8 - (CONV_W - 1) + j, tc), :] * cw_ref[j:j + 1, :]
    ext_ref[0:8, :] = xb[tc - 8:tc, :]

    gates = _bdot(xc, wg_ref[...])
    r = _sigmoid(gates[:, 0:w] + b_a)
    i = _sigmoid(gates[:, w:2 * w] + b_i)
    log_a = LRU_C * r * (-_softplus(-lam))
    a = jnp.exp(log_a)
    th = jnp.tanh(log_a)
    u = jnp.sqrt(-2.0 * th / (1.0 - th)) * (i * xc)
    row = lax.broadcasted_iota(jnp.int32, (tc, 1), 0)
    if valid < tc:
        live = row < valid
        a = jnp.where(live, a, 1.0)
        u = jnp.where(live, u, 0.0)
    span = 1
    while span < tc:
        ok = row >= span
        a_s = pltpu.roll(a, span, 0)
        u_s = pltpu.roll(u, span, 0)
        u = jnp.where(ok, a * u_s + u, u)
        a = jnp.where(ok, a * a_s, a)
        span *= 2
    h = a * hc_ref[...] + u
    hc_ref[...] = h[tc - 1:tc, :]
    h_ref[...] = h[tc - 1:tc, :]
    gelu = 0.5 * gb * (1.0 + jnp.tanh(0.7978845608028654 * (gb + 0.044715 * (gb * gb * gb))))
    o_ref[...] = h * gelu


def _lru(x, cv0, h0, conv_w, w_gates, vecs, valid):
    b, t, _ = x.shape
    tc = min(LRU_TILE, t)
    const = lambda s: pl.BlockSpec(s, lambda i, j: (0,) * len(s))
    return pl.pallas_call(
        functools.partial(_lru_kernel, tc, valid if valid < tc else tc),
        grid=(b, t // tc),
        in_specs=[
            pl.BlockSpec((None, tc, 2 * LRU_WIDTH), lambda i, j: (i, j, 0)),
            pl.BlockSpec((None, 8, LRU_WIDTH), lambda i, j: (i, 0, 0)),
            pl.BlockSpec((None, 1, LRU_WIDTH), lambda i, j: (i, 0, 0)),
            const((CONV_W, LRU_WIDTH)), const((LRU_WIDTH, 2 * LRU_WIDTH)), const((4, LRU_WIDTH)),
        ],
        out_specs=[
            pl.BlockSpec((None, tc, LRU_WIDTH), lambda i, j: (i, j, 0)),
            pl.BlockSpec((None, 1, LRU_WIDTH), lambda i, j: (i, 0, 0)),
        ],
        out_shape=[
            jax.ShapeDtypeStruct((b, t, LRU_WIDTH), F32),
            jax.ShapeDtypeStruct((b, 1, LRU_WIDTH), F32),
        ],
        scratch_shapes=[pltpu.VMEM((tc + 8, LRU_WIDTH), F32), pltpu.VMEM((1, LRU_WIDTH), F32)],
        compiler_params=_cparams(("arbitrary", "arbitrary")),
        name="lru",
    )(x, cv0, h0, conv_w, w_gates, vecs)


def _mixout_kernel(x_ref, oa_ref, ob_ref, oc_ref, w_ref, vec_ref, y_ref):
    mix = (jnp.dot(oa_ref[...].astype(BF16), w_ref[0:512, :], preferred_element_type=F32)
           + jnp.dot(ob_ref[...].astype(BF16), w_ref[512:768, :], preferred_element_type=F32)
           + jnp.dot(oc_ref[...].astype(BF16), w_ref[768:1024, :], preferred_element_type=F32)
           + vec_ref[0:1, :])
    y_ref[...] = x_ref[...] + _rmsnorm(mix, vec_ref[1:2, :])


def _mixout(x, oa, ob, oc, w, vecs):
    m = x.shape[0]
    tm = min(ROW_TILE, m)
    row = lambda w_: pl.BlockSpec((tm, w_), lambda i: (i, 0))
    const = lambda s: pl.BlockSpec(s, lambda i: (0, 0))
    return pl.pallas_call(
        _mixout_kernel,
        grid=(m // tm,),
        in_specs=[row(D_MODEL), row(512), row(256), row(256), const((D_MODEL, D_MODEL)), const((2, D_MODEL))],
        out_specs=row(D_MODEL),
        out_shape=jax.ShapeDtypeStruct((m, D_MODEL), F32),
        compiler_params=_cparams(("arbitrary",)),
        name="mixout",
    )(x, oa, ob, oc, w, vecs)


def _ffn_kernel(x_ref, p_ref, wg_ref, wu_ref, wd_ref, pg_ref, pw_ref, vec_ref, y_ref):
    x = x_ref[...]
    f = _rmsnorm(x, vec_ref[0:1, :]).astype(BF16)
    acc = None
    for lo in range(0, D_FF, FF_CHUNK):
        gate = jnp.dot(f, wg_ref[:, lo:lo + FF_CHUNK], preferred_element_type=F32)
        up = jnp.dot(f, wu_ref[:, lo:lo + FF_CHUNK], preferred_element_type=F32)
        hid = (gate * _sigmoid(gate) * up).astype(BF16)
        part = jnp.dot(hid, wd_ref[lo:lo + FF_CHUNK, :], preferred_element_type=F32)
        acc = part if acc is None else acc + part
    x2 = x + _rmsnorm(acc, vec_ref[1:2, :])
    gate = _sigmoid(jnp.dot(x2.astype(BF16), pg_ref[...], preferred_element_type=F32))
    emb = jnp.dot(p_ref[...].astype(BF16), pw_ref[...], preferred_element_type=F32)
    y_ref[...] = x2 + gate * emb


def _ffn(x, p, wg, wu, wd, pg, pw, vecs):
    m = x.shape[0]
    tm = min(ROW_TILE, m)
    row = lambda w_: pl.BlockSpec((tm, w_), lambda i: (i, 0))
    const = lambda s: pl.BlockSpec(s, lambda i: (0, 0), pipeline_mode=pl.Buffered(1))
    return pl.pallas_call(
        _ffn_kernel,
        grid=(m // tm,),
        in_specs=[row(D_MODEL), row(PLE_DIM), const((D_MODEL, D_FF)), const((D_MODEL, D_FF)),
                  const((D_FF, D_MODEL)), const((D_MODEL, D_MODEL)), const((PLE_DIM, D_MODEL)),
                  const((2, D_MODEL))],
        out_specs=row(D_MODEL),
        out_shape=jax.ShapeDtypeStruct((m, D_MODEL), F32),
        compiler_params=_cparams(("arbitrary",)),
        name="ffn",
    )(x, p, wg, wu, wd, pg, pw, vecs)


def _block_diag(w):
    nb, n, _ = w.shape
    eye = jnp.eye(nb, dtype=w.dtype)
    return (eye[:, None, :, None] * w[:, :, None, :]).reshape(nb * n, nb * n)


def _layer(x, p, state, lw, sample):
    b, t, _ = x.shape
    m = b * t
    x2 = x.reshape(m, D_MODEL)
    q, kv, rw, lr = _in_proj(x2, lw['norm_mix_pre'], lw['w_in'], lw['b_in'])
    q = q.reshape(b, t, 512)
    kv = kv.reshape(b, t, 256)
    rw = rw.reshape(b, t, RWKV_PROJ)
    lr = lr.reshape(b, t, 2 * LRU_WIDTH)
    ck, cv, sh0, wkv0, conv0, h0 = state

    if sample:
        o_a, nk, nv = _attn_sample(q, kv, ck.reshape(b, WINDOW, KV_WIDTH), cv.reshape(b, WINDOW, KV_WIDTH),
                                   lw['attn_sinks'])
        nk = nk.reshape(b, WINDOW, N_KV_HEADS, HEAD_DIM)
        nv = nv.reshape(b, WINDOW, N_KV_HEADS, HEAD_DIM)
    else:
        o_a = _attn_prompt(q, kv, lw['attn_sinks'])
        nk = kv[:, t - WINDOW:, 0:KV_WIDTH].reshape(b, WINDOW, N_KV_HEADS, HEAD_DIM)
        nv = kv[:, t - WINDOW:, KV_WIDTH:].reshape(b, WINDOW, N_KV_HEADS, HEAD_DIM)

    tp = t if t % SAMPLE_PAD == 0 else SAMPLE_PAD * (-(-t // SAMPLE_PAD))
    pad = lambda z: z if tp == t else jnp.pad(z, ((0, 0), (0, tp - t), (0, 0)))

    st0 = wkv0.transpose(0, 3, 1, 2).reshape(b, RWKV_HEAD, RWKV_WIDTH)
    o_b, st = _rwkv(pad(rw), sh0.reshape(b, 1, RWKV_PROJ), st0, lw['rwkv_mu'], lw['rwkv_lora'],
                    lw['rwkv_g_up'], lw['rwkv_vecs'], lw['ones_blk'], t)
    nwkv = st.reshape(b, RWKV_HEAD, RWKV_HEADS, RWKV_HEAD).transpose(0, 2, 3, 1)
    nsh = rw[:, t - 1, :]

    cv0 = jnp.pad(conv0, ((0, 0), (8 - (CONV_W - 1), 0), (0, 0)))
    o_c, nh = _lru(pad(lr), cv0, h0.reshape(b, 1, LRU_WIDTH), lw['lru_conv_w'], lw['lru_w_gates'],
                   lw['lru_vecs'], t)
    nconv = lr[:, t - (CONV_W - 1):, 0:LRU_WIDTH]
    nh = nh.reshape(b, LRU_WIDTH)

    x2 = _mixout(x2, o_a.reshape(m, 512), o_b[:, :t].reshape(m, 256), o_c[:, :t].reshape(m, 256),
                 lw['w_out'], lw['mix_vecs'])
    x2 = _ffn(x2, p.reshape(m, PLE_DIM), lw['ffn_w_gate'], lw['ffn_w_up'], lw['ffn_w_down'],
              lw['ple_gate_w'], lw['ple_w'], lw['ffn_vecs'])
    return x2.reshape(b, t, D_MODEL), (nk, nv, nsh, nwkv, nconv, nh)


def kernel(x_prompt, x_sample, cache_k, cache_v, state_shift, state_wkv, state_conv, state_lru,
           p_prompt, p_sample, norm_mix_pre, norm_mix_post, norm_ffn_pre, norm_ffn_post,
           w_in, b_in, attn_sinks, rwkv_mu, rwkv_w0, rwkv_w_up, rwkv_a0, rwkv_a_up, rwkv_g_up,
           rwkv_k_k, rwkv_k_a, rwkv_r_k, rwkv_ln_w, rwkv_ln_b, lru_conv_w, lru_conv_b,
           lru_w_a, lru_b_a, lru_w_i, lru_b_i, lru_L, w_out, b_out, ffn_w_gate, ffn_w_up,
           ffn_w_down, ple_w, ple_gate_w):
    bp = x_prompt.shape[0]
    head_id = jnp.arange(RWKV_WIDTH) // RWKV_HEAD
    ones_blk = (head_id[:, None] == head_id[None, :]).astype(BF16)
    xp, xs = x_prompt, x_sample
    new_p, new_s = [], []
    for i in range(DEPTH):
        zeros_w = jnp.zeros((64, RWKV_WIDTH), F32)
        lora = jnp.concatenate([jnp.concatenate([rwkv_w_up[i], zeros_w], axis=1),
                                jnp.concatenate([zeros_w, rwkv_a_up[i]], axis=1)], axis=0)
        lw = dict(
            norm_mix_pre=norm_mix_pre[i][None], w_in=w_in[i].astype(BF16), b_in=b_in[i][None],
            attn_sinks=attn_sinks[i],
            rwkv_mu=rwkv_mu[i][None], rwkv_lora=lora.astype(BF16), rwkv_g_up=rwkv_g_up[i].astype(BF16),
            rwkv_vecs=jnp.stack([rwkv_w0[i], rwkv_a0[i], rwkv_k_k[i], rwkv_k_a[i],
                                 rwkv_r_k[i].reshape(RWKV_WIDTH), rwkv_ln_w[i], rwkv_ln_b[i],
                                 jnp.zeros((RWKV_WIDTH,), F32)]),
            ones_blk=ones_blk,
            lru_conv_w=lru_conv_w[i],
            lru_w_gates=jnp.concatenate([_block_diag(lru_w_a[i]), _block_diag(lru_w_i[i])], axis=1).astype(BF16),
            lru_vecs=jnp.stack([lru_conv_b[i], lru_b_a[i], lru_b_i[i], lru_L[i]]),
            w_out=w_out[i].astype(BF16), mix_vecs=jnp.stack([b_out[i], norm_mix_post[i]]),
            ffn_w_gate=ffn_w_gate[i].astype(BF16), ffn_w_up=ffn_w_up[i].astype(BF16),
            ffn_w_down=ffn_w_down[i].astype(BF16), ple_gate_w=ple_gate_w[i].astype(BF16),
            ple_w=ple_w[i].astype(BF16), ffn_vecs=jnp.stack([norm_ffn_pre[i], norm_ffn_post[i]]),
        )
        st_p = (None, None,
                jnp.zeros((bp, RWKV_PROJ), F32),
                jnp.zeros((bp, RWKV_HEADS, RWKV_HEAD, RWKV_HEAD), F32),
                jnp.zeros((bp, CONV_W - 1, LRU_WIDTH), F32),
                jnp.zeros((bp, LRU_WIDTH), F32))
        xp, sp = _layer(xp, p_prompt[i], st_p, lw, False)
        st_s = (cache_k[i], cache_v[i], state_shift[i], state_wkv[i], state_conv[i], state_lru[i])
        xs, ss = _layer(xs, p_sample[i], st_s, lw, True)
        new_p.append(sp)
        new_s.append(ss)

    def stk(lst, j):
        return jnp.stack([s[j] for s in lst], axis=0)

    return (xp, xs,
            stk(new_p, 0), stk(new_p, 1), stk(new_p, 2), stk(new_p, 3), stk(new_p, 4), stk(new_p, 5),
            stk(new_s, 0), stk(new_s, 1), stk(new_s, 2), stk(new_s, 3), stk(new_s, 4), stk(new_s, 5))
```

```python
import functools

import jax
import jax.numpy as jnp
from jax import lax
from jax.experimental import pallas as pl
from jax.experimental.pallas import tpu as pltpu

F32 = jnp.float32
BF16 = jnp.bfloat16

D_MODEL = 1024
DEPTH = 4
HEAD_DIM = 64
ATTN_WIDTH = 512
N_HEADS = 8
N_KV_HEADS = 2
GQA_GROUP = 4
KV_WIDTH = 128
WINDOW = 128
RWKV_WIDTH = 256
RWKV_HEADS = 4
RWKV_HEAD = 64
RWKV_PROJ = 1024
LRU_WIDTH = 256
CONV_W = 4
LRU_C = 8.0
D_FF = 2816
PLE_DIM = 256
RMS_EPS = 1e-6
GN_EPS = 64e-5
IN_COLS = 2304

ROW_TILE = 512
FF_CHUNK = 1408
RWKV_CHUNK = 64
RWKV_SEQS = 4
RWKV_SHORT_SEQS = 8
LRU_TILE = 512
LRU_SHORT_SEQS = 32
SAMPLE_PAD = 8
VMEM_LIMIT = 56 * 1024 * 1024


def _cparams(sem):
    return pltpu.CompilerParams(dimension_semantics=sem, vmem_limit_bytes=VMEM_LIMIT)


def _bdot(a, b):
    return jnp.dot(a.astype(BF16), b.astype(BF16), preferred_element_type=F32)


def _dot_exact_rhs(a, b_bf16, passes=3):
    acc = None
    rem = a
    for _ in range(passes):
        piece = rem.astype(BF16)
        rem = rem - piece.astype(F32)
        t = jnp.dot(piece, b_bf16, preferred_element_type=F32)
        acc = t if acc is None else acc + t
    return acc


def _rmsnorm(x, g):
    ms = jnp.mean(x * x, axis=-1, keepdims=True)
    return x * lax.rsqrt(ms + RMS_EPS) * g


def _softplus(x):
    return jnp.maximum(x, 0.0) + jnp.log1p(jnp.exp(-jnp.abs(x)))


def _sigmoid(x):
    return 1.0 / (1.0 + jnp.exp(-x))


def _in_kernel(x_ref, g_ref, w_ref, b_ref, q_ref, kv_ref, rw_ref, lr_ref):
    h = _rmsnorm(x_ref[...], g_ref[...]).astype(BF16)
    for ref, lo, hi in ((q_ref, 0, 512), (kv_ref, 512, 768), (rw_ref, 768, 1792), (lr_ref, 1792, 2304)):
        ref[...] = jnp.dot(h, w_ref[:, lo:hi], preferred_element_type=F32) + b_ref[:, lo:hi]


def _in_proj(x, g, w, b):
    m = x.shape[0]
    tm = min(ROW_TILE, m)
    row = lambda w_: pl.BlockSpec((tm, w_), lambda i: (i, 0))
    const = lambda s: pl.BlockSpec(s, lambda i: (0, 0))
    return pl.pallas_call(
        _in_kernel,
        grid=(m // tm,),
        in_specs=[row(D_MODEL), const((1, D_MODEL)), const((D_MODEL, IN_COLS)), const((1, IN_COLS))],
        out_specs=[row(512), row(256), row(1024), row(512)],
        out_shape=[jax.ShapeDtypeStruct((m, n), F32) for n in (512, 256, 1024, 512)],
        compiler_params=_cparams(("arbitrary",)),
        name="in_proj",
    )(x, g, w, b)


def _attn_prompt_kernel(sink_ref, q_ref, kvp_ref, kvc_ref, o_ref):
    j = pl.program_id(1)
    q = q_ref[...] * (HEAD_DIM ** -0.5)
    kvp = kvp_ref[...]
    kvc = kvc_ref[...]
    qi = lax.broadcasted_iota(jnp.int32, (WINDOW, 2 * WINDOW), 0) + WINDOW
    kj = lax.broadcasted_iota(jnp.int32, (WINDOW, 2 * WINDOW), 1)
    d = qi - kj
    mask = (d >= 0) & (d <= WINDOW) & ((j > 0) | (kj >= WINDOW))
    nt = (((1,), (1,)), ((), ()))
    scores, values = [], []
    for g in range(N_KV_HEADS):
        ks = slice(g * HEAD_DIM, (g + 1) * HEAD_DIM)
        vs = slice(KV_WIDTH + g * HEAD_DIM, KV_WIDTH + (g + 1) * HEAD_DIM)
        k_ext = jnp.concatenate([kvp[:, ks], kvc[:, ks]], axis=0).astype(BF16)
        values.append(jnp.concatenate([kvp[:, vs], kvc[:, vs]], axis=0).astype(BF16))
        qg = jnp.concatenate([q[:, (g * GQA_GROUP + hh) * HEAD_DIM:(g * GQA_GROUP + hh + 1) * HEAD_DIM]
                              for hh in range(GQA_GROUP)], axis=0).astype(BF16)
        scores.append(lax.dot_general(qg, k_ext, nt, preferred_element_type=F32))
    probs = []
    for g in range(N_KV_HEADS):
        pg = []
        for hh in range(GQA_GROUP):
            s = jnp.where(mask, scores[g][hh * WINDOW:(hh + 1) * WINDOW], -1e30)
            sink = sink_ref[g * GQA_GROUP + hh]
            m = jnp.maximum(jnp.max(s, axis=-1, keepdims=True), sink)
            e = jnp.exp(s - m)
            den = jnp.sum(e, axis=-1, keepdims=True) + jnp.exp(sink - m)
            pg.append((e / den).astype(BF16))
        probs.append(jnp.concatenate(pg, axis=0))
    outs = []
    for g in range(N_KV_HEADS):
        og = jnp.dot(probs[g], values[g], preferred_element_type=F32)
        outs.extend(og[hh * WINDOW:(hh + 1) * WINDOW] for hh in range(GQA_GROUP))
    o_ref[...] = jnp.concatenate(outs, axis=-1)


def _attn_prompt(q, kv, sinks):
    b, t, _ = q.shape
    nb = t // WINDOW
    return pl.pallas_call(
        _attn_prompt_kernel,
        grid=(b, nb),
        in_specs=[
            pl.BlockSpec(memory_space=pltpu.SMEM),
            pl.BlockSpec((None, WINDOW, ATTN_WIDTH), lambda i, j: (i, j, 0)),
            pl.BlockSpec((None, WINDOW, 2 * KV_WIDTH), lambda i, j: (i, jnp.maximum(j - 1, 0), 0)),
            pl.BlockSpec((None, WINDOW, 2 * KV_WIDTH), lambda i, j: (i, j, 0)),
        ],
        out_specs=pl.BlockSpec((None, WINDOW, ATTN_WIDTH), lambda i, j: (i, j, 0)),
        out_shape=jax.ShapeDtypeStruct((b, t, ATTN_WIDTH), F32),
        compiler_params=_cparams(("arbitrary", "arbitrary")),
        name="attn_prompt",
    )(sinks, q, kv, kv)


def _attn_sample_kernel(tn, sink_ref, q_ref, kvn_ref, ck_ref, cv_ref, o_ref, nk_ref, nv_ref):
    rows = GQA_GROUP * tn
    ck = ck_ref[...]
    cv = cv_ref[...]
    kvn = kvn_ref[...]
    row = lax.broadcasted_iota(jnp.int32, (1, rows, 1), 1)
    tok = row % tn
    col = lax.broadcasted_iota(jnp.int32, (1, 1, WINDOW), 2)
    cmask = col >= tok
    for g in range(N_KV_HEADS):
        ks = slice(g * HEAD_DIM, (g + 1) * HEAD_DIM)
        vs = slice(KV_WIDTH + g * HEAD_DIM, KV_WIDTH + (g + 1) * HEAD_DIM)
        qg = q_ref[:, g] * (HEAD_DIM ** -0.5)
        qg_b = qg.astype(BF16).astype(F32)
        sc = jnp.einsum('bqd,bkd->bqk', qg.astype(BF16), ck[:, :, ks].astype(BF16),
                        preferred_element_type=F32)
        sc = jnp.where(cmask, sc, -1e30)
        sink = jnp.zeros((1, rows, 1), F32)
        for hh in range(GQA_GROUP):
            sink = jnp.where(row // tn == hh, sink_ref[g * GQA_GROUP + hh], sink)
        m = jnp.maximum(jnp.max(sc, axis=-1, keepdims=True), sink)
        sn = []
        for jn in range(tn):
            kn = kvn[:, jn:jn + 1, ks].astype(BF16).astype(F32)
            s_j = jnp.sum(qg_b * kn, axis=-1, keepdims=True)
            s_j = jnp.where(tok >= jn, s_j, -1e30)
            sn.append(s_j)
            m = jnp.maximum(m, s_j)
        ec = jnp.exp(sc - m)
        den = jnp.sum(ec, axis=-1, keepdims=True) + jnp.exp(sink - m)
        en = [jnp.exp(s_j - m) for s_j in sn]
        for e_j in en:
            den = den + e_j
        inv = 1.0 / den
        o = jnp.einsum('bqk,bkd->bqd', (ec * inv).astype(BF16), cv[:, :, ks].astype(BF16),
                       preferred_element_type=F32)
        for jn in range(tn):
            vn = kvn[:, jn:jn + 1, vs].astype(BF16).astype(F32)
            o = o + (en[jn] * inv).astype(BF16).astype(F32) * vn
        o_ref[:, g] = o
    nk_ref[:, 0:WINDOW - tn, :] = ck[:, tn:WINDOW, :]
    nk_ref[:, WINDOW - tn:WINDOW, :] = kvn[:, :, 0:KV_WIDTH]
    nv_ref[:, 0:WINDOW - tn, :] = cv[:, tn:WINDOW, :]
    nv_ref[:, WINDOW - tn:WINDOW, :] = kvn[:, :, KV_WIDTH:2 * KV_WIDTH]


def _attn_sample(q, kv, ck, cv, sinks):
    b, tn, _ = q.shape
    bb = 8
    rows = GQA_GROUP * tn
    qh = q.reshape(b, tn, N_KV_HEADS, GQA_GROUP, HEAD_DIM).transpose(0, 2, 3, 1, 4)
    qh = qh.reshape(b, N_KV_HEADS, rows, HEAD_DIM)
    o, nk, nv = pl.pallas_call(
        functools.partial(_attn_sample_kernel, tn),
        grid=(b // bb,),
        in_specs=[
            pl.BlockSpec(memory_space=pltpu.SMEM),
            pl.BlockSpec((bb, N_KV_HEADS, rows, HEAD_DIM), lambda i: (i, 0, 0, 0)),
            pl.BlockSpec((bb, tn, 2 * KV_WIDTH), lambda i: (i, 0, 0)),
            pl.BlockSpec((bb, WINDOW, KV_WIDTH), lambda i: (i, 0, 0)),
            pl.BlockSpec((bb, WINDOW, KV_WIDTH), lambda i: (i, 0, 0)),
        ],
        out_specs=[
            pl.BlockSpec((bb, N_KV_HEADS, rows, HEAD_DIM), lambda i: (i, 0, 0, 0)),
            pl.BlockSpec((bb, WINDOW, KV_WIDTH), lambda i: (i, 0, 0)),
            pl.BlockSpec((bb, WINDOW, KV_WIDTH), lambda i: (i, 0, 0)),
        ],
        out_shape=[
            jax.ShapeDtypeStruct((b, N_KV_HEADS, rows, HEAD_DIM), F32),
            jax.ShapeDtypeStruct((b, WINDOW, KV_WIDTH), F32),
            jax.ShapeDtypeStruct((b, WINDOW, KV_WIDTH), F32),
        ],
        compiler_params=_cparams(("arbitrary",)),
        name="attn_sample",
    )(sinks, qh, kv, ck, cv)
    o = o.reshape(b, N_KV_HEADS, GQA_GROUP, tn, HEAD_DIM).transpose(0, 3, 1, 2, 4)
    return o.reshape(b, tn, ATTN_WIDTH), nk, nv


def _tile_rows(x, n):
    return jnp.concatenate([x] * n, axis=0)


def _rwkv_kernel(c, valid, nseq, x_ref, sh0_ref, st0_ref, mu_ref, lora_ref, gup_ref, vec_ref, ones_ref,
                 o_ref, st_ref, last_ref, state_ref):
    hw = RWKV_WIDTH
    n = RWKV_HEAD
    cw = RWKV_HEADS * c

    @pl.when(pl.program_id(1) == 0)
    def _():
        last_ref[...] = sh0_ref[...]
        state_ref[...] = st0_ref[...]

    row = lax.broadcasted_iota(jnp.int32, (c, 1), 0)
    lane128 = lax.broadcasted_iota(jnp.int32, (c, 128), 1)
    tt = lax.broadcasted_iota(jnp.int32, (c, c), 0)
    ts = lax.broadcasted_iota(jnp.int32, (c, c), 1)
    tri = jnp.where(ts <= tt, 1.0, 0.0).astype(BF16)
    head_rows = (lax.broadcasted_iota(jnp.int32, (cw, hw), 0) // c
                 == lax.broadcasted_iota(jnp.int32, (cw, hw), 1) // n)
    t_i = lax.broadcasted_iota(jnp.int32, (c, cw), 0)
    s_i = lax.broadcasted_iota(jnp.int32, (c, cw), 1) % c
    strict = s_i < t_i
    incl = s_i <= t_i
    eye_all = jnp.where(s_i == t_i, 1.0, 0.0)
    blk = (lax.broadcasted_iota(jnp.int32, (cw, cw), 0) // c
           == lax.broadcasted_iota(jnp.int32, (cw, cw), 1) // c)
    sblk = (lax.broadcasted_iota(jnp.int32, (hw, hw), 0) // n
            == lax.broadcasted_iota(jnp.int32, (hw, hw), 1) // n)
    key_diag = (lax.broadcasted_iota(jnp.int32, (n, hw), 0)
                == lax.broadcasted_iota(jnp.int32, (n, hw), 1) % n)
    masks = (row, lane128, tri, head_rows, strict, incl, eye_all, blk, sblk, key_diag)
    chains = [_rwkv_chunk(c, valid, s, masks, x_ref, mu_ref, lora_ref, gup_ref, vec_ref, ones_ref,
                          o_ref, st_ref, last_ref, state_ref) for s in range(nseq)]
    while chains:
        chains = [ch for ch in chains if next(ch, "done") != "done"]


def _rwkv_chunk(c, valid, s, masks, x_ref, mu_ref, lora_ref, gup_ref, vec_ref, ones_ref,
                o_ref, st_ref, last_ref, state_ref):
    row, lane128, tri, head_rows, strict, incl, eye_all, blk, sblk, key_diag = masks
    hw = RWKV_WIDTH
    n = RWKV_HEAD
    cw = RWKV_HEADS * c
    x = x_ref[s]
    prev = jnp.where(row == 0, last_ref[s], pltpu.roll(x, 1, 0))
    last_ref[s] = x[c - 1:c, :]
    xs = x + (prev - x) * mu_ref[...]

    r = xs[:, 0:hw]
    k = xs[:, hw:2 * hw]
    v = xs[:, 2 * hw:3 * hw]
    wa = xs[:, 3 * hw:3 * hw + 128]
    xg = xs[:, 3 * hw + 128:]
    lora_in = jnp.where(lane128 < 64, jnp.tanh(wa), wa)
    lora = _bdot(lora_in, lora_ref[...])
    w0, a0, k_k, k_a, r_k, ln_w, ln_b = (vec_ref[i:i + 1, :] for i in range(7))
    g = _bdot(_sigmoid(xg), gup_ref[...])
    ones_blk = ones_ref[...]
    kk = k * k_k
    ss = _dot_exact_rhs(kk * kk, ones_blk, passes=2)
    yield
    w_log = -_softplus(-(w0 + lora[:, 0:hw])) - 0.5
    logw = -jnp.exp(w_log)
    a = _sigmoid(a0 + lora[:, hw:2 * hw])
    kk = kk * lax.rsqrt(jnp.maximum(ss, 1e-24))
    k2 = k * (1.0 + (a - 1.0) * k_a)
    bv = kk * a
    if valid < c:
        live = row < valid
        logw = jnp.where(live, logw, 0.0)
        kk = jnp.where(live, kk, 0.0)
        bv = jnp.where(live, bv, 0.0)
        k2 = jnp.where(live, k2, 0.0)
        v = jnp.where(live, v, 0.0)

    pieces = []
    rem = logw
    for _ in range(3):
        p = rem.astype(BF16)
        rem = rem - p.astype(F32)
        pieces.append(jnp.dot(tri, p, preferred_element_type=F32))
    yield
    cum = pieces[0] + (pieces[1] + pieces[2])
    e_inc = jnp.exp(cum)
    e_exc = jnp.exp(cum - logw)
    e_inv = jnp.exp(-cum)
    kq = kk * e_exc
    rq = r * e_inc
    kd = k2 * e_inv
    bd = bv * e_inv
    w_end = e_inc[c - 1:c, :]
    kend = kd * w_end
    bend = bd * w_end

    def expand(z):
        return jnp.where(head_rows, _tile_rows(z, RWKV_HEADS), 0.0)

    lhs = jnp.concatenate([kq, rq], axis=0)
    rhs = jnp.concatenate([expand(bd), expand(kd)], axis=0)
    nt = (((1,), (1,)), ((), ()))
    prod = lax.dot_general(lhs.astype(BF16), rhs.astype(BF16), nt, preferred_element_type=F32)
    st = state_ref[s]
    st_d = jnp.where(sblk, _tile_rows(st, RWKV_HEADS), 0.0)
    kq_st = _bdot(kq, st_d)
    rq_st = _bdot(rq, st_d)
    yield
    a_b = jnp.where(strict, prod[0:c, 0:cw], 0.0)
    a_k = jnp.where(strict, prod[0:c, cw:2 * cw], 0.0)
    p_b = jnp.where(incl, prod[c:2 * c, 0:cw], 0.0)
    p_k = jnp.where(incl, prod[c:2 * c, cw:2 * cw], 0.0)

    def bdiag(z):
        return jnp.where(blk, _tile_rows(z, RWKV_HEADS), 0.0)

    xm = -a_b
    tinv = eye_all + xm
    v_d = expand(v)
    rhs_u = kq_st + _bdot(a_k, v_d)
    y_part = rq_st + _bdot(p_k, v_d)
    span = 1
    while 2 * span < c:
        xm = _bdot(xm, bdiag(xm))
        yield
        tinv = tinv + _bdot(tinv, bdiag(xm))
        span *= 2
    yield
    u = _bdot(tinv, expand(rhs_u))
    yield
    y = y_part - _bdot(p_b, expand(u))

    dm = jnp.where(key_diag, w_end, 0.0)
    lhs_s = jnp.concatenate([kend, -bend, dm], axis=0)
    rhs_s = jnp.concatenate([v, u, st], axis=0)
    tn_dims = (((0,), (0,)), ((), ()))
    gm = lax.dot_general(lhs_s.astype(BF16), rhs_s.astype(BF16), tn_dims, preferred_element_type=F32)
    yield
    gm = jnp.where(sblk, gm, 0.0)
    st_new = gm[0:n] + gm[n:2 * n] + (gm[2 * n:3 * n] + gm[3 * n:4 * n])
    state_ref[s] = st_new
    st_ref[s] = st_new

    mean = _dot_exact_rhs(y, ones_blk, passes=2) * (1.0 / n)
    bonus = _dot_exact_rhs(r * k2 * r_k, ones_blk, passes=2) * v
    yield
    yc = y - mean
    var = _dot_exact_rhs(yc * yc, ones_blk, passes=2) * (1.0 / n)
    yield
    yn = yc * lax.rsqrt(var + GN_EPS) * ln_w + ln_b
    o_ref[s] = (yn + bonus) * g


def _rwkv(x, sh0, st0, mu, lora_w, g_up, vecs, ones_blk, valid):
    b, t, _ = x.shape
    c = min(RWKV_CHUNK, t)
    nseq = min(RWKV_SEQS if t > c else RWKV_SHORT_SEQS, b)
    const = lambda s: pl.BlockSpec(s, lambda i, j: (0,) * len(s))
    return pl.pallas_call(
        functools.partial(_rwkv_kernel, c, valid if valid < c else c, nseq),
        grid=(b // nseq, t // c),
        in_specs=[
            pl.BlockSpec((nseq, c, RWKV_PROJ), lambda i, j: (i, j, 0)),
            pl.BlockSpec((nseq, 1, RWKV_PROJ), lambda i, j: (i, 0, 0)),
            pl.BlockSpec((nseq, RWKV_HEAD, RWKV_WIDTH), lambda i, j: (i, 0, 0)),
            const((1, RWKV_PROJ)), const((128, 512)), const((128, RWKV_WIDTH)),
            const((8, RWKV_WIDTH)), const((RWKV_WIDTH, RWKV_WIDTH)),
        ],
        out_specs=[
            pl.BlockSpec((nseq, c, RWKV_WIDTH), lambda i, j: (i, j, 0)),
            pl.BlockSpec((nseq, RWKV_HEAD, RWKV_WIDTH), lambda i, j: (i, 0, 0)),
        ],
        out_shape=[
            jax.ShapeDtypeStruct((b, t, RWKV_WIDTH), F32),
            jax.ShapeDtypeStruct((b, RWKV_HEAD, RWKV_WIDTH), F32),
        ],
        scratch_shapes=[pltpu.VMEM((nseq, 1, RWKV_PROJ), F32), pltpu.VMEM((nseq, RWKV_HEAD, RWKV_WIDTH), F32)],
        compiler_params=_cparams(("arbitrary", "arbitrary")),
        name="rwkv",
    )(x, sh0, st0, mu, lora_w, g_up, vecs, ones_blk)


def _lru_kernel(tc, valid, x_ref, cv0_ref, h0_ref, cw_ref, wg_ref, vec_ref, o_ref, h_ref, ext_ref, hc_ref):
    ti = pl.program_id(1)
    w = LRU_WIDTH

    @pl.when(ti == 0)
    def _():
        ext_ref[0:8, :] = cv0_ref[...]
        hc_ref[...] = h0_ref[...]

    xb = x_ref[:, 0:w]
    gb = x_ref[:, w:2 * w]
    ext_ref[8:8 + tc, :] = xb
    xc = vec_ref[0:1, :] + xb * cw_ref[CONV_W - 1:CONV_W, :]
    for j in range(CONV_W - 1):
        xc = xc + ext_ref[pl.ds(8 - (CONV_W - 1) + j, tc), :] * cw_ref[j:j + 1, :]
    ext_ref[0:8, :] = xb[tc - 8:tc, :]

    row = lax.broadcasted_iota(jnp.int32, (tc, 1), 0)
    a, u = _lru_scan(xc, row, tc, valid, wg_ref, vec_ref)
    h = a * hc_ref[...] + u
    hc_ref[...] = h[tc - 1:tc, :]
    h_ref[...] = h[tc - 1:tc, :]
    o_ref[...] = h * _gelu_tanh(gb)


def _gelu_tanh(x):
    return 0.5 * x * (1.0 + jnp.tanh(0.7978845608028654 * (x + 0.044715 * (x * x * x))))


def _lru_scan(xc, t, period, valid, wg_ref, vec_ref):
    w = LRU_WIDTH
    _, b_a, b_i, lam = (vec_ref[i:i + 1, :] for i in range(4))
    gates = _bdot(xc, wg_ref[...])
    r = _sigmoid(gates[:, 0:w] + b_a)
    i = _sigmoid(gates[:, w:2 * w] + b_i)
    log_a = LRU_C * r * (-_softplus(-lam))
    a = jnp.exp(log_a)
    th = jnp.tanh(log_a)
    u = jnp.sqrt(-2.0 * th / (1.0 - th)) * (i * xc)
    if valid < period:
        live = t < valid
        a = jnp.where(live, a, 1.0)
        u = jnp.where(live, u, 0.0)
    span = 1
    while span < period:
        ok = t >= span
        a_s = pltpu.roll(a, span, 0)
        u_s = pltpu.roll(u, span, 0)
        u = jnp.where(ok, a * u_s + u, u)
        a = jnp.where(ok, a * a_s, a)
        span *= 2
    return a, u


def _lru_short_kernel(valid, x_ref, cv0_ref, h0_ref, cw_ref, wg_ref, vec_ref, o_ref, h_ref):
    nb = x_ref.shape[0]
    w = LRU_WIDTH
    rows = nb * SAMPLE_PAD
    x = x_ref[...].reshape(rows, 2 * w)
    xb = x[:, 0:w]
    gb = x[:, w:2 * w]
    cv0 = cv0_ref[...].reshape(rows, w)
    t = lax.broadcasted_iota(jnp.int32, (rows, 1), 0) % SAMPLE_PAD
    xc = vec_ref[0:1, :] + xb * cw_ref[CONV_W - 1:CONV_W, :]
    for j in range(CONV_W - 1):
        back = CONV_W - 1 - j
        prev = jnp.where(t >= back, pltpu.roll(xb, back, 0),
                         pltpu.roll(cv0, (back - SAMPLE_PAD) % rows, 0))
        xc = xc + prev * cw_ref[j:j + 1, :]
    a, u = _lru_scan(xc, t, SAMPLE_PAD, valid, wg_ref, vec_ref)
    h0 = jnp.broadcast_to(h0_ref[...], (nb, SAMPLE_PAD, w)).reshape(rows, w)
    h = a * h0 + u
    o_ref[...] = (h * _gelu_tanh(gb)).reshape(nb, SAMPLE_PAD, w)
    h_ref[...] = h.reshape(nb, SAMPLE_PAD, w)[:, SAMPLE_PAD - 1:SAMPLE_PAD, :]


def _lru(x, cv0, h0, conv_w, w_gates, vecs, valid):
    b, t, _ = x.shape
    if t == SAMPLE_PAD:
        nb = min(LRU_SHORT_SEQS, b)
        const1 = lambda s: pl.BlockSpec(s, lambda i: (0,) * len(s))
        seq = lambda r, w_: pl.BlockSpec((nb, r, w_), lambda i: (i, 0, 0))
        return pl.pallas_call(
            functools.partial(_lru_short_kernel, valid),
            grid=(b // nb,),
            in_specs=[seq(SAMPLE_PAD, 2 * LRU_WIDTH), seq(8, LRU_WIDTH), seq(1, LRU_WIDTH),
                      const1((CONV_W, LRU_WIDTH)), const1((LRU_WIDTH, 2 * LRU_WIDTH)), const1((4, LRU_WIDTH))],
            out_specs=[seq(SAMPLE_PAD, LRU_WIDTH), seq(1, LRU_WIDTH)],
            out_shape=[jax.ShapeDtypeStruct((b, t, LRU_WIDTH), F32), jax.ShapeDtypeStruct((b, 1, LRU_WIDTH), F32)],
            compiler_params=_cparams(("arbitrary",)),
            name="lru_short",
        )(x, cv0, h0, conv_w, w_gates, vecs)
    tc = min(LRU_TILE, t)
    const = lambda s: pl.BlockSpec(s, lambda i, j: (0,) * len(s))
    return pl.pallas_call(
        functools.partial(_lru_kernel, tc, valid if valid < tc else tc),
        grid=(b, t // tc),
        in_specs=[
            pl.BlockSpec((None, tc, 2 * LRU_WIDTH), lambda i, j: (i, j, 0)),
            pl.BlockSpec((None, 8, LRU_WIDTH), lambda i, j: (i, 0, 0)),
            pl.BlockSpec((None, 1, LRU_WIDTH), lambda i, j: (i, 0, 0)),
            const((CONV_W, LRU_WIDTH)), const((LRU_WIDTH, 2 * LRU_WIDTH)), const((4, LRU_WIDTH)),
        ],
        out_specs=[
            pl.BlockSpec((None, tc, LRU_WIDTH), lambda i, j: (i, j, 0)),
            pl.BlockSpec((None, 1, LRU_WIDTH), lambda i, j: (i, 0, 0)),
        ],
        out_shape=[
            jax.ShapeDtypeStruct((b, t, LRU_WIDTH), F32),
            jax.ShapeDtypeStruct((b, 1, LRU_WIDTH), F32),
        ],
        scratch_shapes=[pltpu.VMEM((tc + 8, LRU_WIDTH), F32), pltpu.VMEM((1, LRU_WIDTH), F32)],
        compiler_params=_cparams(("arbitrary", "arbitrary")),
        name="lru",
    )(x, cv0, h0, conv_w, w_gates, vecs)


def _post_kernel(x_ref, oa_ref, ob_ref, oc_ref, p_ref, wo_ref, wg_ref, wu_ref, wd_ref, pg_ref, pw_ref,
                 vec_ref, y_ref):
    mix = (jnp.dot(oa_ref[...].astype(BF16), wo_ref[0:512, :], preferred_element_type=F32)
           + jnp.dot(ob_ref[...].astype(BF16), wo_ref[512:768, :], preferred_element_type=F32)
           + jnp.dot(oc_ref[...].astype(BF16), wo_ref[768:1024, :], preferred_element_type=F32)
           + vec_ref[0:1, :])
    x = x_ref[...] + _rmsnorm(mix, vec_ref[1:2, :])
    f = _rmsnorm(x, vec_ref[2:3, :]).astype(BF16)
    acc = None
    for lo in range(0, D_FF, FF_CHUNK):
        gate = jnp.dot(f, wg_ref[:, lo:lo + FF_CHUNK], preferred_element_type=F32)
        up = jnp.dot(f, wu_ref[:, lo:lo + FF_CHUNK], preferred_element_type=F32)
        hid = (gate * _sigmoid(gate) * up).astype(BF16)
        part = jnp.dot(hid, wd_ref[lo:lo + FF_CHUNK, :], preferred_element_type=F32)
        acc = part if acc is None else acc + part
    x2 = x + _rmsnorm(acc, vec_ref[3:4, :])
    gate = _sigmoid(jnp.dot(x2.astype(BF16), pg_ref[...], preferred_element_type=F32))
    emb = jnp.dot(p_ref[...].astype(BF16), pw_ref[...], preferred_element_type=F32)
    y_ref[...] = x2 + gate * emb


def _post(x, oa, ob, oc, p, wo, wg, wu, wd, pg, pw, vecs):
    m = x.shape[0]
    tm = min(ROW_TILE, m)
    row = lambda w_: pl.BlockSpec((tm, w_), lambda i: (i, 0))
    const = lambda s: pl.BlockSpec(s, lambda i: (0, 0), pipeline_mode=pl.Buffered(1))
    return pl.pallas_call(
        _post_kernel,
        grid=(m // tm,),
        in_specs=[row(D_MODEL), row(512), row(256), row(256), row(PLE_DIM),
                  const((D_MODEL, D_MODEL)), const((D_MODEL, D_FF)), const((D_MODEL, D_FF)),
                  const((D_FF, D_MODEL)), const((D_MODEL, D_MODEL)), const((PLE_DIM, D_MODEL)),
                  const((4, D_MODEL))],
        out_specs=row(D_MODEL),
        out_shape=jax.ShapeDtypeStruct((m, D_MODEL), F32),
        compiler_params=_cparams(("arbitrary",)),
        name="post",
    )(x, oa, ob, oc, p, wo, wg, wu, wd, pg, pw, vecs)


def _block_diag(w):
    nb, n, _ = w.shape
    eye = jnp.eye(nb, dtype=w.dtype)
    return (eye[:, None, :, None] * w[:, :, None, :]).reshape(nb * n, nb * n)


def _layer(x, p, state, lw, sample):
    b, t, _ = x.shape
    m = b * t
    x2 = x.reshape(m, D_MODEL)
    q, kv, rw, lr = _in_proj(x2, lw['norm_mix_pre'], lw['w_in'], lw['b_in'])
    q = q.reshape(b, t, 512)
    kv = kv.reshape(b, t, 256)
    rw = rw.reshape(b, t, RWKV_PROJ)
    lr = lr.reshape(b, t, 2 * LRU_WIDTH)
    ck, cv, sh0, wkv0, conv0, h0 = state

    if sample:
        o_a, nk, nv = _attn_sample(q, kv, ck.reshape(b, WINDOW, KV_WIDTH), cv.reshape(b, WINDOW, KV_WIDTH),
                                   lw['attn_sinks'])
        nk = nk.reshape(b, WINDOW, N_KV_HEADS, HEAD_DIM)
        nv = nv.reshape(b, WINDOW, N_KV_HEADS, HEAD_DIM)
    else:
        o_a = _attn_prompt(q, kv, lw['attn_sinks'])
        nk = kv[:, t - WINDOW:, 0:KV_WIDTH].reshape(b, WINDOW, N_KV_HEADS, HEAD_DIM)
        nv = kv[:, t - WINDOW:, KV_WIDTH:].reshape(b, WINDOW, N_KV_HEADS, HEAD_DIM)

    tp = t if t % SAMPLE_PAD == 0 else SAMPLE_PAD * (-(-t // SAMPLE_PAD))
    pad = lambda z: z if tp == t else jnp.pad(z, ((0, 0), (0, tp - t), (0, 0)))

    st0 = wkv0.transpose(0, 3, 1, 2).reshape(b, RWKV_HEAD, RWKV_WIDTH)
    o_b, st = _rwkv(pad(rw), sh0.reshape(b, 1, RWKV_PROJ), st0, lw['rwkv_mu'], lw['rwkv_lora'],
                    lw['rwkv_g_up'], lw['rwkv_vecs'], lw['ones_blk'], t)
    nwkv = st.reshape(b, RWKV_HEAD, RWKV_HEADS, RWKV_HEAD).transpose(0, 2, 3, 1)
    nsh = rw[:, t - 1, :]

    cv0 = jnp.pad(conv0, ((0, 0), (8 - (CONV_W - 1), 0), (0, 0)))
    o_c, nh = _lru(pad(lr), cv0, h0.reshape(b, 1, LRU_WIDTH), lw['lru_conv_w'], lw['lru_w_gates'],
                   lw['lru_vecs'], t)
    nconv = lr[:, t - (CONV_W - 1):, 0:LRU_WIDTH]
    nh = nh.reshape(b, LRU_WIDTH)

    x2 = _post(x2, o_a.reshape(m, 512), o_b[:, :t].reshape(m, 256), o_c[:, :t].reshape(m, 256),
               p.reshape(m, PLE_DIM), lw['w_out'], lw['ffn_w_gate'], lw['ffn_w_up'], lw['ffn_w_down'],
               lw['ple_gate_w'], lw['ple_w'], lw['post_vecs'])
    return x2.reshape(b, t, D_MODEL), (nk, nv, nsh, nwkv, nconv, nh)


def kernel(x_prompt, x_sample, cache_k, cache_v, state_shift, state_wkv, state_conv, state_lru,
           p_prompt, p_sample, norm_mix_pre, norm_mix_post, norm_ffn_pre, norm_ffn_post,
           w_in, b_in, attn_sinks, rwkv_mu, rwkv_w0, rwkv_w_up, rwkv_a0, rwkv_a_up, rwkv_g_up,
           rwkv_k_k, rwkv_k_a, rwkv_r_k, rwkv_ln_w, rwkv_ln_b, lru_conv_w, lru_conv_b,
           lru_w_a, lru_b_a, lru_w_i, lru_b_i, lru_L, w_out, b_out, ffn_w_gate, ffn_w_up,
           ffn_w_down, ple_w, ple_gate_w):
    bp = x_prompt.shape[0]
    head_id = jnp.arange(RWKV_WIDTH) // RWKV_HEAD
    ones_blk = (head_id[:, None] == head_id[None, :]).astype(BF16)
    xp, xs = x_prompt, x_sample
    new_p, new_s = [], []
    for i in range(DEPTH):
        zeros_w = jnp.zeros((64, RWKV_WIDTH), F32)
        lora = jnp.concatenate([jnp.concatenate([rwkv_w_up[i], zeros_w], axis=1),
                                jnp.concatenate([zeros_w, rwkv_a_up[i]], axis=1)], axis=0)
        lw = dict(
            norm_mix_pre=norm_mix_pre[i][None], w_in=w_in[i].astype(BF16), b_in=b_in[i][None],
            attn_sinks=attn_sinks[i],
            rwkv_mu=rwkv_mu[i][None], rwkv_lora=lora.astype(BF16), rwkv_g_up=rwkv_g_up[i].astype(BF16),
            rwkv_vecs=jnp.stack([rwkv_w0[i], rwkv_a0[i], rwkv_k_k[i], rwkv_k_a[i],
                                 rwkv_r_k[i].reshape(RWKV_WIDTH), rwkv_ln_w[i], rwkv_ln_b[i],
                                 jnp.zeros((RWKV_WIDTH,), F32)]),
            ones_blk=ones_blk,
            lru_conv_w=lru_conv_w[i],
            lru_w_gates=jnp.concatenate([_block_diag(lru_w_a[i]), _block_diag(lru_w_i[i])], axis=1).astype(BF16),
            lru_vecs=jnp.stack([lru_conv_b[i], lru_b_a[i], lru_b_i[i], lru_L[i]]),
            w_out=w_out[i].astype(BF16),
            ffn_w_gate=ffn_w_gate[i].astype(BF16), ffn_w_up=ffn_w_up[i].astype(BF16),
            ffn_w_down=ffn_w_down[i].astype(BF16), ple_gate_w=ple_gate_w[i].astype(BF16),
            ple_w=ple_w[i].astype(BF16),
            post_vecs=jnp.stack([b_out[i], norm_mix_post[i], norm_ffn_pre[i], norm_ffn_post[i]]),
        )
        st_p = (None, None,
                jnp.zeros((bp, RWKV_PROJ), F32),
                jnp.zeros((bp, RWKV_HEADS, RWKV_HEAD, RWKV_HEAD), F32),
                jnp.zeros((bp, CONV_W - 1, LRU_WIDTH), F32),
                jnp.zeros((bp, LRU_WIDTH), F32))
        xp, sp = _layer(xp, p_prompt[i], st_p, lw, False)
        st_s = (cache_k[i], cache_v[i], state_shift[i], state_wkv[i], state_conv[i], state_lru[i])
        xs, ss = _layer(xs, p_sample[i], st_s, lw, True)
        new_p.append(sp)
        new_s.append(ss)

    def stk(lst, j):
        return jnp.stack([s[j] for s in lst], axis=0)

    return (xp, xs,
            stk(new_p, 0), stk(new_p, 1), stk(new_p, 2), stk(new_p, 3), stk(new_p, 4), stk(new_p, 5),
            stk(new_s, 0), stk(new_s, 1), stk(new_s, 2), stk(new_s, 3), stk(new_s, 4), stk(new_s, 5))
```

```python
import functools

import jax
import jax.numpy as jnp
from jax import lax
from jax.experimental import pallas as pl
from jax.experimental.pallas import tpu as pltpu

F32 = jnp.float32
BF16 = jnp.bfloat16

D_MODEL = 1024
DEPTH = 4
HEAD_DIM = 64
ATTN_WIDTH = 512
N_HEADS = 8
N_KV_HEADS = 2
GQA_GROUP = 4
KV_WIDTH = 128
WINDOW = 128
RWKV_WIDTH = 256
RWKV_HEADS = 4
RWKV_HEAD = 64
RWKV_PROJ = 1024
LRU_WIDTH = 256
CONV_W = 4
LRU_C = 8.0
D_FF = 2816
PLE_DIM = 256
RMS_EPS = 1e-6
GN_EPS = 64e-5
IN_COLS = 2304

ROW_TILE = 512
FF_CHUNK = 1408
RWKV_CHUNK = 64
RWKV_SEQS = 4
RWKV_SHORT_SEQS = 8
LRU_TILE = 512
LRU_SHORT_SEQS = 32
ATTN_SAMPLE_SEQS = 8
SAMPLE_PAD = 8
VMEM_LIMIT = 56 * 1024 * 1024


def _cparams(sem):
    return pltpu.CompilerParams(dimension_semantics=sem, vmem_limit_bytes=VMEM_LIMIT)


def _layer_spec(shape, layer, nidx, **kw):
    zeros = (0,) * len(shape)
    if nidx == 1:
        return pl.BlockSpec((None,) + tuple(shape), lambda i: (layer,) + zeros, **kw)
    return pl.BlockSpec((None,) + tuple(shape), lambda i, j: (layer,) + zeros, **kw)


def _bdot(a, b):
    return jnp.dot(a.astype(BF16), b.astype(BF16), preferred_element_type=F32)


def _hi_lo_rows(x):
    hi = x.astype(BF16)
    lo = (x - hi.astype(F32)).astype(BF16)
    return jnp.concatenate([hi, lo], axis=0)


def _block_sums(x, ones_blk):
    rows = x.shape[0]
    res = jnp.dot(_hi_lo_rows(x), ones_blk, preferred_element_type=F32)
    return res[0:rows] + res[rows:2 * rows]


def _rmsnorm(x, g):
    ms = jnp.mean(x * x, axis=-1, keepdims=True)
    return x * lax.rsqrt(ms + RMS_EPS) * g


def _softplus(x):
    return jnp.maximum(x, 0.0) + jnp.log1p(jnp.exp(-jnp.abs(x)))


def _sigmoid(x):
    return 1.0 / (1.0 + jnp.exp(-x))


def _gelu_tanh(x):
    return 0.5 * x * (1.0 + jnp.tanh(0.7978845608028654 * (x + 0.044715 * (x * x * x))))


def _in_kernel(x_ref, g_ref, w_ref, b_ref, q_ref, kv_ref, rw_ref, lr_ref):
    h = _rmsnorm(x_ref[...], g_ref[...]).astype(BF16)
    for ref, lo, hi in ((q_ref, 0, 512), (kv_ref, 512, 768), (rw_ref, 768, 1792), (lr_ref, 1792, 2304)):
        ref[...] = jnp.dot(h, w_ref[:, lo:hi], preferred_element_type=F32) + b_ref[:, lo:hi]


def _in_proj(x, g, w, b, layer):
    m = x.shape[0]
    tm = min(ROW_TILE, m)
    row = lambda w_: pl.BlockSpec((tm, w_), lambda i: (i, 0))
    return pl.pallas_call(
        _in_kernel,
        grid=(m // tm,),
        in_specs=[row(D_MODEL), _layer_spec((1, D_MODEL), layer, 1),
                  _layer_spec((D_MODEL, IN_COLS), layer, 1), _layer_spec((1, IN_COLS), layer, 1)],
        out_specs=[row(512), row(256), row(1024), row(512)],
        out_shape=[jax.ShapeDtypeStruct((m, n), F32) for n in (512, 256, 1024, 512)],
        compiler_params=_cparams(("arbitrary",)),
        name="in_proj",
    )(x, g, w, b)


def _attn_prompt_kernel(layer, sink_ref, q_ref, kvp_ref, kvc_ref, o_ref):
    j = pl.program_id(1)
    q = q_ref[...] * (HEAD_DIM ** -0.5)
    kvp = kvp_ref[...]
    kvc = kvc_ref[...]
    qi = lax.broadcasted_iota(jnp.int32, (WINDOW, 2 * WINDOW), 0) + WINDOW
    kj = lax.broadcasted_iota(jnp.int32, (WINDOW, 2 * WINDOW), 1)
    d = qi - kj
    mask = (d >= 0) & (d <= WINDOW) & ((j > 0) | (kj >= WINDOW))
    nt = (((1,), (1,)), ((), ()))
    scores, values = [], []
    for g in range(N_KV_HEADS):
        ks = slice(g * HEAD_DIM, (g + 1) * HEAD_DIM)
        vs = slice(KV_WIDTH + g * HEAD_DIM, KV_WIDTH + (g + 1) * HEAD_DIM)
        k_ext = jnp.concatenate([kvp[:, ks], kvc[:, ks]], axis=0).astype(BF16)
        values.append(jnp.concatenate([kvp[:, vs], kvc[:, vs]], axis=0).astype(BF16))
        qg = jnp.concatenate([q[:, (g * GQA_GROUP + hh) * HEAD_DIM:(g * GQA_GROUP + hh + 1) * HEAD_DIM]
                              for hh in range(GQA_GROUP)], axis=0).astype(BF16)
        scores.append(lax.dot_general(qg, k_ext, nt, preferred_element_type=F32))
    probs = []
    for g in range(N_KV_HEADS):
        pg = []
        for hh in range(GQA_GROUP):
            s = jnp.where(mask, scores[g][hh * WINDOW:(hh + 1) * WINDOW], -1e30)
            sink = sink_ref[layer, g * GQA_GROUP + hh]
            m = jnp.maximum(jnp.max(s, axis=-1, keepdims=True), sink)
            e = jnp.exp(s - m)
            den = jnp.sum(e, axis=-1, keepdims=True) + jnp.exp(sink - m)
            pg.append((e / den).astype(BF16))
        probs.append(jnp.concatenate(pg, axis=0))
    outs = []
    for g in range(N_KV_HEADS):
        og = jnp.dot(probs[g], values[g], preferred_element_type=F32)
        outs.extend(og[hh * WINDOW:(hh + 1) * WINDOW] for hh in range(GQA_GROUP))
    o_ref[...] = jnp.concatenate(outs, axis=-1)


def _attn_prompt(q, kv, sinks, layer):
    b, t, _ = q.shape
    nb = t // WINDOW
    return pl.pallas_call(
        functools.partial(_attn_prompt_kernel, layer),
        grid=(b, nb),
        in_specs=[
            pl.BlockSpec(memory_space=pltpu.SMEM),
            pl.BlockSpec((None, WINDOW, ATTN_WIDTH), lambda i, j: (i, j, 0)),
            pl.BlockSpec((None, WINDOW, 2 * KV_WIDTH), lambda i, j: (i, jnp.maximum(j - 1, 0), 0)),
            pl.BlockSpec((None, WINDOW, 2 * KV_WIDTH), lambda i, j: (i, j, 0)),
        ],
        out_specs=pl.BlockSpec((None, WINDOW, ATTN_WIDTH), lambda i, j: (i, j, 0)),
        out_shape=jax.ShapeDtypeStruct((b, t, ATTN_WIDTH), F32),
        compiler_params=_cparams(("arbitrary", "arbitrary")),
        name="attn_prompt",
    )(sinks, q, kv, kv)


def _attn_sample_kernel(layer, tn, sink_ref, q_ref, kvn_ref, ck_ref, cv_ref, o_ref, nk_ref, nv_ref):
    rows = GQA_GROUP * tn
    ck = ck_ref[...]
    cv = cv_ref[...]
    kvn = kvn_ref[...]
    row = lax.broadcasted_iota(jnp.int32, (1, rows, 1), 1)
    tok = row % tn
    col = lax.broadcasted_iota(jnp.int32, (1, 1, WINDOW), 2)
    cmask = col >= tok
    for g in range(N_KV_HEADS):
        ks = slice(g * HEAD_DIM, (g + 1) * HEAD_DIM)
        vs = slice(KV_WIDTH + g * HEAD_DIM, KV_WIDTH + (g + 1) * HEAD_DIM)
        qg = q_ref[:, g] * (HEAD_DIM ** -0.5)
        qg_b = qg.astype(BF16).astype(F32)
        sc = jnp.einsum('bqd,bkd->bqk', qg.astype(BF16), ck[:, :, ks].astype(BF16),
                        preferred_element_type=F32)
        sc = jnp.where(cmask, sc, -1e30)
        sink = jnp.zeros((1, rows, 1), F32)
        for hh in range(GQA_GROUP):
            sink = jnp.where(row // tn == hh, sink_ref[layer, g * GQA_GROUP + hh], sink)
        m = jnp.maximum(jnp.max(sc, axis=-1, keepdims=True), sink)
        sn = []
        for jn in range(tn):
            kn = kvn[:, jn:jn + 1, ks].astype(BF16).astype(F32)
            s_j = jnp.sum(qg_b * kn, axis=-1, keepdims=True)
            s_j = jnp.where(tok >= jn, s_j, -1e30)
            sn.append(s_j)
            m = jnp.maximum(m, s_j)
        ec = jnp.exp(sc - m)
        den = jnp.sum(ec, axis=-1, keepdims=True) + jnp.exp(sink - m)
        en = [jnp.exp(s_j - m) for s_j in sn]
        for e_j in en:
            den = den + e_j
        inv = 1.0 / den
        o = jnp.einsum('bqk,bkd->bqd', (ec * inv).astype(BF16), cv[:, :, ks].astype(BF16),
                       preferred_element_type=F32)
        for jn in range(tn):
            vn = kvn[:, jn:jn + 1, vs].astype(BF16).astype(F32)
            o = o + (en[jn] * inv).astype(BF16).astype(F32) * vn
        o_ref[:, g] = o
    nk_ref[:, 0:WINDOW - tn, :] = ck[:, tn:WINDOW, :]
    nk_ref[:, WINDOW - tn:WINDOW, :] = kvn[:, :, 0:KV_WIDTH]
    nv_ref[:, 0:WINDOW - tn, :] = cv[:, tn:WINDOW, :]
    nv_ref[:, WINDOW - tn:WINDOW, :] = kvn[:, :, KV_WIDTH:2 * KV_WIDTH]


def _attn_sample(q, kv, ck, cv, sinks, layer):
    b, tn, _ = q.shape
    bb = ATTN_SAMPLE_SEQS
    rows = GQA_GROUP * tn
    qh = q.reshape(b, tn, N_KV_HEADS, GQA_GROUP, HEAD_DIM).transpose(0, 2, 3, 1, 4)
    qh = qh.reshape(b, N_KV_HEADS, rows, HEAD_DIM)
    cache = pl.BlockSpec((None, bb, WINDOW, KV_WIDTH), lambda i: (layer, i, 0, 0))
    o, nk, nv = pl.pallas_call(
        functools.partial(_attn_sample_kernel, layer, tn),
        grid=(b // bb,),
        in_specs=[
            pl.BlockSpec(memory_space=pltpu.SMEM),
            pl.BlockSpec((bb, N_KV_HEADS, rows, HEAD_DIM), lambda i: (i, 0, 0, 0)),
            pl.BlockSpec((bb, tn, 2 * KV_WIDTH), lambda i: (i, 0, 0)),
            cache, cache,
        ],
        out_specs=[
            pl.BlockSpec((bb, N_KV_HEADS, rows, HEAD_DIM), lambda i: (i, 0, 0, 0)),
            pl.BlockSpec((bb, WINDOW, KV_WIDTH), lambda i: (i, 0, 0)),
            pl.BlockSpec((bb, WINDOW, KV_WIDTH), lambda i: (i, 0, 0)),
        ],
        out_shape=[
            jax.ShapeDtypeStruct((b, N_KV_HEADS, rows, HEAD_DIM), F32),
            jax.ShapeDtypeStruct((b, WINDOW, KV_WIDTH), F32),
            jax.ShapeDtypeStruct((b, WINDOW, KV_WIDTH), F32),
        ],
        compiler_params=_cparams(("arbitrary",)),
        name="attn_sample",
    )(sinks, qh, kv, ck, cv)
    o = o.reshape(b, N_KV_HEADS, GQA_GROUP, tn, HEAD_DIM).transpose(0, 3, 1, 2, 4)
    return o.reshape(b, tn, ATTN_WIDTH), nk, nv


def _tile_rows(x, n):
    return jnp.concatenate([x] * n, axis=0)


def _rwkv_kernel(c, valid, nseq, x_ref, sh0_ref, st0_ref, mu_ref, lora_ref, gup_ref, vec_ref, ones_ref,
                 o_ref, st_ref, last_ref, state_ref):
    hw = RWKV_WIDTH
    n = RWKV_HEAD
    cw = RWKV_HEADS * c

    @pl.when(pl.program_id(1) == 0)
    def _():
        last_ref[...] = sh0_ref[...]
        state_ref[...] = st0_ref[...]

    row = lax.broadcasted_iota(jnp.int32, (c, 1), 0)
    lane128 = lax.broadcasted_iota(jnp.int32, (c, 128), 1)
    tri = jnp.where(lax.broadcasted_iota(jnp.int32, (c, c), 1) <= lax.broadcasted_iota(jnp.int32, (c, c), 0),
                    1.0, 0.0).astype(BF16)
    tri3 = jnp.concatenate([tri, tri, tri], axis=1)
    head_rows = (lax.broadcasted_iota(jnp.int32, (cw, hw), 0) // c
                 == lax.broadcasted_iota(jnp.int32, (cw, hw), 1) // n)
    t_i = lax.broadcasted_iota(jnp.int32, (c, cw), 0)
    s_i = lax.broadcasted_iota(jnp.int32, (c, cw), 1) % c
    strict = s_i < t_i
    incl = s_i <= t_i
    eye_all = jnp.where(s_i == t_i, 1.0, 0.0)
    blk = (lax.broadcasted_iota(jnp.int32, (cw, cw), 0) // c
           == lax.broadcasted_iota(jnp.int32, (cw, cw), 1) // c)
    sblk = (lax.broadcasted_iota(jnp.int32, (hw, hw), 0) // n
            == lax.broadcasted_iota(jnp.int32, (hw, hw), 1) // n)
    key_diag = (lax.broadcasted_iota(jnp.int32, (n, hw), 0)
                == lax.broadcasted_iota(jnp.int32, (n, hw), 1) % n)
    masks = (row, lane128, tri3, head_rows, strict, incl, eye_all, blk, sblk, key_diag)
    chains = [_rwkv_chunk(c, valid, s, masks, x_ref, mu_ref, lora_ref, gup_ref, vec_ref, ones_ref,
                          o_ref, st_ref, last_ref, state_ref) for s in range(nseq)]
    while chains:
        chains = [ch for ch in chains if next(ch, "done") != "done"]


def _rwkv_chunk(c, valid, s, masks, x_ref, mu_ref, lora_ref, gup_ref, vec_ref, ones_ref,
                o_ref, st_ref, last_ref, state_ref):
    row, lane128, tri3, head_rows, strict, incl, eye_all, blk, sblk, key_diag = masks
    hw = RWKV_WIDTH
    n = RWKV_HEAD
    cw = RWKV_HEADS * c
    x = x_ref[s]
    prev = jnp.where(row == 0, last_ref[s], pltpu.roll(x, 1, 0))
    last_ref[s] = x[c - 1:c, :]
    xs = x + (prev - x) * mu_ref[...]

    r = xs[:, 0:hw]
    k = xs[:, hw:2 * hw]
    v = xs[:, 2 * hw:3 * hw]
    wa = xs[:, 3 * hw:3 * hw + 128]
    xg = xs[:, 3 * hw + 128:]
    lora_in = jnp.where(lane128 < 64, jnp.tanh(wa), wa)
    lora = _bdot(lora_in, lora_ref[...])
    w0, a0, k_k, k_a, r_k, ln_w, ln_b = (vec_ref[i:i + 1, :] for i in range(7))
    g = _bdot(_sigmoid(xg), gup_ref[...])
    ones_blk = ones_ref[...]
    kk = k * k_k
    ss = _block_sums(kk * kk, ones_blk)
    yield
    w_log = -_softplus(-(w0 + lora[:, 0:hw])) - 0.5
    logw = -jnp.exp(w_log)
    a = _sigmoid(a0 + lora[:, hw:2 * hw])
    kk = kk * lax.rsqrt(jnp.maximum(ss, 1e-24))
    k2 = k * (1.0 + (a - 1.0) * k_a)
    bv = kk * a
    if valid < c:
        live = row < valid
        logw = jnp.where(live, logw, 0.0)
        kk = jnp.where(live, kk, 0.0)
        bv = jnp.where(live, bv, 0.0)
        k2 = jnp.where(live, k2, 0.0)
        v = jnp.where(live, v, 0.0)

    pieces = []
    rem = logw
    for _ in range(3):
        p = rem.astype(BF16)
        rem = rem - p.astype(F32)
        pieces.append(p)
    cum = jnp.dot(tri3, jnp.concatenate(pieces, axis=0), preferred_element_type=F32)
    bonus_sum = _block_sums(r * k2 * r_k, ones_blk)
    yield
    e_inc = jnp.exp(cum)
    e_exc = jnp.exp(cum - logw)
    e_inv = jnp.exp(-cum)
    kq = kk * e_exc
    rq = r * e_inc
    kd = k2 * e_inv
    bd = bv * e_inv
    w_end = e_inc[c - 1:c, :]
    kend = kd * w_end
    bend = bd * w_end

    def expand(z):
        return jnp.where(head_rows, _tile_rows(z, RWKV_HEADS), 0.0)

    lhs = jnp.concatenate([kq, rq], axis=0).astype(BF16)
    rhs = jnp.concatenate([expand(bd), expand(kd)], axis=0).astype(BF16)
    nt = (((1,), (1,)), ((), ()))
    prod = lax.dot_general(lhs, rhs, nt, preferred_element_type=F32)
    st = state_ref[s]
    st_d = jnp.where(sblk, _tile_rows(st, RWKV_HEADS), 0.0)
    from_state = jnp.dot(lhs, st_d.astype(BF16), preferred_element_type=F32)
    yield
    a_b = jnp.where(strict, prod[0:c, 0:cw], 0.0)
    a_k = jnp.where(strict, prod[0:c, cw:2 * cw], 0.0)
    p_b = jnp.where(incl, prod[c:2 * c, 0:cw], 0.0)
    p_k = jnp.where(incl, prod[c:2 * c, cw:2 * cw], 0.0)

    def bdiag(z):
        return jnp.where(blk, _tile_rows(z, RWKV_HEADS), 0.0)

    xm = -a_b
    tinv = eye_all + xm
    v_d = expand(v)
    from_v = _bdot(jnp.concatenate([a_k, p_k], axis=0), v_d)
    rhs_u = from_state[0:c] + from_v[0:c]
    y_part = from_state[c:2 * c] + from_v[c:2 * c]
    levels = c.bit_length() - 1
    xm = _bdot(xm, bdiag(xm))
    yield
    for _ in range(1, levels - 1):
        both = _bdot(jnp.concatenate([xm, tinv], axis=0), bdiag(xm))
        yield
        xm = both[0:c]
        tinv = tinv + both[c:2 * c]
    tinv = tinv + _bdot(tinv, bdiag(xm))
    yield
    u = _bdot(tinv, expand(rhs_u))
    yield
    y = y_part - _bdot(p_b, expand(u))

    dm = jnp.where(key_diag, w_end, 0.0)
    lhs_s = jnp.concatenate([kend, -bend, dm], axis=0)
    rhs_s = jnp.concatenate([v, u, st], axis=0)
    tn_dims = (((0,), (0,)), ((), ()))
    gm = lax.dot_general(lhs_s.astype(BF16), rhs_s.astype(BF16), tn_dims, preferred_element_type=F32)
    yield
    gm = jnp.where(sblk, gm, 0.0)
    st_new = gm[0:n] + gm[n:2 * n] + (gm[2 * n:3 * n] + gm[3 * n:4 * n])
    state_ref[s] = st_new
    st_ref[s] = st_new

    mean = _block_sums(y, ones_blk) * (1.0 / n)
    yield
    yc = y - mean
    var = _block_sums(yc * yc, ones_blk) * (1.0 / n)
    yield
    yn = yc * lax.rsqrt(var + GN_EPS) * ln_w + ln_b
    o_ref[s] = (yn + bonus_sum * v) * g


def _rwkv(x, sh0, st0, mu, lora_w, g_up, vecs, ones_blk, valid, layer, state_layer):
    b, t, _ = x.shape
    c = min(RWKV_CHUNK, t)
    nseq = min(RWKV_SEQS if t > c else RWKV_SHORT_SEQS, b)
    wspec = lambda s: _layer_spec(s, layer, 2)
    return pl.pallas_call(
        functools.partial(_rwkv_kernel, c, valid if valid < c else c, nseq),
        grid=(b // nseq, t // c),
        in_specs=[
            pl.BlockSpec((nseq, c, RWKV_PROJ), lambda i, j: (i, j, 0)),
            pl.BlockSpec((None, nseq, 1, RWKV_PROJ), lambda i, j: (state_layer, i, 0, 0)),
            pl.BlockSpec((None, nseq, RWKV_HEAD, RWKV_WIDTH), lambda i, j: (state_layer, i, 0, 0)),
            wspec((1, RWKV_PROJ)), wspec((128, 512)), wspec((128, RWKV_WIDTH)), wspec((8, RWKV_WIDTH)),
            pl.BlockSpec((RWKV_WIDTH, RWKV_WIDTH), lambda i, j: (0, 0)),
        ],
        out_specs=[
            pl.BlockSpec((nseq, c, RWKV_WIDTH), lambda i, j: (i, j, 0)),
            pl.BlockSpec((nseq, RWKV_HEAD, RWKV_WIDTH), lambda i, j: (i, 0, 0)),
        ],
        out_shape=[
            jax.ShapeDtypeStruct((b, t, RWKV_WIDTH), F32),
            jax.ShapeDtypeStruct((b, RWKV_HEAD, RWKV_WIDTH), F32),
        ],
        scratch_shapes=[pltpu.VMEM((nseq, 1, RWKV_PROJ), F32), pltpu.VMEM((nseq, RWKV_HEAD, RWKV_WIDTH), F32)],
        compiler_params=_cparams(("arbitrary", "arbitrary")),
        name="rwkv",
    )(x, sh0, st0, mu, lora_w, g_up, vecs, ones_blk)


def _lru_scan(xc, t, period, valid, wg_ref, vec_ref):
    w = LRU_WIDTH
    _, b_a, b_i, lam = (vec_ref[i:i + 1, :] for i in range(4))
    gates = _bdot(xc, wg_ref[...])
    r = _sigmoid(gates[:, 0:w] + b_a)
    i = _sigmoid(gates[:, w:2 * w] + b_i)
    log_a = LRU_C * r * (-_softplus(-lam))
    a = jnp.exp(log_a)
    th = jnp.tanh(log_a)
    u = jnp.sqrt(-2.0 * th / (1.0 - th)) * (i * xc)
    if valid < period:
        live = t < valid
        a = jnp.where(live, a, 1.0)
        u = jnp.where(live, u, 0.0)
    span = 1
    while span < period:
        ok = t >= span
        a_s = pltpu.roll(a, span, 0)
        u_s = pltpu.roll(u, span, 0)
        u = jnp.where(ok, a * u_s + u, u)
        a = jnp.where(ok, a * a_s, a)
        span *= 2
    return a, u


def _lru_kernel(tc, x_ref, cv0_ref, h0_ref, cw_ref, wg_ref, vec_ref, o_ref, h_ref, ext_ref, hc_ref):
    ti = pl.program_id(1)
    w = LRU_WIDTH

    @pl.when(ti == 0)
    def _():
        ext_ref[0:8, :] = cv0_ref[...]
        hc_ref[...] = h0_ref[...]

    xb = x_ref[:, 0:w]
    gb = x_ref[:, w:2 * w]
    ext_ref[8:8 + tc, :] = xb
    xc = vec_ref[0:1, :] + xb * cw_ref[CONV_W - 1:CONV_W, :]
    for j in range(CONV_W - 1):
        xc = xc + ext_ref[pl.ds(8 - (CONV_W - 1) + j, tc), :] * cw_ref[j:j + 1, :]
    ext_ref[0:8, :] = xb[tc - 8:tc, :]
    row = lax.broadcasted_iota(jnp.int32, (tc, 1), 0)
    a, u = _lru_scan(xc, row, tc, tc, wg_ref, vec_ref)
    h = a * hc_ref[...] + u
    hc_ref[...] = h[tc - 1:tc, :]
    h_ref[...] = h[tc - 1:tc, :]
    o_ref[...] = h * _gelu_tanh(gb)


def _lru_short_kernel(valid, x_ref, cv0_ref, h0_ref, cw_ref, wg_ref, vec_ref, o_ref, h_ref):
    nb = x_ref.shape[0]
    w = LRU_WIDTH
    rows = nb * SAMPLE_PAD
    x = x_ref[...].reshape(rows, 2 * w)
    xb = x[:, 0:w]
    gb = x[:, w:2 * w]
    cv0 = cv0_ref[...].reshape(rows, w)
    t = lax.broadcasted_iota(jnp.int32, (rows, 1), 0) % SAMPLE_PAD
    xc = vec_ref[0:1, :] + xb * cw_ref[CONV_W - 1:CONV_W, :]
    for j in range(CONV_W - 1):
        back = CONV_W - 1 - j
        prev = jnp.where(t >= back, pltpu.roll(xb, back, 0),
                         pltpu.roll(cv0, (back - SAMPLE_PAD) % rows, 0))
        xc = xc + prev * cw_ref[j:j + 1, :]
    a, u = _lru_scan(xc, t, SAMPLE_PAD, valid, wg_ref, vec_ref)
    h0 = jnp.broadcast_to(h0_ref[...], (nb, SAMPLE_PAD, w)).reshape(rows, w)
    h = a * h0 + u
    o_ref[...] = (h * _gelu_tanh(gb)).reshape(nb, SAMPLE_PAD, w)
    h_ref[...] = h.reshape(nb, SAMPLE_PAD, w)[:, SAMPLE_PAD - 1:SAMPLE_PAD, :]


def _lru(x, cv0, h0, conv_w, w_gates, vecs, valid, layer, state_layer):
    b, t, _ = x.shape
    out_shape = [jax.ShapeDtypeStruct((b, t, LRU_WIDTH), F32), jax.ShapeDtypeStruct((b, 1, LRU_WIDTH), F32)]
    if t == SAMPLE_PAD:
        nb = min(LRU_SHORT_SEQS, b)
        seq = lambda r, w_: pl.BlockSpec((nb, r, w_), lambda i: (i, 0, 0))
        sseq = lambda r, w_: pl.BlockSpec((None, nb, r, w_), lambda i: (state_layer, i, 0, 0))
        wspec = lambda s: _layer_spec(s, layer, 1)
        return pl.pallas_call(
            functools.partial(_lru_short_kernel, valid),
            grid=(b // nb,),
            in_specs=[seq(SAMPLE_PAD, 2 * LRU_WIDTH), sseq(8, LRU_WIDTH), sseq(1, LRU_WIDTH),
                      wspec((CONV_W, LRU_WIDTH)), wspec((LRU_WIDTH, 2 * LRU_WIDTH)), wspec((4, LRU_WIDTH))],
            out_specs=[seq(SAMPLE_PAD, LRU_WIDTH), seq(1, LRU_WIDTH)],
            out_shape=out_shape,
            compiler_params=_cparams(("arbitrary",)),
            name="lru_short",
        )(x, cv0, h0, conv_w, w_gates, vecs)
    tc = min(LRU_TILE, t)
    wspec = lambda s: _layer_spec(s, layer, 2)
    return pl.pallas_call(
        functools.partial(_lru_kernel, tc),
        grid=(b, t // tc),
        in_specs=[
            pl.BlockSpec((None, tc, 2 * LRU_WIDTH), lambda i, j: (i, j, 0)),
            pl.BlockSpec((None, None, 8, LRU_WIDTH), lambda i, j: (state_layer, i, 0, 0)),
            pl.BlockSpec((None, None, 1, LRU_WIDTH), lambda i, j: (state_layer, i, 0, 0)),
            wspec((CONV_W, LRU_WIDTH)), wspec((LRU_WIDTH, 2 * LRU_WIDTH)), wspec((4, LRU_WIDTH)),
        ],
        out_specs=[
            pl.BlockSpec((None, tc, LRU_WIDTH), lambda i, j: (i, j, 0)),
            pl.BlockSpec((None, 1, LRU_WIDTH), lambda i, j: (i, 0, 0)),
        ],
        out_shape=out_shape,
        scratch_shapes=[pltpu.VMEM((tc + 8, LRU_WIDTH), F32), pltpu.VMEM((1, LRU_WIDTH), F32)],
        compiler_params=_cparams(("arbitrary", "arbitrary")),
        name="lru",
    )(x, cv0, h0, conv_w, w_gates, vecs)


def _post_kernel(x_ref, oa_ref, ob_ref, oc_ref, p_ref, wo_ref, wg_ref, wu_ref, wd_ref, pg_ref, pw_ref,
                 vec_ref, y_ref):
    mix = (jnp.dot(oa_ref[...].astype(BF16), wo_ref[0:512, :], preferred_element_type=F32)
           + jnp.dot(ob_ref[...].astype(BF16), wo_ref[512:768, :], preferred_element_type=F32)
           + jnp.dot(oc_ref[...].astype(BF16), wo_ref[768:1024, :], preferred_element_type=F32)
           + vec_ref[0:1, :])
    x = x_ref[...] + _rmsnorm(mix, vec_ref[1:2, :])
    f = _rmsnorm(x, vec_ref[2:3, :]).astype(BF16)
    acc = None
    for lo in range(0, D_FF, FF_CHUNK):
        gate = jnp.dot(f, wg_ref[:, lo:lo + FF_CHUNK], preferred_element_type=F32)
        up = jnp.dot(f, wu_ref[:, lo:lo + FF_CHUNK], preferred_element_type=F32)
        hid = (gate * _sigmoid(gate) * up).astype(BF16)
        part = jnp.dot(hid, wd_ref[lo:lo + FF_CHUNK, :], preferred_element_type=F32)
        acc = part if acc is None else acc + part
    x2 = x + _rmsnorm(acc, vec_ref[3:4, :])
    gate = _sigmoid(jnp.dot(x2.astype(BF16), pg_ref[...], preferred_element_type=F32))
    emb = jnp.dot(p_ref[...].astype(BF16), pw_ref[...], preferred_element_type=F32)
    y_ref[...] = x2 + gate * emb


def _post(x, oa, ob, oc, p, wo, wg, wu, wd, pg, pw, vecs, layer):
    m = x.shape[0]
    tm = min(ROW_TILE, m)
    row = lambda w_: pl.BlockSpec((tm, w_), lambda i: (i, 0))
    wspec = lambda s: _layer_spec(s, layer, 1, pipeline_mode=pl.Buffered(1))
    return pl.pallas_call(
        _post_kernel,
        grid=(m // tm,),
        in_specs=[row(D_MODEL), row(512), row(256), row(256),
                  pl.BlockSpec((None, tm, PLE_DIM), lambda i: (layer, i, 0)),
                  wspec((D_MODEL, D_MODEL)), wspec((D_MODEL, D_FF)), wspec((D_MODEL, D_FF)),
                  wspec((D_FF, D_MODEL)), wspec((D_MODEL, D_MODEL)), wspec((PLE_DIM, D_MODEL)),
                  wspec((4, D_MODEL))],
        out_specs=row(D_MODEL),
        out_shape=jax.ShapeDtypeStruct((m, D_MODEL), F32),
        compiler_params=_cparams(("arbitrary",)),
        name="post",
    )(x, oa, ob, oc, p, wo, wg, wu, wd, pg, pw, vecs)


def _block_diag(w):
    nl, nb, n, _ = w.shape
    eye = jnp.eye(nb, dtype=w.dtype)
    return (eye[None, :, None, :, None] * w[:, :, :, None, :]).reshape(nl, nb * n, nb * n)


def _layer(x, p, state, wts, layer, sample):
    b, t, _ = x.shape
    m = b * t
    x2 = x.reshape(m, D_MODEL)
    q, kv, rw, lr = _in_proj(x2, wts['norm_mix_pre'], wts['w_in'], wts['b_in'], layer)
    q = q.reshape(b, t, 512)
    kv = kv.reshape(b, t, 256)
    rw = rw.reshape(b, t, RWKV_PROJ)
    lr = lr.reshape(b, t, 2 * LRU_WIDTH)
    ck, cv, sh0, st0, cv0, h0 = state
    state_layer = layer if sample else 0

    if sample:
        o_a, nk, nv = _attn_sample(q, kv, ck, cv, wts['attn_sinks'], layer)
        nk = nk.reshape(b, WINDOW, N_KV_HEADS, HEAD_DIM)
        nv = nv.reshape(b, WINDOW, N_KV_HEADS, HEAD_DIM)
    else:
        o_a = _attn_prompt(q, kv, wts['attn_sinks'], layer)
        nk = kv[:, t - WINDOW:, 0:KV_WIDTH].reshape(b, WINDOW, N_KV_HEADS, HEAD_DIM)
        nv = kv[:, t - WINDOW:, KV_WIDTH:].reshape(b, WINDOW, N_KV_HEADS, HEAD_DIM)

    tp = SAMPLE_PAD * (-(-t // SAMPLE_PAD))
    pad = lambda z: z if tp == t else jnp.pad(z, ((0, 0), (0, tp - t), (0, 0)))

    o_b, st = _rwkv(pad(rw), sh0, st0, wts['rwkv_mu'], wts['rwkv_lora'], wts['rwkv_g_up'], wts['rwkv_vecs'],
                    wts['ones_blk'], t, layer, state_layer)
    nwkv = st.reshape(b, RWKV_HEAD, RWKV_HEADS, RWKV_HEAD).transpose(0, 2, 3, 1)
    nsh = rw[:, t - 1, :]

    o_c, nh = _lru(pad(lr), cv0, h0, wts['lru_conv_w'], wts['lru_w_gates'], wts['lru_vecs'], t, layer,
                   state_layer)
    nconv = lr[:, t - (CONV_W - 1):, 0:LRU_WIDTH]
    nh = nh.reshape(b, LRU_WIDTH)

    x2 = _post(x2, o_a.reshape(m, 512), o_b[:, :t].reshape(m, 256), o_c[:, :t].reshape(m, 256), p,
               wts['w_out'], wts['ffn_w_gate'], wts['ffn_w_up'], wts['ffn_w_down'],
               wts['ple_gate_w'], wts['ple_w'], wts['post_vecs'], layer)
    return x2.reshape(b, t, D_MODEL), (nk, nv, nsh, nwkv, nconv, nh)


def kernel(x_prompt, x_sample, cache_k, cache_v, state_shift, state_wkv, state_conv, state_lru,
           p_prompt, p_sample, norm_mix_pre, norm_mix_post, norm_ffn_pre, norm_ffn_post,
           w_in, b_in, attn_sinks, rwkv_mu, rwkv_w0, rwkv_w_up, rwkv_a0, rwkv_a_up, rwkv_g_up,
           rwkv_k_k, rwkv_k_a, rwkv_r_k, rwkv_ln_w, rwkv_ln_b, lru_conv_w, lru_conv_b,
           lru_w_a, lru_b_a, lru_w_i, lru_b_i, lru_L, w_out, b_out, ffn_w_gate, ffn_w_up,
           ffn_w_down, ple_w, ple_gate_w):
    nl = DEPTH
    bp, tp_, _ = x_prompt.shape
    bs, ts, _ = x_sample.shape
    head_id = jnp.arange(RWKV_WIDTH) // RWKV_HEAD
    zeros_w = jnp.zeros((nl, 64, RWKV_WIDTH), F32)
    wts = dict(
        norm_mix_pre=norm_mix_pre[:, None, :], w_in=w_in.astype(BF16), b_in=b_in[:, None, :],
        attn_sinks=attn_sinks,
        rwkv_mu=rwkv_mu[:, None, :],
        rwkv_lora=jnp.concatenate([jnp.concatenate([rwkv_w_up, zeros_w], axis=2),
                                   jnp.concatenate([zeros_w, rwkv_a_up], axis=2)], axis=1).astype(BF16),
        rwkv_g_up=rwkv_g_up.astype(BF16),
        rwkv_vecs=jnp.stack([rwkv_w0, rwkv_a0, rwkv_k_k, rwkv_k_a, rwkv_r_k.reshape(nl, RWKV_WIDTH),
                             rwkv_ln_w, rwkv_ln_b, jnp.zeros((nl, RWKV_WIDTH), F32)], axis=1),
        ones_blk=(head_id[:, None] == head_id[None, :]).astype(BF16),
        lru_conv_w=lru_conv_w,
        lru_w_gates=jnp.concatenate([_block_diag(lru_w_a), _block_diag(lru_w_i)], axis=2).astype(BF16),
        lru_vecs=jnp.stack([lru_conv_b, lru_b_a, lru_b_i, lru_L], axis=1),
        w_out=w_out.astype(BF16), ffn_w_gate=ffn_w_gate.astype(BF16), ffn_w_up=ffn_w_up.astype(BF16),
        ffn_w_down=ffn_w_down.astype(BF16), ple_gate_w=ple_gate_w.astype(BF16), ple_w=ple_w.astype(BF16),
        post_vecs=jnp.stack([b_out, norm_mix_post, norm_ffn_pre, norm_ffn_post], axis=1),
    )
    st_p = (None, None,
            jnp.zeros((1, bp, 1, RWKV_PROJ), F32),
            jnp.zeros((1, bp, RWKV_HEAD, RWKV_WIDTH), F32),
            jnp.zeros((1, bp, 8, LRU_WIDTH), F32),
            jnp.zeros((1, bp, 1, LRU_WIDTH), F32))
    st_s = (cache_k.reshape(nl, bs, WINDOW, KV_WIDTH), cache_v.reshape(nl, bs, WINDOW, KV_WIDTH),
            state_shift.reshape(nl, bs, 1, RWKV_PROJ),
            state_wkv.transpose(0, 1, 4, 2, 3).reshape(nl, bs, RWKV_HEAD, RWKV_WIDTH),
            jnp.pad(state_conv, ((0, 0), (0, 0), (8 - (CONV_W - 1), 0), (0, 0))),
            state_lru.reshape(nl, bs, 1, LRU_WIDTH))
    pp = p_prompt.reshape(nl, bp * tp_, PLE_DIM)
    ps = p_sample.reshape(nl, bs * ts, PLE_DIM)

    xp, xs = x_prompt, x_sample
    new_p, new_s = [], []
    for i in range(nl):
        xp, sp = _layer(xp, pp, st_p, wts, i, False)
        xs, ss = _layer(xs, ps, st_s, wts, i, True)
        new_p.append(sp)
        new_s.append(ss)

    def stk(lst, j):
        return jnp.stack([s[j] for s in lst], axis=0)

    return (xp, xs,
            stk(new_p, 0), stk(new_p, 1), stk(new_p, 2), stk(new_p, 3), stk(new_p, 4), stk(new_p, 5),
            stk(new_s, 0), stk(new_s, 1), stk(new_s, 2), stk(new_s, 3), stk(new_s, 4), stk(new_s, 5))
```

```python
import functools

import jax
import jax.numpy as jnp
from jax import lax
from jax.experimental import pallas as pl
from jax.experimental.pallas import tpu as pltpu

F32 = jnp.float32
BF16 = jnp.bfloat16

D_MODEL = 1024
DEPTH = 4
HEAD_DIM = 64
ATTN_WIDTH = 512
N_HEADS = 8
N_KV_HEADS = 2
GQA_GROUP = 4
KV_WIDTH = 128
WINDOW = 128
RWKV_WIDTH = 256
RWKV_HEADS = 4
RWKV_HEAD = 64
RWKV_PROJ = 1024
LRU_WIDTH = 256
CONV_W = 4
LRU_C = 8.0
D_FF = 2816
PLE_DIM = 256
RMS_EPS = 1e-6
GN_EPS = 64e-5
IN_COLS = 2304

ROW_TILE = 512
FF_CHUNK = 1408
RWKV_CHUNK = 64
RWKV_SEQS = 4
RWKV_SHORT_SEQS = 8
LRU_TILE = 512
LRU_SHORT_SEQS = 32
ATTN_SAMPLE_SEQS = 8
ATTN_Q_BLOCKS = 4
SAMPLE_PAD = 8
VMEM_LIMIT = 56 * 1024 * 1024


def _cparams(sem):
    return pltpu.CompilerParams(dimension_semantics=sem, vmem_limit_bytes=VMEM_LIMIT)


def _layer_spec(shape, layer, nidx, **kw):
    zeros = (0,) * len(shape)
    if nidx == 1:
        return pl.BlockSpec((None,) + tuple(shape), lambda i: (layer,) + zeros, **kw)
    return pl.BlockSpec((None,) + tuple(shape), lambda i, j: (layer,) + zeros, **kw)


def _bdot(a, b):
    return jnp.dot(a.astype(BF16), b.astype(BF16), preferred_element_type=F32)


def _hi_lo_rows(x):
    hi = x.astype(BF16)
    lo = (x - hi.astype(F32)).astype(BF16)
    return jnp.concatenate([hi, lo], axis=0)


def _block_sums(x, ones_blk):
    rows = x.shape[0]
    res = jnp.dot(_hi_lo_rows(x), ones_blk, preferred_element_type=F32)
    return res[0:rows] + res[rows:2 * rows]


def _rmsnorm(x, g):
    ms = jnp.mean(x * x, axis=-1, keepdims=True)
    return x * lax.rsqrt(ms + RMS_EPS) * g


def _softplus(x):
    return jnp.maximum(x, 0.0) + jnp.log1p(jnp.exp(-jnp.abs(x)))


def _sigmoid(x):
    return 1.0 / (1.0 + jnp.exp(-x))


def _gelu_tanh(x):
    return 0.5 * x * (1.0 + jnp.tanh(0.7978845608028654 * (x + 0.044715 * (x * x * x))))


def _in_kernel(x_ref, g_ref, w_ref, b_ref, q_ref, kv_ref, rw_ref, lr_ref):
    h = _rmsnorm(x_ref[...], g_ref[...]).astype(BF16)
    for ref, lo, hi in ((q_ref, 0, 512), (kv_ref, 512, 768), (rw_ref, 768, 1792), (lr_ref, 1792, 2304)):
        ref[...] = jnp.dot(h, w_ref[:, lo:hi], preferred_element_type=F32) + b_ref[:, lo:hi]


def _in_proj(x, g, w, b, layer):
    m = x.shape[0]
    tm = min(ROW_TILE, m)
    row = lambda w_: pl.BlockSpec((tm, w_), lambda i: (i, 0))
    return pl.pallas_call(
        _in_kernel,
        grid=(m // tm,),
        in_specs=[row(D_MODEL), _layer_spec((1, D_MODEL), layer, 1),
                  _layer_spec((D_MODEL, IN_COLS), layer, 1), _layer_spec((1, IN_COLS), layer, 1)],
        out_specs=[row(512), row(256), row(1024), row(512)],
        out_shape=[jax.ShapeDtypeStruct((m, n), F32) for n in (512, 256, 1024, 512)],
        compiler_params=_cparams(("arbitrary",)),
        name="in_proj",
    )(x, g, w, b)


def _attn_prompt_kernel(layer, nq, sink_ref, q_ref, kvp_ref, kvc_ref, o_ref):
    j = pl.program_id(1)
    log2e = 1.4426950408889634
    q = q_ref[...] * (HEAD_DIM ** -0.5 * log2e)
    kv = jnp.concatenate([kvp_ref[...], kvc_ref[...]], axis=0)
    kj = lax.broadcasted_iota(jnp.int32, (2 * WINDOW, WINDOW), 0)
    qi = lax.broadcasted_iota(jnp.int32, (2 * WINDOW, WINDOW), 1) + WINDOW
    d = qi - kj
    band = (d >= 0) & (d <= WINDOW)
    first = band & ((j > 0) | (kj >= WINDOW))
    nt = (((1,), (1,)), ((), ()))
    keys = kv[:, 0:KV_WIDTH].astype(BF16)
    v_t = kv[:, KV_WIDTH:2 * KV_WIDTH].T
    ones_rows = (lax.broadcasted_iota(jnp.int32, (8, 2 * WINDOW), 0) == 0).astype(F32)
    scores = {}
    for blk in range(nq):
        qb = q[blk * WINDOW:(blk + 1) * WINDOW]
        for g in range(N_KV_HEADS):
            qg = jnp.concatenate([qb[:, (g * GQA_GROUP + hh) * HEAD_DIM:(g * GQA_GROUP + hh + 1) * HEAD_DIM]
                                  for hh in range(GQA_GROUP)], axis=0).astype(BF16)
            k_ext = keys[blk * WINDOW:(blk + 2) * WINDOW, g * HEAD_DIM:(g + 1) * HEAD_DIM]
            scores[blk, g] = lax.dot_general(k_ext, qg, nt, preferred_element_type=F32)
    probs, sink_terms = {}, {}
    for blk in range(nq):
        mask = first if blk == 0 else band
        for g in range(N_KV_HEADS):
            pg, sg = [], []
            for hh in range(GQA_GROUP):
                s = jnp.where(mask, scores[blk, g][:, hh * WINDOW:(hh + 1) * WINDOW], -1e30)
                sink = sink_ref[layer, g * GQA_GROUP + hh] * log2e
                m = jnp.maximum(jnp.max(s, axis=0, keepdims=True), sink)
                pg.append(jnp.exp2(s - m).astype(BF16))
                sg.append(jnp.exp2(sink - m))
            probs[blk, g] = jnp.concatenate(pg, axis=1)
            sink_terms[blk, g] = jnp.concatenate(sg, axis=1)
    for blk in range(nq):
        outs = []
        for g in range(N_KV_HEADS):
            v_aug = jnp.concatenate([v_t[g * HEAD_DIM:(g + 1) * HEAD_DIM, blk * WINDOW:(blk + 2) * WINDOW],
                                     ones_rows], axis=0).astype(BF16)
            og = jnp.dot(v_aug, probs[blk, g], preferred_element_type=F32)
            den = og[HEAD_DIM:HEAD_DIM + 1, :] + sink_terms[blk, g]
            og = og[0:HEAD_DIM, :] * (1.0 / den)
            outs.extend(og[:, hh * WINDOW:(hh + 1) * WINDOW] for hh in range(GQA_GROUP))
        o_ref[blk * WINDOW:(blk + 1) * WINDOW, :] = jnp.concatenate(outs, axis=0).T


def _attn_prompt(q, kv, sinks, layer):
    b, t, _ = q.shape
    nq = ATTN_Q_BLOCKS
    tq = nq * WINDOW
    return pl.pallas_call(
        functools.partial(_attn_prompt_kernel, layer, nq),
        grid=(b, t // tq),
        in_specs=[
            pl.BlockSpec(memory_space=pltpu.SMEM),
            pl.BlockSpec((None, tq, ATTN_WIDTH), lambda i, j: (i, j, 0)),
            pl.BlockSpec((None, WINDOW, 2 * KV_WIDTH), lambda i, j: (i, jnp.maximum(j * nq - 1, 0), 0)),
            pl.BlockSpec((None, tq, 2 * KV_WIDTH), lambda i, j: (i, j, 0)),
        ],
        out_specs=pl.BlockSpec((None, tq, ATTN_WIDTH), lambda i, j: (i, j, 0)),
        out_shape=jax.ShapeDtypeStruct((b, t, ATTN_WIDTH), F32),
        compiler_params=_cparams(("arbitrary", "arbitrary")),
        name="attn_prompt",
    )(sinks, q, kv, kv)


def _attn_sample_kernel(layer, tn, sink_ref, q_ref, kvn_ref, ck_ref, cv_ref, o_ref, nk_ref, nv_ref):
    rows = GQA_GROUP * tn
    ck = ck_ref[...]
    cv = cv_ref[...]
    kvn = kvn_ref[...]
    row = lax.broadcasted_iota(jnp.int32, (1, rows, 1), 1)
    tok = row % tn
    col = lax.broadcasted_iota(jnp.int32, (1, 1, WINDOW), 2)
    cmask = col >= tok
    for g in range(N_KV_HEADS):
        ks = slice(g * HEAD_DIM, (g + 1) * HEAD_DIM)
        vs = slice(KV_WIDTH + g * HEAD_DIM, KV_WIDTH + (g + 1) * HEAD_DIM)
        qg = q_ref[:, g] * (HEAD_DIM ** -0.5)
        qg_b = qg.astype(BF16).astype(F32)
        sc = jnp.einsum('bqd,bkd->bqk', qg.astype(BF16), ck[:, :, ks].astype(BF16),
                        preferred_element_type=F32)
        sc = jnp.where(cmask, sc, -1e30)
        sink = jnp.zeros((1, rows, 1), F32)
        for hh in range(GQA_GROUP):
            sink = jnp.where(row // tn == hh, sink_ref[layer, g * GQA_GROUP + hh], sink)
        m = jnp.maximum(jnp.max(sc, axis=-1, keepdims=True), sink)
        sn = []
        for jn in range(tn):
            kn = kvn[:, jn:jn + 1, ks].astype(BF16).astype(F32)
            s_j = jnp.sum(qg_b * kn, axis=-1, keepdims=True)
            s_j = jnp.where(tok >= jn, s_j, -1e30)
            sn.append(s_j)
            m = jnp.maximum(m, s_j)
        ec = jnp.exp(sc - m)
        den = jnp.sum(ec, axis=-1, keepdims=True) + jnp.exp(sink - m)
        en = [jnp.exp(s_j - m) for s_j in sn]
        for e_j in en:
            den = den + e_j
        inv = 1.0 / den
        o = jnp.einsum('bqk,bkd->bqd', (ec * inv).astype(BF16), cv[:, :, ks].astype(BF16),
                       preferred_element_type=F32)
        for jn in range(tn):
            vn = kvn[:, jn:jn + 1, vs].astype(BF16).astype(F32)
            o = o + (en[jn] * inv).astype(BF16).astype(F32) * vn
        o_ref[:, g] = o
    nk_ref[:, 0:WINDOW - tn, :] = ck[:, tn:WINDOW, :]
    nk_ref[:, WINDOW - tn:WINDOW, :] = kvn[:, :, 0:KV_WIDTH]
    nv_ref[:, 0:WINDOW - tn, :] = cv[:, tn:WINDOW, :]
    nv_ref[:, WINDOW - tn:WINDOW, :] = kvn[:, :, KV_WIDTH:2 * KV_WIDTH]


def _attn_sample(q, kv, ck, cv, sinks, layer):
    b, tn, _ = q.shape
    bb = ATTN_SAMPLE_SEQS
    rows = GQA_GROUP * tn
    qh = q.reshape(b, tn, N_KV_HEADS, GQA_GROUP, HEAD_DIM).transpose(0, 2, 3, 1, 4)
    qh = qh.reshape(b, N_KV_HEADS, rows, HEAD_DIM)
    cache = pl.BlockSpec((None, bb, WINDOW, KV_WIDTH), lambda i: (layer, i, 0, 0))
    o, nk, nv = pl.pallas_call(
        functools.partial(_attn_sample_kernel, layer, tn),
        grid=(b // bb,),
        in_specs=[
            pl.BlockSpec(memory_space=pltpu.SMEM),
            pl.BlockSpec((bb, N_KV_HEADS, rows, HEAD_DIM), lambda i: (i, 0, 0, 0)),
            pl.BlockSpec((bb, tn, 2 * KV_WIDTH), lambda i: (i, 0, 0)),
            cache, cache,
        ],
        out_specs=[
            pl.BlockSpec((bb, N_KV_HEADS, rows, HEAD_DIM), lambda i: (i, 0, 0, 0)),
            pl.BlockSpec((bb, WINDOW, KV_WIDTH), lambda i: (i, 0, 0)),
            pl.BlockSpec((bb, WINDOW, KV_WIDTH), lambda i: (i, 0, 0)),
        ],
        out_shape=[
            jax.ShapeDtypeStruct((b, N_KV_HEADS, rows, HEAD_DIM), F32),
            jax.ShapeDtypeStruct((b, WINDOW, KV_WIDTH), F32),
            jax.ShapeDtypeStruct((b, WINDOW, KV_WIDTH), F32),
        ],
        compiler_params=_cparams(("arbitrary",)),
        name="attn_sample",
    )(sinks, qh, kv, ck, cv)
    o = o.reshape(b, N_KV_HEADS, GQA_GROUP, tn, HEAD_DIM).transpose(0, 3, 1, 2, 4)
    return o.reshape(b, tn, ATTN_WIDTH), nk, nv


def _tile_rows(x, n):
    return jnp.concatenate([x] * n, axis=0)


def _rwkv_kernel(c, valid, nseq, x_ref, sh0_ref, st0_ref, mu_ref, lora_ref, gup_ref, vec_ref, ones_ref,
                 stacked_ref, o_ref, st_ref, last_ref, state_ref):
    del stacked_ref
    hw = RWKV_WIDTH
    n = RWKV_HEAD
    cw = RWKV_HEADS * c

    @pl.when(pl.program_id(1) == 0)
    def _():
        last_ref[...] = sh0_ref[...]
        for s in range(nseq):
            for h in range(RWKV_HEADS):
                state_ref[s, :, h * n:(h + 1) * n] = st0_ref[s, h]

    row = lax.broadcasted_iota(jnp.int32, (c, 1), 0)
    lane128 = lax.broadcasted_iota(jnp.int32, (c, 128), 1)
    tri = jnp.where(lax.broadcasted_iota(jnp.int32, (c, c), 1) <= lax.broadcasted_iota(jnp.int32, (c, c), 0),
                    1.0, 0.0).astype(BF16)
    tri3 = jnp.concatenate([tri, tri, tri], axis=1)
    head_rows = (lax.broadcasted_iota(jnp.int32, (cw, hw), 0) // c
                 == lax.broadcasted_iota(jnp.int32, (cw, hw), 1) // n)
    t_i = lax.broadcasted_iota(jnp.int32, (c, cw), 0)
    s_i = lax.broadcasted_iota(jnp.int32, (c, cw), 1) % c
    strict = s_i < t_i
    incl = s_i <= t_i
    eye_all = jnp.where(s_i == t_i, 1.0, 0.0)
    blk = (lax.broadcasted_iota(jnp.int32, (cw, cw), 0) // c
           == lax.broadcasted_iota(jnp.int32, (cw, cw), 1) // c)
    sblk = (lax.broadcasted_iota(jnp.int32, (hw, hw), 0) // n
            == lax.broadcasted_iota(jnp.int32, (hw, hw), 1) // n)
    masks = (row, lane128, tri3, head_rows, strict, incl, eye_all, blk, sblk)
    chains = [_rwkv_chunk(c, valid, s, masks, x_ref, mu_ref, lora_ref, gup_ref, vec_ref, ones_ref,
                          o_ref, st_ref, last_ref, state_ref) for s in range(nseq)]
    while chains:
        chains = [ch for ch in chains if next(ch, "done") != "done"]


def _rwkv_chunk(c, valid, s, masks, x_ref, mu_ref, lora_ref, gup_ref, vec_ref, ones_ref,
                o_ref, st_ref, last_ref, state_ref):
    row, lane128, tri3, head_rows, strict, incl, eye_all, blk, sblk = masks
    hw = RWKV_WIDTH
    n = RWKV_HEAD
    cw = RWKV_HEADS * c
    x = x_ref[s]
    prev = jnp.where(row == 0, last_ref[s], pltpu.roll(x, 1, 0))
    last_ref[s] = x[c - 1:c, :]
    xs = x + (prev - x) * mu_ref[...]

    r = xs[:, 0:hw]
    k = xs[:, hw:2 * hw]
    v = xs[:, 2 * hw:3 * hw]
    wa = xs[:, 3 * hw:3 * hw + 128]
    xg = xs[:, 3 * hw + 128:]
    lora_in = jnp.where(lane128 < 64, jnp.tanh(wa), wa)
    lora = _bdot(lora_in, lora_ref[...])
    w0, a0, k_k, k_a, r_k, ln_w, ln_b = (vec_ref[i:i + 1, :] for i in range(7))
    g = _bdot(_sigmoid(xg), gup_ref[...])
    ones_blk = ones_ref[...]
    kk = k * k_k
    ss = _block_sums(kk * kk, ones_blk)
    yield
    w_log = -_softplus(-(w0 + lora[:, 0:hw])) - 0.5
    logw = -jnp.exp(w_log)
    a = _sigmoid(a0 + lora[:, hw:2 * hw])
    kk = kk * lax.rsqrt(jnp.maximum(ss, 1e-24))
    k2 = k * (1.0 + (a - 1.0) * k_a)
    bv = kk * a
    if valid < c:
        live = row < valid
        logw = jnp.where(live, logw, 0.0)
        kk = jnp.where(live, kk, 0.0)
        bv = jnp.where(live, bv, 0.0)
        k2 = jnp.where(live, k2, 0.0)
        v = jnp.where(live, v, 0.0)

    pieces = []
    rem = logw
    for _ in range(3):
        p = rem.astype(BF16)
        rem = rem - p.astype(F32)
        pieces.append(p)
    cum = jnp.dot(tri3, jnp.concatenate(pieces, axis=0), preferred_element_type=F32)
    bonus_sum = _block_sums(r * k2 * r_k, ones_blk)
    yield
    e_inc = jnp.exp(cum)
    e_exc = jnp.exp(cum - logw)
    e_inv = jnp.exp(-cum)
    kq = kk * e_exc
    rq = r * e_inc
    kd = k2 * e_inv
    bd = bv * e_inv
    w_end = e_inc[c - 1:c, :]
    kend = kd * w_end
    bend = bd * w_end

    def expand(z):
        return jnp.where(head_rows, _tile_rows(z, RWKV_HEADS), 0.0)

    lhs = jnp.concatenate([kq, rq], axis=0).astype(BF16)
    rhs = jnp.concatenate([expand(bd), expand(kd)], axis=0).astype(BF16)
    nt = (((1,), (1,)), ((), ()))
    prod = lax.dot_general(lhs, rhs, nt, preferred_element_type=F32)
    st = state_ref[s]
    st_d = jnp.where(sblk, _tile_rows(st, RWKV_HEADS), 0.0)
    from_state = lax.dot_general(lhs, st_d.astype(BF16), nt, preferred_element_type=F32)
    yield
    a_b = jnp.where(strict, prod[0:c, 0:cw], 0.0)
    a_k = jnp.where(strict, prod[0:c, cw:2 * cw], 0.0)
    p_b = jnp.where(incl, prod[c:2 * c, 0:cw], 0.0)
    p_k = jnp.where(incl, prod[c:2 * c, cw:2 * cw], 0.0)

    def bdiag(z):
        return jnp.where(blk, _tile_rows(z, RWKV_HEADS), 0.0)

    xm = -a_b
    tinv = eye_all + xm
    v_d = expand(v)
    from_v = _bdot(jnp.concatenate([a_k, p_k], axis=0), v_d)
    rhs_u = from_state[0:c] + from_v[0:c]
    y_part = from_state[c:2 * c] + from_v[c:2 * c]
    levels = c.bit_length() - 1
    xm = _bdot(xm, bdiag(xm))
    yield
    for _ in range(1, levels - 1):
        both = _bdot(jnp.concatenate([xm, tinv], axis=0), bdiag(xm))
        yield
        xm = both[0:c]
        tinv = tinv + both[c:2 * c]
    tinv = tinv + _bdot(tinv, bdiag(xm))
    yield
    u = _bdot(tinv, expand(rhs_u))
    yield
    y = y_part - _bdot(p_b, expand(u))

    lhs_s = jnp.concatenate([v, u], axis=0)
    rhs_s = jnp.concatenate([kend, -bend], axis=0)
    tn_dims = (((0,), (0,)), ((), ()))
    gm = lax.dot_general(lhs_s.astype(BF16), rhs_s.astype(BF16), tn_dims, preferred_element_type=F32)
    yield
    gm = jnp.where(sblk, gm, 0.0)
    st_new = st * w_end + (gm[0:n] + gm[n:2 * n] + (gm[2 * n:3 * n] + gm[3 * n:4 * n]))
    state_ref[s] = st_new
    for h in range(RWKV_HEADS):
        st_ref[s, h] = st_new[:, h * n:(h + 1) * n]

    mean = _block_sums(y, ones_blk) * (1.0 / n)
    yield
    yc = y - mean
    var = _block_sums(yc * yc, ones_blk) * (1.0 / n)
    yield
    yn = yc * lax.rsqrt(var + GN_EPS) * ln_w + ln_b
    o_ref[s] = (yn + bonus_sum * v) * g


def _rwkv(x, sh0, st0, mu, lora_w, g_up, vecs, ones_blk, stacked, valid, layer, state_layer):
    b, t, _ = x.shape
    c = min(RWKV_CHUNK, t)
    nseq = min(RWKV_SEQS if t > c else RWKV_SHORT_SEQS, b)
    wspec = lambda s: _layer_spec(s, layer, 2)
    hstate = (nseq, RWKV_HEADS, RWKV_HEAD, RWKV_HEAD)
    return pl.pallas_call(
        functools.partial(_rwkv_kernel, c, valid if valid < c else c, nseq),
        grid=(b // nseq, t // c),
        in_specs=[
            pl.BlockSpec((nseq, c, RWKV_PROJ), lambda i, j: (i, j, 0)),
            pl.BlockSpec((None, nseq, 1, RWKV_PROJ), lambda i, j: (state_layer, i, 0, 0)),
            pl.BlockSpec((None,) + hstate, lambda i, j: (state_layer, i, 0, 0, 0)),
            wspec((1, RWKV_PROJ)), wspec((128, 512)), wspec((128, RWKV_WIDTH)), wspec((8, RWKV_WIDTH)),
            pl.BlockSpec((RWKV_WIDTH, RWKV_WIDTH), lambda i, j: (0, 0)),
            pl.BlockSpec(memory_space=pl.ANY),
        ],
        out_specs=[
            pl.BlockSpec((nseq, c, RWKV_WIDTH), lambda i, j: (i, j, 0)),
            pl.BlockSpec((None,) + hstate, lambda i, j: (layer, i, 0, 0, 0)),
        ],
        out_shape=[
            jax.ShapeDtypeStruct((b, t, RWKV_WIDTH), F32),
            jax.ShapeDtypeStruct(stacked.shape, F32),
        ],
        scratch_shapes=[pltpu.VMEM((nseq, 1, RWKV_PROJ), F32), pltpu.VMEM((nseq, RWKV_HEAD, RWKV_WIDTH), F32)],
        input_output_aliases={8: 1},
        compiler_params=_cparams(("arbitrary", "arbitrary")),
        name="rwkv",
    )(x, sh0, st0, mu, lora_w, g_up, vecs, ones_blk, stacked)


def _lru_scan(xc, t, period, valid, wg_ref, vec_ref):
    w = LRU_WIDTH
    _, b_a, b_i, lam = (vec_ref[i:i + 1, :] for i in range(4))
    gates = _bdot(xc, wg_ref[...])
    r = _sigmoid(gates[:, 0:w] + b_a)
    i = _sigmoid(gates[:, w:2 * w] + b_i)
    log_a = LRU_C * r * (-_softplus(-lam))
    a = jnp.exp(log_a)
    th = jnp.tanh(log_a)
    u = jnp.sqrt(-2.0 * th / (1.0 - th)) * (i * xc)
    if valid < period:
        live = t < valid
        a = jnp.where(live, a, 1.0)
        u = jnp.where(live, u, 0.0)
    span = 1
    while span < period:
        ok = t >= span
        a_s = pltpu.roll(a, span, 0)
        u_s = pltpu.roll(u, span, 0)
        u = jnp.where(ok, a * u_s + u, u)
        a = jnp.where(ok, a * a_s, a)
        span *= 2
    return a, u


def _lru_kernel(tc, x_ref, cv0_ref, h0_ref, cw_ref, wg_ref, vec_ref, o_ref, h_ref, ext_ref, hc_ref):
    ti = pl.program_id(1)
    w = LRU_WIDTH

    @pl.when(ti == 0)
    def _():
        ext_ref[0:8, :] = cv0_ref[...]
        hc_ref[...] = h0_ref[...]

    xb = x_ref[:, 0:w]
    gb = x_ref[:, w:2 * w]
    ext_ref[8:8 + tc, :] = xb
    xc = vec_ref[0:1, :] + xb * cw_ref[CONV_W - 1:CONV_W, :]
    for j in range(CONV_W - 1):
        xc = xc + ext_ref[pl.ds(8 - (CONV_W - 1) + j, tc), :] * cw_ref[j:j + 1, :]
    ext_ref[0:8, :] = xb[tc - 8:tc, :]
    row = lax.broadcasted_iota(jnp.int32, (tc, 1), 0)
    a, u = _lru_scan(xc, row, tc, tc, wg_ref, vec_ref)
    h = a * hc_ref[...] + u
    hc_ref[...] = h[tc - 1:tc, :]
    h_ref[...] = h[tc - 1:tc, :]
    o_ref[...] = h * _gelu_tanh(gb)


def _lru_short_kernel(valid, x_ref, cv0_ref, h0_ref, cw_ref, wg_ref, vec_ref, o_ref, h_ref):
    nb = x_ref.shape[0]
    w = LRU_WIDTH
    rows = nb * SAMPLE_PAD
    x = x_ref[...].reshape(rows, 2 * w)
    xb = x[:, 0:w]
    gb = x[:, w:2 * w]
    cv0 = cv0_ref[...].reshape(rows, w)
    t = lax.broadcasted_iota(jnp.int32, (rows, 1), 0) % SAMPLE_PAD
    xc = vec_ref[0:1, :] + xb * cw_ref[CONV_W - 1:CONV_W, :]
    for j in range(CONV_W - 1):
        back = CONV_W - 1 - j
        prev = jnp.where(t >= back, pltpu.roll(xb, back, 0),
                         pltpu.roll(cv0, (back - SAMPLE_PAD) % rows, 0))
        xc = xc + prev * cw_ref[j:j + 1, :]
    a, u = _lru_scan(xc, t, SAMPLE_PAD, valid, wg_ref, vec_ref)
    h0 = jnp.broadcast_to(h0_ref[...], (nb, SAMPLE_PAD, w)).reshape(rows, w)
    h = a * h0 + u
    o_ref[...] = (h * _gelu_tanh(gb)).reshape(nb, SAMPLE_PAD, w)
    h_ref[...] = h.reshape(nb, SAMPLE_PAD, w)[:, SAMPLE_PAD - 1:SAMPLE_PAD, :]


def _lru(x, cv0, h0, conv_w, w_gates, vecs, valid, layer, state_layer):
    b, t, _ = x.shape
    out_shape = [jax.ShapeDtypeStruct((b, t, LRU_WIDTH), F32), jax.ShapeDtypeStruct((b, 1, LRU_WIDTH), F32)]
    if t == SAMPLE_PAD:
        nb = min(LRU_SHORT_SEQS, b)
        seq = lambda r, w_: pl.BlockSpec((nb, r, w_), lambda i: (i, 0, 0))
        sseq = lambda r, w_: pl.BlockSpec((None, nb, r, w_), lambda i: (state_layer, i, 0, 0))
        wspec = lambda s: _layer_spec(s, layer, 1)
        return pl.pallas_call(
            functools.partial(_lru_short_kernel, valid),
            grid=(b // nb,),
            in_specs=[seq(SAMPLE_PAD, 2 * LRU_WIDTH), sseq(8, LRU_WIDTH), sseq(1, LRU_WIDTH),
                      wspec((CONV_W, LRU_WIDTH)), wspec((LRU_WIDTH, 2 * LRU_WIDTH)), wspec((4, LRU_WIDTH))],
            out_specs=[seq(SAMPLE_PAD, LRU_WIDTH), seq(1, LRU_WIDTH)],
            out_shape=out_shape,
            compiler_params=_cparams(("arbitrary",)),
            name="lru_short",
        )(x, cv0, h0, conv_w, w_gates, vecs)
    tc = min(LRU_TILE, t)
    wspec = lambda s: _layer_spec(s, layer, 2)
    return pl.pallas_call(
        functools.partial(_lru_kernel, tc),
        grid=(b, t // tc),
        in_specs=[
            pl.BlockSpec((None, tc, 2 * LRU_WIDTH), lambda i, j: (i, j, 0)),
            pl.BlockSpec((None, None, 8, LRU_WIDTH), lambda i, j: (state_layer, i, 0, 0)),
            pl.BlockSpec((None, None, 1, LRU_WIDTH), lambda i, j: (state_layer, i, 0, 0)),
            wspec((CONV_W, LRU_WIDTH)), wspec((LRU_WIDTH, 2 * LRU_WIDTH)), wspec((4, LRU_WIDTH)),
        ],
        out_specs=[
            pl.BlockSpec((None, tc, LRU_WIDTH), lambda i, j: (i, j, 0)),
            pl.BlockSpec((None, 1, LRU_WIDTH), lambda i, j: (i, 0, 0)),
        ],
        out_shape=out_shape,
        scratch_shapes=[pltpu.VMEM((tc + 8, LRU_WIDTH), F32), pltpu.VMEM((1, LRU_WIDTH), F32)],
        compiler_params=_cparams(("arbitrary", "arbitrary")),
        name="lru",
    )(x, cv0, h0, conv_w, w_gates, vecs)


def _post_kernel(x_ref, oa_ref, ob_ref, oc_ref, p_ref, wo_ref, wg_ref, wu_ref, wd_ref, pg_ref, pw_ref,
                 vec_ref, y_ref):
    mix = (jnp.dot(oa_ref[...].astype(BF16), wo_ref[0:512, :], preferred_element_type=F32)
           + jnp.dot(ob_ref[...].astype(BF16), wo_ref[512:768, :], preferred_element_type=F32)
           + jnp.dot(oc_ref[...].astype(BF16), wo_ref[768:1024, :], preferred_element_type=F32)
           + vec_ref[0:1, :])
    x = x_ref[...] + _rmsnorm(mix, vec_ref[1:2, :])
    f = _rmsnorm(x, vec_ref[2:3, :]).astype(BF16)
    acc = None
    for lo in range(0, D_FF, FF_CHUNK):
        gate = jnp.dot(f, wg_ref[:, lo:lo + FF_CHUNK], preferred_element_type=F32)
        up = jnp.dot(f, wu_ref[:, lo:lo + FF_CHUNK], preferred_element_type=F32)
        hid = (gate * _sigmoid(gate) * up).astype(BF16)
        part = jnp.dot(hid, wd_ref[lo:lo + FF_CHUNK, :], preferred_element_type=F32)
        acc = part if acc is None else acc + part
    x2 = x + _rmsnorm(acc, vec_ref[3:4, :])
    gate = _sigmoid(jnp.dot(x2.astype(BF16), pg_ref[...], preferred_element_type=F32))
    emb = jnp.dot(p_ref[...].astype(BF16), pw_ref[...], preferred_element_type=F32)
    y_ref[...] = x2 + gate * emb


def _post(x, oa, ob, oc, p, wo, wg, wu, wd, pg, pw, vecs, layer):
    m = x.shape[0]
    tm = min(ROW_TILE, m)
    row = lambda w_: pl.BlockSpec((tm, w_), lambda i: (i, 0))
    wspec = lambda s: _layer_spec(s, layer, 1, pipeline_mode=pl.Buffered(1))
    return pl.pallas_call(
        _post_kernel,
        grid=(m // tm,),
        in_specs=[row(D_MODEL), row(512), row(256), row(256),
                  pl.BlockSpec((None, tm, PLE_DIM), lambda i: (layer, i, 0)),
                  wspec((D_MODEL, D_MODEL)), wspec((D_MODEL, D_FF)), wspec((D_MODEL, D_FF)),
                  wspec((D_FF, D_MODEL)), wspec((D_MODEL, D_MODEL)), wspec((PLE_DIM, D_MODEL)),
                  wspec((4, D_MODEL))],
        out_specs=row(D_MODEL),
        out_shape=jax.ShapeDtypeStruct((m, D_MODEL), F32),
        compiler_params=_cparams(("arbitrary",)),
        name="post",
    )(x, oa, ob, oc, p, wo, wg, wu, wd, pg, pw, vecs)


def _block_diag(w):
    nl, nb, n, _ = w.shape
    eye = jnp.eye(nb, dtype=w.dtype)
    return (eye[None, :, None, :, None] * w[:, :, :, None, :]).reshape(nl, nb * n, nb * n)


def _layer(x, p, state, wkv_all, wts, layer, sample):
    b, t, _ = x.shape
    m = b * t
    x2 = x.reshape(m, D_MODEL)
    q, kv, rw, lr = _in_proj(x2, wts['norm_mix_pre'], wts['w_in'], wts['b_in'], layer)
    q = q.reshape(b, t, 512)
    kv = kv.reshape(b, t, 256)
    rw = rw.reshape(b, t, RWKV_PROJ)
    lr = lr.reshape(b, t, 2 * LRU_WIDTH)
    ck, cv, sh0, st0, cv0, h0 = state
    state_layer = layer if sample else 0

    if sample:
        o_a, nk, nv = _attn_sample(q, kv, ck, cv, wts['attn_sinks'], layer)
        nk = nk.reshape(b, WINDOW, N_KV_HEADS, HEAD_DIM)
        nv = nv.reshape(b, WINDOW, N_KV_HEADS, HEAD_DIM)
    else:
        o_a = _attn_prompt(q, kv, wts['attn_sinks'], layer)
        nk = kv[:, t - WINDOW:, 0:KV_WIDTH].reshape(b, WINDOW, N_KV_HEADS, HEAD_DIM)
        nv = kv[:, t - WINDOW:, KV_WIDTH:].reshape(b, WINDOW, N_KV_HEADS, HEAD_DIM)

    tp = SAMPLE_PAD * (-(-t // SAMPLE_PAD))
    pad = lambda z: z if tp == t else jnp.pad(z, ((0, 0), (0, tp - t), (0, 0)))

    o_b, wkv_all = _rwkv(pad(rw), sh0, st0, wts['rwkv_mu'], wts['rwkv_lora'], wts['rwkv_g_up'],
                         wts['rwkv_vecs'], wts['ones_blk'], wkv_all, t, layer, state_layer)
    nsh = rw[:, t - 1, :]

    o_c, nh = _lru(pad(lr), cv0, h0, wts['lru_conv_w'], wts['lru_w_gates'], wts['lru_vecs'], t, layer,
                   state_layer)
    nconv = lr[:, t - (CONV_W - 1):, 0:LRU_WIDTH]
    nh = nh.reshape(b, LRU_WIDTH)

    x2 = _post(x2, o_a.reshape(m, 512), o_b[:, :t].reshape(m, 256), o_c[:, :t].reshape(m, 256), p,
               wts['w_out'], wts['ffn_w_gate'], wts['ffn_w_up'], wts['ffn_w_down'],
               wts['ple_gate_w'], wts['ple_w'], wts['post_vecs'], layer)
    return x2.reshape(b, t, D_MODEL), (nk, nv, nsh, nconv, nh), wkv_all


def kernel(x_prompt, x_sample, cache_k, cache_v, state_shift, state_wkv, state_conv, state_lru,
           p_prompt, p_sample, norm_mix_pre, norm_mix_post, norm_ffn_pre, norm_ffn_post,
           w_in, b_in, attn_sinks, rwkv_mu, rwkv_w0, rwkv_w_up, rwkv_a0, rwkv_a_up, rwkv_g_up,
           rwkv_k_k, rwkv_k_a, rwkv_r_k, rwkv_ln_w, rwkv_ln_b, lru_conv_w, lru_conv_b,
           lru_w_a, lru_b_a, lru_w_i, lru_b_i, lru_L, w_out, b_out, ffn_w_gate, ffn_w_up,
           ffn_w_down, ple_w, ple_gate_w):
    nl = DEPTH
    bp, tp_, _ = x_prompt.shape
    bs, ts, _ = x_sample.shape
    head_id = jnp.arange(RWKV_WIDTH) // RWKV_HEAD
    zeros_w = jnp.zeros((nl, 64, RWKV_WIDTH), F32)
    wts = dict(
        norm_mix_pre=norm_mix_pre[:, None, :], w_in=w_in.astype(BF16), b_in=b_in[:, None, :],
        attn_sinks=attn_sinks,
        rwkv_mu=rwkv_mu[:, None, :],
        rwkv_lora=jnp.concatenate([jnp.concatenate([rwkv_w_up, zeros_w], axis=2),
                                   jnp.concatenate([zeros_w, rwkv_a_up], axis=2)], axis=1).astype(BF16),
        rwkv_g_up=rwkv_g_up.astype(BF16),
        rwkv_vecs=jnp.stack([rwkv_w0, rwkv_a0, rwkv_k_k, rwkv_k_a, rwkv_r_k.reshape(nl, RWKV_WIDTH),
                             rwkv_ln_w, rwkv_ln_b, jnp.zeros((nl, RWKV_WIDTH), F32)], axis=1),
        ones_blk=(head_id[:, None] == head_id[None, :]).astype(BF16),
        lru_conv_w=lru_conv_w,
        lru_w_gates=jnp.concatenate([_block_diag(lru_w_a), _block_diag(lru_w_i)], axis=2).astype(BF16),
        lru_vecs=jnp.stack([lru_conv_b, lru_b_a, lru_b_i, lru_L], axis=1),
        w_out=w_out.astype(BF16), ffn_w_gate=ffn_w_gate.astype(BF16), ffn_w_up=ffn_w_up.astype(BF16),
        ffn_w_down=ffn_w_down.astype(BF16), ple_gate_w=ple_gate_w.astype(BF16), ple_w=ple_w.astype(BF16),
        post_vecs=jnp.stack([b_out, norm_mix_post, norm_ffn_pre, norm_ffn_post], axis=1),
    )
    st_p = (None, None,
            jnp.zeros((1, bp, 1, RWKV_PROJ), F32),
            jnp.zeros((1, bp, RWKV_HEADS, RWKV_HEAD, RWKV_HEAD), F32),
            jnp.zeros((1, bp, 8, LRU_WIDTH), F32),
            jnp.zeros((1, bp, 1, LRU_WIDTH), F32))
    st_s = (cache_k.reshape(nl, bs, WINDOW, KV_WIDTH), cache_v.reshape(nl, bs, WINDOW, KV_WIDTH),
            state_shift.reshape(nl, bs, 1, RWKV_PROJ),
            state_wkv,
            jnp.pad(state_conv, ((0, 0), (0, 0), (8 - (CONV_W - 1), 0), (0, 0))),
            state_lru.reshape(nl, bs, 1, LRU_WIDTH))
    pp = p_prompt.reshape(nl, bp * tp_, PLE_DIM)
    ps = p_sample.reshape(nl, bs * ts, PLE_DIM)

    xp, xs = x_prompt, x_sample
    wkv_p = jnp.zeros((nl, bp, RWKV_HEADS, RWKV_HEAD, RWKV_HEAD), F32)
    wkv_s = jnp.zeros((nl, bs, RWKV_HEADS, RWKV_HEAD, RWKV_HEAD), F32)
    new_p, new_s = [], []
    for i in range(nl):
        xp, sp, wkv_p = _layer(xp, pp, st_p, wkv_p, wts, i, False)
        xs, ss, wkv_s = _layer(xs, ps, st_s, wkv_s, wts, i, True)
        new_p.append(sp)
        new_s.append(ss)

    def stk(lst, j):
        return jnp.stack([s[j] for s in lst], axis=0)

    return (xp, xs,
            stk(new_p, 0), stk(new_p, 1), stk(new_p, 2), wkv_p, stk(new_p, 3), stk(new_p, 4),
            stk(new_s, 0), stk(new_s, 1), stk(new_s, 2), wkv_s, stk(new_s, 3), stk(new_s, 4))
```

```python
import functools

import jax
import jax.numpy as jnp
from jax import lax
from jax.experimental import pallas as pl
from jax.experimental.pallas import tpu as pltpu

F32 = jnp.float32
BF16 = jnp.bfloat16

D_MODEL = 1024
DEPTH = 4
HEAD_DIM = 64
ATTN_WIDTH = 512
N_HEADS = 8
N_KV_HEADS = 2
GQA_GROUP = 4
KV_WIDTH = 128
WINDOW = 128
RWKV_WIDTH = 256
RWKV_HEADS = 4
RWKV_HEAD = 64
RWKV_PROJ = 1024
LRU_WIDTH = 256
CONV_W = 4
LRU_C = 8.0
D_FF = 2816
PLE_DIM = 256
RMS_EPS = 1e-6
GN_EPS = 64e-5
IN_COLS = 2304

ROW_TILE = 512
FF_CHUNK = 1408
RWKV_CHUNK = 64
RWKV_SEQS = 4
LRU_TILE = 512
ATTN_SAMPLE_SEQS = 8
ATTN_Q_BLOCKS = 4
VMEM_LIMIT = 56 * 1024 * 1024


def _cparams(sem):
    return pltpu.CompilerParams(dimension_semantics=sem, vmem_limit_bytes=VMEM_LIMIT)


def _layer_spec(shape, layer, nidx, **kw):
    zeros = (0,) * len(shape)
    if nidx == 1:
        return pl.BlockSpec((None,) + tuple(shape), lambda i: (layer,) + zeros, **kw)
    return pl.BlockSpec((None,) + tuple(shape), lambda i, j: (layer,) + zeros, **kw)


def _bdot(a, b):
    return jnp.dot(a.astype(BF16), b.astype(BF16), preferred_element_type=F32)


def _hi_lo_rows(x):
    hi = x.astype(BF16)
    lo = (x - hi.astype(F32)).astype(BF16)
    return jnp.concatenate([hi, lo], axis=0)


def _block_sums(x, ones_blk):
    rows = x.shape[0]
    res = jnp.dot(_hi_lo_rows(x), ones_blk, preferred_element_type=F32)
    return res[0:rows] + res[rows:2 * rows]


def _rmsnorm(x, g):
    ms = jnp.mean(x * x, axis=-1, keepdims=True)
    return x * lax.rsqrt(ms + RMS_EPS) * g


def _softplus(x):
    return jnp.maximum(x, 0.0) + jnp.log1p(jnp.exp(-jnp.abs(x)))


def _sigmoid(x):
    return 1.0 / (1.0 + jnp.exp(-x))


def _gelu_tanh(x):
    return 0.5 * x * (1.0 + jnp.tanh(0.7978845608028654 * (x + 0.044715 * (x * x * x))))


def _in_kernel(x_ref, g_ref, w_ref, b_ref, q_ref, kv_ref, rw_ref, lr_ref):
    h = _rmsnorm(x_ref[...], g_ref[...]).astype(BF16)
    for ref, lo, hi in ((q_ref, 0, 512), (kv_ref, 512, 768), (rw_ref, 768, 1792), (lr_ref, 1792, 2304)):
        ref[...] = jnp.dot(h, w_ref[:, lo:hi], preferred_element_type=F32) + b_ref[:, lo:hi]


def _in_proj(x, g, w, b, layer):
    m = x.shape[0]
    tm = min(ROW_TILE, m)
    row = lambda w_: pl.BlockSpec((tm, w_), lambda i: (i, 0))
    return pl.pallas_call(
        _in_kernel,
        grid=(m // tm,),
        in_specs=[row(D_MODEL), _layer_spec((1, D_MODEL), layer, 1),
                  _layer_spec((D_MODEL, IN_COLS), layer, 1), _layer_spec((1, IN_COLS), layer, 1)],
        out_specs=[row(512), row(256), row(1024), row(512)],
        out_shape=[jax.ShapeDtypeStruct((m, n), F32) for n in (512, 256, 1024, 512)],
        compiler_params=_cparams(("arbitrary",)),
        name="in_proj",
    )(x, g, w, b)


def _attn_prompt_kernel(layer, nq, sink_ref, q_ref, kvp_ref, kvc_ref, o_ref):
    j = pl.program_id(1)
    log2e = 1.4426950408889634
    q = q_ref[...] * (HEAD_DIM ** -0.5 * log2e)
    kv = jnp.concatenate([kvp_ref[...], kvc_ref[...]], axis=0)
    kj = lax.broadcasted_iota(jnp.int32, (2 * WINDOW, WINDOW), 0)
    qi = lax.broadcasted_iota(jnp.int32, (2 * WINDOW, WINDOW), 1) + WINDOW
    d = qi - kj
    band = (d >= 0) & (d <= WINDOW)
    first = band & ((j > 0) | (kj >= WINDOW))
    nt = (((1,), (1,)), ((), ()))
    keys = kv[:, 0:KV_WIDTH].astype(BF16)
    v_t = kv[:, KV_WIDTH:2 * KV_WIDTH].T
    ones_rows = (lax.broadcasted_iota(jnp.int32, (8, 2 * WINDOW), 0) == 0).astype(F32)
    scores = {}
    for blk in range(nq):
        qb = q[blk * WINDOW:(blk + 1) * WINDOW]
        for g in range(N_KV_HEADS):
            qg = jnp.concatenate([qb[:, (g * GQA_GROUP + hh) * HEAD_DIM:(g * GQA_GROUP + hh + 1) * HEAD_DIM]
                                  for hh in range(GQA_GROUP)], axis=0).astype(BF16)
            k_ext = keys[blk * WINDOW:(blk + 2) * WINDOW, g * HEAD_DIM:(g + 1) * HEAD_DIM]
            scores[blk, g] = lax.dot_general(k_ext, qg, nt, preferred_element_type=F32)
    probs, sink_terms = {}, {}
    for blk in range(nq):
        mask = first if blk == 0 else band
        for g in range(N_KV_HEADS):
            pg, sg = [], []
            for hh in range(GQA_GROUP):
                s = jnp.where(mask, scores[blk, g][:, hh * WINDOW:(hh + 1) * WINDOW], -1e30)
                sink = sink_ref[layer, g * GQA_GROUP + hh] * log2e
                m = jnp.maximum(jnp.max(s, axis=0, keepdims=True), sink)
                pg.append(jnp.exp2(s - m).astype(BF16))
                sg.append(jnp.exp2(sink - m))
            probs[blk, g] = jnp.concatenate(pg, axis=1)
            sink_terms[blk, g] = jnp.concatenate(sg, axis=1)
    for blk in range(nq):
        outs = []
        for g in range(N_KV_HEADS):
            v_aug = jnp.concatenate([v_t[g * HEAD_DIM:(g + 1) * HEAD_DIM, blk * WINDOW:(blk + 2) * WINDOW],
                                     ones_rows], axis=0).astype(BF16)
            og = jnp.dot(v_aug, probs[blk, g], preferred_element_type=F32)
            den = og[HEAD_DIM:HEAD_DIM + 1, :] + sink_terms[blk, g]
            og = og[0:HEAD_DIM, :] * (1.0 / den)
            outs.extend(og[:, hh * WINDOW:(hh + 1) * WINDOW] for hh in range(GQA_GROUP))
        o_ref[blk * WINDOW:(blk + 1) * WINDOW, :] = jnp.concatenate(outs, axis=0).T


def _attn_prompt(q, kv, sinks, layer):
    b, t, _ = q.shape
    nq = ATTN_Q_BLOCKS
    tq = nq * WINDOW
    return pl.pallas_call(
        functools.partial(_attn_prompt_kernel, layer, nq),
        grid=(b, t // tq),
        in_specs=[
            pl.BlockSpec(memory_space=pltpu.SMEM),
            pl.BlockSpec((None, tq, ATTN_WIDTH), lambda i, j: (i, j, 0)),
            pl.BlockSpec((None, WINDOW, 2 * KV_WIDTH), lambda i, j: (i, jnp.maximum(j * nq - 1, 0), 0)),
            pl.BlockSpec((None, tq, 2 * KV_WIDTH), lambda i, j: (i, j, 0)),
        ],
        out_specs=pl.BlockSpec((None, tq, ATTN_WIDTH), lambda i, j: (i, j, 0)),
        out_shape=jax.ShapeDtypeStruct((b, t, ATTN_WIDTH), F32),
        compiler_params=_cparams(("arbitrary", "arbitrary")),
        name="attn_prompt",
    )(sinks, q, kv, kv)


def _attn_sample_kernel(layer, tn, sink_ref, q_ref, kvn_ref, ck_ref, cv_ref, nk_in_ref, nv_in_ref,
                        o_ref, nk_ref, nv_ref):
    del nk_in_ref, nv_in_ref
    bb = q_ref.shape[0]
    rows = GQA_GROUP * tn
    kvn = kvn_ref[...]
    row = lax.broadcasted_iota(jnp.int32, (1, rows, 1), 1)
    tok = row % tn
    col = lax.broadcasted_iota(jnp.int32, (1, 1, WINDOW), 2)
    cmask = col >= tok
    for g in range(N_KV_HEADS):
        k_t = ck_ref[:, g]
        v_t = cv_ref[:, g]
        qg = q_ref[:, g] * (HEAD_DIM ** -0.5)
        qg_b = qg.astype(BF16).astype(F32)
        sc = jnp.einsum('bqd,bdw->bqw', qg.astype(BF16), k_t.astype(BF16), preferred_element_type=F32)
        sc = jnp.where(cmask, sc, -1e30)
        sink = jnp.zeros((1, rows, 1), F32)
        for hh in range(GQA_GROUP):
            sink = jnp.where(row // tn == hh, sink_ref[layer, g * GQA_GROUP + hh], sink)
        m = jnp.maximum(jnp.max(sc, axis=-1, keepdims=True), sink)
        sn = []
        for jn in range(tn):
            kn = kvn[:, g * tn + jn:g * tn + jn + 1, :].astype(BF16).astype(F32)
            s_j = jnp.sum(qg_b * kn, axis=-1, keepdims=True)
            s_j = jnp.where(tok >= jn, s_j, -1e30)
            sn.append(s_j)
            m = jnp.maximum(m, s_j)
        ec = jnp.exp(sc - m)
        den = jnp.sum(ec, axis=-1, keepdims=True) + jnp.exp(sink - m)
        en = [jnp.exp(s_j - m) for s_j in sn]
        for e_j in en:
            den = den + e_j
        inv = 1.0 / den
        o = jnp.einsum('bqw,bdw->bqd', (ec * inv).astype(BF16), v_t.astype(BF16), preferred_element_type=F32)
        for jn in range(tn):
            vn = kvn[:, (N_KV_HEADS + g) * tn + jn:(N_KV_HEADS + g) * tn + jn + 1, :].astype(BF16).astype(F32)
            o = o + (en[jn] * inv).astype(BF16).astype(F32) * vn
        o_ref[:, g] = o
    new_t = kvn.reshape(bb * 4 * tn, HEAD_DIM).T
    lane = lax.broadcasted_iota(jnp.int32, (HEAD_DIM, WINDOW), 1)
    for c_ref, n_ref, which in ((ck_ref, nk_ref, 0), (cv_ref, nv_ref, 1)):
        for g in range(N_KV_HEADS):
            for b in range(bb):
                src = (b * 2 * N_KV_HEADS + which * N_KV_HEADS + g) * tn
                fresh = pltpu.roll(new_t, (WINDOW - tn - src) % WINDOW, 1)
                kept = pltpu.roll(c_ref[b, g], WINDOW - tn, 1)
                n_ref[b, g] = jnp.where(lane >= WINDOW - tn, fresh, kept)


def _attn_sample(q, kv, ck, cv, nk_all, nv_all, sinks, tn, layer):
    b = q.shape[0] // tn
    bb = ATTN_SAMPLE_SEQS
    assert bb * 4 * tn == WINDOW
    rows = GQA_GROUP * tn
    qh = q.reshape(tn, b, N_KV_HEADS, GQA_GROUP, HEAD_DIM).transpose(1, 2, 3, 0, 4)
    qh = qh.reshape(b, N_KV_HEADS, rows, HEAD_DIM)
    kvn = kv.reshape(tn, b, 2 * N_KV_HEADS, HEAD_DIM).transpose(1, 2, 0, 3).reshape(b, 4 * tn, HEAD_DIM)
    cache = pl.BlockSpec((None, bb, N_KV_HEADS, HEAD_DIM, WINDOW), lambda i: (layer, i, 0, 0, 0))
    heads = pl.BlockSpec((bb, N_KV_HEADS, rows, HEAD_DIM), lambda i: (i, 0, 0, 0))
    o, nk_all, nv_all = pl.pallas_call(
        functools.partial(_attn_sample_kernel, layer, tn),
        grid=(b // bb,),
        in_specs=[
            pl.BlockSpec(memory_space=pltpu.SMEM),
            heads,
            pl.BlockSpec((bb, 4 * tn, HEAD_DIM), lambda i: (i, 0, 0)),
            cache, cache,
            pl.BlockSpec(memory_space=pl.ANY), pl.BlockSpec(memory_space=pl.ANY),
        ],
        out_specs=[heads, cache, cache],
        out_shape=[
            jax.ShapeDtypeStruct((b, N_KV_HEADS, rows, HEAD_DIM), F32),
            jax.ShapeDtypeStruct(nk_all.shape, F32),
            jax.ShapeDtypeStruct(nv_all.shape, F32),
        ],
        input_output_aliases={5: 1, 6: 2},
        compiler_params=_cparams(("arbitrary",)),
        name="attn_sample",
    )(sinks, qh, kvn, ck, cv, nk_all, nv_all)
    o = o.reshape(b, N_KV_HEADS, GQA_GROUP, tn, HEAD_DIM).transpose(3, 0, 1, 2, 4)
    return o.reshape(tn * b, ATTN_WIDTH), nk_all, nv_all


def _tile_rows(x, n):
    return jnp.concatenate([x] * n, axis=0)


def _rwkv_kernel(c, nseq, x_ref, sh0_ref, st0_ref, mu_ref, lora_ref, gup_ref, vec_ref, ones_ref,
                 stacked_ref, o_ref, st_ref, last_ref, state_ref):
    del stacked_ref
    hw = RWKV_WIDTH
    n = RWKV_HEAD
    cw = RWKV_HEADS * c

    @pl.when(pl.program_id(1) == 0)
    def _():
        last_ref[...] = sh0_ref[...]
        for s in range(nseq):
            for h in range(RWKV_HEADS):
                state_ref[s, :, h * n:(h + 1) * n] = st0_ref[s, h]

    row = lax.broadcasted_iota(jnp.int32, (c, 1), 0)
    lane128 = lax.broadcasted_iota(jnp.int32, (c, 128), 1)
    tri = jnp.where(lax.broadcasted_iota(jnp.int32, (c, c), 1) <= lax.broadcasted_iota(jnp.int32, (c, c), 0),
                    1.0, 0.0).astype(BF16)
    tri3 = jnp.concatenate([tri, tri, tri], axis=1)
    head_rows = (lax.broadcasted_iota(jnp.int32, (cw, hw), 0) // c
                 == lax.broadcasted_iota(jnp.int32, (cw, hw), 1) // n)
    t_i = lax.broadcasted_iota(jnp.int32, (c, cw), 0)
    s_i = lax.broadcasted_iota(jnp.int32, (c, cw), 1) % c
    strict = s_i < t_i
    incl = s_i <= t_i
    eye_all = jnp.where(s_i == t_i, 1.0, 0.0)
    blk = (lax.broadcasted_iota(jnp.int32, (cw, cw), 0) // c
           == lax.broadcasted_iota(jnp.int32, (cw, cw), 1) // c)
    sblk = (lax.broadcasted_iota(jnp.int32, (hw, hw), 0) // n
            == lax.broadcasted_iota(jnp.int32, (hw, hw), 1) // n)
    masks = (row, lane128, tri3, head_rows, strict, incl, eye_all, blk, sblk)
    chains = [_rwkv_chunk(c, s, masks, x_ref, mu_ref, lora_ref, gup_ref, vec_ref, ones_ref,
                          o_ref, st_ref, last_ref, state_ref) for s in range(nseq)]
    while chains:
        chains = [ch for ch in chains if next(ch, "done") != "done"]


def _rwkv_chunk(c, s, masks, x_ref, mu_ref, lora_ref, gup_ref, vec_ref, ones_ref,
                o_ref, st_ref, last_ref, state_ref):
    row, lane128, tri3, head_rows, strict, incl, eye_all, blk, sblk = masks
    hw = RWKV_WIDTH
    n = RWKV_HEAD
    cw = RWKV_HEADS * c
    x = x_ref[s]
    prev = jnp.where(row == 0, last_ref[s], pltpu.roll(x, 1, 0))
    last_ref[s] = x[c - 1:c, :]
    xs = x + (prev - x) * mu_ref[...]

    r = xs[:, 0:hw]
    k = xs[:, hw:2 * hw]
    v = xs[:, 2 * hw:3 * hw]
    wa = xs[:, 3 * hw:3 * hw + 128]
    xg = xs[:, 3 * hw + 128:]
    lora_in = jnp.where(lane128 < 64, jnp.tanh(wa), wa)
    lora = _bdot(lora_in, lora_ref[...])
    w0, a0, k_k, k_a, r_k, ln_w, ln_b = (vec_ref[i:i + 1, :] for i in range(7))
    g = _bdot(_sigmoid(xg), gup_ref[...])
    ones_blk = ones_ref[...]
    kk = k * k_k
    ss = _block_sums(kk * kk, ones_blk)
    yield
    w_log = -_softplus(-(w0 + lora[:, 0:hw])) - 0.5
    logw = -jnp.exp(w_log)
    a = _sigmoid(a0 + lora[:, hw:2 * hw])
    kk = kk * lax.rsqrt(jnp.maximum(ss, 1e-24))
    k2 = k * (1.0 + (a - 1.0) * k_a)
    bv = kk * a

    pieces = []
    rem = logw
    for _ in range(3):
        p = rem.astype(BF16)
        rem = rem - p.astype(F32)
        pieces.append(p)
    cum = jnp.dot(tri3, jnp.concatenate(pieces, axis=0), preferred_element_type=F32)
    bonus_sum = _block_sums(r * k2 * r_k, ones_blk)
    yield
    e_inc = jnp.exp(cum)
    e_exc = jnp.exp(cum - logw)
    e_inv = jnp.exp(-cum)
    kq = kk * e_exc
    rq = r * e_inc
    kd = k2 * e_inv
    bd = bv * e_inv
    w_end = e_inc[c - 1:c, :]
    kend = kd * w_end
    bend = bd * w_end

    def expand(z):
        return jnp.where(head_rows, _tile_rows(z, RWKV_HEADS), 0.0)

    lhs = jnp.concatenate([kq, rq], axis=0).astype(BF16)
    rhs = jnp.concatenate([expand(bd), expand(kd)], axis=0).astype(BF16)
    nt = (((1,), (1,)), ((), ()))
    prod = lax.dot_general(lhs, rhs, nt, preferred_element_type=F32)
    st = state_ref[s]
    st_d = jnp.where(sblk, _tile_rows(st, RWKV_HEADS), 0.0)
    from_state = lax.dot_general(lhs, st_d.astype(BF16), nt, preferred_element_type=F32)
    yield
    a_b = jnp.where(strict, prod[0:c, 0:cw], 0.0)
    a_k = jnp.where(strict, prod[0:c, cw:2 * cw], 0.0)
    p_b = jnp.where(incl, prod[c:2 * c, 0:cw], 0.0)
    p_k = jnp.where(incl, prod[c:2 * c, cw:2 * cw], 0.0)

    def bdiag(z):
        return jnp.where(blk, _tile_rows(z, RWKV_HEADS), 0.0)

    xm = -a_b
    tinv = eye_all + xm
    v_d = expand(v)
    from_v = _bdot(jnp.concatenate([a_k, p_k], axis=0), v_d)
    rhs_u = from_state[0:c] + from_v[0:c]
    y_part = from_state[c:2 * c] + from_v[c:2 * c]
    levels = c.bit_length() - 1
    xm = _bdot(xm, bdiag(xm))
    yield
    for _ in range(1, levels - 1):
        both = _bdot(jnp.concatenate([xm, tinv], axis=0), bdiag(xm))
        yield
        xm = both[0:c]
        tinv = tinv + both[c:2 * c]
    tinv = tinv + _bdot(tinv, bdiag(xm))
    yield
    u = _bdot(tinv, expand(rhs_u))
    yield
    y = y_part - _bdot(p_b, expand(u))

    lhs_s = jnp.concatenate([v, u], axis=0)
    rhs_s = jnp.concatenate([kend, -bend], axis=0)
    tn_dims = (((0,), (0,)), ((), ()))
    gm = lax.dot_general(lhs_s.astype(BF16), rhs_s.astype(BF16), tn_dims, preferred_element_type=F32)
    yield
    gm = jnp.where(sblk, gm, 0.0)
    st_new = st * w_end + (gm[0:n] + gm[n:2 * n] + (gm[2 * n:3 * n] + gm[3 * n:4 * n]))
    state_ref[s] = st_new
    for h in range(RWKV_HEADS):
        st_ref[s, h] = st_new[:, h * n:(h + 1) * n]

    mean = _block_sums(y, ones_blk) * (1.0 / n)
    yield
    yc = y - mean
    var = _block_sums(yc * yc, ones_blk) * (1.0 / n)
    yield
    yn = yc * lax.rsqrt(var + GN_EPS) * ln_w + ln_b
    o_ref[s] = (yn + bonus_sum * v) * g


def _rwkv(x, sh0, st0, mu, lora_w, g_up, vecs, ones_blk, stacked, layer):
    b, t, _ = x.shape
    c = RWKV_CHUNK
    nseq = min(RWKV_SEQS, b)
    wspec = lambda s: _layer_spec(s, layer, 2)
    hstate = (nseq, RWKV_HEADS, RWKV_HEAD, RWKV_HEAD)
    return pl.pallas_call(
        functools.partial(_rwkv_kernel, c, nseq),
        grid=(b // nseq, t // c),
        in_specs=[
            pl.BlockSpec((nseq, c, RWKV_PROJ), lambda i, j: (i, j, 0)),
            pl.BlockSpec((nseq, 1, RWKV_PROJ), lambda i, j: (i, 0, 0)),
            pl.BlockSpec(hstate, lambda i, j: (i, 0, 0, 0)),
            wspec((1, RWKV_PROJ)), wspec((128, 512)), wspec((128, RWKV_WIDTH)), wspec((8, RWKV_WIDTH)),
            pl.BlockSpec((RWKV_WIDTH, RWKV_WIDTH), lambda i, j: (0, 0)),
            pl.BlockSpec(memory_space=pl.ANY),
        ],
        out_specs=[
            pl.BlockSpec((nseq, c, RWKV_WIDTH), lambda i, j: (i, j, 0)),
            pl.BlockSpec((None,) + hstate, lambda i, j: (layer, i, 0, 0, 0)),
        ],
        out_shape=[
            jax.ShapeDtypeStruct((b, t, RWKV_WIDTH), F32),
            jax.ShapeDtypeStruct(stacked.shape, F32),
        ],
        scratch_shapes=[pltpu.VMEM((nseq, 1, RWKV_PROJ), F32), pltpu.VMEM((nseq, RWKV_HEAD, RWKV_WIDTH), F32)],
        input_output_aliases={8: 1},
        compiler_params=_cparams(("arbitrary", "arbitrary")),
        name="rwkv",
    )(x, sh0, st0, mu, lora_w, g_up, vecs, ones_blk, stacked)


def _rwkv_step_kernel(tn, nb, x_ref, sh0_ref, st0_ref, mu_ref, lora_ref, gup_ref, vec_ref, ones_ref,
                      stacked_ref, o_ref, st_ref, vt_ref, nat_ref, yt_ref):
    del stacked_ref
    hw = RWKV_WIDTH
    n = RWKV_HEAD
    h = pl.program_id(0)
    rows = tn * nb

    @pl.when(h == 0)
    def _():
        x = x_ref[...]
        prev = jnp.concatenate([sh0_ref[...], x[0:rows - nb]], axis=0)
        xs = x + (prev - x) * mu_ref[...]
        r = xs[:, 0:hw]
        k = xs[:, hw:2 * hw]
        v = xs[:, 2 * hw:3 * hw]
        wa = xs[:, 3 * hw:3 * hw + 128]
        xg = xs[:, 3 * hw + 128:]
        lane128 = lax.broadcasted_iota(jnp.int32, (rows, 128), 1)
        lora = _bdot(jnp.where(lane128 < 64, jnp.tanh(wa), wa), lora_ref[...])
        w0, a0, k_k, k_a, r_k = (vec_ref[i:i + 1, :] for i in range(5))
        ones_blk = ones_ref[...]
        w_log = -_softplus(-(w0 + lora[:, 0:hw])) - 0.5
        decay = jnp.exp(-jnp.exp(w_log))
        a = _sigmoid(a0 + lora[:, hw:2 * hw])
        kk = k * k_k
        kk = kk * lax.rsqrt(jnp.maximum(_block_sums(kk * kk, ones_blk), 1e-24))
        k2 = k * (1.0 + (a - 1.0) * k_a)
        nat_ref[0] = _bdot(_sigmoid(xg), gup_ref[...])
        nat_ref[1] = v
        nat_ref[2] = _block_sums(r * k2 * r_k, ones_blk)
        for i, z in enumerate((r, decay, k2, v, kk, kk * a)):
            for t in range(tn):
                for half in range(hw // 128):
                    vt_ref[i, t, half * 128:(half + 1) * 128, :] = (
                        z[t * nb:(t + 1) * nb, half * 128:(half + 1) * 128].T)

    base = pl.multiple_of(h * n, n)

    def value_row(vi, carry):
        s = st0_ref[vi]
        for t in range(tn):
            r_t, w_t, k_t, _, kk_t, b_t = (vt_ref[i, t, pl.ds(base, n), :] for i in range(6))
            v_row = vt_ref[3, t, pl.ds(base + vi, 1), :]
            u = jnp.sum(s * kk_t, axis=0, keepdims=True)
            s = s * w_t - u * b_t + v_row * k_t
            yt_ref[t, pl.ds(base + vi, 1), :] = jnp.sum(s * r_t, axis=0, keepdims=True)
        st_ref[vi] = s
        return carry

    lax.fori_loop(0, n, value_row, 0, unroll=4)

    @pl.when(h == RWKV_HEADS - 1)
    def _():
        ones_blk = ones_ref[...]
        ln_w = vec_ref[5:6, :]
        ln_b = vec_ref[6:7, :]
        y = jnp.concatenate(
            [jnp.concatenate([yt_ref[t, half * 128:(half + 1) * 128, :].T for half in range(hw // 128)], axis=1)
             for t in range(tn)], axis=0)
        mean = _block_sums(y, ones_blk) * (1.0 / n)
        yc = y - mean
        var = _block_sums(yc * yc, ones_blk) * (1.0 / n)
        yn = yc * lax.rsqrt(var + GN_EPS) * ln_w + ln_b
        o_ref[...] = (yn + nat_ref[2] * nat_ref[1]) * nat_ref[0]


def _rwkv_sample(x, sh0, st0, mu, lora_w, g_up, vecs, ones_blk, stacked, tn, layer):
    rows = x.shape[0]
    nb = rows // tn
    wspec = lambda s: _layer_spec(s, layer, 1)
    head_state = pl.BlockSpec((None, None, RWKV_HEAD, RWKV_HEAD, nb), lambda h: (layer, h, 0, 0, 0))
    return pl.pallas_call(
        functools.partial(_rwkv_step_kernel, tn, nb),
        grid=(RWKV_HEADS,),
        in_specs=[
            pl.BlockSpec((rows, RWKV_PROJ), lambda h: (0, 0)),
            wspec((nb, RWKV_PROJ)), head_state,
            wspec((1, RWKV_PROJ)), wspec((128, 512)), wspec((128, RWKV_WIDTH)), wspec((8, RWKV_WIDTH)),
            pl.BlockSpec((RWKV_WIDTH, RWKV_WIDTH), lambda h: (0, 0)),
            pl.BlockSpec(memory_space=pl.ANY),
        ],
        out_specs=[pl.BlockSpec((rows, RWKV_WIDTH), lambda h: (0, 0)), head_state],
        out_shape=[jax.ShapeDtypeStruct((rows, RWKV_WIDTH), F32), jax.ShapeDtypeStruct(stacked.shape, F32)],
        scratch_shapes=[pltpu.VMEM((6, tn, RWKV_WIDTH, nb), F32), pltpu.VMEM((3, rows, RWKV_WIDTH), F32),
                        pltpu.VMEM((tn, RWKV_WIDTH, nb), F32)],
        input_output_aliases={8: 1},
        compiler_params=_cparams(("arbitrary",)),
        name="rwkv_step",
    )(x, sh0, st0, mu, lora_w, g_up, vecs, ones_blk, stacked)


def _lru_scan(xc, t, period, wg_ref, vec_ref):
    w = LRU_WIDTH
    _, b_a, b_i, lam = (vec_ref[i:i + 1, :] for i in range(4))
    gates = _bdot(xc, wg_ref[...])
    r = _sigmoid(gates[:, 0:w] + b_a)
    i = _sigmoid(gates[:, w:2 * w] + b_i)
    log_a = LRU_C * r * (-_softplus(-lam))
    a = jnp.exp(log_a)
    th = jnp.tanh(log_a)
    u = jnp.sqrt(-2.0 * th / (1.0 - th)) * (i * xc)
    span = 1
    while span < period:
        ok = t >= span
        a_s = pltpu.roll(a, span, 0)
        u_s = pltpu.roll(u, span, 0)
        u = jnp.where(ok, a * u_s + u, u)
        a = jnp.where(ok, a * a_s, a)
        span *= 2
    return a, u


def _lru_kernel(tc, x_ref, cv0_ref, h0_ref, cw_ref, wg_ref, vec_ref, o_ref, h_ref, ext_ref, hc_ref):
    ti = pl.program_id(1)
    w = LRU_WIDTH

    @pl.when(ti == 0)
    def _():
        ext_ref[0:8, :] = cv0_ref[...]
        hc_ref[...] = h0_ref[...]

    xb = x_ref[:, 0:w]
    gb = x_ref[:, w:2 * w]
    ext_ref[8:8 + tc, :] = xb
    xc = vec_ref[0:1, :] + xb * cw_ref[CONV_W - 1:CONV_W, :]
    for j in range(CONV_W - 1):
        xc = xc + ext_ref[pl.ds(8 - (CONV_W - 1) + j, tc), :] * cw_ref[j:j + 1, :]
    ext_ref[0:8, :] = xb[tc - 8:tc, :]
    row = lax.broadcasted_iota(jnp.int32, (tc, 1), 0)
    a, u = _lru_scan(xc, row, tc, wg_ref, vec_ref)
    h = a * hc_ref[...] + u
    hc_ref[...] = h[tc - 1:tc, :]
    h_ref[...] = h[tc - 1:tc, :]
    o_ref[...] = h * _gelu_tanh(gb)


def _lru_step_kernel(tn, nb, x_ref, cv0_ref, h0_ref, cw_ref, wg_ref, vec_ref, o_ref, h_ref):
    w = LRU_WIDTH
    rows = tn * nb
    xb = x_ref[:, 0:w]
    gb = x_ref[:, w:2 * w]
    ext = jnp.concatenate([cv0_ref[j] for j in range(CONV_W - 1)] + [xb], axis=0)
    xc = vec_ref[0:1, :]
    for j in range(CONV_W):
        xc = xc + ext[j * nb:j * nb + rows] * cw_ref[j:j + 1, :]
    _, b_a, b_i, lam = (vec_ref[i:i + 1, :] for i in range(4))
    gates = _bdot(xc, wg_ref[...])
    r = _sigmoid(gates[:, 0:w] + b_a)
    i = _sigmoid(gates[:, w:2 * w] + b_i)
    log_a = LRU_C * r * (-_softplus(-lam))
    a = jnp.exp(log_a)
    th = jnp.tanh(log_a)
    u = jnp.sqrt(-2.0 * th / (1.0 - th)) * (i * xc)
    gelu = _gelu_tanh(gb)
    h = h0_ref[...]
    for t in range(tn):
        h = a[t * nb:(t + 1) * nb] * h + u[t * nb:(t + 1) * nb]
        o_ref[t * nb:(t + 1) * nb, :] = h * gelu[t * nb:(t + 1) * nb]
    h_ref[...] = h


def _lru_sample(x, cv0, h0, conv_w, w_gates, vecs, tn, layer):
    rows = x.shape[0]
    nb = rows // tn
    wspec = lambda s: _layer_spec(s, layer, 1)
    return pl.pallas_call(
        functools.partial(_lru_step_kernel, tn, nb),
        grid=(1,),
        in_specs=[pl.BlockSpec((rows, 2 * LRU_WIDTH), lambda i: (0, 0)),
                  wspec((CONV_W - 1, nb, LRU_WIDTH)), wspec((nb, LRU_WIDTH)),
                  wspec((CONV_W, LRU_WIDTH)), wspec((LRU_WIDTH, 2 * LRU_WIDTH)), wspec((4, LRU_WIDTH))],
        out_specs=[pl.BlockSpec((rows, LRU_WIDTH), lambda i: (0, 0)), pl.BlockSpec((nb, LRU_WIDTH), lambda i: (0, 0))],
        out_shape=[jax.ShapeDtypeStruct((rows, LRU_WIDTH), F32), jax.ShapeDtypeStruct((nb, LRU_WIDTH), F32)],
        compiler_params=_cparams(("arbitrary",)),
        name="lru_step",
    )(x, cv0, h0, conv_w, w_gates, vecs)


def _lru(x, cv0, h0, conv_w, w_gates, vecs, layer):
    b, t, _ = x.shape
    out_shape = [jax.ShapeDtypeStruct((b, t, LRU_WIDTH), F32), jax.ShapeDtypeStruct((b, 1, LRU_WIDTH), F32)]
    tc = min(LRU_TILE, t)
    wspec = lambda s: _layer_spec(s, layer, 2)
    return pl.pallas_call(
        functools.partial(_lru_kernel, tc),
        grid=(b, t // tc),
        in_specs=[
            pl.BlockSpec((None, tc, 2 * LRU_WIDTH), lambda i, j: (i, j, 0)),
            pl.BlockSpec((None, 8, LRU_WIDTH), lambda i, j: (i, 0, 0)),
            pl.BlockSpec((None, 1, LRU_WIDTH), lambda i, j: (i, 0, 0)),
            wspec((CONV_W, LRU_WIDTH)), wspec((LRU_WIDTH, 2 * LRU_WIDTH)), wspec((4, LRU_WIDTH)),
        ],
        out_specs=[
            pl.BlockSpec((None, tc, LRU_WIDTH), lambda i, j: (i, j, 0)),
            pl.BlockSpec((None, 1, LRU_WIDTH), lambda i, j: (i, 0, 0)),
        ],
        out_shape=out_shape,
        scratch_shapes=[pltpu.VMEM((tc + 8, LRU_WIDTH), F32), pltpu.VMEM((1, LRU_WIDTH), F32)],
        compiler_params=_cparams(("arbitrary", "arbitrary")),
        name="lru",
    )(x, cv0, h0, conv_w, w_gates, vecs)


def _post_kernel(x_ref, oa_ref, ob_ref, oc_ref, p_ref, wo_ref, wg_ref, wu_ref, wd_ref, pg_ref, pw_ref,
                 vec_ref, y_ref):
    mix = (jnp.dot(oa_ref[...].astype(BF16), wo_ref[0:512, :], preferred_element_type=F32)
           + jnp.dot(ob_ref[...].astype(BF16), wo_ref[512:768, :], preferred_element_type=F32)
           + jnp.dot(oc_ref[...].astype(BF16), wo_ref[768:1024, :], preferred_element_type=F32)
           + vec_ref[0:1, :])
    x = x_ref[...] + _rmsnorm(mix, vec_ref[1:2, :])
    f = _rmsnorm(x, vec_ref[2:3, :]).astype(BF16)
    acc = None
    for lo in range(0, D_FF, FF_CHUNK):
        gate = jnp.dot(f, wg_ref[:, lo:lo + FF_CHUNK], preferred_element_type=F32)
        up = jnp.dot(f, wu_ref[:, lo:lo + FF_CHUNK], preferred_element_type=F32)
        hid = (gate * _sigmoid(gate) * up).astype(BF16)
        part = jnp.dot(hid, wd_ref[lo:lo + FF_CHUNK, :], preferred_element_type=F32)
        acc = part if acc is None else acc + part
    x2 = x + _rmsnorm(acc, vec_ref[3:4, :])
    gate = _sigmoid(jnp.dot(x2.astype(BF16), pg_ref[...], preferred_element_type=F32))
    emb = jnp.dot(p_ref[...].astype(BF16), pw_ref[...], preferred_element_type=F32)
    y_ref[...] = x2 + gate * emb


def _post(x, oa, ob, oc, p, wo, wg, wu, wd, pg, pw, vecs, layer):
    m = x.shape[0]
    tm = min(ROW_TILE, m)
    row = lambda w_: pl.BlockSpec((tm, w_), lambda i: (i, 0))
    wspec = lambda s: _layer_spec(s, layer, 1, pipeline_mode=pl.Buffered(1))
    return pl.pallas_call(
        _post_kernel,
        grid=(m // tm,),
        in_specs=[row(D_MODEL), row(512), row(256), row(256),
                  pl.BlockSpec((None, tm, PLE_DIM), lambda i: (layer, i, 0)),
                  wspec((D_MODEL, D_MODEL)), wspec((D_MODEL, D_FF)), wspec((D_MODEL, D_FF)),
                  wspec((D_FF, D_MODEL)), wspec((D_MODEL, D_MODEL)), wspec((PLE_DIM, D_MODEL)),
                  wspec((4, D_MODEL))],
        out_specs=row(D_MODEL),
        out_shape=jax.ShapeDtypeStruct((m, D_MODEL), F32),
        compiler_params=_cparams(("arbitrary",)),
        name="post",
    )(x, oa, ob, oc, p, wo, wg, wu, wd, pg, pw, vecs)


def _block_diag(w):
    nl, nb, n, _ = w.shape
    eye = jnp.eye(nb, dtype=w.dtype)
    return (eye[None, :, None, :, None] * w[:, :, :, None, :]).reshape(nl, nb * n, nb * n)


def _post_layer(x2, o_a, o_b, o_c, p, wts, layer):
    return _post(x2, o_a, o_b, o_c, p, wts['w_out'], wts['ffn_w_gate'], wts['ffn_w_up'], wts['ffn_w_down'],
                 wts['ple_gate_w'], wts['ple_w'], wts['post_vecs'], layer)


def _layer_prompt(x2, b, t, p, zeros, wkv_all, wts, layer):
    m = b * t
    q, kv, rw, lr = _in_proj(x2, wts['norm_mix_pre'], wts['w_in'], wts['b_in'], layer)
    kv = kv.reshape(b, t, 256)
    rw = rw.reshape(b, t, RWKV_PROJ)
    lr = lr.reshape(b, t, 2 * LRU_WIDTH)
    sh0, st0, cv0, h0 = zeros
    o_a = _attn_prompt(q.reshape(b, t, 512), kv, wts['attn_sinks'], layer)
    nk = kv[:, t - WINDOW:, 0:KV_WIDTH].reshape(b, WINDOW, N_KV_HEADS, HEAD_DIM)
    nv = kv[:, t - WINDOW:, KV_WIDTH:].reshape(b, WINDOW, N_KV_HEADS, HEAD_DIM)
    o_b, wkv_all = _rwkv(rw, sh0, st0, wts['rwkv_mu'], wts['rwkv_lora'], wts['rwkv_g_up'],
                         wts['rwkv_vecs'], wts['ones_blk'], wkv_all, layer)
    nsh = rw[:, t - 1, :]
    o_c, nh = _lru(lr, cv0, h0, wts['lru_conv_w'], wts['lru_w_gates'], wts['lru_vecs'], layer)
    nconv = lr[:, t - (CONV_W - 1):, 0:LRU_WIDTH]
    x2 = _post_layer(x2, o_a.reshape(m, 512), o_b.reshape(m, 256), o_c.reshape(m, 256), p, wts, layer)
    return x2, (nk, nv, nsh, nconv, nh.reshape(b, LRU_WIDTH)), wkv_all


def _layer_sample(x2, b, t, p, state, outs, wts, layer):
    ck, cv, sh0, st0, cv0, h0 = state
    nk_all, nv_all, wkv_all = outs
    q, kv, rw, lr = _in_proj(x2, wts['norm_mix_pre'], wts['w_in'], wts['b_in'], layer)
    o_a, nk_all, nv_all = _attn_sample(q, kv, ck, cv, nk_all, nv_all, wts['attn_sinks'], t, layer)
    o_b, wkv_all = _rwkv_sample(rw, sh0, st0, wts['rwkv_mu'], wts['rwkv_lora'], wts['rwkv_g_up'],
                                wts['rwkv_vecs'], wts['ones_blk'], wkv_all, t, layer)
    nsh = rw[(t - 1) * b:, :]
    o_c, nh = _lru_sample(lr, cv0, h0, wts['lru_conv_w'], wts['lru_w_gates'], wts['lru_vecs'], t, layer)
    nconv = lr[(t - (CONV_W - 1)) * b:, 0:LRU_WIDTH].reshape(CONV_W - 1, b, LRU_WIDTH)
    x2 = _post_layer(x2, o_a, o_b, o_c, p, wts, layer)
    return x2, (nsh, nconv, nh), (nk_all, nv_all, wkv_all)


def kernel(x_prompt, x_sample, cache_k, cache_v, state_shift, state_wkv, state_conv, state_lru,
           p_prompt, p_sample, norm_mix_pre, norm_mix_post, norm_ffn_pre, norm_ffn_post,
           w_in, b_in, attn_sinks, rwkv_mu, rwkv_w0, rwkv_w_up, rwkv_a0, rwkv_a_up, rwkv_g_up,
           rwkv_k_k, rwkv_k_a, rwkv_r_k, rwkv_ln_w, rwkv_ln_b, lru_conv_w, lru_conv_b,
           lru_w_a, lru_b_a, lru_w_i, lru_b_i, lru_L, w_out, b_out, ffn_w_gate, ffn_w_up,
           ffn_w_down, ple_w, ple_gate_w):
    nl = DEPTH
    bp, tp_, _ = x_prompt.shape
    bs, ts, _ = x_sample.shape
    head_id = jnp.arange(RWKV_WIDTH) // RWKV_HEAD
    zeros_w = jnp.zeros((nl, 64, RWKV_WIDTH), F32)
    wts = dict(
        norm_mix_pre=norm_mix_pre[:, None, :], w_in=w_in.astype(BF16), b_in=b_in[:, None, :],
        attn_sinks=attn_sinks,
        rwkv_mu=rwkv_mu[:, None, :],
        rwkv_lora=jnp.concatenate([jnp.concatenate([rwkv_w_up, zeros_w], axis=2),
                                   jnp.concatenate([zeros_w, rwkv_a_up], axis=2)], axis=1).astype(BF16),
        rwkv_g_up=rwkv_g_up.astype(BF16),
        rwkv_vecs=jnp.stack([rwkv_w0, rwkv_a0, rwkv_k_k, rwkv_k_a, rwkv_r_k.reshape(nl, RWKV_WIDTH),
                             rwkv_ln_w, rwkv_ln_b, jnp.zeros((nl, RWKV_WIDTH), F32)], axis=1),
        ones_blk=(head_id[:, None] == head_id[None, :]).astype(BF16),
        lru_conv_w=lru_conv_w,
        lru_w_gates=jnp.concatenate([_block_diag(lru_w_a), _block_diag(lru_w_i)], axis=2).astype(BF16),
        lru_vecs=jnp.stack([lru_conv_b, lru_b_a, lru_b_i, lru_L], axis=1),
        w_out=w_out.astype(BF16), ffn_w_gate=ffn_w_gate.astype(BF16), ffn_w_up=ffn_w_up.astype(BF16),
        ffn_w_down=ffn_w_down.astype(BF16), ple_gate_w=ple_gate_w.astype(BF16), ple_w=ple_w.astype(BF16),
        post_vecs=jnp.stack([b_out, norm_mix_post, norm_ffn_pre, norm_ffn_post], axis=1),
    )
    zeros_p = (jnp.zeros((bp, 1, RWKV_PROJ), F32),
               jnp.zeros((bp, RWKV_HEADS, RWKV_HEAD, RWKV_HEAD), F32),
               jnp.zeros((bp, 8, LRU_WIDTH), F32),
               jnp.zeros((bp, 1, LRU_WIDTH), F32))
    st_s = (cache_k.transpose(0, 1, 3, 4, 2), cache_v.transpose(0, 1, 3, 4, 2),
            state_shift,
            state_wkv.transpose(0, 2, 3, 4, 1),
            state_conv.transpose(0, 2, 1, 3),
            state_lru)
    pp = p_prompt.reshape(nl, bp * tp_, PLE_DIM)
    ps = p_sample.transpose(0, 2, 1, 3).reshape(nl, ts * bs, PLE_DIM)

    xp = x_prompt.reshape(bp * tp_, D_MODEL)
    xs = x_sample.transpose(1, 0, 2).reshape(ts * bs, D_MODEL)
    wkv_p = jnp.zeros((nl, bp, RWKV_HEADS, RWKV_HEAD, RWKV_HEAD), F32)
    outs_s = (jnp.zeros((nl, bs, N_KV_HEADS, HEAD_DIM, WINDOW), F32),
              jnp.zeros((nl, bs, N_KV_HEADS, HEAD_DIM, WINDOW), F32),
              jnp.zeros((nl, RWKV_HEADS, RWKV_HEAD, RWKV_HEAD, bs), F32))
    new_p, new_s = [], []
    for i in range(nl):
        xp, sp, wkv_p = _layer_prompt(xp, bp, tp_, pp, zeros_p, wkv_p, wts, i)
        xs, ss, outs_s = _layer_sample(xs, bs, ts, ps, st_s, outs_s, wts, i)
        new_p.append(sp)
        new_s.append(ss)

    def stk(lst, j):
        return jnp.stack([s[j] for s in lst], axis=0)

    nk_s, nv_s, wkv_s = outs_s
    return (xp.reshape(bp, tp_, D_MODEL), xs.reshape(ts, bs, D_MODEL).transpose(1, 0, 2),
            stk(new_p, 0), stk(new_p, 1), stk(new_p, 2), wkv_p, stk(new_p, 3), stk(new_p, 4),
            nk_s.transpose(0, 1, 4, 2, 3), nv_s.transpose(0, 1, 4, 2, 3), stk(new_s, 0),
            wkv_s.transpose(0, 4, 1, 2, 3), stk(new_s, 1).transpose(0, 2, 1, 3), stk(new_s, 2))
```

```python
import functools

import jax
import jax.numpy as jnp
from jax import lax
from jax.experimental import pallas as pl
from jax.experimental.pallas import tpu as pltpu

F32 = jnp.float32
BF16 = jnp.bfloat16

D_MODEL = 1024
DEPTH = 4
HEAD_DIM = 64
ATTN_WIDTH = 512
N_HEADS = 8
N_KV_HEADS = 2
GQA_GROUP = 4
KV_WIDTH = 128
WINDOW = 128
RWKV_WIDTH = 256
RWKV_HEADS = 4
RWKV_HEAD = 64
RWKV_PROJ = 1024
LRU_WIDTH = 256
CONV_W = 4
LRU_C = 8.0
D_FF = 2816
PLE_DIM = 256
RMS_EPS = 1e-6
GN_EPS = 64e-5
IN_COLS = 2304

ROW_TILE = 512
FF_CHUNK = 1408
RWKV_CHUNK = 64
RWKV_SEQS = 4
RWKV_SUBCHUNKS = 4
RWKV_STAGGER = 3
LRU_TILE = 512
ATTN_SAMPLE_SEQS = 8
ATTN_Q_BLOCKS = 4
VMEM_LIMIT = 56 * 1024 * 1024


def _cparams(sem):
    return pltpu.CompilerParams(dimension_semantics=sem, vmem_limit_bytes=VMEM_LIMIT)


def _layer_spec(shape, layer, nidx, **kw):
    zeros = (0,) * len(shape)
    if nidx == 1:
        return pl.BlockSpec((None,) + tuple(shape), lambda i: (layer,) + zeros, **kw)
    return pl.BlockSpec((None,) + tuple(shape), lambda i, j: (layer,) + zeros, **kw)


def _bdot(a, b):
    return jnp.dot(a.astype(BF16), b.astype(BF16), preferred_element_type=F32)


def _hi_lo_rows(x):
    hi = x.astype(BF16)
    lo = (x - hi.astype(F32)).astype(BF16)
    return jnp.concatenate([hi, lo], axis=0)


def _block_sums(x, ones_blk):
    rows = x.shape[0]
    res = jnp.dot(_hi_lo_rows(x), ones_blk, preferred_element_type=F32)
    return res[0:rows] + res[rows:2 * rows]


def _rmsnorm(x, g):
    ms = jnp.mean(x * x, axis=-1, keepdims=True)
    return x * lax.rsqrt(ms + RMS_EPS) * g


def _softplus(x):
    return jnp.maximum(x, 0.0) + jnp.log1p(jnp.exp(-jnp.abs(x)))


def _sigmoid(x):
    return 1.0 / (1.0 + jnp.exp(-x))


def _gelu_tanh(x):
    return 0.5 * x * (1.0 + jnp.tanh(0.7978845608028654 * (x + 0.044715 * (x * x * x))))


def _in_kernel(x_ref, g_ref, w_ref, b_ref, q_ref, kv_ref, rw_ref, lr_ref):
    h = _rmsnorm(x_ref[...], g_ref[...]).astype(BF16)
    for ref, lo, hi in ((q_ref, 0, 512), (kv_ref, 512, 768), (rw_ref, 768, 1792), (lr_ref, 1792, 2304)):
        ref[...] = jnp.dot(h, w_ref[:, lo:hi], preferred_element_type=F32) + b_ref[:, lo:hi]


def _in_proj(x, g, w, b, layer):
    m = x.shape[0]
    tm = min(ROW_TILE, m)
    row = lambda w_: pl.BlockSpec((tm, w_), lambda i: (i, 0))
    return pl.pallas_call(
        _in_kernel,
        grid=(m // tm,),
        in_specs=[row(D_MODEL), _layer_spec((1, D_MODEL), layer, 1),
                  _layer_spec((D_MODEL, IN_COLS), layer, 1), _layer_spec((1, IN_COLS), layer, 1)],
        out_specs=[row(512), row(256), row(1024), row(512)],
        out_shape=[jax.ShapeDtypeStruct((m, n), F32) for n in (512, 256, 1024, 512)],
        compiler_params=_cparams(("arbitrary",)),
        name="in_proj",
    )(x, g, w, b)


def _attn_prompt_kernel(layer, nq, sink_ref, q_ref, kvp_ref, kvc_ref, o_ref):
    j = pl.program_id(1)
    log2e = 1.4426950408889634
    q = q_ref[...] * (HEAD_DIM ** -0.5 * log2e)
    kv = jnp.concatenate([kvp_ref[...], kvc_ref[...]], axis=0)
    kj = lax.broadcasted_iota(jnp.int32, (2 * WINDOW, WINDOW), 0)
    qi = lax.broadcasted_iota(jnp.int32, (2 * WINDOW, WINDOW), 1) + WINDOW
    d = qi - kj
    band = (d >= 0) & (d <= WINDOW)
    first = band & ((j > 0) | (kj >= WINDOW))
    nt = (((1,), (1,)), ((), ()))
    keys = kv[:, 0:KV_WIDTH].astype(BF16)
    v_t = kv[:, KV_WIDTH:2 * KV_WIDTH].T
    ones_rows = (lax.broadcasted_iota(jnp.int32, (8, 2 * WINDOW), 0) == 0).astype(F32)
    scores = {}
    for blk in range(nq):
        qb = q[blk * WINDOW:(blk + 1) * WINDOW]
        for g in range(N_KV_HEADS):
            qg = jnp.concatenate([qb[:, (g * GQA_GROUP + hh) * HEAD_DIM:(g * GQA_GROUP + hh + 1) * HEAD_DIM]
                                  for hh in range(GQA_GROUP)], axis=0).astype(BF16)
            k_ext = keys[blk * WINDOW:(blk + 2) * WINDOW, g * HEAD_DIM:(g + 1) * HEAD_DIM]
            scores[blk, g] = lax.dot_general(k_ext, qg, nt, preferred_element_type=F32)
    probs, sink_terms = {}, {}
    for blk in range(nq):
        mask = first if blk == 0 else band
        for g in range(N_KV_HEADS):
            pg, sg = [], []
            for hh in range(GQA_GROUP):
                s = jnp.where(mask, scores[blk, g][:, hh * WINDOW:(hh + 1) * WINDOW], -1e30)
                sink = sink_ref[layer, g * GQA_GROUP + hh] * log2e
                m = jnp.maximum(jnp.max(s, axis=0, keepdims=True), sink)
                pg.append(jnp.exp2(s - m).astype(BF16))
                sg.append(jnp.exp2(sink - m))
            probs[blk, g] = jnp.concatenate(pg, axis=1)
            sink_terms[blk, g] = jnp.concatenate(sg, axis=1)
    for blk in range(nq):
        outs = []
        for g in range(N_KV_HEADS):
            v_aug = jnp.concatenate([v_t[g * HEAD_DIM:(g + 1) * HEAD_DIM, blk * WINDOW:(blk + 2) * WINDOW],
                                     ones_rows], axis=0).astype(BF16)
            og = jnp.dot(v_aug, probs[blk, g], preferred_element_type=F32)
            den = og[HEAD_DIM:HEAD_DIM + 1, :] + sink_terms[blk, g]
            og = og[0:HEAD_DIM, :] * (1.0 / den)
            outs.extend(og[:, hh * WINDOW:(hh + 1) * WINDOW] for hh in range(GQA_GROUP))
        o_ref[blk * WINDOW:(blk + 1) * WINDOW, :] = jnp.concatenate(outs, axis=0).T


def _attn_prompt(q, kv, sinks, layer):
    b, t, _ = q.shape
    nq = ATTN_Q_BLOCKS
    tq = nq * WINDOW
    return pl.pallas_call(
        functools.partial(_attn_prompt_kernel, layer, nq),
        grid=(b, t // tq),
        in_specs=[
            pl.BlockSpec(memory_space=pltpu.SMEM),
            pl.BlockSpec((None, tq, ATTN_WIDTH), lambda i, j: (i, j, 0)),
            pl.BlockSpec((None, WINDOW, 2 * KV_WIDTH), lambda i, j: (i, jnp.maximum(j * nq - 1, 0), 0)),
            pl.BlockSpec((None, tq, 2 * KV_WIDTH), lambda i, j: (i, j, 0)),
        ],
        out_specs=pl.BlockSpec((None, tq, ATTN_WIDTH), lambda i, j: (i, j, 0)),
        out_shape=jax.ShapeDtypeStruct((b, t, ATTN_WIDTH), F32),
        compiler_params=_cparams(("arbitrary", "arbitrary")),
        name="attn_prompt",
    )(sinks, q, kv, kv)


def _attn_sample_kernel(layer, tn, sink_ref, q_ref, kvn_ref, ck_ref, cv_ref, nk_in_ref, nv_in_ref,
                        o_ref, nk_ref, nv_ref):
    del nk_in_ref, nv_in_ref
    bb = q_ref.shape[0]
    rows = GQA_GROUP * tn
    kvn = kvn_ref[...]
    row = lax.broadcasted_iota(jnp.int32, (1, rows, 1), 1)
    tok = row % tn
    col = lax.broadcasted_iota(jnp.int32, (1, 1, WINDOW), 2)
    cmask = col >= tok
    for g in range(N_KV_HEADS):
        k_t = ck_ref[:, g]
        v_t = cv_ref[:, g]
        qg = q_ref[:, g] * (HEAD_DIM ** -0.5)
        qg_b = qg.astype(BF16).astype(F32)
        sc = jnp.einsum('bqd,bdw->bqw', qg.astype(BF16), k_t.astype(BF16), preferred_element_type=F32)
        sc = jnp.where(cmask, sc, -1e30)
        sink = jnp.zeros((1, rows, 1), F32)
        for hh in range(GQA_GROUP):
            sink = jnp.where(row // tn == hh, sink_ref[layer, g * GQA_GROUP + hh], sink)
        m = jnp.maximum(jnp.max(sc, axis=-1, keepdims=True), sink)
        sn = []
        for jn in range(tn):
            kn = kvn[:, g * tn + jn:g * tn + jn + 1, :].astype(BF16).astype(F32)
            s_j = jnp.sum(qg_b * kn, axis=-1, keepdims=True)
            s_j = jnp.where(tok >= jn, s_j, -1e30)
            sn.append(s_j)
            m = jnp.maximum(m, s_j)
        ec = jnp.exp(sc - m)
        den = jnp.sum(ec, axis=-1, keepdims=True) + jnp.exp(sink - m)
        en = [jnp.exp(s_j - m) for s_j in sn]
        for e_j in en:
            den = den + e_j
        inv = 1.0 / den
        o = jnp.einsum('bqw,bdw->bqd', (ec * inv).astype(BF16), v_t.astype(BF16), preferred_element_type=F32)
        for jn in range(tn):
            vn = kvn[:, (N_KV_HEADS + g) * tn + jn:(N_KV_HEADS + g) * tn + jn + 1, :].astype(BF16).astype(F32)
            o = o + (en[jn] * inv).astype(BF16).astype(F32) * vn
        o_ref[:, g] = o
    new_t = kvn.reshape(bb * 4 * tn, HEAD_DIM).T
    lane = lax.broadcasted_iota(jnp.int32, (HEAD_DIM, WINDOW), 1)
    for c_ref, n_ref, which in ((ck_ref, nk_ref, 0), (cv_ref, nv_ref, 1)):
        for g in range(N_KV_HEADS):
            for b in range(bb):
                src = (b * 2 * N_KV_HEADS + which * N_KV_HEADS + g) * tn
                fresh = pltpu.roll(new_t, (WINDOW - tn - src) % WINDOW, 1)
                kept = pltpu.roll(c_ref[b, g], WINDOW - tn, 1)
                n_ref[b, g] = jnp.where(lane >= WINDOW - tn, fresh, kept)


def _attn_sample(q, kv, ck, cv, nk_all, nv_all, sinks, tn, layer):
    b = q.shape[0] // tn
    bb = ATTN_SAMPLE_SEQS
    assert bb * 4 * tn == WINDOW
    rows = GQA_GROUP * tn
    qh = q.reshape(tn, b, N_KV_HEADS, GQA_GROUP, HEAD_DIM).transpose(1, 2, 3, 0, 4)
    qh = qh.reshape(b, N_KV_HEADS, rows, HEAD_DIM)
    kvn = kv.reshape(tn, b, 2 * N_KV_HEADS, HEAD_DIM).transpose(1, 2, 0, 3).reshape(b, 4 * tn, HEAD_DIM)
    cache = pl.BlockSpec((None, bb, N_KV_HEADS, HEAD_DIM, WINDOW), lambda i: (layer, i, 0, 0, 0))
    heads = pl.BlockSpec((bb, N_KV_HEADS, rows, HEAD_DIM), lambda i: (i, 0, 0, 0))
    o, nk_all, nv_all = pl.pallas_call(
        functools.partial(_attn_sample_kernel, layer, tn),
        grid=(b // bb,),
        in_specs=[
            pl.BlockSpec(memory_space=pltpu.SMEM),
            heads,
            pl.BlockSpec((bb, 4 * tn, HEAD_DIM), lambda i: (i, 0, 0)),
            cache, cache,
            pl.BlockSpec(memory_space=pl.ANY), pl.BlockSpec(memory_space=pl.ANY),
        ],
        out_specs=[heads, cache, cache],
        out_shape=[
            jax.ShapeDtypeStruct((b, N_KV_HEADS, rows, HEAD_DIM), F32),
            jax.ShapeDtypeStruct(nk_all.shape, F32),
            jax.ShapeDtypeStruct(nv_all.shape, F32),
        ],
        input_output_aliases={5: 1, 6: 2},
        compiler_params=_cparams(("arbitrary",)),
        name="attn_sample",
    )(sinks, qh, kvn, ck, cv, nk_all, nv_all)
    o = o.reshape(b, N_KV_HEADS, GQA_GROUP, tn, HEAD_DIM).transpose(3, 0, 1, 2, 4)
    return o.reshape(tn * b, ATTN_WIDTH), nk_all, nv_all


def _tile_rows(x, n):
    return jnp.concatenate([x] * n, axis=0)


def _rwkv_kernel(c, nseq, x_ref, sh0_ref, st0_ref, mu_ref, lora_ref, gup_ref, vec_ref, ones_ref,
                 stacked_ref, o_ref, st_ref, last_ref, state_ref):
    del stacked_ref
    hw = RWKV_WIDTH
    n = RWKV_HEAD
    cw = RWKV_HEADS * c

    @pl.when(pl.program_id(1) == 0)
    def _():
        last_ref[...] = sh0_ref[...]
        for s in range(nseq):
            for h in range(RWKV_HEADS):
                state_ref[s, :, h * n:(h + 1) * n] = st0_ref[s, h]

    row = lax.broadcasted_iota(jnp.int32, (c, 1), 0)
    lane128 = lax.broadcasted_iota(jnp.int32, (c, 128), 1)
    tri = jnp.where(lax.broadcasted_iota(jnp.int32, (c, c), 1) <= lax.broadcasted_iota(jnp.int32, (c, c), 0),
                    1.0, 0.0).astype(BF16)
    tri3 = jnp.concatenate([tri, tri, tri], axis=1)
    head_rows = (lax.broadcasted_iota(jnp.int32, (cw, hw), 0) // c
                 == lax.broadcasted_iota(jnp.int32, (cw, hw), 1) // n)
    t_i = lax.broadcasted_iota(jnp.int32, (c, cw), 0)
    s_i = lax.broadcasted_iota(jnp.int32, (c, cw), 1) % c
    strict = s_i < t_i
    incl = s_i <= t_i
    eye_all = jnp.where(s_i == t_i, 1.0, 0.0)
    blk = (lax.broadcasted_iota(jnp.int32, (cw, cw), 0) // c
           == lax.broadcasted_iota(jnp.int32, (cw, cw), 1) // c)
    sblk = (lax.broadcasted_iota(jnp.int32, (hw, hw), 0) // n
            == lax.broadcasted_iota(jnp.int32, (hw, hw), 1) // n)
    masks = (row, lane128, tri3, head_rows, strict, incl, eye_all, blk, sblk)
    nsub = x_ref.shape[1] // c
    state_owner = [0] * nseq
    chains = {(sub, s): _rwkv_chunk(c, s, sub, nsub, state_owner, masks, x_ref, mu_ref, lora_ref, gup_ref,
                                    vec_ref, ones_ref, o_ref, st_ref, last_ref, state_ref)
              for sub in range(nsub) for s in range(nseq)}
    rnd = 0
    while chains:
        for key in sorted(chains):
            if rnd >= key[0] * RWKV_STAGGER and next(chains[key], "done") == "done":
                del chains[key]
        rnd += 1


def _rwkv_chunk(c, s, sub, nsub, state_owner, masks, x_ref, mu_ref, lora_ref, gup_ref, vec_ref, ones_ref,
                o_ref, st_ref, last_ref, state_ref):
    row, lane128, tri3, head_rows, strict, incl, eye_all, blk, sblk = masks
    hw = RWKV_WIDTH
    n = RWKV_HEAD
    cw = RWKV_HEADS * c
    x = x_ref[s, sub * c:(sub + 1) * c, :]
    before = last_ref[s] if sub == 0 else x_ref[s, sub * c - 1:sub * c, :]
    prev = jnp.where(row == 0, before, pltpu.roll(x, 1, 0))
    xs = x + (prev - x) * mu_ref[...]

    r = xs[:, 0:hw]
    k = xs[:, hw:2 * hw]
    v = xs[:, 2 * hw:3 * hw]
    wa = xs[:, 3 * hw:3 * hw + 128]
    xg = xs[:, 3 * hw + 128:]
    lora_in = jnp.where(lane128 < 64, jnp.tanh(wa), wa)
    lora = _bdot(lora_in, lora_ref[...])
    w0, a0, k_k, k_a, r_k, ln_w, ln_b = (vec_ref[i:i + 1, :] for i in range(7))
    g = _bdot(_sigmoid(xg), gup_ref[...])
    ones_blk = ones_ref[...]
    kk = k * k_k
    ss = _block_sums(kk * kk, ones_blk)
    yield
    w_log = -_softplus(-(w0 + lora[:, 0:hw])) - 0.5
    logw = -jnp.exp(w_log)
    a = _sigmoid(a0 + lora[:, hw:2 * hw])
    kk = kk * lax.rsqrt(jnp.maximum(ss, 1e-24))
    k2 = k * (1.0 + (a - 1.0) * k_a)
    bv = kk * a

    pieces = []
    rem = logw
    for _ in range(3):
        p = rem.astype(BF16)
        rem = rem - p.astype(F32)
        pieces.append(p)
    cum = jnp.dot(tri3, jnp.concatenate(pieces, axis=0), preferred_element_type=F32)
    bonus_sum = _block_sums(r * k2 * r_k, ones_blk)
    yield
    e_inc = jnp.exp(cum)
    e_exc = jnp.exp(cum - logw)
    e_inv = jnp.exp(-cum)
    kq = kk * e_exc
    rq = r * e_inc
    kd = k2 * e_inv
    bd = bv * e_inv
    w_end = e_inc[c - 1:c, :]
    kend = kd * w_end
    bend = bd * w_end

    def expand(z):
        return jnp.where(head_rows, _tile_rows(z, RWKV_HEADS), 0.0)

    lhs = jnp.concatenate([kq, rq], axis=0).astype(BF16)
    rhs = jnp.concatenate([expand(bd), expand(kd)], axis=0).astype(BF16)
    nt = (((1,), (1,)), ((), ()))
    prod = lax.dot_general(lhs, rhs, nt, preferred_element_type=F32)
    yield
    a_b = jnp.where(strict, prod[0:c, 0:cw], 0.0)
    a_k = jnp.where(strict, prod[0:c, cw:2 * cw], 0.0)
    p_b = jnp.where(incl, prod[c:2 * c, 0:cw], 0.0)
    p_k = jnp.where(incl, prod[c:2 * c, cw:2 * cw], 0.0)

    def bdiag(z):
        return jnp.where(blk, _tile_rows(z, RWKV_HEADS), 0.0)

    xm = -a_b
    tinv = eye_all + xm
    v_d = expand(v)
    from_v = _bdot(jnp.concatenate([a_k, p_k], axis=0), v_d)
    levels = c.bit_length() - 1
    xm = _bdot(xm, bdiag(xm))
    yield
    for _ in range(1, levels - 1):
        both = _bdot(jnp.concatenate([xm, tinv], axis=0), bdiag(xm))
        yield
        xm = both[0:c]
        tinv = tinv + both[c:2 * c]
    tinv = tinv + _bdot(tinv, bdiag(xm))
    yield
    assert state_owner[s] == sub
    st = state_ref[s]
    st_d = jnp.where(sblk, _tile_rows(st, RWKV_HEADS), 0.0)
    from_state = lax.dot_general(lhs, st_d.astype(BF16), nt, preferred_element_type=F32)
    yield
    u = _bdot(tinv, expand(from_state[0:c] + from_v[0:c]))
    yield
    y = from_state[c:2 * c] + from_v[c:2 * c] - _bdot(p_b, expand(u))

    lhs_s = jnp.concatenate([v, u], axis=0)
    rhs_s = jnp.concatenate([kend, -bend], axis=0)
    tn_dims = (((0,), (0,)), ((), ()))
    gm = lax.dot_general(lhs_s.astype(BF16), rhs_s.astype(BF16), tn_dims, preferred_element_type=F32)
    yield
    gm = jnp.where(sblk, gm, 0.0)
    st_new = st * w_end + (gm[0:n] + gm[n:2 * n] + (gm[2 * n:3 * n] + gm[3 * n:4 * n]))
    state_ref[s] = st_new
    state_owner[s] = sub + 1
    if sub == nsub - 1:
        state_owner[s] = 0
        last_ref[s] = x[c - 1:c, :]
        for h in range(RWKV_HEADS):
            st_ref[s, h] = st_new[:, h * n:(h + 1) * n]

    mean = _block_sums(y, ones_blk) * (1.0 / n)
    yield
    yc = y - mean
    var = _block_sums(yc * yc, ones_blk) * (1.0 / n)
    yield
    yn = yc * lax.rsqrt(var + GN_EPS) * ln_w + ln_b
    o_ref[s, sub * c:(sub + 1) * c, :] = (yn + bonus_sum * v) * g


def _rwkv(x, sh0, st0, mu, lora_w, g_up, vecs, ones_blk, stacked, layer):
    b, t, _ = x.shape
    c = RWKV_CHUNK
    tstep = RWKV_SUBCHUNKS * c
    nseq = min(RWKV_SEQS, b)
    wspec = lambda s: _layer_spec(s, layer, 2)
    hstate = (nseq, RWKV_HEADS, RWKV_HEAD, RWKV_HEAD)
    return pl.pallas_call(
        functools.partial(_rwkv_kernel, c, nseq),
        grid=(b // nseq, t // tstep),
        in_specs=[
            pl.BlockSpec((nseq, tstep, RWKV_PROJ), lambda i, j: (i, j, 0)),
            pl.BlockSpec((nseq, 1, RWKV_PROJ), lambda i, j: (i, 0, 0)),
            pl.BlockSpec(hstate, lambda i, j: (i, 0, 0, 0)),
            wspec((1, RWKV_PROJ)), wspec((128, 512)), wspec((128, RWKV_WIDTH)), wspec((8, RWKV_WIDTH)),
            pl.BlockSpec((RWKV_WIDTH, RWKV_WIDTH), lambda i, j: (0, 0)),
            pl.BlockSpec(memory_space=pl.ANY),
        ],
        out_specs=[
            pl.BlockSpec((nseq, tstep, RWKV_WIDTH), lambda i, j: (i, j, 0)),
            pl.BlockSpec((None,) + hstate, lambda i, j: (layer, i, 0, 0, 0)),
        ],
        out_shape=[
            jax.ShapeDtypeStruct((b, t, RWKV_WIDTH), F32),
            jax.ShapeDtypeStruct(stacked.shape, F32),
        ],
        scratch_shapes=[pltpu.VMEM((nseq, 1, RWKV_PROJ), F32), pltpu.VMEM((nseq, RWKV_HEAD, RWKV_WIDTH), F32)],
        input_output_aliases={8: 1},
        compiler_params=_cparams(("arbitrary", "arbitrary")),
        name="rwkv",
    )(x, sh0, st0, mu, lora_w, g_up, vecs, ones_blk, stacked)


def _rwkv_step_kernel(tn, nb, x_ref, sh0_ref, st0_ref, mu_ref, lora_ref, gup_ref, vec_ref, ones_ref,
                      stacked_ref, o_ref, st_ref, vt_ref, nat_ref, yt_ref):
    del stacked_ref
    hw = RWKV_WIDTH
    n = RWKV_HEAD
    h = pl.program_id(0)
    rows = tn * nb

    @pl.when(h == 0)
    def _():
        x = x_ref[...]
        prev = jnp.concatenate([sh0_ref[...], x[0:rows - nb]], axis=0)
        xs = x + (prev - x) * mu_ref[...]
        r = xs[:, 0:hw]
        k = xs[:, hw:2 * hw]
        v = xs[:, 2 * hw:3 * hw]
        wa = xs[:, 3 * hw:3 * hw + 128]
        xg = xs[:, 3 * hw + 128:]
        lane128 = lax.broadcasted_iota(jnp.int32, (rows, 128), 1)
        lora = _bdot(jnp.where(lane128 < 64, jnp.tanh(wa), wa), lora_ref[...])
        w0, a0, k_k, k_a, r_k = (vec_ref[i:i + 1, :] for i in range(5))
        ones_blk = ones_ref[...]
        w_log = -_softplus(-(w0 + lora[:, 0:hw])) - 0.5
        decay = jnp.exp(-jnp.exp(w_log))
        a = _sigmoid(a0 + lora[:, hw:2 * hw])
        kk = k * k_k
        kk = kk * lax.rsqrt(jnp.maximum(_block_sums(kk * kk, ones_blk), 1e-24))
        k2 = k * (1.0 + (a - 1.0) * k_a)
        nat_ref[0] = _bdot(_sigmoid(xg), gup_ref[...])
        nat_ref[1] = v
        nat_ref[2] = _block_sums(r * k2 * r_k, ones_blk)
        for i, z in enumerate((r, decay, k2, v, kk, kk * a)):
            for t in range(tn):
                for half in range(hw // 128):
                    vt_ref[i, t, half * 128:(half + 1) * 128, :] = (
                        z[t * nb:(t + 1) * nb, half * 128:(half + 1) * 128].T)

    base = pl.multiple_of(h * n, n)

    def value_row(vi, carry):
        s = st0_ref[vi]
        for t in range(tn):
            r_t, w_t, k_t, _, kk_t, b_t = (vt_ref[i, t, pl.ds(base, n), :] for i in range(6))
            v_row = vt_ref[3, t, pl.ds(base + vi, 1), :]
            u = jnp.sum(s * kk_t, axis=0, keepdims=True)
            s = s * w_t - u * b_t + v_row * k_t
            yt_ref[t, pl.ds(base + vi, 1), :] = jnp.sum(s * r_t, axis=0, keepdims=True)
        st_ref[vi] = s
        return carry

    lax.fori_loop(0, n, value_row, 0, unroll=4)

    @pl.when(h == RWKV_HEADS - 1)
    def _():
        ones_blk = ones_ref[...]
        ln_w = vec_ref[5:6, :]
        ln_b = vec_ref[6:7, :]
        y = jnp.concatenate(
            [jnp.concatenate([yt_ref[t, half * 128:(half + 1) * 128, :].T for half in range(hw // 128)], axis=1)
             for t in range(tn)], axis=0)
        mean = _block_sums(y, ones_blk) * (1.0 / n)
        yc = y - mean
        var = _block_sums(yc * yc, ones_blk) * (1.0 / n)
        yn = yc * lax.rsqrt(var + GN_EPS) * ln_w + ln_b
        o_ref[...] = (yn + nat_ref[2] * nat_ref[1]) * nat_ref[0]


def _rwkv_sample(x, sh0, st0, mu, lora_w, g_up, vecs, ones_blk, stacked, tn, layer):
    rows = x.shape[0]
    nb = rows // tn
    wspec = lambda s: _layer_spec(s, layer, 1)
    head_state = pl.BlockSpec((None, None, RWKV_HEAD, RWKV_HEAD, nb), lambda h: (layer, h, 0, 0, 0))
    return pl.pallas_call(
        functools.partial(_rwkv_step_kernel, tn, nb),
        grid=(RWKV_HEADS,),
        in_specs=[
            pl.BlockSpec((rows, RWKV_PROJ), lambda h: (0, 0)),
            wspec((nb, RWKV_PROJ)), head_state,
            wspec((1, RWKV_PROJ)), wspec((128, 512)), wspec((128, RWKV_WIDTH)), wspec((8, RWKV_WIDTH)),
            pl.BlockSpec((RWKV_WIDTH, RWKV_WIDTH), lambda h: (0, 0)),
            pl.BlockSpec(memory_space=pl.ANY),
        ],
        out_specs=[pl.BlockSpec((rows, RWKV_WIDTH), lambda h: (0, 0)), head_state],
        out_shape=[jax.ShapeDtypeStruct((rows, RWKV_WIDTH), F32), jax.ShapeDtypeStruct(stacked.shape, F32)],
        scratch_shapes=[pltpu.VMEM((6, tn, RWKV_WIDTH, nb), F32), pltpu.VMEM((3, rows, RWKV_WIDTH), F32),
                        pltpu.VMEM((tn, RWKV_WIDTH, nb), F32)],
        input_output_aliases={8: 1},
        compiler_params=_cparams(("arbitrary",)),
        name="rwkv_step",
    )(x, sh0, st0, mu, lora_w, g_up, vecs, ones_blk, stacked)


def _lru_scan(xc, t, period, wg_ref, vec_ref):
    w = LRU_WIDTH
    _, b_a, b_i, lam = (vec_ref[i:i + 1, :] for i in range(4))
    gates = _bdot(xc, wg_ref[...])
    r = _sigmoid(gates[:, 0:w] + b_a)
    i = _sigmoid(gates[:, w:2 * w] + b_i)
    log_a = LRU_C * r * (-_softplus(-lam))
    a = jnp.exp(log_a)
    th = jnp.tanh(log_a)
    u = jnp.sqrt(-2.0 * th / (1.0 - th)) * (i * xc)
    span = 1
    while span < period:
        ok = t >= span
        a_s = pltpu.roll(a, span, 0)
        u_s = pltpu.roll(u, span, 0)
        u = jnp.where(ok, a * u_s + u, u)
        a = jnp.where(ok, a * a_s, a)
        span *= 2
    return a, u


def _lru_kernel(tc, x_ref, cv0_ref, h0_ref, cw_ref, wg_ref, vec_ref, o_ref, h_ref, ext_ref, hc_ref):
    ti = pl.program_id(1)
    w = LRU_WIDTH

    @pl.when(ti == 0)
    def _():
        ext_ref[0:8, :] = cv0_ref[...]
        hc_ref[...] = h0_ref[...]

    xb = x_ref[:, 0:w]
    gb = x_ref[:, w:2 * w]
    ext_ref[8:8 + tc, :] = xb
    xc = vec_ref[0:1, :] + xb * cw_ref[CONV_W - 1:CONV_W, :]
    for j in range(CONV_W - 1):
        xc = xc + ext_ref[pl.ds(8 - (CONV_W - 1) + j, tc), :] * cw_ref[j:j + 1, :]
    ext_ref[0:8, :] = xb[tc - 8:tc, :]
    row = lax.broadcasted_iota(jnp.int32, (tc, 1), 0)
    a, u = _lru_scan(xc, row, tc, wg_ref, vec_ref)
    h = a * hc_ref[...] + u
    hc_ref[...] = h[tc - 1:tc, :]
    h_ref[...] = h[tc - 1:tc, :]
    o_ref[...] = h * _gelu_tanh(gb)


def _lru_step_kernel(tn, nb, x_ref, cv0_ref, h0_ref, cw_ref, wg_ref, vec_ref, o_ref, h_ref):
    w = LRU_WIDTH
    rows = tn * nb
    xb = x_ref[:, 0:w]
    gb = x_ref[:, w:2 * w]
    ext = jnp.concatenate([cv0_ref[j] for j in range(CONV_W - 1)] + [xb], axis=0)
    xc = vec_ref[0:1, :]
    for j in range(CONV_W):
        xc = xc + ext[j * nb:j * nb + rows] * cw_ref[j:j + 1, :]
    _, b_a, b_i, lam = (vec_ref[i:i + 1, :] for i in range(4))
    gates = _bdot(xc, wg_ref[...])
    r = _sigmoid(gates[:, 0:w] + b_a)
    i = _sigmoid(gates[:, w:2 * w] + b_i)
    log_a = LRU_C * r * (-_softplus(-lam))
    a = jnp.exp(log_a)
    th = jnp.tanh(log_a)
    u = jnp.sqrt(-2.0 * th / (1.0 - th)) * (i * xc)
    gelu = _gelu_tanh(gb)
    h = h0_ref[...]
    for t in range(tn):
        h = a[t * nb:(t + 1) * nb] * h + u[t * nb:(t + 1) * nb]
        o_ref[t * nb:(t + 1) * nb, :] = h * gelu[t * nb:(t + 1) * nb]
    h_ref[...] = h


def _lru_sample(x, cv0, h0, conv_w, w_gates, vecs, tn, layer):
    rows = x.shape[0]
    nb = rows // tn
    wspec = lambda s: _layer_spec(s, layer, 1)
    return pl.pallas_call(
        functools.partial(_lru_step_kernel, tn, nb),
        grid=(1,),
        in_specs=[pl.BlockSpec((rows, 2 * LRU_WIDTH), lambda i: (0, 0)),
                  wspec((CONV_W - 1, nb, LRU_WIDTH)), wspec((nb, LRU_WIDTH)),
                  wspec((CONV_W, LRU_WIDTH)), wspec((LRU_WIDTH, 2 * LRU_WIDTH)), wspec((4, LRU_WIDTH))],
        out_specs=[pl.BlockSpec((rows, LRU_WIDTH), lambda i: (0, 0)), pl.BlockSpec((nb, LRU_WIDTH), lambda i: (0, 0))],
        out_shape=[jax.ShapeDtypeStruct((rows, LRU_WIDTH), F32), jax.ShapeDtypeStruct((nb, LRU_WIDTH), F32)],
        compiler_params=_cparams(("arbitrary",)),
        name="lru_step",
    )(x, cv0, h0, conv_w, w_gates, vecs)


def _lru(x, cv0, h0, conv_w, w_gates, vecs, layer):
    b, t, _ = x.shape
    out_shape = [jax.ShapeDtypeStruct((b, t, LRU_WIDTH), F32), jax.ShapeDtypeStruct((b, 1, LRU_WIDTH), F32)]
    tc = min(LRU_TILE, t)
    wspec = lambda s: _layer_spec(s, layer, 2)
    return pl.pallas_call(
        functools.partial(_lru_kernel, tc),
        grid=(b, t // tc),
        in_specs=[
            pl.BlockSpec((None, tc, 2 * LRU_WIDTH), lambda i, j: (i, j, 0)),
            pl.BlockSpec((None, 8, LRU_WIDTH), lambda i, j: (i, 0, 0)),
            pl.BlockSpec((None, 1, LRU_WIDTH), lambda i, j: (i, 0, 0)),
            wspec((CONV_W, LRU_WIDTH)), wspec((LRU_WIDTH, 2 * LRU_WIDTH)), wspec((4, LRU_WIDTH)),
        ],
        out_specs=[
            pl.BlockSpec((None, tc, LRU_WIDTH), lambda i, j: (i, j, 0)),
            pl.BlockSpec((None, 1, LRU_WIDTH), lambda i, j: (i, 0, 0)),
        ],
        out_shape=out_shape,
        scratch_shapes=[pltpu.VMEM((tc + 8, LRU_WIDTH), F32), pltpu.VMEM((1, LRU_WIDTH), F32)],
        compiler_params=_cparams(("arbitrary", "arbitrary")),
        name="lru",
    )(x, cv0, h0, conv_w, w_gates, vecs)


def _post_kernel(x_ref, oa_ref, ob_ref, oc_ref, p_ref, wo_ref, wg_ref, wu_ref, wd_ref, pg_ref, pw_ref,
                 vec_ref, y_ref):
    mix = (jnp.dot(oa_ref[...].astype(BF16), wo_ref[0:512, :], preferred_element_type=F32)
           + jnp.dot(ob_ref[...].astype(BF16), wo_ref[512:768, :], preferred_element_type=F32)
           + jnp.dot(oc_ref[...].astype(BF16), wo_ref[768:1024, :], preferred_element_type=F32)
           + vec_ref[0:1, :])
    x = x_ref[...] + _rmsnorm(mix, vec_ref[1:2, :])
    f = _rmsnorm(x, vec_ref[2:3, :]).astype(BF16)
    acc = None
    for lo in range(0, D_FF, FF_CHUNK):
        gate = jnp.dot(f, wg_ref[:, lo:lo + FF_CHUNK], preferred_element_type=F32)
        up = jnp.dot(f, wu_ref[:, lo:lo + FF_CHUNK], preferred_element_type=F32)
        hid = (gate * _sigmoid(gate) * up).astype(BF16)
        part = jnp.dot(hid, wd_ref[lo:lo + FF_CHUNK, :], preferred_element_type=F32)
        acc = part if acc is None else acc + part
    x2 = x + _rmsnorm(acc, vec_ref[3:4, :])
    gate = _sigmoid(jnp.dot(x2.astype(BF16), pg_ref[...], preferred_element_type=F32))
    emb = jnp.dot(p_ref[...].astype(BF16), pw_ref[...], preferred_element_type=F32)
    y_ref[...] = x2 + gate * emb


def _post(x, oa, ob, oc, p, wo, wg, wu, wd, pg, pw, vecs, layer):
    m = x.shape[0]
    tm = min(ROW_TILE, m)
    row = lambda w_: pl.BlockSpec((tm, w_), lambda i: (i, 0))
    wspec = lambda s: _layer_spec(s, layer, 1, pipeline_mode=pl.Buffered(1))
    return pl.pallas_call(
        _post_kernel,
        grid=(m // tm,),
        in_specs=[row(D_MODEL), row(512), row(256), row(256),
                  pl.BlockSpec((None, tm, PLE_DIM), lambda i: (layer, i, 0)),
                  wspec((D_MODEL, D_MODEL)), wspec((D_MODEL, D_FF)), wspec((D_MODEL, D_FF)),
                  wspec((D_FF, D_MODEL)), wspec((D_MODEL, D_MODEL)), wspec((PLE_DIM, D_MODEL)),
                  wspec((4, D_MODEL))],
        out_specs=row(D_MODEL),
        out_shape=jax.ShapeDtypeStruct((m, D_MODEL), F32),
        compiler_params=_cparams(("arbitrary",)),
        name="post",
    )(x, oa, ob, oc, p, wo, wg, wu, wd, pg, pw, vecs)


def _block_diag(w):
    nl, nb, n, _ = w.shape
    eye = jnp.eye(nb, dtype=w.dtype)
    return (eye[None, :, None, :, None] * w[:, :, :, None, :]).reshape(nl, nb * n, nb * n)


def _post_layer(x2, o_a, o_b, o_c, p, wts, layer):
    return _post(x2, o_a, o_b, o_c, p, wts['w_out'], wts['ffn_w_gate'], wts['ffn_w_up'], wts['ffn_w_down'],
                 wts['ple_gate_w'], wts['ple_w'], wts['post_vecs'], layer)


def _layer_prompt(x2, b, t, p, zeros, wkv_all, wts, layer):
    m = b * t
    q, kv, rw, lr = _in_proj(x2, wts['norm_mix_pre'], wts['w_in'], wts['b_in'], layer)
    kv = kv.reshape(b, t, 256)
    rw = rw.reshape(b, t, RWKV_PROJ)
    lr = lr.reshape(b, t, 2 * LRU_WIDTH)
    sh0, st0, cv0, h0 = zeros
    o_a = _attn_prompt(q.reshape(b, t, 512), kv, wts['attn_sinks'], layer)
    nk = kv[:, t - WINDOW:, 0:KV_WIDTH].reshape(b, WINDOW, N_KV_HEADS, HEAD_DIM)
    nv = kv[:, t - WINDOW:, KV_WIDTH:].reshape(b, WINDOW, N_KV_HEADS, HEAD_DIM)
    o_b, wkv_all = _rwkv(rw, sh0, st0, wts['rwkv_mu'], wts['rwkv_lora'], wts['rwkv_g_up'],
                         wts['rwkv_vecs'], wts['ones_blk'], wkv_all, layer)
    nsh = rw[:, t - 1, :]
    o_c, nh = _lru(lr, cv0, h0, wts['lru_conv_w'], wts['lru_w_gates'], wts['lru_vecs'], layer)
    nconv = lr[:, t - (CONV_W - 1):, 0:LRU_WIDTH]
    x2 = _post_layer(x2, o_a.reshape(m, 512), o_b.reshape(m, 256), o_c.reshape(m, 256), p, wts, layer)
    return x2, (nk, nv, nsh, nconv, nh.reshape(b, LRU_WIDTH)), wkv_all


def _layer_sample(x2, b, t, p, state, outs, wts, layer):
    ck, cv, sh0, st0, cv0, h0 = state
    nk_all, nv_all, wkv_all = outs
    q, kv, rw, lr = _in_proj(x2, wts['norm_mix_pre'], wts['w_in'], wts['b_in'], layer)
    o_a, nk_all, nv_all = _attn_sample(q, kv, ck, cv, nk_all, nv_all, wts['attn_sinks'], t, layer)
    o_b, wkv_all = _rwkv_sample(rw, sh0, st0, wts['rwkv_mu'], wts['rwkv_lora'], wts['rwkv_g_up'],
                                wts['rwkv_vecs'], wts['ones_blk'], wkv_all, t, layer)
    nsh = rw[(t - 1) * b:, :]
    o_c, nh = _lru_sample(lr, cv0, h0, wts['lru_conv_w'], wts['lru_w_gates'], wts['lru_vecs'], t, layer)
    nconv = lr[(t - (CONV_W - 1)) * b:, 0:LRU_WIDTH].reshape(CONV_W - 1, b, LRU_WIDTH)
    x2 = _post_layer(x2, o_a, o_b, o_c, p, wts, layer)
    return x2, (nsh, nconv, nh), (nk_all, nv_all, wkv_all)


def kernel(x_prompt, x_sample, cache_k, cache_v, state_shift, state_wkv, state_conv, state_lru,
           p_prompt, p_sample, norm_mix_pre, norm_mix_post, norm_ffn_pre, norm_ffn_post,
           w_in, b_in, attn_sinks, rwkv_mu, rwkv_w0, rwkv_w_up, rwkv_a0, rwkv_a_up, rwkv_g_up,
           rwkv_k_k, rwkv_k_a, rwkv_r_k, rwkv_ln_w, rwkv_ln_b, lru_conv_w, lru_conv_b,
           lru_w_a, lru_b_a, lru_w_i, lru_b_i, lru_L, w_out, b_out, ffn_w_gate, ffn_w_up,
           ffn_w_down, ple_w, ple_gate_w):
    nl = DEPTH
    bp, tp_, _ = x_prompt.shape
    bs, ts, _ = x_sample.shape
    head_id = jnp.arange(RWKV_WIDTH) // RWKV_HEAD
    zeros_w = jnp.zeros((nl, 64, RWKV_WIDTH), F32)
    wts = dict(
        norm_mix_pre=norm_mix_pre[:, None, :], w_in=w_in.astype(BF16), b_in=b_in[:, None, :],
        attn_sinks=attn_sinks,
        rwkv_mu=rwkv_mu[:, None, :],
        rwkv_lora=jnp.concatenate([jnp.concatenate([rwkv_w_up, zeros_w], axis=2),
                                   jnp.concatenate([zeros_w, rwkv_a_up], axis=2)], axis=1).astype(BF16),
        rwkv_g_up=rwkv_g_up.astype(BF16),
        rwkv_vecs=jnp.stack([rwkv_w0, rwkv_a0, rwkv_k_k, rwkv_k_a, rwkv_r_k.reshape(nl, RWKV_WIDTH),
                             rwkv_ln_w, rwkv_ln_b, jnp.zeros((nl, RWKV_WIDTH), F32)], axis=1),
        ones_blk=(head_id[:, None] == head_id[None, :]).astype(BF16),
        lru_conv_w=lru_conv_w,
        lru_w_gates=jnp.concatenate([_block_diag(lru_w_a), _block_diag(lru_w_i)], axis=2).astype(BF16),
        lru_vecs=jnp.stack([lru_conv_b, lru_b_a, lru_b_i, lru_L], axis=1),
        w_out=w_out.astype(BF16), ffn_w_gate=ffn_w_gate.astype(BF16), ffn_w_up=ffn_w_up.astype(BF16),
        ffn_w_down=ffn_w_down.astype(BF16), ple_gate_w=ple_gate_w.astype(BF16), ple_w=ple_w.astype(BF16),
        post_vecs=jnp.stack([b_out, norm_mix_post, norm_ffn_pre, norm_ffn_post], axis=1),
    )
    zeros_p = (jnp.zeros((bp, 1, RWKV_PROJ), F32),
               jnp.zeros((bp, RWKV_HEADS, RWKV_HEAD, RWKV_HEAD), F32),
               jnp.zeros((bp, 8, LRU_WIDTH), F32),
               jnp.zeros((bp, 1, LRU_WIDTH), F32))
    st_s = (cache_k.transpose(0, 1, 3, 4, 2), cache_v.transpose(0, 1, 3, 4, 2),
            state_shift,
            state_wkv.transpose(0, 2, 3, 4, 1),
            state_conv.transpose(0, 2, 1, 3),
            state_lru)
    pp = p_prompt.reshape(nl, bp * tp_, PLE_DIM)
    ps = p_sample.transpose(0, 2, 1, 3).reshape(nl, ts * bs, PLE_DIM)

    xp = x_prompt.reshape(bp * tp_, D_MODEL)
    xs = x_sample.transpose(1, 0, 2).reshape(ts * bs, D_MODEL)
    wkv_p = jnp.zeros((nl, bp, RWKV_HEADS, RWKV_HEAD, RWKV_HEAD), F32)
    outs_s = (jnp.zeros((nl, bs, N_KV_HEADS, HEAD_DIM, WINDOW), F32),
              jnp.zeros((nl, bs, N_KV_HEADS, HEAD_DIM, WINDOW), F32),
              jnp.zeros((nl, RWKV_HEADS, RWKV_HEAD, RWKV_HEAD, bs), F32))
    new_p, new_s = [], []
    for i in range(nl):
        xp, sp, wkv_p = _layer_prompt(xp, bp, tp_, pp, zeros_p, wkv_p, wts, i)
        xs, ss, outs_s = _layer_sample(xs, bs, ts, ps, st_s, outs_s, wts, i)
        new_p.append(sp)
        new_s.append(ss)

    def stk(lst, j):
        return jnp.stack([s[j] for s in lst], axis=0)

    nk_s, nv_s, wkv_s = outs_s
    return (xp.reshape(bp, tp_, D_MODEL), xs.reshape(ts, bs, D_MODEL).transpose(1, 0, 2),
            stk(new_p, 0), stk(new_p, 1), stk(new_p, 2), wkv_p, stk(new_p, 3), stk(new_p, 4),
            nk_s.transpose(0, 1, 4, 2, 3), nv_s.transpose(0, 1, 4, 2, 3), stk(new_s, 0),
            wkv_s.transpose(0, 4, 1, 2, 3), stk(new_s, 1).transpose(0, 2, 1, 3), stk(new_s, 2))
```

```python
import functools

import jax
import jax.numpy as jnp
from jax import lax
from jax.experimental import pallas as pl
from jax.experimental.pallas import tpu as pltpu

F32 = jnp.float32
BF16 = jnp.bfloat16

D_MODEL = 1024
DEPTH = 4
HEAD_DIM = 64
ATTN_WIDTH = 512
N_HEADS = 8
N_KV_HEADS = 2
GQA_GROUP = 4
KV_WIDTH = 128
WINDOW = 128
RWKV_WIDTH = 256
RWKV_HEADS = 4
RWKV_HEAD = 64
RWKV_PROJ = 1024
LRU_WIDTH = 256
CONV_W = 4
LRU_C = 8.0
D_FF = 2816
PLE_DIM = 256
RMS_EPS = 1e-6
GN_EPS = 64e-5
IN_COLS = 2304

ROW_TILE = 512
FF_CHUNK = 1408
RWKV_CHUNK = 64
RWKV_SEQS = 4
RWKV_SUBCHUNKS = 4
RWKV_STAGGER = 3
LRU_TILE = 512
LRU_SCAN_BLOCK = 32
ATTN_SAMPLE_SEQS = 8
ATTN_Q_BLOCKS = 8
VMEM_LIMIT = 56 * 1024 * 1024


def _cparams(sem):
    return pltpu.CompilerParams(dimension_semantics=sem, vmem_limit_bytes=VMEM_LIMIT)


def _layer_spec(shape, layer, nidx, **kw):
    zeros = (0,) * len(shape)
    if nidx == 1:
        return pl.BlockSpec((None,) + tuple(shape), lambda i: (layer,) + zeros, **kw)
    return pl.BlockSpec((None,) + tuple(shape), lambda i, j: (layer,) + zeros, **kw)


def _bdot(a, b):
    return jnp.dot(a.astype(BF16), b.astype(BF16), preferred_element_type=F32)


def _hi_lo_rows(x):
    hi = x.astype(BF16)
    lo = (x - hi.astype(F32)).astype(BF16)
    return jnp.concatenate([hi, lo], axis=0)


def _block_sums(x, ones_blk):
    rows = x.shape[0]
    res = jnp.dot(_hi_lo_rows(x), ones_blk, preferred_element_type=F32)
    return res[0:rows] + res[rows:2 * rows]


def _rmsnorm(x, g):
    ms = jnp.mean(x * x, axis=-1, keepdims=True)
    return x * lax.rsqrt(ms + RMS_EPS) * g


def _softplus(x):
    return jnp.maximum(x, 0.0) + jnp.log1p(jnp.exp(-jnp.abs(x)))


def _sigmoid(x):
    return 1.0 / (1.0 + jnp.exp(-x))


def _gelu_tanh(x):
    return 0.5 * x * (1.0 + jnp.tanh(0.7978845608028654 * (x + 0.044715 * (x * x * x))))


def _in_kernel(x_ref, g_ref, w_ref, b_ref, q_ref, kv_ref, rw_ref, lr_ref):
    h = _rmsnorm(x_ref[...], g_ref[...]).astype(BF16)
    for ref, lo, hi in ((q_ref, 0, 512), (kv_ref, 512, 768), (rw_ref, 768, 1792), (lr_ref, 1792, 2304)):
        ref[...] = jnp.dot(h, w_ref[:, lo:hi], preferred_element_type=F32) + b_ref[:, lo:hi]


def _in_proj(x, g, w, b, layer):
    m = x.shape[0]
    tm = min(ROW_TILE, m)
    row = lambda w_: pl.BlockSpec((tm, w_), lambda i: (i, 0))
    return pl.pallas_call(
        _in_kernel,
        grid=(m // tm,),
        in_specs=[row(D_MODEL), _layer_spec((1, D_MODEL), layer, 1),
                  _layer_spec((D_MODEL, IN_COLS), layer, 1), _layer_spec((1, IN_COLS), layer, 1)],
        out_specs=[row(512), row(256), row(1024), row(512)],
        out_shape=[jax.ShapeDtypeStruct((m, n), F32) for n in (512, 256, 1024, 512)],
        compiler_params=_cparams(("arbitrary",)),
        name="in_proj",
    )(x, g, w, b)


def _attn_prompt_kernel(layer, nq, sink_ref, q_ref, kvp_ref, kvc_ref, o_ref):
    j = pl.program_id(1)
    log2e = 1.4426950408889634
    q = q_ref[...] * (HEAD_DIM ** -0.5 * log2e)
    kv = jnp.concatenate([kvp_ref[...], kvc_ref[...]], axis=0)
    kj = lax.broadcasted_iota(jnp.int32, (2 * WINDOW, WINDOW), 0)
    qi = lax.broadcasted_iota(jnp.int32, (2 * WINDOW, WINDOW), 1) + WINDOW
    d = qi - kj
    band = (d >= 0) & (d <= WINDOW)
    first = band & ((j > 0) | (kj >= WINDOW))
    nt = (((1,), (1,)), ((), ()))
    keys = kv[:, 0:KV_WIDTH].astype(BF16)
    v_t = kv[:, KV_WIDTH:2 * KV_WIDTH].T
    ones_rows = (lax.broadcasted_iota(jnp.int32, (8, 2 * WINDOW), 0) == 0).astype(F32)
    scores = {}
    for blk in range(nq):
        qb = q[blk * WINDOW:(blk + 1) * WINDOW]
        for g in range(N_KV_HEADS):
            qg = jnp.concatenate([qb[:, (g * GQA_GROUP + hh) * HEAD_DIM:(g * GQA_GROUP + hh + 1) * HEAD_DIM]
                                  for hh in range(GQA_GROUP)], axis=0).astype(BF16)
            k_ext = keys[blk * WINDOW:(blk + 2) * WINDOW, g * HEAD_DIM:(g + 1) * HEAD_DIM]
            scores[blk, g] = lax.dot_general(k_ext, qg, nt, preferred_element_type=F32)
    probs, sink_terms = {}, {}
    for blk in range(nq):
        mask = first if blk == 0 else band
        for g in range(N_KV_HEADS):
            pg, sg = [], []
            for hh in range(GQA_GROUP):
                s = jnp.where(mask, scores[blk, g][:, hh * WINDOW:(hh + 1) * WINDOW], -1e30)
                sink = sink_ref[layer, g * GQA_GROUP + hh] * log2e
                m = jnp.maximum(jnp.max(s, axis=0, keepdims=True), sink)
                pg.append(jnp.exp2(s - m).astype(BF16))
                sg.append(jnp.exp2(sink - m))
            probs[blk, g] = jnp.concatenate(pg, axis=1)
            sink_terms[blk, g] = jnp.concatenate(sg, axis=1)
    for blk in range(nq):
        outs = []
        for g in range(N_KV_HEADS):
            v_aug = jnp.concatenate([v_t[g * HEAD_DIM:(g + 1) * HEAD_DIM, blk * WINDOW:(blk + 2) * WINDOW],
                                     ones_rows], axis=0).astype(BF16)
            og = jnp.dot(v_aug, probs[blk, g], preferred_element_type=F32)
            den = og[HEAD_DIM:HEAD_DIM + 1, :] + sink_terms[blk, g]
            og = og[0:HEAD_DIM, :] * (1.0 / den)
            outs.extend(og[:, hh * WINDOW:(hh + 1) * WINDOW] for hh in range(GQA_GROUP))
        o_ref[blk * WINDOW:(blk + 1) * WINDOW, :] = jnp.concatenate(outs, axis=0).T


def _attn_prompt(q, kv, sinks, layer):
    b, t, _ = q.shape
    nq = ATTN_Q_BLOCKS
    tq = nq * WINDOW
    return pl.pallas_call(
        functools.partial(_attn_prompt_kernel, layer, nq),
        grid=(b, t // tq),
        in_specs=[
            pl.BlockSpec(memory_space=pltpu.SMEM),
            pl.BlockSpec((None, tq, ATTN_WIDTH), lambda i, j: (i, j, 0)),
            pl.BlockSpec((None, WINDOW, 2 * KV_WIDTH), lambda i, j: (i, jnp.maximum(j * nq - 1, 0), 0)),
            pl.BlockSpec((None, tq, 2 * KV_WIDTH), lambda i, j: (i, j, 0)),
        ],
        out_specs=pl.BlockSpec((None, tq, ATTN_WIDTH), lambda i, j: (i, j, 0)),
        out_shape=jax.ShapeDtypeStruct((b, t, ATTN_WIDTH), F32),
        compiler_params=_cparams(("arbitrary", "arbitrary")),
        name="attn_prompt",
    )(sinks, q, kv, kv)


def _attn_sample_kernel(layer, tn, sink_ref, q_ref, kvn_ref, ck_ref, cv_ref, nk_in_ref, nv_in_ref,
                        o_ref, nk_ref, nv_ref):
    del nk_in_ref, nv_in_ref
    bb = q_ref.shape[0]
    rows = GQA_GROUP * tn
    kvn = kvn_ref[...]
    row = lax.broadcasted_iota(jnp.int32, (1, rows, 1), 1)
    tok = row % tn
    col = lax.broadcasted_iota(jnp.int32, (1, 1, WINDOW), 2)
    cmask = col >= tok
    for g in range(N_KV_HEADS):
        k_t = ck_ref[:, g]
        v_t = cv_ref[:, g]
        qg = q_ref[:, g] * (HEAD_DIM ** -0.5)
        qg_b = qg.astype(BF16).astype(F32)
        sc = jnp.einsum('bqd,bdw->bqw', qg.astype(BF16), k_t.astype(BF16), preferred_element_type=F32)
        sc = jnp.where(cmask, sc, -1e30)
        sink = jnp.zeros((1, rows, 1), F32)
        for hh in range(GQA_GROUP):
            sink = jnp.where(row // tn == hh, sink_ref[layer, g * GQA_GROUP + hh], sink)
        m = jnp.maximum(jnp.max(sc, axis=-1, keepdims=True), sink)
        sn = []
        for jn in range(tn):
            kn = kvn[:, g * tn + jn:g * tn + jn + 1, :].astype(BF16).astype(F32)
            s_j = jnp.sum(qg_b * kn, axis=-1, keepdims=True)
            s_j = jnp.where(tok >= jn, s_j, -1e30)
            sn.append(s_j)
            m = jnp.maximum(m, s_j)
        ec = jnp.exp(sc - m)
        den = jnp.sum(ec, axis=-1, keepdims=True) + jnp.exp(sink - m)
        en = [jnp.exp(s_j - m) for s_j in sn]
        for e_j in en:
            den = den + e_j
        inv = 1.0 / den
        o = jnp.einsum('bqw,bdw->bqd', (ec * inv).astype(BF16), v_t.astype(BF16), preferred_element_type=F32)
        for jn in range(tn):
            vn = kvn[:, (N_KV_HEADS + g) * tn + jn:(N_KV_HEADS + g) * tn + jn + 1, :].astype(BF16).astype(F32)
            o = o + (en[jn] * inv).astype(BF16).astype(F32) * vn
        o_ref[:, g] = o
    new_t = kvn.reshape(bb * 4 * tn, HEAD_DIM).T
    lane = lax.broadcasted_iota(jnp.int32, (HEAD_DIM, WINDOW), 1)
    for c_ref, n_ref, which in ((ck_ref, nk_ref, 0), (cv_ref, nv_ref, 1)):
        for g in range(N_KV_HEADS):
            for b in range(bb):
                src = (b * 2 * N_KV_HEADS + which * N_KV_HEADS + g) * tn
                fresh = pltpu.roll(new_t, (WINDOW - tn - src) % WINDOW, 1)
                kept = pltpu.roll(c_ref[b, g], WINDOW - tn, 1)
                n_ref[b, g] = jnp.where(lane >= WINDOW - tn, fresh, kept)


def _attn_sample(q, kv, ck, cv, nk_all, nv_all, sinks, tn, layer):
    b = q.shape[0] // tn
    bb = ATTN_SAMPLE_SEQS
    assert bb * 4 * tn == WINDOW
    rows = GQA_GROUP * tn
    qh = q.reshape(tn, b, N_KV_HEADS, GQA_GROUP, HEAD_DIM).transpose(1, 2, 3, 0, 4)
    qh = qh.reshape(b, N_KV_HEADS, rows, HEAD_DIM)
    kvn = kv.reshape(tn, b, 2 * N_KV_HEADS, HEAD_DIM).transpose(1, 2, 0, 3).reshape(b, 4 * tn, HEAD_DIM)
    cache = pl.BlockSpec((None, bb, N_KV_HEADS, HEAD_DIM, WINDOW), lambda i: (layer, i, 0, 0, 0))
    heads = pl.BlockSpec((bb, N_KV_HEADS, rows, HEAD_DIM), lambda i: (i, 0, 0, 0))
    o, nk_all, nv_all = pl.pallas_call(
        functools.partial(_attn_sample_kernel, layer, tn),
        grid=(b // bb,),
        in_specs=[
            pl.BlockSpec(memory_space=pltpu.SMEM),
            heads,
            pl.BlockSpec((bb, 4 * tn, HEAD_DIM), lambda i: (i, 0, 0)),
            cache, cache,
            pl.BlockSpec(memory_space=pl.ANY), pl.BlockSpec(memory_space=pl.ANY),
        ],
        out_specs=[heads, cache, cache],
        out_shape=[
            jax.ShapeDtypeStruct((b, N_KV_HEADS, rows, HEAD_DIM), F32),
            jax.ShapeDtypeStruct(nk_all.shape, F32),
            jax.ShapeDtypeStruct(nv_all.shape, F32),
        ],
        input_output_aliases={5: 1, 6: 2},
        compiler_params=_cparams(("arbitrary",)),
        name="attn_sample",
    )(sinks, qh, kvn, ck, cv, nk_all, nv_all)
    o = o.reshape(b, N_KV_HEADS, GQA_GROUP, tn, HEAD_DIM).transpose(3, 0, 1, 2, 4)
    return o.reshape(tn * b, ATTN_WIDTH), nk_all, nv_all


def _tile_rows(x, n):
    return jnp.concatenate([x] * n, axis=0)


def _rwkv_kernel(c, nseq, x_ref, sh0_ref, st0_ref, mu_ref, lora_ref, gup_ref, vec_ref, ones_ref,
                 stacked_ref, o_ref, st_ref, last_ref, state_ref):
    del stacked_ref
    hw = RWKV_WIDTH
    n = RWKV_HEAD
    cw = RWKV_HEADS * c

    @pl.when(pl.program_id(1) == 0)
    def _():
        last_ref[...] = sh0_ref[...]
        for s in range(nseq):
            for h in range(RWKV_HEADS):
                state_ref[s, :, h * n:(h + 1) * n] = st0_ref[s, h]

    row = lax.broadcasted_iota(jnp.int32, (c, 1), 0)
    lane128 = lax.broadcasted_iota(jnp.int32, (c, 128), 1)
    tri = jnp.where(lax.broadcasted_iota(jnp.int32, (c, c), 1) <= lax.broadcasted_iota(jnp.int32, (c, c), 0),
                    1.0, 0.0).astype(BF16)
    tri3 = jnp.concatenate([tri, tri, tri], axis=1)
    head_rows = (lax.broadcasted_iota(jnp.int32, (cw, hw), 0) // c
                 == lax.broadcasted_iota(jnp.int32, (cw, hw), 1) // n)
    t_i = lax.broadcasted_iota(jnp.int32, (c, cw), 0)
    s_i = lax.broadcasted_iota(jnp.int32, (c, cw), 1) % c
    strict = s_i < t_i
    incl = s_i <= t_i
    eye_all = jnp.where(s_i == t_i, 1.0, 0.0)
    blk = (lax.broadcasted_iota(jnp.int32, (cw, cw), 0) // c
           == lax.broadcasted_iota(jnp.int32, (cw, cw), 1) // c)
    sblk = (lax.broadcasted_iota(jnp.int32, (hw, hw), 0) // n
            == lax.broadcasted_iota(jnp.int32, (hw, hw), 1) // n)
    masks = (row, lane128, tri3, head_rows, strict, incl, eye_all, blk, sblk)
    nsub = x_ref.shape[1] // c
    state_owner = [0] * nseq
    chains = {(sub, s): _rwkv_chunk(c, s, sub, nsub, state_owner, masks, x_ref, mu_ref, lora_ref, gup_ref,
                                    vec_ref, ones_ref, o_ref, st_ref, last_ref, state_ref)
              for sub in range(nsub) for s in range(nseq)}
    rnd = 0
    while chains:
        for key in sorted(chains):
            if rnd >= key[0] * RWKV_STAGGER and next(chains[key], "done") == "done":
                del chains[key]
        rnd += 1


def _rwkv_chunk(c, s, sub, nsub, state_owner, masks, x_ref, mu_ref, lora_ref, gup_ref, vec_ref, ones_ref,
                o_ref, st_ref, last_ref, state_ref):
    row, lane128, tri3, head_rows, strict, incl, eye_all, blk, sblk = masks
    hw = RWKV_WIDTH
    n = RWKV_HEAD
    cw = RWKV_HEADS * c
    x = x_ref[s, sub * c:(sub + 1) * c, :]
    before = last_ref[s] if sub == 0 else x_ref[s, sub * c - 1:sub * c, :]
    prev = jnp.where(row == 0, before, pltpu.roll(x, 1, 0))
    xs = x + (prev - x) * mu_ref[...]

    r = xs[:, 0:hw]
    k = xs[:, hw:2 * hw]
    v = xs[:, 2 * hw:3 * hw]
    wa = xs[:, 3 * hw:3 * hw + 128]
    xg = xs[:, 3 * hw + 128:]
    lora_in = jnp.where(lane128 < 64, jnp.tanh(wa), wa)
    lora = _bdot(lora_in, lora_ref[...])
    w0, a0, k_k, k_a, r_k, ln_w, ln_b = (vec_ref[i:i + 1, :] for i in range(7))
    g = _bdot(_sigmoid(xg), gup_ref[...])
    ones_blk = ones_ref[...]
    kk = k * k_k
    ss = _block_sums(kk * kk, ones_blk)
    yield
    w_log = -_softplus(-(w0 + lora[:, 0:hw])) - 0.5
    logw = -jnp.exp(w_log)
    a = _sigmoid(a0 + lora[:, hw:2 * hw])
    kk = kk * lax.rsqrt(jnp.maximum(ss, 1e-24))
    k2 = k * (1.0 + (a - 1.0) * k_a)
    bv = kk * a

    pieces = []
    rem = logw
    for _ in range(3):
        p = rem.astype(BF16)
        rem = rem - p.astype(F32)
        pieces.append(p)
    cum = jnp.dot(tri3, jnp.concatenate(pieces, axis=0), preferred_element_type=F32)
    bonus_sum = _block_sums(r * k2 * r_k, ones_blk)
    yield
    e_inc = jnp.exp(cum)
    e_exc = jnp.exp(cum - logw)
    e_inv = jnp.exp(-cum)
    kq = kk * e_exc
    rq = r * e_inc
    kd = k2 * e_inv
    bd = bv * e_inv
    w_end = e_inc[c - 1:c, :]
    kend = kd * w_end
    bend = bd * w_end

    def expand(z):
        return jnp.where(head_rows, _tile_rows(z, RWKV_HEADS), 0.0)

    lhs = jnp.concatenate([kq, rq], axis=0).astype(BF16)
    rhs = jnp.concatenate([expand(bd), expand(kd)], axis=0).astype(BF16)
    nt = (((1,), (1,)), ((), ()))
    prod = lax.dot_general(lhs, rhs, nt, preferred_element_type=F32)
    yield
    a_b = jnp.where(strict, prod[0:c, 0:cw], 0.0)
    a_k = jnp.where(strict, prod[0:c, cw:2 * cw], 0.0)
    p_b = jnp.where(incl, prod[c:2 * c, 0:cw], 0.0)
    p_k = jnp.where(incl, prod[c:2 * c, cw:2 * cw], 0.0)

    def bdiag(z):
        return jnp.where(blk, _tile_rows(z, RWKV_HEADS), 0.0)

    xm = -a_b
    tinv = eye_all + xm
    v_d = expand(v)
    from_v = _bdot(jnp.concatenate([a_k, p_k], axis=0), v_d)
    levels = c.bit_length() - 1
    xm = _bdot(xm, bdiag(xm))
    yield
    for _ in range(1, levels - 1):
        both = _bdot(jnp.concatenate([xm, tinv], axis=0), bdiag(xm))
        yield
        xm = both[0:c]
        tinv = tinv + both[c:2 * c]
    tinv = tinv + _bdot(tinv, bdiag(xm))
    yield
    assert state_owner[s] == sub
    st = state_ref[s]
    st_d = jnp.where(sblk, _tile_rows(st, RWKV_HEADS), 0.0)
    from_state = lax.dot_general(lhs, st_d.astype(BF16), nt, preferred_element_type=F32)
    yield
    u = _bdot(tinv, expand(from_state[0:c] + from_v[0:c]))
    yield
    y = from_state[c:2 * c] + from_v[c:2 * c] - _bdot(p_b, expand(u))

    lhs_s = jnp.concatenate([v, u], axis=0)
    rhs_s = jnp.concatenate([kend, -bend], axis=0)
    tn_dims = (((0,), (0,)), ((), ()))
    gm = lax.dot_general(lhs_s.astype(BF16), rhs_s.astype(BF16), tn_dims, preferred_element_type=F32)
    yield
    gm = jnp.where(sblk, gm, 0.0)
    st_new = st * w_end + (gm[0:n] + gm[n:2 * n] + (gm[2 * n:3 * n] + gm[3 * n:4 * n]))
    state_ref[s] = st_new
    state_owner[s] = sub + 1
    if sub == nsub - 1:
        state_owner[s] = 0
        last_ref[s] = x[c - 1:c, :]
        for h in range(RWKV_HEADS):
            st_ref[s, h] = st_new[:, h * n:(h + 1) * n]

    mean = _block_sums(y, ones_blk) * (1.0 / n)
    yield
    yc = y - mean
    var = _block_sums(yc * yc, ones_blk) * (1.0 / n)
    yield
    yn = yc * lax.rsqrt(var + GN_EPS) * ln_w + ln_b
    o_ref[s, sub * c:(sub + 1) * c, :] = (yn + bonus_sum * v) * g


def _rwkv(x, sh0, st0, mu, lora_w, g_up, vecs, ones_blk, stacked, layer):
    b, t, _ = x.shape
    c = RWKV_CHUNK
    tstep = RWKV_SUBCHUNKS * c
    nseq = min(RWKV_SEQS, b)
    wspec = lambda s: _layer_spec(s, layer, 2)
    hstate = (nseq, RWKV_HEADS, RWKV_HEAD, RWKV_HEAD)
    return pl.pallas_call(
        functools.partial(_rwkv_kernel, c, nseq),
        grid=(b // nseq, t // tstep),
        in_specs=[
            pl.BlockSpec((nseq, tstep, RWKV_PROJ), lambda i, j: (i, j, 0)),
            pl.BlockSpec((nseq, 1, RWKV_PROJ), lambda i, j: (i, 0, 0)),
            pl.BlockSpec(hstate, lambda i, j: (i, 0, 0, 0)),
            wspec((1, RWKV_PROJ)), wspec((128, 512)), wspec((128, RWKV_WIDTH)), wspec((8, RWKV_WIDTH)),
            pl.BlockSpec((RWKV_WIDTH, RWKV_WIDTH), lambda i, j: (0, 0)),
            pl.BlockSpec(memory_space=pl.ANY),
        ],
        out_specs=[
            pl.BlockSpec((nseq, tstep, RWKV_WIDTH), lambda i, j: (i, j, 0)),
            pl.BlockSpec((None,) + hstate, lambda i, j: (layer, i, 0, 0, 0)),
        ],
        out_shape=[
            jax.ShapeDtypeStruct((b, t, RWKV_WIDTH), F32),
            jax.ShapeDtypeStruct(stacked.shape, F32),
        ],
        scratch_shapes=[pltpu.VMEM((nseq, 1, RWKV_PROJ), F32), pltpu.VMEM((nseq, RWKV_HEAD, RWKV_WIDTH), F32)],
        input_output_aliases={8: 1},
        compiler_params=_cparams(("arbitrary", "arbitrary")),
        name="rwkv",
    )(x, sh0, st0, mu, lora_w, g_up, vecs, ones_blk, stacked)


def _rwkv_step_kernel(tn, nb, x_ref, sh0_ref, st0_ref, mu_ref, lora_ref, gup_ref, vec_ref, ones_ref,
                      stacked_ref, o_ref, st_ref, vt_ref, nat_ref, yt_ref):
    del stacked_ref
    hw = RWKV_WIDTH
    n = RWKV_HEAD
    h = pl.program_id(0)
    rows = tn * nb

    @pl.when(h == 0)
    def _():
        x = x_ref[...]
        prev = jnp.concatenate([sh0_ref[...], x[0:rows - nb]], axis=0)
        xs = x + (prev - x) * mu_ref[...]
        r = xs[:, 0:hw]
        k = xs[:, hw:2 * hw]
        v = xs[:, 2 * hw:3 * hw]
        wa = xs[:, 3 * hw:3 * hw + 128]
        xg = xs[:, 3 * hw + 128:]
        lane128 = lax.broadcasted_iota(jnp.int32, (rows, 128), 1)
        lora = _bdot(jnp.where(lane128 < 64, jnp.tanh(wa), wa), lora_ref[...])
        w0, a0, k_k, k_a, r_k = (vec_ref[i:i + 1, :] for i in range(5))
        ones_blk = ones_ref[...]
        w_log = -_softplus(-(w0 + lora[:, 0:hw])) - 0.5
        decay = jnp.exp(-jnp.exp(w_log))
        a = _sigmoid(a0 + lora[:, hw:2 * hw])
        kk = k * k_k
        kk = kk * lax.rsqrt(jnp.maximum(_block_sums(kk * kk, ones_blk), 1e-24))
        k2 = k * (1.0 + (a - 1.0) * k_a)
        nat_ref[0] = _bdot(_sigmoid(xg), gup_ref[...])
        nat_ref[1] = v
        nat_ref[2] = _block_sums(r * k2 * r_k, ones_blk)
        for i, z in enumerate((r, decay, k2, v, kk, kk * a)):
            for t in range(tn):
                for half in range(hw // 128):
                    vt_ref[i, t, half * 128:(half + 1) * 128, :] = (
                        z[t * nb:(t + 1) * nb, half * 128:(half + 1) * 128].T)

    base = pl.multiple_of(h * n, n)

    def value_row(vi, carry):
        s = st0_ref[vi]
        for t in range(tn):
            r_t, w_t, k_t, _, kk_t, b_t = (vt_ref[i, t, pl.ds(base, n), :] for i in range(6))
            v_row = vt_ref[3, t, pl.ds(base + vi, 1), :]
            u = jnp.sum(s * kk_t, axis=0, keepdims=True)
            s = s * w_t - u * b_t + v_row * k_t
            yt_ref[t, pl.ds(base + vi, 1), :] = jnp.sum(s * r_t, axis=0, keepdims=True)
        st_ref[vi] = s
        return carry

    lax.fori_loop(0, n, value_row, 0, unroll=4)

    @pl.when(h == RWKV_HEADS - 1)
    def _():
        ones_blk = ones_ref[...]
        ln_w = vec_ref[5:6, :]
        ln_b = vec_ref[6:7, :]
        y = jnp.concatenate(
            [jnp.concatenate([yt_ref[t, half * 128:(half + 1) * 128, :].T for half in range(hw // 128)], axis=1)
             for t in range(tn)], axis=0)
        mean = _block_sums(y, ones_blk) * (1.0 / n)
        yc = y - mean
        var = _block_sums(yc * yc, ones_blk) * (1.0 / n)
        yn = yc * lax.rsqrt(var + GN_EPS) * ln_w + ln_b
        o_ref[...] = (yn + nat_ref[2] * nat_ref[1]) * nat_ref[0]


def _rwkv_sample(x, sh0, st0, mu, lora_w, g_up, vecs, ones_blk, stacked, tn, layer):
    rows = x.shape[0]
    nb = rows // tn
    wspec = lambda s: _layer_spec(s, layer, 1)
    head_state = pl.BlockSpec((None, None, RWKV_HEAD, RWKV_HEAD, nb), lambda h: (layer, h, 0, 0, 0))
    return pl.pallas_call(
        functools.partial(_rwkv_step_kernel, tn, nb),
        grid=(RWKV_HEADS,),
        in_specs=[
            pl.BlockSpec((rows, RWKV_PROJ), lambda h: (0, 0)),
            wspec((nb, RWKV_PROJ)), head_state,
            wspec((1, RWKV_PROJ)), wspec((128, 512)), wspec((128, RWKV_WIDTH)), wspec((8, RWKV_WIDTH)),
            pl.BlockSpec((RWKV_WIDTH, RWKV_WIDTH), lambda h: (0, 0)),
            pl.BlockSpec(memory_space=pl.ANY),
        ],
        out_specs=[pl.BlockSpec((rows, RWKV_WIDTH), lambda h: (0, 0)), head_state],
        out_shape=[jax.ShapeDtypeStruct((rows, RWKV_WIDTH), F32), jax.ShapeDtypeStruct(stacked.shape, F32)],
        scratch_shapes=[pltpu.VMEM((6, tn, RWKV_WIDTH, nb), F32), pltpu.VMEM((3, rows, RWKV_WIDTH), F32),
                        pltpu.VMEM((tn, RWKV_WIDTH, nb), F32)],
        input_output_aliases={8: 1},
        compiler_params=_cparams(("arbitrary",)),
        name="rwkv_step",
    )(x, sh0, st0, mu, lora_w, g_up, vecs, ones_blk, stacked)


def _lru_gates(xc, wg_ref, vec_ref):
    w = LRU_WIDTH
    _, b_a, b_i, lam = (vec_ref[i:i + 1, :] for i in range(4))
    gates = _bdot(xc, wg_ref[...])
    r = _sigmoid(gates[:, 0:w] + b_a)
    i = _sigmoid(gates[:, w:2 * w] + b_i)
    log_a = LRU_C * r * (-_softplus(-lam))
    a = jnp.exp(log_a)
    th = jnp.tanh(log_a)
    u = jnp.sqrt(-2.0 * th / (1.0 - th)) * (i * xc)
    return a, u


def _lru_kernel(tc, x_ref, cv0_ref, h0_ref, cw_ref, wg_ref, vec_ref, o_ref, h_ref, ext_ref, hc_ref):
    ti = pl.program_id(1)
    w = LRU_WIDTH
    blk = LRU_SCAN_BLOCK

    @pl.when(ti == 0)
    def _():
        ext_ref[0:8, :] = cv0_ref[...]
        hc_ref[...] = h0_ref[...]

    xb = x_ref[:, 0:w]
    gb = x_ref[:, w:2 * w]
    ext_ref[8:8 + tc, :] = xb
    xc = vec_ref[0:1, :] + xb * cw_ref[CONV_W - 1:CONV_W, :]
    for j in range(CONV_W - 1):
        xc = xc + ext_ref[pl.ds(8 - (CONV_W - 1) + j, tc), :] * cw_ref[j:j + 1, :]
    ext_ref[0:8, :] = xb[tc - 8:tc, :]
    a, u = _lru_gates(xc, wg_ref, vec_ref)
    pos = lax.broadcasted_iota(jnp.int32, (tc, 1), 0) % blk
    span = 1
    while span < 8:
        ok = pos >= span
        a_s = pltpu.roll(a, span, 0)
        u_s = pltpu.roll(u, span, 0)
        u = jnp.where(ok, a * u_s + u, u)
        a = jnp.where(ok, a * a_s, a)
        span *= 2
    a = a.reshape(tc // blk, blk, w)
    u = u.reshape(tc // blk, blk, w)
    while span < blk:
        u = jnp.concatenate([u[:, :span], a[:, span:] * u[:, :blk - span] + u[:, span:]], axis=1)
        a = jnp.concatenate([a[:, :span], a[:, span:] * a[:, :blk - span]], axis=1)
        span *= 2
    gelu = _gelu_tanh(gb)
    carry = hc_ref[...]
    for b in range(tc // blk):
        h = a[b] * carry + u[b]
        carry = h[blk - 1:blk, :]
        o_ref[b * blk:(b + 1) * blk, :] = h * gelu[b * blk:(b + 1) * blk]
    hc_ref[...] = carry
    h_ref[...] = carry


def _lru_step_kernel(tn, nb, x_ref, cv0_ref, h0_ref, cw_ref, wg_ref, vec_ref, o_ref, h_ref):
    w = LRU_WIDTH
    rows = tn * nb
    xb = x_ref[:, 0:w]
    gb = x_ref[:, w:2 * w]
    ext = jnp.concatenate([cv0_ref[j] for j in range(CONV_W - 1)] + [xb], axis=0)
    xc = vec_ref[0:1, :]
    for j in range(CONV_W):
        xc = xc + ext[j * nb:j * nb + rows] * cw_ref[j:j + 1, :]
    a, u = _lru_gates(xc, wg_ref, vec_ref)
    gelu = _gelu_tanh(gb)
    h = h0_ref[...]
    for t in range(tn):
        h = a[t * nb:(t + 1) * nb] * h + u[t * nb:(t + 1) * nb]
        o_ref[t * nb:(t + 1) * nb, :] = h * gelu[t * nb:(t + 1) * nb]
    h_ref[...] = h


def _lru_sample(x, cv0, h0, conv_w, w_gates, vecs, tn, layer):
    rows = x.shape[0]
    nb = rows // tn
    wspec = lambda s: _layer_spec(s, layer, 1)
    return pl.pallas_call(
        functools.partial(_lru_step_kernel, tn, nb),
        grid=(1,),
        in_specs=[pl.BlockSpec((rows, 2 * LRU_WIDTH), lambda i: (0, 0)),
                  wspec((CONV_W - 1, nb, LRU_WIDTH)), wspec((nb, LRU_WIDTH)),
                  wspec((CONV_W, LRU_WIDTH)), wspec((LRU_WIDTH, 2 * LRU_WIDTH)), wspec((4, LRU_WIDTH))],
        out_specs=[pl.BlockSpec((rows, LRU_WIDTH), lambda i: (0, 0)), pl.BlockSpec((nb, LRU_WIDTH), lambda i: (0, 0))],
        out_shape=[jax.ShapeDtypeStruct((rows, LRU_WIDTH), F32), jax.ShapeDtypeStruct((nb, LRU_WIDTH), F32)],
        compiler_params=_cparams(("arbitrary",)),
        name="lru_step",
    )(x, cv0, h0, conv_w, w_gates, vecs)


def _lru(x, cv0, h0, conv_w, w_gates, vecs, layer):
    b, t, _ = x.shape
    out_shape = [jax.ShapeDtypeStruct((b, t, LRU_WIDTH), F32), jax.ShapeDtypeStruct((b, 1, LRU_WIDTH), F32)]
    tc = min(LRU_TILE, t)
    wspec = lambda s: _layer_spec(s, layer, 2)
    return pl.pallas_call(
        functools.partial(_lru_kernel, tc),
        grid=(b, t // tc),
        in_specs=[
            pl.BlockSpec((None, tc, 2 * LRU_WIDTH), lambda i, j: (i, j, 0)),
            pl.BlockSpec((None, 8, LRU_WIDTH), lambda i, j: (i, 0, 0)),
            pl.BlockSpec((None, 1, LRU_WIDTH), lambda i, j: (i, 0, 0)),
            wspec((CONV_W, LRU_WIDTH)), wspec((LRU_WIDTH, 2 * LRU_WIDTH)), wspec((4, LRU_WIDTH)),
        ],
        out_specs=[
            pl.BlockSpec((None, tc, LRU_WIDTH), lambda i, j: (i, j, 0)),
            pl.BlockSpec((None, 1, LRU_WIDTH), lambda i, j: (i, 0, 0)),
        ],
        out_shape=out_shape,
        scratch_shapes=[pltpu.VMEM((tc + 8, LRU_WIDTH), F32), pltpu.VMEM((1, LRU_WIDTH), F32)],
        compiler_params=_cparams(("arbitrary", "arbitrary")),
        name="lru",
    )(x, cv0, h0, conv_w, w_gates, vecs)


def _post_kernel(x_ref, oa_ref, ob_ref, oc_ref, p_ref, wo_ref, wg_ref, wu_ref, wd_ref, pg_ref, pw_ref,
                 vec_ref, y_ref):
    mix = (jnp.dot(oa_ref[...].astype(BF16), wo_ref[0:512, :], preferred_element_type=F32)
           + jnp.dot(ob_ref[...].astype(BF16), wo_ref[512:768, :], preferred_element_type=F32)
           + jnp.dot(oc_ref[...].astype(BF16), wo_ref[768:1024, :], preferred_element_type=F32)
           + vec_ref[0:1, :])
    x = x_ref[...] + _rmsnorm(mix, vec_ref[1:2, :])
    f = _rmsnorm(x, vec_ref[2:3, :]).astype(BF16)
    acc = None
    for lo in range(0, D_FF, FF_CHUNK):
        gate = jnp.dot(f, wg_ref[:, lo:lo + FF_CHUNK], preferred_element_type=F32)
        up = jnp.dot(f, wu_ref[:, lo:lo + FF_CHUNK], preferred_element_type=F32)
        hid = (gate * _sigmoid(gate) * up).astype(BF16)
        part = jnp.dot(hid, wd_ref[lo:lo + FF_CHUNK, :], preferred_element_type=F32)
        acc = part if acc is None else acc + part
    x2 = x + _rmsnorm(acc, vec_ref[3:4, :])
    gate = _sigmoid(jnp.dot(x2.astype(BF16), pg_ref[...], preferred_element_type=F32))
    emb = jnp.dot(p_ref[...].astype(BF16), pw_ref[...], preferred_element_type=F32)
    y_ref[...] = x2 + gate * emb


def _post(x, oa, ob, oc, p, wo, wg, wu, wd, pg, pw, vecs, layer):
    m = x.shape[0]
    tm = min(ROW_TILE, m)
    row = lambda w_: pl.BlockSpec((tm, w_), lambda i: (i, 0))
    wspec = lambda s: _layer_spec(s, layer, 1, pipeline_mode=pl.Buffered(1))
    return pl.pallas_call(
        _post_kernel,
        grid=(m // tm,),
        in_specs=[row(D_MODEL), row(512), row(256), row(256),
                  pl.BlockSpec((None, tm, PLE_DIM), lambda i: (layer, i, 0)),
                  wspec((D_MODEL, D_MODEL)), wspec((D_MODEL, D_FF)), wspec((D_MODEL, D_FF)),
                  wspec((D_FF, D_MODEL)), wspec((D_MODEL, D_MODEL)), wspec((PLE_DIM, D_MODEL)),
                  wspec((4, D_MODEL))],
        out_specs=row(D_MODEL),
        out_shape=jax.ShapeDtypeStruct((m, D_MODEL), F32),
        compiler_params=_cparams(("arbitrary",)),
        name="post",
    )(x, oa, ob, oc, p, wo, wg, wu, wd, pg, pw, vecs)


def _block_diag(w):
    nl, nb, n, _ = w.shape
    eye = jnp.eye(nb, dtype=w.dtype)
    return (eye[None, :, None, :, None] * w[:, :, :, None, :]).reshape(nl, nb * n, nb * n)


def _post_layer(x2, o_a, o_b, o_c, p, wts, layer):
    return _post(x2, o_a, o_b, o_c, p, wts['w_out'], wts['ffn_w_gate'], wts['ffn_w_up'], wts['ffn_w_down'],
                 wts['ple_gate_w'], wts['ple_w'], wts['post_vecs'], layer)


def _layer_prompt(x2, b, t, p, zeros, wkv_all, wts, layer):
    m = b * t
    q, kv, rw, lr = _in_proj(x2, wts['norm_mix_pre'], wts['w_in'], wts['b_in'], layer)
    kv = kv.reshape(b, t, 256)
    rw = rw.reshape(b, t, RWKV_PROJ)
    lr = lr.reshape(b, t, 2 * LRU_WIDTH)
    sh0, st0, cv0, h0 = zeros
    o_a = _attn_prompt(q.reshape(b, t, 512), kv, wts['attn_sinks'], layer)
    nk = kv[:, t - WINDOW:, 0:KV_WIDTH].reshape(b, WINDOW, N_KV_HEADS, HEAD_DIM)
    nv = kv[:, t - WINDOW:, KV_WIDTH:].reshape(b, WINDOW, N_KV_HEADS, HEAD_DIM)
    o_b, wkv_all = _rwkv(rw, sh0, st0, wts['rwkv_mu'], wts['rwkv_lora'], wts['rwkv_g_up'],
                         wts['rwkv_vecs'], wts['ones_blk'], wkv_all, layer)
    nsh = rw[:, t - 1, :]
    o_c, nh = _lru(lr, cv0, h0, wts['lru_conv_w'], wts['lru_w_gates'], wts['lru_vecs'], layer)
    nconv = lr[:, t - (CONV_W - 1):, 0:LRU_WIDTH]
    x2 = _post_layer(x2, o_a.reshape(m, 512), o_b.reshape(m, 256), o_c.reshape(m, 256), p, wts, layer)
    return x2, (nk, nv, nsh, nconv, nh.reshape(b, LRU_WIDTH)), wkv_all


def _layer_sample(x2, b, t, p, state, outs, wts, layer):
    ck, cv, sh0, st0, cv0, h0 = state
    nk_all, nv_all, wkv_all = outs
    q, kv, rw, lr = _in_proj(x2, wts['norm_mix_pre'], wts['w_in'], wts['b_in'], layer)
    o_a, nk_all, nv_all = _attn_sample(q, kv, ck, cv, nk_all, nv_all, wts['attn_sinks'], t, layer)
    o_b, wkv_all = _rwkv_sample(rw, sh0, st0, wts['rwkv_mu'], wts['rwkv_lora'], wts['rwkv_g_up'],
                                wts['rwkv_vecs'], wts['ones_blk'], wkv_all, t, layer)
    nsh = rw[(t - 1) * b:, :]
    o_c, nh = _lru_sample(lr, cv0, h0, wts['lru_conv_w'], wts['lru_w_gates'], wts['lru_vecs'], t, layer)
    nconv = lr[(t - (CONV_W - 1)) * b:, 0:LRU_WIDTH].reshape(CONV_W - 1, b, LRU_WIDTH)
    x2 = _post_layer(x2, o_a, o_b, o_c, p, wts, layer)
    return x2, (nsh, nconv, nh), (nk_all, nv_all, wkv_all)


def kernel(x_prompt, x_sample, cache_k, cache_v, state_shift, state_wkv, state_conv, state_lru,
           p_prompt, p_sample, norm_mix_pre, norm_mix_post, norm_ffn_pre, norm_ffn_post,
           w_in, b_in, attn_sinks, rwkv_mu, rwkv_w0, rwkv_w_up, rwkv_a0, rwkv_a_up, rwkv_g_up,
           rwkv_k_k, rwkv_k_a, rwkv_r_k, rwkv_ln_w, rwkv_ln_b, lru_conv_w, lru_conv_b,
           lru_w_a, lru_b_a, lru_w_i, lru_b_i, lru_L, w_out, b_out, ffn_w_gate, ffn_w_up,
           ffn_w_down, ple_w, ple_gate_w):
    nl = DEPTH
    bp, tp_, _ = x_prompt.shape
    bs, ts, _ = x_sample.shape
    head_id = jnp.arange(RWKV_WIDTH) // RWKV_HEAD
    zeros_w = jnp.zeros((nl, 64, RWKV_WIDTH), F32)
    wts = dict(
        norm_mix_pre=norm_mix_pre[:, None, :], w_in=w_in.astype(BF16), b_in=b_in[:, None, :],
        attn_sinks=attn_sinks,
        rwkv_mu=rwkv_mu[:, None, :],
        rwkv_lora=jnp.concatenate([jnp.concatenate([rwkv_w_up, zeros_w], axis=2),
                                   jnp.concatenate([zeros_w, rwkv_a_up], axis=2)], axis=1).astype(BF16),
        rwkv_g_up=rwkv_g_up.astype(BF16),
        rwkv_vecs=jnp.stack([rwkv_w0, rwkv_a0, rwkv_k_k, rwkv_k_a, rwkv_r_k.reshape(nl, RWKV_WIDTH),
                             rwkv_ln_w, rwkv_ln_b, jnp.zeros((nl, RWKV_WIDTH), F32)], axis=1),
        ones_blk=(head_id[:, None] == head_id[None, :]).astype(BF16),
        lru_conv_w=lru_conv_w,
        lru_w_gates=jnp.concatenate([_block_diag(lru_w_a), _block_diag(lru_w_i)], axis=2).astype(BF16),
        lru_vecs=jnp.stack([lru_conv_b, lru_b_a, lru_b_i, lru_L], axis=1),
        w_out=w_out.astype(BF16), ffn_w_gate=ffn_w_gate.astype(BF16), ffn_w_up=ffn_w_up.astype(BF16),
        ffn_w_down=ffn_w_down.astype(BF16), ple_gate_w=ple_gate_w.astype(BF16), ple_w=ple_w.astype(BF16),
        post_vecs=jnp.stack([b_out, norm_mix_post, norm_ffn_pre, norm_ffn_post], axis=1),
    )
    zeros_p = (jnp.zeros((bp, 1, RWKV_PROJ), F32),
               jnp.zeros((bp, RWKV_HEADS, RWKV_HEAD, RWKV_HEAD), F32),
               jnp.zeros((bp, 8, LRU_WIDTH), F32),
               jnp.zeros((bp, 1, LRU_WIDTH), F32))
    st_s = (cache_k.transpose(0, 1, 3, 4, 2), cache_v.transpose(0, 1, 3, 4, 2),
            state_shift,
            state_wkv.transpose(0, 2, 3, 4, 1),
            state_conv.transpose(0, 2, 1, 3),
            state_lru)
    pp = p_prompt.reshape(nl, bp * tp_, PLE_DIM)
    ps = p_sample.transpose(0, 2, 1, 3).reshape(nl, ts * bs, PLE_DIM)

    xp = x_prompt.reshape(bp * tp_, D_MODEL)
    xs = x_sample.transpose(1, 0, 2).reshape(ts * bs, D_MODEL)
    wkv_p = jnp.zeros((nl, bp, RWKV_HEADS, RWKV_HEAD, RWKV_HEAD), F32)
    outs_s = (jnp.zeros((nl, bs, N_KV_HEADS, HEAD_DIM, WINDOW), F32),
              jnp.zeros((nl, bs, N_KV_HEADS, HEAD_DIM, WINDOW), F32),
              jnp.zeros((nl, RWKV_HEADS, RWKV_HEAD, RWKV_HEAD, bs), F32))
    new_p, new_s = [], []
    for i in range(nl):
        xp, sp, wkv_p = _layer_prompt(xp, bp, tp_, pp, zeros_p, wkv_p, wts, i)
        xs, ss, outs_s = _layer_sample(xs, bs, ts, ps, st_s, outs_s, wts, i)
        new_p.append(sp)
        new_s.append(ss)

    def stk(lst, j):
        return jnp.stack([s[j] for s in lst], axis=0)

    nk_s, nv_s, wkv_s = outs_s
    return (xp.reshape(bp, tp_, D_MODEL), xs.reshape(ts, bs, D_MODEL).transpose(1, 0, 2),
            stk(new_p, 0), stk(new_p, 1), stk(new_p, 2), wkv_p, stk(new_p, 3), stk(new_p, 4),
            nk_s.transpose(0, 1, 4, 2, 3), nv_s.transpose(0, 1, 4, 2, 3), stk(new_s, 0),
            wkv_s.transpose(0, 4, 1, 2, 3), stk(new_s, 1).transpose(0, 2, 1, 3), stk(new_s, 2))
```

```python
import functools

import jax
import jax.numpy as jnp
from jax import lax
from jax.experimental import pallas as pl
from jax.experimental.pallas import tpu as pltpu

F32 = jnp.float32
BF16 = jnp.bfloat16

D_MODEL = 1024
DEPTH = 4
HEAD_DIM = 64
ATTN_WIDTH = 512
N_HEADS = 8
N_KV_HEADS = 2
GQA_GROUP = 4
KV_WIDTH = 128
WINDOW = 128
RWKV_WIDTH = 256
RWKV_HEADS = 4
RWKV_HEAD = 64
RWKV_PROJ = 1024
LRU_WIDTH = 256
CONV_W = 4
LRU_C = 8.0
D_FF = 2816
PLE_DIM = 256
RMS_EPS = 1e-6
GN_EPS = 64e-5
IN_COLS = 2304

ROW_TILE = 512
IN_ROW_TILE = 1024
FF_CHUNK = 1408
RWKV_CHUNK = 64
RWKV_SEQS = 4
RWKV_SUBCHUNKS = 4
RWKV_STAGGER = 3
LRU_TILE = 512
LRU_SCAN_BLOCK = 32
ATTN_SAMPLE_SEQS = 8
ATTN_Q_BLOCKS = 8
VMEM_LIMIT = 56 * 1024 * 1024


def _cparams(sem):
    return pltpu.CompilerParams(dimension_semantics=sem, vmem_limit_bytes=VMEM_LIMIT)


def _layer_spec(shape, layer, nidx, **kw):
    zeros = (0,) * len(shape)
    if nidx == 1:
        return pl.BlockSpec((None,) + tuple(shape), lambda i: (layer,) + zeros, **kw)
    return pl.BlockSpec((None,) + tuple(shape), lambda i, j: (layer,) + zeros, **kw)


def _bdot(a, b):
    return jnp.dot(a.astype(BF16), b.astype(BF16), preferred_element_type=F32)


def _block_sums(x, ones_blk):
    return jnp.dot(x.astype(BF16), ones_blk, preferred_element_type=F32)


def _rmsnorm(x, g):
    ms = jnp.mean(x * x, axis=-1, keepdims=True)
    return x * lax.rsqrt(ms + RMS_EPS) * g


def _softplus(x):
    return jnp.maximum(x, 0.0) + jnp.log1p(jnp.exp(-jnp.abs(x)))


def _sigmoid(x):
    return 1.0 / (1.0 + jnp.exp(-x))


def _gelu_tanh(x):
    return 0.5 * x * (1.0 + jnp.tanh(0.7978845608028654 * (x + 0.044715 * (x * x * x))))


def _in_kernel(x_ref, g_ref, w_ref, b_ref, q_ref, kv_ref, rw_ref, lr_ref):
    h = _rmsnorm(x_ref[...], g_ref[...]).astype(BF16)
    for ref, lo, hi in ((q_ref, 0, 512), (kv_ref, 512, 768), (rw_ref, 768, 1792), (lr_ref, 1792, 2304)):
        ref[...] = jnp.dot(h, w_ref[:, lo:hi], preferred_element_type=F32) + b_ref[:, lo:hi]


def _in_proj(x, g, w, b, layer):
    m = x.shape[0]
    tm = min(IN_ROW_TILE, m)
    row = lambda w_: pl.BlockSpec((tm, w_), lambda i: (i, 0))
    return pl.pallas_call(
        _in_kernel,
        grid=(m // tm,),
        in_specs=[row(D_MODEL), _layer_spec((1, D_MODEL), layer, 1),
                  _layer_spec((D_MODEL, IN_COLS), layer, 1), _layer_spec((1, IN_COLS), layer, 1)],
        out_specs=[row(512), row(256), row(1024), row(512)],
        out_shape=[jax.ShapeDtypeStruct((m, n), F32) for n in (512, 256, 1024, 512)],
        compiler_params=_cparams(("arbitrary",)),
        name="in_proj",
    )(x, g, w, b)


def _attn_prompt_kernel(layer, nq, sink_ref, q_ref, kvp_ref, kvc_ref, o_ref):
    j = pl.program_id(1)
    log2e = 1.4426950408889634
    q = q_ref[...] * (HEAD_DIM ** -0.5 * log2e)
    kv = jnp.concatenate([kvp_ref[...], kvc_ref[...]], axis=0)
    kj = lax.broadcasted_iota(jnp.int32, (2 * WINDOW, WINDOW), 0)
    qi = lax.broadcasted_iota(jnp.int32, (2 * WINDOW, WINDOW), 1) + WINDOW
    d = qi - kj
    band = (d >= 0) & (d <= WINDOW)
    first = band & ((j > 0) | (kj >= WINDOW))
    nt = (((1,), (1,)), ((), ()))
    keys = kv[:, 0:KV_WIDTH].astype(BF16)
    v_t = kv[:, KV_WIDTH:2 * KV_WIDTH].T
    ones_rows = (lax.broadcasted_iota(jnp.int32, (8, 2 * WINDOW), 0) == 0).astype(F32)
    scores = {}
    for blk in range(nq):
        qb = q[blk * WINDOW:(blk + 1) * WINDOW]
        for g in range(N_KV_HEADS):
            qg = jnp.concatenate([qb[:, (g * GQA_GROUP + hh) * HEAD_DIM:(g * GQA_GROUP + hh + 1) * HEAD_DIM]
                                  for hh in range(GQA_GROUP)], axis=0).astype(BF16)
            k_ext = keys[blk * WINDOW:(blk + 2) * WINDOW, g * HEAD_DIM:(g + 1) * HEAD_DIM]
            scores[blk, g] = lax.dot_general(k_ext, qg, nt, preferred_element_type=F32)
    probs, sink_terms = {}, {}
    for blk in range(nq):
        mask = first if blk == 0 else band
        for g in range(N_KV_HEADS):
            pg, sg = [], []
            for hh in range(GQA_GROUP):
                s = jnp.where(mask, scores[blk, g][:, hh * WINDOW:(hh + 1) * WINDOW], -1e30)
                sink = sink_ref[layer, g * GQA_GROUP + hh] * log2e
                m = jnp.maximum(jnp.max(s, axis=0, keepdims=True), sink)
                pg.append(jnp.exp2(s - m).astype(BF16))
                sg.append(jnp.exp2(sink - m))
            probs[blk, g] = jnp.concatenate(pg, axis=1)
            sink_terms[blk, g] = jnp.concatenate(sg, axis=1)
    for blk in range(nq):
        outs = []
        for g in range(N_KV_HEADS):
            v_aug = jnp.concatenate([v_t[g * HEAD_DIM:(g + 1) * HEAD_DIM, blk * WINDOW:(blk + 2) * WINDOW],
                                     ones_rows], axis=0).astype(BF16)
            og = jnp.dot(v_aug, probs[blk, g], preferred_element_type=F32)
            den = og[HEAD_DIM:HEAD_DIM + 1, :] + sink_terms[blk, g]
            og = og[0:HEAD_DIM, :] * (1.0 / den)
            outs.extend(og[:, hh * WINDOW:(hh + 1) * WINDOW] for hh in range(GQA_GROUP))
        o_ref[blk * WINDOW:(blk + 1) * WINDOW, :] = jnp.concatenate(outs, axis=0).T


def _attn_prompt(q, kv, sinks, layer):
    b, t, _ = q.shape
    nq = ATTN_Q_BLOCKS
    tq = nq * WINDOW
    return pl.pallas_call(
        functools.partial(_attn_prompt_kernel, layer, nq),
        grid=(b, t // tq),
        in_specs=[
            pl.BlockSpec(memory_space=pltpu.SMEM),
            pl.BlockSpec((None, tq, ATTN_WIDTH), lambda i, j: (i, j, 0)),
            pl.BlockSpec((None, WINDOW, 2 * KV_WIDTH), lambda i, j: (i, jnp.maximum(j * nq - 1, 0), 0)),
            pl.BlockSpec((None, tq, 2 * KV_WIDTH), lambda i, j: (i, j, 0)),
        ],
        out_specs=pl.BlockSpec((None, tq, ATTN_WIDTH), lambda i, j: (i, j, 0)),
        out_shape=jax.ShapeDtypeStruct((b, t, ATTN_WIDTH), F32),
        compiler_params=_cparams(("arbitrary", "arbitrary")),
        name="attn_prompt",
    )(sinks, q, kv, kv)


def _attn_sample_kernel(layer, tn, sink_ref, q_ref, kvn_ref, ck_ref, cv_ref, nk_in_ref, nv_in_ref,
                        o_ref, nk_ref, nv_ref):
    del nk_in_ref, nv_in_ref
    bb = q_ref.shape[0]
    rows = GQA_GROUP * tn
    kvn = kvn_ref[...]
    row = lax.broadcasted_iota(jnp.int32, (1, rows, 1), 1)
    tok = row % tn
    col = lax.broadcasted_iota(jnp.int32, (1, 1, WINDOW), 2)
    cmask = col >= tok
    for g in range(N_KV_HEADS):
        k_t = ck_ref[:, g]
        v_t = cv_ref[:, g]
        qg = q_ref[:, g] * (HEAD_DIM ** -0.5)
        qg_b = qg.astype(BF16).astype(F32)
        sc = jnp.einsum('bqd,bdw->bqw', qg.astype(BF16), k_t.astype(BF16), preferred_element_type=F32)
        sc = jnp.where(cmask, sc, -1e30)
        sink = jnp.zeros((1, rows, 1), F32)
        for hh in range(GQA_GROUP):
            sink = jnp.where(row // tn == hh, sink_ref[layer, g * GQA_GROUP + hh], sink)
        m = jnp.maximum(jnp.max(sc, axis=-1, keepdims=True), sink)
        sn = []
        for jn in range(tn):
            kn = kvn[:, g * tn + jn:g * tn + jn + 1, :].astype(BF16).astype(F32)
            s_j = jnp.sum(qg_b * kn, axis=-1, keepdims=True)
            s_j = jnp.where(tok >= jn, s_j, -1e30)
            sn.append(s_j)
            m = jnp.maximum(m, s_j)
        ec = jnp.exp(sc - m)
        den = jnp.sum(ec, axis=-1, keepdims=True) + jnp.exp(sink - m)
        en = [jnp.exp(s_j - m) for s_j in sn]
        for e_j in en:
            den = den + e_j
        inv = 1.0 / den
        o = jnp.einsum('bqw,bdw->bqd', (ec * inv).astype(BF16), v_t.astype(BF16), preferred_element_type=F32)
        for jn in range(tn):
            vn = kvn[:, (N_KV_HEADS + g) * tn + jn:(N_KV_HEADS + g) * tn + jn + 1, :].astype(BF16).astype(F32)
            o = o + (en[jn] * inv).astype(BF16).astype(F32) * vn
        o_ref[:, g] = o
    new_t = kvn.reshape(bb * 4 * tn, HEAD_DIM).T
    lane = lax.broadcasted_iota(jnp.int32, (HEAD_DIM, WINDOW), 1)
    for c_ref, n_ref, which in ((ck_ref, nk_ref, 0), (cv_ref, nv_ref, 1)):
        for g in range(N_KV_HEADS):
            for b in range(bb):
                src = (b * 2 * N_KV_HEADS + which * N_KV_HEADS + g) * tn
                fresh = pltpu.roll(new_t, (WINDOW - tn - src) % WINDOW, 1)
                kept = pltpu.roll(c_ref[b, g], WINDOW - tn, 1)
                n_ref[b, g] = jnp.where(lane >= WINDOW - tn, fresh, kept)


def _attn_sample(q, kv, ck, cv, nk_all, nv_all, sinks, tn, layer):
    b = q.shape[0] // tn
    bb = ATTN_SAMPLE_SEQS
    assert bb * 4 * tn == WINDOW
    rows = GQA_GROUP * tn
    qh = q.reshape(tn, b, N_KV_HEADS, GQA_GROUP, HEAD_DIM).transpose(1, 2, 3, 0, 4)
    qh = qh.reshape(b, N_KV_HEADS, rows, HEAD_DIM)
    kvn = kv.reshape(tn, b, 2 * N_KV_HEADS, HEAD_DIM).transpose(1, 2, 0, 3).reshape(b, 4 * tn, HEAD_DIM)
    cache = pl.BlockSpec((None, bb, N_KV_HEADS, HEAD_DIM, WINDOW), lambda i: (layer, i, 0, 0, 0))
    heads = pl.BlockSpec((bb, N_KV_HEADS, rows, HEAD_DIM), lambda i: (i, 0, 0, 0))
    o, nk_all, nv_all = pl.pallas_call(
        functools.partial(_attn_sample_kernel, layer, tn),
        grid=(b // bb,),
        in_specs=[
            pl.BlockSpec(memory_space=pltpu.SMEM),
            heads,
            pl.BlockSpec((bb, 4 * tn, HEAD_DIM), lambda i: (i, 0, 0)),
            cache, cache,
            pl.BlockSpec(memory_space=pl.ANY), pl.BlockSpec(memory_space=pl.ANY),
        ],
        out_specs=[heads, cache, cache],
        out_shape=[
            jax.ShapeDtypeStruct((b, N_KV_HEADS, rows, HEAD_DIM), F32),
            jax.ShapeDtypeStruct(nk_all.shape, F32),
            jax.ShapeDtypeStruct(nv_all.shape, F32),
        ],
        input_output_aliases={5: 1, 6: 2},
        compiler_params=_cparams(("arbitrary",)),
        name="attn_sample",
    )(sinks, qh, kvn, ck, cv, nk_all, nv_all)
    o = o.reshape(b, N_KV_HEADS, GQA_GROUP, tn, HEAD_DIM).transpose(3, 0, 1, 2, 4)
    return o.reshape(tn * b, ATTN_WIDTH), nk_all, nv_all


def _tile_rows(x, n):
    return jnp.concatenate([x] * n, axis=0)


def _rwkv_kernel(c, nseq, x_ref, sh0_ref, st0_ref, mu_ref, lora_ref, gup_ref, vec_ref, ones_ref,
                 stacked_ref, o_ref, st_ref, last_ref, state_ref):
    del stacked_ref
    hw = RWKV_WIDTH
    n = RWKV_HEAD
    cw = RWKV_HEADS * c

    @pl.when(pl.program_id(1) == 0)
    def _():
        last_ref[...] = sh0_ref[...]
        for s in range(nseq):
            for h in range(RWKV_HEADS):
                state_ref[s, :, h * n:(h + 1) * n] = st0_ref[s, h]

    row = lax.broadcasted_iota(jnp.int32, (c, 1), 0)
    lane128 = lax.broadcasted_iota(jnp.int32, (c, 128), 1)
    tri = jnp.where(lax.broadcasted_iota(jnp.int32, (c, c), 1) <= lax.broadcasted_iota(jnp.int32, (c, c), 0),
                    1.0, 0.0).astype(BF16)
    tri3 = jnp.concatenate([tri, tri, tri], axis=1)
    head_rows = (lax.broadcasted_iota(jnp.int32, (cw, hw), 0) // c
                 == lax.broadcasted_iota(jnp.int32, (cw, hw), 1) // n)
    t_i = lax.broadcasted_iota(jnp.int32, (c, cw), 0)
    s_i = lax.broadcasted_iota(jnp.int32, (c, cw), 1) % c
    strict = s_i < t_i
    incl = s_i <= t_i
    eye_all = jnp.where(s_i == t_i, 1.0, 0.0)
    blk = (lax.broadcasted_iota(jnp.int32, (cw, cw), 0) // c
           == lax.broadcasted_iota(jnp.int32, (cw, cw), 1) // c)
    sblk = (lax.broadcasted_iota(jnp.int32, (hw, hw), 0) // n
            == lax.broadcasted_iota(jnp.int32, (hw, hw), 1) // n)
    masks = (row, lane128, tri3, head_rows, strict, incl, eye_all, blk, sblk)
    nsub = x_ref.shape[1] // c
    state_owner = [0] * nseq
    shared = {"lora": lora_ref, "gup": gup_ref, "sums": ones_ref}
    chains = {sub: [_rwkv_chunk(c, s, sub, nsub, state_owner, masks, x_ref, mu_ref, vec_ref,
                                o_ref, st_ref, last_ref, state_ref) for s in range(nseq)]
              for sub in range(nsub)}
    answers = {sub: [None] * nseq for sub in range(nsub)}
    rnd = 0
    while chains:
        for sub in sorted(chains):
            if rnd < sub * RWKV_STAGGER:
                continue
            asked = [_advance(ch, ans) for ch, ans in zip(chains[sub], answers[sub])]
            if asked[0] == "done":
                del chains[sub]
                continue
            results = [[] for _ in range(nseq)]
            for i, (kind, _) in enumerate(asked[0] or ()):
                stacked = _bdot(jnp.concatenate([req[i][1] for req in asked], axis=0), shared[kind][...])
                for s in range(nseq):
                    results[s].append(stacked[s * c:(s + 1) * c])
            answers[sub] = [tuple(res) for res in results]
        rnd += 1


def _advance(chain, answer):
    try:
        return next(chain) if answer is None else chain.send(answer)
    except StopIteration:
        return "done"


def _rwkv_chunk(c, s, sub, nsub, state_owner, masks, x_ref, mu_ref, vec_ref, o_ref, st_ref, last_ref, state_ref):
    row, lane128, tri3, head_rows, strict, incl, eye_all, blk, sblk = masks
    hw = RWKV_WIDTH
    n = RWKV_HEAD
    cw = RWKV_HEADS * c
    x = x_ref[s, sub * c:(sub + 1) * c, :]
    before = last_ref[s] if sub == 0 else x_ref[s, sub * c - 1:sub * c, :]
    prev = jnp.where(row == 0, before, pltpu.roll(x, 1, 0))
    xs = x + (prev - x) * mu_ref[...]

    r = xs[:, 0:hw]
    k = xs[:, hw:2 * hw]
    v = xs[:, 2 * hw:3 * hw]
    wa = xs[:, 3 * hw:3 * hw + 128]
    xg = xs[:, 3 * hw + 128:]
    lora_in = jnp.where(lane128 < 64, jnp.tanh(wa), wa)
    w0, a0, k_k, k_a, r_k, ln_w, ln_b = (vec_ref[i:i + 1, :] for i in range(7))
    kk = k * k_k
    lora, g, ss = yield (("lora", lora_in), ("gup", _sigmoid(xg)), ("sums", kk * kk))
    w_log = -_softplus(-(w0 + lora[:, 0:hw])) - 0.5
    logw = -jnp.exp(w_log)
    a = _sigmoid(a0 + lora[:, hw:2 * hw])
    kk = kk * lax.rsqrt(jnp.maximum(ss, 1e-24))
    k2 = k * (1.0 + (a - 1.0) * k_a)
    bv = kk * a

    pieces = []
    rem = logw
    for _ in range(3):
        p = rem.astype(BF16)
        rem = rem - p.astype(F32)
        pieces.append(p)
    cum = jnp.dot(tri3, jnp.concatenate(pieces, axis=0), preferred_element_type=F32)
    (bonus_sum,) = yield (("sums", r * k2 * r_k),)
    e_inc = jnp.exp(cum)
    e_exc = jnp.exp(cum - logw)
    e_inv = jnp.exp(-cum)
    kq = kk * e_exc
    rq = r * e_inc
    kd = k2 * e_inv
    bd = bv * e_inv
    w_end = e_inc[c - 1:c, :]
    kend = kd * w_end
    bend = bd * w_end

    def expand(z):
        return jnp.where(head_rows, _tile_rows(z, RWKV_HEADS), 0.0)

    lhs = jnp.concatenate([kq, rq], axis=0).astype(BF16)
    rhs = jnp.concatenate([expand(bd), expand(kd)], axis=0).astype(BF16)
    nt = (((1,), (1,)), ((), ()))
    prod = lax.dot_general(lhs, rhs, nt, preferred_element_type=F32)
    yield
    a_b = jnp.where(strict, prod[0:c, 0:cw], 0.0)
    a_k = jnp.where(strict, prod[0:c, cw:2 * cw], 0.0)
    p_b = jnp.where(incl, prod[c:2 * c, 0:cw], 0.0)
    p_k = jnp.where(incl, prod[c:2 * c, cw:2 * cw], 0.0)

    def bdiag(z):
        return jnp.where(blk, _tile_rows(z, RWKV_HEADS), 0.0)

    xm = -a_b
    tinv = eye_all + xm
    v_d = expand(v)
    from_v = _bdot(jnp.concatenate([a_k, p_k], axis=0), v_d)
    levels = c.bit_length() - 1
    xm = _bdot(xm, bdiag(xm))
    yield
    for _ in range(1, levels - 1):
        both = _bdot(jnp.concatenate([xm, tinv], axis=0), bdiag(xm))
        yield
        xm = both[0:c]
        tinv = tinv + both[c:2 * c]
    tinv = tinv + _bdot(tinv, bdiag(xm))
    yield
    assert state_owner[s] == sub
    st = state_ref[s]
    st_d = jnp.where(sblk, _tile_rows(st, RWKV_HEADS), 0.0)
    from_state = lax.dot_general(lhs, st_d.astype(BF16), nt, preferred_element_type=F32)
    yield
    u = _bdot(tinv, expand(from_state[0:c] + from_v[0:c]))
    yield
    y = from_state[c:2 * c] + from_v[c:2 * c] - _bdot(p_b, expand(u))

    lhs_s = jnp.concatenate([v, u], axis=0)
    rhs_s = jnp.concatenate([kend, -bend], axis=0)
    tn_dims = (((0,), (0,)), ((), ()))
    gm = lax.dot_general(lhs_s.astype(BF16), rhs_s.astype(BF16), tn_dims, preferred_element_type=F32)
    yield
    gm = jnp.where(sblk, gm, 0.0)
    st_new = st * w_end + (gm[0:n] + gm[n:2 * n] + (gm[2 * n:3 * n] + gm[3 * n:4 * n]))
    state_ref[s] = st_new
    state_owner[s] = sub + 1
    if sub == nsub - 1:
        state_owner[s] = 0
        last_ref[s] = x[c - 1:c, :]
        for h in range(RWKV_HEADS):
            st_ref[s, h] = st_new[:, h * n:(h + 1) * n]

    (y_sum,) = yield (("sums", y),)
    yc = y - y_sum * (1.0 / n)
    (sq_sum,) = yield (("sums", yc * yc),)
    yn = yc * lax.rsqrt(sq_sum * (1.0 / n) + GN_EPS) * ln_w + ln_b
    o_ref[s, sub * c:(sub + 1) * c, :] = (yn + bonus_sum * v) * g


def _rwkv(x, sh0, st0, mu, lora_w, g_up, vecs, ones_blk, stacked, layer):
    b, t, _ = x.shape
    c = RWKV_CHUNK
    tstep = RWKV_SUBCHUNKS * c
    nseq = min(RWKV_SEQS, b)
    wspec = lambda s: _layer_spec(s, layer, 2)
    hstate = (nseq, RWKV_HEADS, RWKV_HEAD, RWKV_HEAD)
    return pl.pallas_call(
        functools.partial(_rwkv_kernel, c, nseq),
        grid=(b // nseq, t // tstep),
        in_specs=[
            pl.BlockSpec((nseq, tstep, RWKV_PROJ), lambda i, j: (i, j, 0)),
            pl.BlockSpec((nseq, 1, RWKV_PROJ), lambda i, j: (i, 0, 0)),
            pl.BlockSpec(hstate, lambda i, j: (i, 0, 0, 0)),
            wspec((1, RWKV_PROJ)), wspec((128, 512)), wspec((128, RWKV_WIDTH)), wspec((8, RWKV_WIDTH)),
            pl.BlockSpec((RWKV_WIDTH, RWKV_WIDTH), lambda i, j: (0, 0)),
            pl.BlockSpec(memory_space=pl.ANY),
        ],
        out_specs=[
            pl.BlockSpec((nseq, tstep, RWKV_WIDTH), lambda i, j: (i, j, 0)),
            pl.BlockSpec((None,) + hstate, lambda i, j: (layer, i, 0, 0, 0)),
        ],
        out_shape=[
            jax.ShapeDtypeStruct((b, t, RWKV_WIDTH), F32),
            jax.ShapeDtypeStruct(stacked.shape, F32),
        ],
        scratch_shapes=[pltpu.VMEM((nseq, 1, RWKV_PROJ), F32), pltpu.VMEM((nseq, RWKV_HEAD, RWKV_WIDTH), F32)],
        input_output_aliases={8: 1},
        compiler_params=_cparams(("arbitrary", "arbitrary")),
        name="rwkv",
    )(x, sh0, st0, mu, lora_w, g_up, vecs, ones_blk, stacked)


def _rwkv_step_kernel(tn, nb, x_ref, sh0_ref, st0_ref, mu_ref, lora_ref, gup_ref, vec_ref, ones_ref,
                      stacked_ref, o_ref, st_ref, vt_ref, nat_ref, yt_ref):
    del stacked_ref
    hw = RWKV_WIDTH
    n = RWKV_HEAD
    h = pl.program_id(0)
    rows = tn * nb

    @pl.when(h == 0)
    def _():
        x = x_ref[...]
        prev = jnp.concatenate([sh0_ref[...], x[0:rows - nb]], axis=0)
        xs = x + (prev - x) * mu_ref[...]
        r = xs[:, 0:hw]
        k = xs[:, hw:2 * hw]
        v = xs[:, 2 * hw:3 * hw]
        wa = xs[:, 3 * hw:3 * hw + 128]
        xg = xs[:, 3 * hw + 128:]
        lane128 = lax.broadcasted_iota(jnp.int32, (rows, 128), 1)
        lora = _bdot(jnp.where(lane128 < 64, jnp.tanh(wa), wa), lora_ref[...])
        w0, a0, k_k, k_a, r_k = (vec_ref[i:i + 1, :] for i in range(5))
        ones_blk = ones_ref[...]
        w_log = -_softplus(-(w0 + lora[:, 0:hw])) - 0.5
        decay = jnp.exp(-jnp.exp(w_log))
        a = _sigmoid(a0 + lora[:, hw:2 * hw])
        kk = k * k_k
        kk = kk * lax.rsqrt(jnp.maximum(_block_sums(kk * kk, ones_blk), 1e-24))
        k2 = k * (1.0 + (a - 1.0) * k_a)
        nat_ref[0] = _bdot(_sigmoid(xg), gup_ref[...])
        nat_ref[1] = v
        nat_ref[2] = _block_sums(r * k2 * r_k, ones_blk)
        for i, z in enumerate((r, decay, k2, v, kk, kk * a)):
            for t in range(tn):
                for half in range(hw // 128):
                    vt_ref[i, t, half * 128:(half + 1) * 128, :] = (
                        z[t * nb:(t + 1) * nb, half * 128:(half + 1) * 128].T)

    base = pl.multiple_of(h * n, n)

    def value_row(vi, carry):
        s = st0_ref[vi]
        for t in range(tn):
            r_t, w_t, k_t, _, kk_t, b_t = (vt_ref[i, t, pl.ds(base, n), :] for i in range(6))
            v_row = vt_ref[3, t, pl.ds(base + vi, 1), :]
            u = jnp.sum(s * kk_t, axis=0, keepdims=True)
            s = s * w_t - u * b_t + v_row * k_t
            yt_ref[t, pl.ds(base + vi, 1), :] = jnp.sum(s * r_t, axis=0, keepdims=True)
        st_ref[vi] = s
        return carry

    lax.fori_loop(0, n, value_row, 0, unroll=4)

    @pl.when(h == RWKV_HEADS - 1)
    def _():
        ones_blk = ones_ref[...]
        ln_w = vec_ref[5:6, :]
        ln_b = vec_ref[6:7, :]
        y = jnp.concatenate(
            [jnp.concatenate([yt_ref[t, half * 128:(half + 1) * 128, :].T for half in range(hw // 128)], axis=1)
             for t in range(tn)], axis=0)
        mean = _block_sums(y, ones_blk) * (1.0 / n)
        yc = y - mean
        var = _block_sums(yc * yc, ones_blk) * (1.0 / n)
        yn = yc * lax.rsqrt(var + GN_EPS) * ln_w + ln_b
        o_ref[...] = (yn + nat_ref[2] * nat_ref[1]) * nat_ref[0]


def _rwkv_sample(x, sh0, st0, mu, lora_w, g_up, vecs, ones_blk, stacked, tn, layer):
    rows = x.shape[0]
    nb = rows // tn
    wspec = lambda s: _layer_spec(s, layer, 1)
    head_state = pl.BlockSpec((None, None, RWKV_HEAD, RWKV_HEAD, nb), lambda h: (layer, h, 0, 0, 0))
    return pl.pallas_call(
        functools.partial(_rwkv_step_kernel, tn, nb),
        grid=(RWKV_HEADS,),
        in_specs=[
            pl.BlockSpec((rows, RWKV_PROJ), lambda h: (0, 0)),
            wspec((nb, RWKV_PROJ)), head_state,
            wspec((1, RWKV_PROJ)), wspec((128, 512)), wspec((128, RWKV_WIDTH)), wspec((8, RWKV_WIDTH)),
            pl.BlockSpec((RWKV_WIDTH, RWKV_WIDTH), lambda h: (0, 0)),
            pl.BlockSpec(memory_space=pl.ANY),
        ],
        out_specs=[pl.BlockSpec((rows, RWKV_WIDTH), lambda h: (0, 0)), head_state],
        out_shape=[jax.ShapeDtypeStruct((rows, RWKV_WIDTH), F32), jax.ShapeDtypeStruct(stacked.shape, F32)],
        scratch_shapes=[pltpu.VMEM((6, tn, RWKV_WIDTH, nb), F32), pltpu.VMEM((3, rows, RWKV_WIDTH), F32),
                        pltpu.VMEM((tn, RWKV_WIDTH, nb), F32)],
        input_output_aliases={8: 1},
        compiler_params=_cparams(("arbitrary",)),
        name="rwkv_step",
    )(x, sh0, st0, mu, lora_w, g_up, vecs, ones_blk, stacked)


def _lru_gates(xc, wg_ref, vec_ref):
    w = LRU_WIDTH
    _, b_a, b_i, lam = (vec_ref[i:i + 1, :] for i in range(4))
    gates = _bdot(xc, wg_ref[...])
    r = _sigmoid(gates[:, 0:w] + b_a)
    i = _sigmoid(gates[:, w:2 * w] + b_i)
    log_a = LRU_C * r * (-_softplus(-lam))
    a = jnp.exp(log_a)
    th = jnp.tanh(log_a)
    u = jnp.sqrt(-2.0 * th / (1.0 - th)) * (i * xc)
    return a, u


def _lru_kernel(tc, x_ref, cv0_ref, h0_ref, cw_ref, wg_ref, vec_ref, o_ref, h_ref, ext_ref, hc_ref):
    ti = pl.program_id(1)
    w = LRU_WIDTH
    blk = LRU_SCAN_BLOCK

    @pl.when(ti == 0)
    def _():
        ext_ref[0:8, :] = cv0_ref[...]
        hc_ref[...] = h0_ref[...]

    xb = x_ref[:, 0:w]
    gb = x_ref[:, w:2 * w]
    ext_ref[8:8 + tc, :] = xb
    xc = vec_ref[0:1, :] + xb * cw_ref[CONV_W - 1:CONV_W, :]
    for j in range(CONV_W - 1):
        xc = xc + ext_ref[pl.ds(8 - (CONV_W - 1) + j, tc), :] * cw_ref[j:j + 1, :]
    ext_ref[0:8, :] = xb[tc - 8:tc, :]
    a, u = _lru_gates(xc, wg_ref, vec_ref)
    pos = lax.broadcasted_iota(jnp.int32, (tc, 1), 0) % blk
    span = 1
    while span < 8:
        ok = pos >= span
        a_s = pltpu.roll(a, span, 0)
        u_s = pltpu.roll(u, span, 0)
        u = jnp.where(ok, a * u_s + u, u)
        a = jnp.where(ok, a * a_s, a)
        span *= 2
    a = a.reshape(tc // blk, blk, w)
    u = u.reshape(tc // blk, blk, w)
    while span < blk:
        u = jnp.concatenate([u[:, :span], a[:, span:] * u[:, :blk - span] + u[:, span:]], axis=1)
        a = jnp.concatenate([a[:, :span], a[:, span:] * a[:, :blk - span]], axis=1)
        span *= 2
    gelu = _gelu_tanh(gb)
    carry = hc_ref[...]
    for b in range(tc // blk):
        h = a[b] * carry + u[b]
        carry = h[blk - 1:blk, :]
        o_ref[b * blk:(b + 1) * blk, :] = h * gelu[b * blk:(b + 1) * blk]
    hc_ref[...] = carry
    h_ref[...] = carry


def _lru_step_kernel(tn, nb, x_ref, cv0_ref, h0_ref, cw_ref, wg_ref, vec_ref, o_ref, h_ref):
    w = LRU_WIDTH
    rows = tn * nb
    xb = x_ref[:, 0:w]
    gb = x_ref[:, w:2 * w]
    ext = jnp.concatenate([cv0_ref[j] for j in range(CONV_W - 1)] + [xb], axis=0)
    xc = vec_ref[0:1, :]
    for j in range(CONV_W):
        xc = xc + ext[j * nb:j * nb + rows] * cw_ref[j:j + 1, :]
    a, u = _lru_gates(xc, wg_ref, vec_ref)
    gelu = _gelu_tanh(gb)
    h = h0_ref[...]
    for t in range(tn):
        h = a[t * nb:(t + 1) * nb] * h + u[t * nb:(t + 1) * nb]
        o_ref[t * nb:(t + 1) * nb, :] = h * gelu[t * nb:(t + 1) * nb]
    h_ref[...] = h


def _lru_sample(x, cv0, h0, conv_w, w_gates, vecs, tn, layer):
    rows = x.shape[0]
    nb = rows // tn
    wspec = lambda s: _layer_spec(s, layer, 1)
    return pl.pallas_call(
        functools.partial(_lru_step_kernel, tn, nb),
        grid=(1,),
        in_specs=[pl.BlockSpec((rows, 2 * LRU_WIDTH), lambda i: (0, 0)),
                  wspec((CONV_W - 1, nb, LRU_WIDTH)), wspec((nb, LRU_WIDTH)),
                  wspec((CONV_W, LRU_WIDTH)), wspec((LRU_WIDTH, 2 * LRU_WIDTH)), wspec((4, LRU_WIDTH))],
        out_specs=[pl.BlockSpec((rows, LRU_WIDTH), lambda i: (0, 0)), pl.BlockSpec((nb, LRU_WIDTH), lambda i: (0, 0))],
        out_shape=[jax.ShapeDtypeStruct((rows, LRU_WIDTH), F32), jax.ShapeDtypeStruct((nb, LRU_WIDTH), F32)],
        compiler_params=_cparams(("arbitrary",)),
        name="lru_step",
    )(x, cv0, h0, conv_w, w_gates, vecs)


def _lru(x, cv0, h0, conv_w, w_gates, vecs, layer):
    b, t, _ = x.shape
    out_shape = [jax.ShapeDtypeStruct((b, t, LRU_WIDTH), F32), jax.ShapeDtypeStruct((b, 1, LRU_WIDTH), F32)]
    tc = min(LRU_TILE, t)
    wspec = lambda s: _layer_spec(s, layer, 2)
    return pl.pallas_call(
        functools.partial(_lru_kernel, tc),
        grid=(b, t // tc),
        in_specs=[
            pl.BlockSpec((None, tc, 2 * LRU_WIDTH), lambda i, j: (i, j, 0)),
            pl.BlockSpec((None, 8, LRU_WIDTH), lambda i, j: (i, 0, 0)),
            pl.BlockSpec((None, 1, LRU_WIDTH), lambda i, j: (i, 0, 0)),
            wspec((CONV_W, LRU_WIDTH)), wspec((LRU_WIDTH, 2 * LRU_WIDTH)), wspec((4, LRU_WIDTH)),
        ],
        out_specs=[
            pl.BlockSpec((None, tc, LRU_WIDTH), lambda i, j: (i, j, 0)),
            pl.BlockSpec((None, 1, LRU_WIDTH), lambda i, j: (i, 0, 0)),
        ],
        out_shape=out_shape,
        scratch_shapes=[pltpu.VMEM((tc + 8, LRU_WIDTH), F32), pltpu.VMEM((1, LRU_WIDTH), F32)],
        compiler_params=_cparams(("arbitrary", "arbitrary")),
        name="lru",
    )(x, cv0, h0, conv_w, w_gates, vecs)


def _post_kernel(x_ref, oa_ref, ob_ref, oc_ref, p_ref, wo_ref, wg_ref, wu_ref, wd_ref, pg_ref, pw_ref,
                 vec_ref, y_ref):
    mix = (jnp.dot(oa_ref[...].astype(BF16), wo_ref[0:512, :], preferred_element_type=F32)
           + jnp.dot(ob_ref[...].astype(BF16), wo_ref[512:768, :], preferred_element_type=F32)
           + jnp.dot(oc_ref[...].astype(BF16), wo_ref[768:1024, :], preferred_element_type=F32)
           + vec_ref[0:1, :])
    x = x_ref[...] + _rmsnorm(mix, vec_ref[1:2, :])
    f = _rmsnorm(x, vec_ref[2:3, :]).astype(BF16)
    acc = None
    for lo in range(0, D_FF, FF_CHUNK):
        gate = jnp.dot(f, wg_ref[:, lo:lo + FF_CHUNK], preferred_element_type=F32)
        up = jnp.dot(f, wu_ref[:, lo:lo + FF_CHUNK], preferred_element_type=F32)
        hid = (gate * _sigmoid(gate) * up).astype(BF16)
        part = jnp.dot(hid, wd_ref[lo:lo + FF_CHUNK, :], preferred_element_type=F32)
        acc = part if acc is None else acc + part
    x2 = x + _rmsnorm(acc, vec_ref[3:4, :])
    gate = _sigmoid(jnp.dot(x2.astype(BF16), pg_ref[...], preferred_element_type=F32))
    emb = jnp.dot(p_ref[...].astype(BF16), pw_ref[...], preferred_element_type=F32)
    y_ref[...] = x2 + gate * emb


def _post(x, oa, ob, oc, p, wo, wg, wu, wd, pg, pw, vecs, layer):
    m = x.shape[0]
    tm = min(ROW_TILE, m)
    row = lambda w_: pl.BlockSpec((tm, w_), lambda i: (i, 0))
    wspec = lambda s: _layer_spec(s, layer, 1, pipeline_mode=pl.Buffered(1))
    return pl.pallas_call(
        _post_kernel,
        grid=(m // tm,),
        in_specs=[row(D_MODEL), row(512), row(256), row(256),
                  pl.BlockSpec((None, tm, PLE_DIM), lambda i: (layer, i, 0)),
                  wspec((D_MODEL, D_MODEL)), wspec((D_MODEL, D_FF)), wspec((D_MODEL, D_FF)),
                  wspec((D_FF, D_MODEL)), wspec((D_MODEL, D_MODEL)), wspec((PLE_DIM, D_MODEL)),
                  wspec((4, D_MODEL))],
        out_specs=row(D_MODEL),
        out_shape=jax.ShapeDtypeStruct((m, D_MODEL), F32),
        compiler_params=_cparams(("arbitrary",)),
        name="post",
    )(x, oa, ob, oc, p, wo, wg, wu, wd, pg, pw, vecs)


def _block_diag(w):
    nl, nb, n, _ = w.shape
    eye = jnp.eye(nb, dtype=w.dtype)
    return (eye[None, :, None, :, None] * w[:, :, :, None, :]).reshape(nl, nb * n, nb * n)


def _post_layer(x2, o_a, o_b, o_c, p, wts, layer):
    return _post(x2, o_a, o_b, o_c, p, wts['w_out'], wts['ffn_w_gate'], wts['ffn_w_up'], wts['ffn_w_down'],
                 wts['ple_gate_w'], wts['ple_w'], wts['post_vecs'], layer)


def _layer_prompt(x2, b, t, p, zeros, wkv_all, wts, layer):
    m = b * t
    q, kv, rw, lr = _in_proj(x2, wts['norm_mix_pre'], wts['w_in'], wts['b_in'], layer)
    kv = kv.reshape(b, t, 256)
    rw = rw.reshape(b, t, RWKV_PROJ)
    lr = lr.reshape(b, t, 2 * LRU_WIDTH)
    sh0, st0, cv0, h0 = zeros
    o_a = _attn_prompt(q.reshape(b, t, 512), kv, wts['attn_sinks'], layer)
    nk = kv[:, t - WINDOW:, 0:KV_WIDTH].reshape(b, WINDOW, N_KV_HEADS, HEAD_DIM)
    nv = kv[:, t - WINDOW:, KV_WIDTH:].reshape(b, WINDOW, N_KV_HEADS, HEAD_DIM)
    o_b, wkv_all = _rwkv(rw, sh0, st0, wts['rwkv_mu'], wts['rwkv_lora'], wts['rwkv_g_up'],
                         wts['rwkv_vecs'], wts['ones_blk'], wkv_all, layer)
    nsh = rw[:, t - 1, :]
    o_c, nh = _lru(lr, cv0, h0, wts['lru_conv_w'], wts['lru_w_gates'], wts['lru_vecs'], layer)
    nconv = lr[:, t - (CONV_W - 1):, 0:LRU_WIDTH]
    x2 = _post_layer(x2, o_a.reshape(m, 512), o_b.reshape(m, 256), o_c.reshape(m, 256), p, wts, layer)
    return x2, (nk, nv, nsh, nconv, nh.reshape(b, LRU_WIDTH)), wkv_all


def _layer_sample(x2, b, t, p, state, outs, wts, layer):
    ck, cv, sh0, st0, cv0, h0 = state
    nk_all, nv_all, wkv_all = outs
    q, kv, rw, lr = _in_proj(x2, wts['norm_mix_pre'], wts['w_in'], wts['b_in'], layer)
    o_a, nk_all, nv_all = _attn_sample(q, kv, ck, cv, nk_all, nv_all, wts['attn_sinks'], t, layer)
    o_b, wkv_all = _rwkv_sample(rw, sh0, st0, wts['rwkv_mu'], wts['rwkv_lora'], wts['rwkv_g_up'],
                                wts['rwkv_vecs'], wts['ones_blk'], wkv_all, t, layer)
    nsh = rw[(t - 1) * b:, :]
    o_c, nh = _lru_sample(lr, cv0, h0, wts['lru_conv_w'], wts['lru_w_gates'], wts['lru_vecs'], t, layer)
    nconv = lr[(t - (CONV_W - 1)) * b:, 0:LRU_WIDTH].reshape(CONV_W - 1, b, LRU_WIDTH)
    x2 = _post_layer(x2, o_a, o_b, o_c, p, wts, layer)
    return x2, (nsh, nconv, nh), (nk_all, nv_all, wkv_all)


def kernel(x_prompt, x_sample, cache_k, cache_v, state_shift, state_wkv, state_conv, state_lru,
           p_prompt, p_sample, norm_mix_pre, norm_mix_post, norm_ffn_pre, norm_ffn_post,
           w_in, b_in, attn_sinks, rwkv_mu, rwkv_w0, rwkv_w_up, rwkv_a0, rwkv_a_up, rwkv_g_up,
           rwkv_k_k, rwkv_k_a, rwkv_r_k, rwkv_ln_w, rwkv_ln_b, lru_conv_w, lru_conv_b,
           lru_w_a, lru_b_a, lru_w_i, lru_b_i, lru_L, w_out, b_out, ffn_w_gate, ffn_w_up,
           ffn_w_down, ple_w, ple_gate_w):
    nl = DEPTH
    bp, tp_, _ = x_prompt.shape
    bs, ts, _ = x_sample.shape
    head_id = jnp.arange(RWKV_WIDTH) // RWKV_HEAD
    zeros_w = jnp.zeros((nl, 64, RWKV_WIDTH), F32)
    wts = dict(
        norm_mix_pre=norm_mix_pre[:, None, :], w_in=w_in.astype(BF16), b_in=b_in[:, None, :],
        attn_sinks=attn_sinks,
        rwkv_mu=rwkv_mu[:, None, :],
        rwkv_lora=jnp.concatenate([jnp.concatenate([rwkv_w_up, zeros_w], axis=2),
                                   jnp.concatenate([zeros_w, rwkv_a_up], axis=2)], axis=1).astype(BF16),
        rwkv_g_up=rwkv_g_up.astype(BF16),
        rwkv_vecs=jnp.stack([rwkv_w0, rwkv_a0, rwkv_k_k, rwkv_k_a, rwkv_r_k.reshape(nl, RWKV_WIDTH),
                             rwkv_ln_w, rwkv_ln_b, jnp.zeros((nl, RWKV_WIDTH), F32)], axis=1),
        ones_blk=(head_id[:, None] == head_id[None, :]).astype(BF16),
        lru_conv_w=lru_conv_w,
        lru_w_gates=jnp.concatenate([_block_diag(lru_w_a), _block_diag(lru_w_i)], axis=2).astype(BF16),
        lru_vecs=jnp.stack([lru_conv_b, lru_b_a, lru_b_i, lru_L], axis=1),
        w_out=w_out.astype(BF16), ffn_w_gate=ffn_w_gate.astype(BF16), ffn_w_up=ffn_w_up.astype(BF16),
        ffn_w_down=ffn_w_down.astype(BF16), ple_gate_w=ple_gate_w.astype(BF16), ple_w=ple_w.astype(BF16),
        post_vecs=jnp.stack([b_out, norm_mix_post, norm_ffn_pre, norm_ffn_post], axis=1),
    )
    zeros_p = (jnp.zeros((bp, 1, RWKV_PROJ), F32),
               jnp.zeros((bp, RWKV_HEADS, RWKV_HEAD, RWKV_HEAD), F32),
               jnp.zeros((bp, 8, LRU_WIDTH), F32),
               jnp.zeros((bp, 1, LRU_WIDTH), F32))
    st_s = (cache_k.transpose(0, 1, 3, 4, 2), cache_v.transpose(0, 1, 3, 4, 2),
            state_shift,
            state_wkv.transpose(0, 2, 3, 4, 1),
            state_conv.transpose(0, 2, 1, 3),
            state_lru)
    pp = p_prompt.reshape(nl, bp * tp_, PLE_DIM)
    ps = p_sample.transpose(0, 2, 1, 3).reshape(nl, ts * bs, PLE_DIM)

    xp = x_prompt.reshape(bp * tp_, D_MODEL)
    xs = x_sample.transpose(1, 0, 2).reshape(ts * bs, D_MODEL)
    wkv_p = jnp.zeros((nl, bp, RWKV_HEADS, RWKV_HEAD, RWKV_HEAD), F32)
    outs_s = (jnp.zeros((nl, bs, N_KV_HEADS, HEAD_DIM, WINDOW), F32),
              jnp.zeros((nl, bs, N_KV_HEADS, HEAD_DIM, WINDOW), F32),
              jnp.zeros((nl, RWKV_HEADS, RWKV_HEAD, RWKV_HEAD, bs), F32))
    new_p, new_s = [], []
    for i in range(nl):
        xp, sp, wkv_p = _layer_prompt(xp, bp, tp_, pp, zeros_p, wkv_p, wts, i)
        xs, ss, outs_s = _layer_sample(xs, bs, ts, ps, st_s, outs_s, wts, i)
        new_p.append(sp)
        new_s.append(ss)

    def stk(lst, j):
        return jnp.stack([s[j] for s in lst], axis=0)

    nk_s, nv_s, wkv_s = outs_s
    return (xp.reshape(bp, tp_, D_MODEL), xs.reshape(ts, bs, D_MODEL).transpose(1, 0, 2),
            stk(new_p, 0), stk(new_p, 1), stk(new_p, 2), wkv_p, stk(new_p, 3), stk(new_p, 4),
            nk_s.transpose(0, 1, 4, 2, 3), nv_s.transpose(0, 1, 4, 2, 3), stk(new_s, 0),
            wkv_s.transpose(0, 4, 1, 2, 3), stk(new_s, 1).transpose(0, 2, 1, 3), stk(new_s, 2))
```

```python
import functools

import jax
import jax.numpy as jnp
from jax import lax
from jax.experimental import pallas as pl
from jax.experimental.pallas import tpu as pltpu

F32 = jnp.float32
BF16 = jnp.bfloat16

D_MODEL = 1024
DEPTH = 4
HEAD_DIM = 64
ATTN_WIDTH = 512
N_HEADS = 8
N_KV_HEADS = 2
GQA_GROUP = 4
KV_WIDTH = 128
WINDOW = 128
RWKV_WIDTH = 256
RWKV_HEADS = 4
RWKV_HEAD = 64
RWKV_PROJ = 1024
LRU_WIDTH = 256
CONV_W = 4
LRU_C = 8.0
D_FF = 2816
PLE_DIM = 256
RMS_EPS = 1e-6
GN_EPS = 64e-5
IN_COLS = 2304

ROW_TILE = 512
IN_ROW_TILE = 1024
FF_CHUNK = 1408
RWKV_CHUNK = 64
RWKV_SEQS = 4
RWKV_SUBCHUNKS = 4
RWKV_STAGGER = 3
LRU_SCAN_BLOCK = 32
ATTN_SAMPLE_SEQS = 8
ATTN_Q_BLOCKS = 8
VMEM_LIMIT = 56 * 1024 * 1024


def _cparams(sem):
    return pltpu.CompilerParams(dimension_semantics=sem, vmem_limit_bytes=VMEM_LIMIT)


def _layer_spec(shape, layer, nidx, **kw):
    zeros = (0,) * len(shape)
    if nidx == 1:
        return pl.BlockSpec((None,) + tuple(shape), lambda i: (layer,) + zeros, **kw)
    return pl.BlockSpec((None,) + tuple(shape), lambda i, j: (layer,) + zeros, **kw)


def _bdot(a, b):
    return jnp.dot(a.astype(BF16), b.astype(BF16), preferred_element_type=F32)


def _block_sums(x, ones_blk):
    return jnp.dot(x.astype(BF16), ones_blk, preferred_element_type=F32)


def _rmsnorm(x, g):
    ms = jnp.mean(x * x, axis=-1, keepdims=True)
    return x * lax.rsqrt(ms + RMS_EPS) * g


def _softplus(x):
    return jnp.maximum(x, 0.0) + jnp.log1p(jnp.exp(-jnp.abs(x)))


def _sigmoid(x):
    return 1.0 / (1.0 + jnp.exp(-x))


def _gelu_tanh(x):
    return 0.5 * x * (1.0 + jnp.tanh(0.7978845608028654 * (x + 0.044715 * (x * x * x))))


def _in_kernel(x_ref, g_ref, w_ref, b_ref, q_ref, kv_ref, rw_ref, lr_ref):
    h = _rmsnorm(x_ref[...], g_ref[...]).astype(BF16)
    for ref, lo, hi in ((q_ref, 0, 512), (kv_ref, 512, 768), (rw_ref, 768, 1792), (lr_ref, 1792, 2304)):
        ref[...] = jnp.dot(h, w_ref[:, lo:hi], preferred_element_type=F32) + b_ref[:, lo:hi]


def _in_proj(x, g, w, b, layer):
    m = x.shape[0]
    tm = min(IN_ROW_TILE, m)
    row = lambda w_: pl.BlockSpec((tm, w_), lambda i: (i, 0))
    return pl.pallas_call(
        _in_kernel,
        grid=(m // tm,),
        in_specs=[row(D_MODEL), _layer_spec((1, D_MODEL), layer, 1),
                  _layer_spec((D_MODEL, IN_COLS), layer, 1), _layer_spec((1, IN_COLS), layer, 1)],
        out_specs=[row(512), row(256), row(1024), row(512)],
        out_shape=[jax.ShapeDtypeStruct((m, n), F32) for n in (512, 256, 1024, 512)],
        compiler_params=_cparams(("arbitrary",)),
        name="in_proj",
    )(x, g, w, b)


def _attn_prompt_kernel(layer, nq, sink_ref, q_ref, kvp_ref, kvc_ref, o_ref):
    j = pl.program_id(1)
    log2e = 1.4426950408889634
    q = q_ref[...] * (HEAD_DIM ** -0.5 * log2e)
    kv = jnp.concatenate([kvp_ref[...], kvc_ref[...]], axis=0)
    kj = lax.broadcasted_iota(jnp.int32, (2 * WINDOW, WINDOW), 0)
    qi = lax.broadcasted_iota(jnp.int32, (2 * WINDOW, WINDOW), 1) + WINDOW
    d = qi - kj
    band = (d >= 0) & (d <= WINDOW)
    first = band & ((j > 0) | (kj >= WINDOW))
    nt = (((1,), (1,)), ((), ()))
    keys = kv[:, 0:KV_WIDTH].astype(BF16)
    v_t = kv[:, KV_WIDTH:2 * KV_WIDTH].T
    ones_rows = (lax.broadcasted_iota(jnp.int32, (8, 2 * WINDOW), 0) == 0).astype(F32)
    scores = {}
    for blk in range(nq):
        qb = q[blk * WINDOW:(blk + 1) * WINDOW]
        for g in range(N_KV_HEADS):
            qg = jnp.concatenate([qb[:, (g * GQA_GROUP + hh) * HEAD_DIM:(g * GQA_GROUP + hh + 1) * HEAD_DIM]
                                  for hh in range(GQA_GROUP)], axis=0).astype(BF16)
            k_ext = keys[blk * WINDOW:(blk + 2) * WINDOW, g * HEAD_DIM:(g + 1) * HEAD_DIM]
            scores[blk, g] = lax.dot_general(k_ext, qg, nt, preferred_element_type=F32)
    probs, sink_terms = {}, {}
    for blk in range(nq):
        mask = first if blk == 0 else band
        for g in range(N_KV_HEADS):
            pg, sg = [], []
            for hh in range(GQA_GROUP):
                s = jnp.where(mask, scores[blk, g][:, hh * WINDOW:(hh + 1) * WINDOW], -1e30)
                sink = sink_ref[layer, g * GQA_GROUP + hh] * log2e
                m = jnp.maximum(jnp.max(s, axis=0, keepdims=True), sink)
                pg.append(jnp.exp2(s - m).astype(BF16))
                sg.append(jnp.exp2(sink - m))
            probs[blk, g] = jnp.concatenate(pg, axis=1)
            sink_terms[blk, g] = jnp.concatenate(sg, axis=1)
    for blk in range(nq):
        outs = []
        for g in range(N_KV_HEADS):
            v_aug = jnp.concatenate([v_t[g * HEAD_DIM:(g + 1) * HEAD_DIM, blk * WINDOW:(blk + 2) * WINDOW],
                                     ones_rows], axis=0).astype(BF16)
            og = jnp.dot(v_aug, probs[blk, g], preferred_element_type=F32)
            den = og[HEAD_DIM:HEAD_DIM + 1, :] + sink_terms[blk, g]
            og = og[0:HEAD_DIM, :] * (1.0 / den)
            outs.extend(og[:, hh * WINDOW:(hh + 1) * WINDOW] for hh in range(GQA_GROUP))
        o_ref[blk * WINDOW:(blk + 1) * WINDOW, :] = jnp.concatenate(outs, axis=0).T


def _attn_prompt(q, kv, sinks, layer):
    b, t, _ = q.shape
    nq = ATTN_Q_BLOCKS
    tq = nq * WINDOW
    return pl.pallas_call(
        functools.partial(_attn_prompt_kernel, layer, nq),
        grid=(b, t // tq),
        in_specs=[
            pl.BlockSpec(memory_space=pltpu.SMEM),
            pl.BlockSpec((None, tq, ATTN_WIDTH), lambda i, j: (i, j, 0)),
            pl.BlockSpec((None, WINDOW, 2 * KV_WIDTH), lambda i, j: (i, jnp.maximum(j * nq - 1, 0), 0)),
            pl.BlockSpec((None, tq, 2 * KV_WIDTH), lambda i, j: (i, j, 0)),
        ],
        out_specs=pl.BlockSpec((None, tq, ATTN_WIDTH), lambda i, j: (i, j, 0)),
        out_shape=jax.ShapeDtypeStruct((b, t, ATTN_WIDTH), F32),
        compiler_params=_cparams(("arbitrary", "arbitrary")),
        name="attn_prompt",
    )(sinks, q, kv, kv)


def _attn_sample_kernel(layer, tn, sink_ref, q_ref, kvn_ref, ck_ref, cv_ref, nk_in_ref, nv_in_ref,
                        o_ref, nk_ref, nv_ref):
    del nk_in_ref, nv_in_ref
    bb = q_ref.shape[0]
    rows = GQA_GROUP * tn
    kvn = kvn_ref[...]
    row = lax.broadcasted_iota(jnp.int32, (1, rows, 1), 1)
    tok = row % tn
    col = lax.broadcasted_iota(jnp.int32, (1, 1, WINDOW), 2)
    cmask = col >= tok
    for g in range(N_KV_HEADS):
        k_t = ck_ref[:, g]
        v_t = cv_ref[:, g]
        qg = q_ref[:, g] * (HEAD_DIM ** -0.5)
        qg_b = qg.astype(BF16).astype(F32)
        sc = jnp.einsum('bqd,bdw->bqw', qg.astype(BF16), k_t.astype(BF16), preferred_element_type=F32)
        sc = jnp.where(cmask, sc, -1e30)
        sink = jnp.zeros((1, rows, 1), F32)
        for hh in range(GQA_GROUP):
            sink = jnp.where(row // tn == hh, sink_ref[layer, g * GQA_GROUP + hh], sink)
        m = jnp.maximum(jnp.max(sc, axis=-1, keepdims=True), sink)
        sn = []
        for jn in range(tn):
            kn = kvn[:, g * tn + jn:g * tn + jn + 1, :].astype(BF16).astype(F32)
            s_j = jnp.sum(qg_b * kn, axis=-1, keepdims=True)
            s_j = jnp.where(tok >= jn, s_j, -1e30)
            sn.append(s_j)
            m = jnp.maximum(m, s_j)
        ec = jnp.exp(sc - m)
        den = jnp.sum(ec, axis=-1, keepdims=True) + jnp.exp(sink - m)
        en = [jnp.exp(s_j - m) for s_j in sn]
        for e_j in en:
            den = den + e_j
        inv = 1.0 / den
        o = jnp.einsum('bqw,bdw->bqd', (ec * inv).astype(BF16), v_t.astype(BF16), preferred_element_type=F32)
        for jn in range(tn):
            vn = kvn[:, (N_KV_HEADS + g) * tn + jn:(N_KV_HEADS + g) * tn + jn + 1, :].astype(BF16).astype(F32)
            o = o + (en[jn] * inv).astype(BF16).astype(F32) * vn
        o_ref[:, g] = o
    new_t = kvn.reshape(bb * 4 * tn, HEAD_DIM).T
    lane = lax.broadcasted_iota(jnp.int32, (HEAD_DIM, WINDOW), 1)
    for c_ref, n_ref, which in ((ck_ref, nk_ref, 0), (cv_ref, nv_ref, 1)):
        for g in range(N_KV_HEADS):
            for b in range(bb):
                src = (b * 2 * N_KV_HEADS + which * N_KV_HEADS + g) * tn
                fresh = pltpu.roll(new_t, (WINDOW - tn - src) % WINDOW, 1)
                kept = pltpu.roll(c_ref[b, g], WINDOW - tn, 1)
                n_ref[b, g] = jnp.where(lane >= WINDOW - tn, fresh, kept)


def _attn_sample(q, kv, ck, cv, nk_all, nv_all, sinks, tn, layer):
    b = q.shape[0] // tn
    bb = ATTN_SAMPLE_SEQS
    assert bb * 4 * tn == WINDOW
    rows = GQA_GROUP * tn
    qh = q.reshape(tn, b, N_KV_HEADS, GQA_GROUP, HEAD_DIM).transpose(1, 2, 3, 0, 4)
    qh = qh.reshape(b, N_KV_HEADS, rows, HEAD_DIM)
    kvn = kv.reshape(tn, b, 2 * N_KV_HEADS, HEAD_DIM).transpose(1, 2, 0, 3).reshape(b, 4 * tn, HEAD_DIM)
    cache = pl.BlockSpec((None, bb, N_KV_HEADS, HEAD_DIM, WINDOW), lambda i: (layer, i, 0, 0, 0))
    heads = pl.BlockSpec((bb, N_KV_HEADS, rows, HEAD_DIM), lambda i: (i, 0, 0, 0))
    o, nk_all, nv_all = pl.pallas_call(
        functools.partial(_attn_sample_kernel, layer, tn),
        grid=(b // bb,),
        in_specs=[
            pl.BlockSpec(memory_space=pltpu.SMEM),
            heads,
            pl.BlockSpec((bb, 4 * tn, HEAD_DIM), lambda i: (i, 0, 0)),
            cache, cache,
            pl.BlockSpec(memory_space=pl.ANY), pl.BlockSpec(memory_space=pl.ANY),
        ],
        out_specs=[heads, cache, cache],
        out_shape=[
            jax.ShapeDtypeStruct((b, N_KV_HEADS, rows, HEAD_DIM), F32),
            jax.ShapeDtypeStruct(nk_all.shape, F32),
            jax.ShapeDtypeStruct(nv_all.shape, F32),
        ],
        input_output_aliases={5: 1, 6: 2},
        compiler_params=_cparams(("arbitrary",)),
        name="attn_sample",
    )(sinks, qh, kvn, ck, cv, nk_all, nv_all)
    o = o.reshape(b, N_KV_HEADS, GQA_GROUP, tn, HEAD_DIM).transpose(3, 0, 1, 2, 4)
    return o.reshape(tn * b, ATTN_WIDTH), nk_all, nv_all


def _tile_rows(x, n):
    return jnp.concatenate([x] * n, axis=0)


def _rwkv_kernel(c, nseq, x_ref, sh0_ref, st0_ref, mu_ref, lora_ref, gup_ref, vec_ref, ones_ref,
                 stacked_ref, o_ref, st_ref, last_ref, state_ref):
    del stacked_ref
    hw = RWKV_WIDTH
    n = RWKV_HEAD
    cw = RWKV_HEADS * c

    @pl.when(pl.program_id(1) == 0)
    def _():
        last_ref[...] = sh0_ref[...]
        for s in range(nseq):
            for h in range(RWKV_HEADS):
                state_ref[s, :, h * n:(h + 1) * n] = st0_ref[s, h]

    row = lax.broadcasted_iota(jnp.int32, (c, 1), 0)
    lane128 = lax.broadcasted_iota(jnp.int32, (c, 128), 1)
    tri = jnp.where(lax.broadcasted_iota(jnp.int32, (c, c), 1) <= lax.broadcasted_iota(jnp.int32, (c, c), 0),
                    1.0, 0.0).astype(BF16)
    tri3 = jnp.concatenate([tri, tri, tri], axis=1)
    head_rows = (lax.broadcasted_iota(jnp.int32, (cw, hw), 0) // c
                 == lax.broadcasted_iota(jnp.int32, (cw, hw), 1) // n)
    t_i = lax.broadcasted_iota(jnp.int32, (c, cw), 0)
    s_i = lax.broadcasted_iota(jnp.int32, (c, cw), 1) % c
    strict = s_i < t_i
    incl = s_i <= t_i
    eye_all = jnp.where(s_i == t_i, 1.0, 0.0)
    blk = (lax.broadcasted_iota(jnp.int32, (cw, cw), 0) // c
           == lax.broadcasted_iota(jnp.int32, (cw, cw), 1) // c)
    sblk = (lax.broadcasted_iota(jnp.int32, (hw, hw), 0) // n
            == lax.broadcasted_iota(jnp.int32, (hw, hw), 1) // n)
    masks = (row, lane128, tri3, head_rows, strict, incl, eye_all, blk, sblk)
    nsub = x_ref.shape[1] // c
    state_owner = [0] * nseq
    shared = {"lora": lora_ref, "gup": gup_ref, "sums": ones_ref}
    chains = {sub: [_rwkv_chunk(c, s, sub, nsub, state_owner, masks, x_ref, mu_ref, vec_ref,
                                o_ref, st_ref, last_ref, state_ref) for s in range(nseq)]
              for sub in range(nsub)}
    answers = {sub: [None] * nseq for sub in range(nsub)}
    rnd = 0
    while chains:
        for sub in sorted(chains):
            if rnd < sub * RWKV_STAGGER:
                continue
            asked = [_advance(ch, ans) for ch, ans in zip(chains[sub], answers[sub])]
            if asked[0] == "done":
                del chains[sub]
                continue
            results = [[] for _ in range(nseq)]
            for i, (kind, _) in enumerate(asked[0] or ()):
                stacked = _bdot(jnp.concatenate([req[i][1] for req in asked], axis=0), shared[kind][...])
                for s in range(nseq):
                    results[s].append(stacked[s * c:(s + 1) * c])
            answers[sub] = [tuple(res) for res in results]
        rnd += 1


def _advance(chain, answer):
    try:
        return next(chain) if answer is None else chain.send(answer)
    except StopIteration:
        return "done"


def _rwkv_chunk(c, s, sub, nsub, state_owner, masks, x_ref, mu_ref, vec_ref, o_ref, st_ref, last_ref, state_ref):
    row, lane128, tri3, head_rows, strict, incl, eye_all, blk, sblk = masks
    hw = RWKV_WIDTH
    n = RWKV_HEAD
    cw = RWKV_HEADS * c
    x = x_ref[s, sub * c:(sub + 1) * c, :]
    before = last_ref[s] if sub == 0 else x_ref[s, sub * c - 1:sub * c, :]
    prev = jnp.where(row == 0, before, pltpu.roll(x, 1, 0))
    xs = x + (prev - x) * mu_ref[...]

    r = xs[:, 0:hw]
    k = xs[:, hw:2 * hw]
    v = xs[:, 2 * hw:3 * hw]
    wa = xs[:, 3 * hw:3 * hw + 128]
    xg = xs[:, 3 * hw + 128:]
    lora_in = jnp.where(lane128 < 64, jnp.tanh(wa), wa)
    w0, a0, k_k, k_a, r_k, ln_w, ln_b = (vec_ref[i:i + 1, :] for i in range(7))
    kk = k * k_k
    lora, g, ss = yield (("lora", lora_in), ("gup", _sigmoid(xg)), ("sums", kk * kk))
    w_log = -_softplus(-(w0 + lora[:, 0:hw])) - 0.5
    logw = -jnp.exp(w_log)
    a = _sigmoid(a0 + lora[:, hw:2 * hw])
    kk = kk * lax.rsqrt(jnp.maximum(ss, 1e-24))
    k2 = k * (1.0 + (a - 1.0) * k_a)
    bv = kk * a

    pieces = []
    rem = logw
    for _ in range(3):
        p = rem.astype(BF16)
        rem = rem - p.astype(F32)
        pieces.append(p)
    cum = jnp.dot(tri3, jnp.concatenate(pieces, axis=0), preferred_element_type=F32)
    (bonus_sum,) = yield (("sums", r * k2 * r_k),)
    e_inc = jnp.exp(cum)
    e_exc = jnp.exp(cum - logw)
    e_inv = jnp.exp(-cum)
    kq = kk * e_exc
    rq = r * e_inc
    kd = k2 * e_inv
    bd = bv * e_inv
    w_end = e_inc[c - 1:c, :]
    kend = kd * w_end
    bend = bd * w_end

    def expand(z):
        return jnp.where(head_rows, _tile_rows(z, RWKV_HEADS), 0.0)

    lhs = jnp.concatenate([kq, rq], axis=0).astype(BF16)
    rhs = jnp.concatenate([expand(bd), expand(kd)], axis=0).astype(BF16)
    nt = (((1,), (1,)), ((), ()))
    prod = lax.dot_general(lhs, rhs, nt, preferred_element_type=F32)
    yield
    a_b = jnp.where(strict, prod[0:c, 0:cw], 0.0)
    a_k = jnp.where(strict, prod[0:c, cw:2 * cw], 0.0)
    p_b = jnp.where(incl, prod[c:2 * c, 0:cw], 0.0)
    p_k = jnp.where(incl, prod[c:2 * c, cw:2 * cw], 0.0)

    def bdiag(z):
        return jnp.where(blk, _tile_rows(z, RWKV_HEADS), 0.0)

    xm = -a_b
    tinv = eye_all + xm
    v_d = expand(v)
    from_v = _bdot(jnp.concatenate([a_k, p_k], axis=0), v_d)
    levels = c.bit_length() - 1
    xm = _bdot(xm, bdiag(xm))
    yield
    for _ in range(1, levels - 1):
        both = _bdot(jnp.concatenate([xm, tinv], axis=0), bdiag(xm))
        yield
        xm = both[0:c]
        tinv = tinv + both[c:2 * c]
    tinv = tinv + _bdot(tinv, bdiag(xm))
    yield
    assert state_owner[s] == sub
    st = state_ref[s]
    st_d = jnp.where(sblk, _tile_rows(st, RWKV_HEADS), 0.0)
    from_state = lax.dot_general(lhs, st_d.astype(BF16), nt, preferred_element_type=F32)
    yield
    u = _bdot(tinv, expand(from_state[0:c] + from_v[0:c]))
    yield
    y = from_state[c:2 * c] + from_v[c:2 * c] - _bdot(p_b, expand(u))

    lhs_s = jnp.concatenate([v, u], axis=0)
    rhs_s = jnp.concatenate([kend, -bend], axis=0)
    tn_dims = (((0,), (0,)), ((), ()))
    gm = lax.dot_general(lhs_s.astype(BF16), rhs_s.astype(BF16), tn_dims, preferred_element_type=F32)
    yield
    gm = jnp.where(sblk, gm, 0.0)
    st_new = st * w_end + (gm[0:n] + gm[n:2 * n] + (gm[2 * n:3 * n] + gm[3 * n:4 * n]))
    state_ref[s] = st_new
    state_owner[s] = sub + 1
    if sub == nsub - 1:
        state_owner[s] = 0
        last_ref[s] = x[c - 1:c, :]
        for h in range(RWKV_HEADS):
            st_ref[s, h] = st_new[:, h * n:(h + 1) * n]

    (y_sum,) = yield (("sums", y),)
    yc = y - y_sum * (1.0 / n)
    (sq_sum,) = yield (("sums", yc * yc),)
    yn = yc * lax.rsqrt(sq_sum * (1.0 / n) + GN_EPS) * ln_w + ln_b
    o_ref[s, sub * c:(sub + 1) * c, :] = (yn + bonus_sum * v) * g


def _rwkv(x, sh0, st0, mu, lora_w, g_up, vecs, ones_blk, stacked, layer):
    b, t, _ = x.shape
    c = RWKV_CHUNK
    tstep = RWKV_SUBCHUNKS * c
    nseq = min(RWKV_SEQS, b)
    wspec = lambda s: _layer_spec(s, layer, 2)
    hstate = (nseq, RWKV_HEADS, RWKV_HEAD, RWKV_HEAD)
    return pl.pallas_call(
        functools.partial(_rwkv_kernel, c, nseq),
        grid=(b // nseq, t // tstep),
        in_specs=[
            pl.BlockSpec((nseq, tstep, RWKV_PROJ), lambda i, j: (i, j, 0)),
            pl.BlockSpec((nseq, 1, RWKV_PROJ), lambda i, j: (i, 0, 0)),
            pl.BlockSpec(hstate, lambda i, j: (i, 0, 0, 0)),
            wspec((1, RWKV_PROJ)), wspec((128, 512)), wspec((128, RWKV_WIDTH)), wspec((8, RWKV_WIDTH)),
            pl.BlockSpec((RWKV_WIDTH, RWKV_WIDTH), lambda i, j: (0, 0)),
            pl.BlockSpec(memory_space=pl.ANY),
        ],
        out_specs=[
            pl.BlockSpec((nseq, tstep, RWKV_WIDTH), lambda i, j: (i, j, 0)),
            pl.BlockSpec((None,) + hstate, lambda i, j: (layer, i, 0, 0, 0)),
        ],
        out_shape=[
            jax.ShapeDtypeStruct((b, t, RWKV_WIDTH), F32),
            jax.ShapeDtypeStruct(stacked.shape, F32),
        ],
        scratch_shapes=[pltpu.VMEM((nseq, 1, RWKV_PROJ), F32), pltpu.VMEM((nseq, RWKV_HEAD, RWKV_WIDTH), F32)],
        input_output_aliases={8: 1},
        compiler_params=_cparams(("arbitrary", "arbitrary")),
        name="rwkv",
    )(x, sh0, st0, mu, lora_w, g_up, vecs, ones_blk, stacked)


def _rwkv_step_kernel(tn, nb, x_ref, sh0_ref, st0_ref, mu_ref, lora_ref, gup_ref, vec_ref, ones_ref,
                      stacked_ref, o_ref, st_ref, vt_ref, nat_ref, yt_ref):
    del stacked_ref
    hw = RWKV_WIDTH
    n = RWKV_HEAD
    h = pl.program_id(0)
    rows = tn * nb

    @pl.when(h == 0)
    def _():
        x = x_ref[...]
        prev = jnp.concatenate([sh0_ref[...], x[0:rows - nb]], axis=0)
        xs = x + (prev - x) * mu_ref[...]
        r = xs[:, 0:hw]
        k = xs[:, hw:2 * hw]
        v = xs[:, 2 * hw:3 * hw]
        wa = xs[:, 3 * hw:3 * hw + 128]
        xg = xs[:, 3 * hw + 128:]
        lane128 = lax.broadcasted_iota(jnp.int32, (rows, 128), 1)
        lora = _bdot(jnp.where(lane128 < 64, jnp.tanh(wa), wa), lora_ref[...])
        w0, a0, k_k, k_a, r_k = (vec_ref[i:i + 1, :] for i in range(5))
        ones_blk = ones_ref[...]
        w_log = -_softplus(-(w0 + lora[:, 0:hw])) - 0.5
        decay = jnp.exp(-jnp.exp(w_log))
        a = _sigmoid(a0 + lora[:, hw:2 * hw])
        kk = k * k_k
        kk = kk * lax.rsqrt(jnp.maximum(_block_sums(kk * kk, ones_blk), 1e-24))
        k2 = k * (1.0 + (a - 1.0) * k_a)
        nat_ref[0] = _bdot(_sigmoid(xg), gup_ref[...])
        nat_ref[1] = v
        nat_ref[2] = _block_sums(r * k2 * r_k, ones_blk)
        for i, z in enumerate((r, decay, k2, v, kk, kk * a)):
            for t in range(tn):
                for half in range(hw // 128):
                    vt_ref[i, t, half * 128:(half + 1) * 128, :] = (
                        z[t * nb:(t + 1) * nb, half * 128:(half + 1) * 128].T)

    base = pl.multiple_of(h * n, n)

    def value_row(vi, carry):
        s = st0_ref[vi]
        for t in range(tn):
            r_t, w_t, k_t, _, kk_t, b_t = (vt_ref[i, t, pl.ds(base, n), :] for i in range(6))
            v_row = vt_ref[3, t, pl.ds(base + vi, 1), :]
            u = jnp.sum(s * kk_t, axis=0, keepdims=True)
            s = s * w_t - u * b_t + v_row * k_t
            yt_ref[t, pl.ds(base + vi, 1), :] = jnp.sum(s * r_t, axis=0, keepdims=True)
        st_ref[vi] = s
        return carry

    lax.fori_loop(0, n, value_row, 0, unroll=4)

    @pl.when(h == RWKV_HEADS - 1)
    def _():
        ones_blk = ones_ref[...]
        ln_w = vec_ref[5:6, :]
        ln_b = vec_ref[6:7, :]
        y = jnp.concatenate(
            [jnp.concatenate([yt_ref[t, half * 128:(half + 1) * 128, :].T for half in range(hw // 128)], axis=1)
             for t in range(tn)], axis=0)
        mean = _block_sums(y, ones_blk) * (1.0 / n)
        yc = y - mean
        var = _block_sums(yc * yc, ones_blk) * (1.0 / n)
        yn = yc * lax.rsqrt(var + GN_EPS) * ln_w + ln_b
        o_ref[...] = (yn + nat_ref[2] * nat_ref[1]) * nat_ref[0]


def _rwkv_sample(x, sh0, st0, mu, lora_w, g_up, vecs, ones_blk, stacked, tn, layer):
    rows = x.shape[0]
    nb = rows // tn
    wspec = lambda s: _layer_spec(s, layer, 1)
    head_state = pl.BlockSpec((None, None, RWKV_HEAD, RWKV_HEAD, nb), lambda h: (layer, h, 0, 0, 0))
    return pl.pallas_call(
        functools.partial(_rwkv_step_kernel, tn, nb),
        grid=(RWKV_HEADS,),
        in_specs=[
            pl.BlockSpec((rows, RWKV_PROJ), lambda h: (0, 0)),
            wspec((nb, RWKV_PROJ)), head_state,
            wspec((1, RWKV_PROJ)), wspec((128, 512)), wspec((128, RWKV_WIDTH)), wspec((8, RWKV_WIDTH)),
            pl.BlockSpec((RWKV_WIDTH, RWKV_WIDTH), lambda h: (0, 0)),
            pl.BlockSpec(memory_space=pl.ANY),
        ],
        out_specs=[pl.BlockSpec((rows, RWKV_WIDTH), lambda h: (0, 0)), head_state],
        out_shape=[jax.ShapeDtypeStruct((rows, RWKV_WIDTH), F32), jax.ShapeDtypeStruct(stacked.shape, F32)],
        scratch_shapes=[pltpu.VMEM((6, tn, RWKV_WIDTH, nb), F32), pltpu.VMEM((3, rows, RWKV_WIDTH), F32),
                        pltpu.VMEM((tn, RWKV_WIDTH, nb), F32)],
        input_output_aliases={8: 1},
        compiler_params=_cparams(("arbitrary",)),
        name="rwkv_step",
    )(x, sh0, st0, mu, lora_w, g_up, vecs, ones_blk, stacked)


def _lru_gates(xc, wg_ref, vec_ref):
    w = LRU_WIDTH
    _, b_a, b_i, lam = (vec_ref[i:i + 1, :] for i in range(4))
    gates = _bdot(xc, wg_ref[...])
    r = _sigmoid(gates[:, 0:w] + b_a)
    i = _sigmoid(gates[:, w:2 * w] + b_i)
    log_a = LRU_C * r * (-_softplus(-lam))
    a = jnp.exp(log_a)
    th = jnp.tanh(log_a)
    u = jnp.sqrt(-2.0 * th / (1.0 - th)) * (i * xc)
    return a, u


def _lru_tile(x, carry, ext_ref, cw_ref, wg_ref, vec_ref):
    tc = x.shape[0]
    w = LRU_WIDTH
    blk = LRU_SCAN_BLOCK
    xb = x[:, 0:w]
    gb = x[:, w:2 * w]
    ext_ref[8:8 + tc, :] = xb
    xc = vec_ref[0:1, :] + xb * cw_ref[CONV_W - 1:CONV_W, :]
    for j in range(CONV_W - 1):
        xc = xc + ext_ref[pl.ds(8 - (CONV_W - 1) + j, tc), :] * cw_ref[j:j + 1, :]
    ext_ref[0:8, :] = xb[tc - 8:tc, :]
    a, u = _lru_gates(xc, wg_ref, vec_ref)
    pos = lax.broadcasted_iota(jnp.int32, (tc, 1), 0) % blk
    span = 1
    while span < 8:
        ok = pos >= span
        a_s = pltpu.roll(a, span, 0)
        u_s = pltpu.roll(u, span, 0)
        u = jnp.where(ok, a * u_s + u, u)
        a = jnp.where(ok, a * a_s, a)
        span *= 2
    a = a.reshape(tc // blk, blk, w)
    u = u.reshape(tc // blk, blk, w)
    while span < blk:
        u = jnp.concatenate([u[:, :span], a[:, span:] * u[:, :blk - span] + u[:, span:]], axis=1)
        a = jnp.concatenate([a[:, :span], a[:, span:] * a[:, :blk - span]], axis=1)
        span *= 2
    gelu = _gelu_tanh(gb)
    outs = []
    for b in range(tc // blk):
        h = a[b] * carry + u[b]
        carry = h[blk - 1:blk, :]
        outs.append(h * gelu[b * blk:(b + 1) * blk])
    return jnp.concatenate(outs, axis=0), carry


def _lru_step_kernel(tn, nb, x_ref, cv0_ref, h0_ref, cw_ref, wg_ref, vec_ref, o_ref, h_ref):
    w = LRU_WIDTH
    rows = tn * nb
    xb = x_ref[:, 0:w]
    gb = x_ref[:, w:2 * w]
    ext = jnp.concatenate([cv0_ref[j] for j in range(CONV_W - 1)] + [xb], axis=0)
    xc = vec_ref[0:1, :]
    for j in range(CONV_W):
        xc = xc + ext[j * nb:j * nb + rows] * cw_ref[j:j + 1, :]
    a, u = _lru_gates(xc, wg_ref, vec_ref)
    gelu = _gelu_tanh(gb)
    h = h0_ref[...]
    for t in range(tn):
        h = a[t * nb:(t + 1) * nb] * h + u[t * nb:(t + 1) * nb]
        o_ref[t * nb:(t + 1) * nb, :] = h * gelu[t * nb:(t + 1) * nb]
    h_ref[...] = h


def _lru_sample(x, cv0, h0, conv_w, w_gates, vecs, tn, layer):
    rows = x.shape[0]
    nb = rows // tn
    wspec = lambda s: _layer_spec(s, layer, 1)
    return pl.pallas_call(
        functools.partial(_lru_step_kernel, tn, nb),
        grid=(1,),
        in_specs=[pl.BlockSpec((rows, 2 * LRU_WIDTH), lambda i: (0, 0)),
                  wspec((CONV_W - 1, nb, LRU_WIDTH)), wspec((nb, LRU_WIDTH)),
                  wspec((CONV_W, LRU_WIDTH)), wspec((LRU_WIDTH, 2 * LRU_WIDTH)), wspec((4, LRU_WIDTH))],
        out_specs=[pl.BlockSpec((rows, LRU_WIDTH), lambda i: (0, 0)), pl.BlockSpec((nb, LRU_WIDTH), lambda i: (0, 0))],
        out_shape=[jax.ShapeDtypeStruct((rows, LRU_WIDTH), F32), jax.ShapeDtypeStruct((nb, LRU_WIDTH), F32)],
        compiler_params=_cparams(("arbitrary",)),
        name="lru_step",
    )(x, cv0, h0, conv_w, w_gates, vecs)


def _post_kernel(lru_tiles, *refs):
    if lru_tiles:
        (x_ref, oa_ref, ob_ref, lr_next_ref, lr_first_ref, cv0_ref, h0_ref, p_ref, wo_ref, wg_ref, wu_ref, wd_ref,
         pg_ref, pw_ref, vec_ref, lcw_ref, lwg_ref, lvec_ref, y_ref, h_ref, oc_ref, ext_ref, hc_ref) = refs
        i = pl.program_id(0)

        @pl.when(i == 0)
        def _():
            ext_ref[0:8, :] = cv0_ref[...]
            first, carry0 = _lru_tile(lr_first_ref[...], h0_ref[...], ext_ref, lcw_ref, lwg_ref, lvec_ref)
            oc_ref[...] = first
            hc_ref[...] = carry0
    else:
        (x_ref, oa_ref, ob_ref, oc_ref, p_ref, wo_ref, wg_ref, wu_ref, wd_ref, pg_ref, pw_ref,
         vec_ref, y_ref) = refs
    mix = (jnp.dot(oa_ref[...].astype(BF16), wo_ref[0:512, :], preferred_element_type=F32)
           + jnp.dot(ob_ref[...].astype(BF16), wo_ref[512:768, :], preferred_element_type=F32)
           + jnp.dot(oc_ref[...].astype(BF16), wo_ref[768:1024, :], preferred_element_type=F32)
           + vec_ref[0:1, :])
    def mixer_ahead():
        nxt = i + 1
        opens = nxt % lru_tiles == 0
        live = nxt < pl.num_programs(0)
        ext_ref[0:8, :] = jnp.where(opens, cv0_ref[...], ext_ref[0:8, :])
        carry_in = jnp.where(opens, h0_ref[...], hc_ref[...])
        ahead, carry = _lru_tile(lr_next_ref[...], carry_in, ext_ref, lcw_ref, lwg_ref, lvec_ref)
        oc_ref[...] = ahead
        carry = jnp.where(live, carry, hc_ref[...])
        hc_ref[...] = carry
        h_ref[...] = carry

    x = x_ref[...] + _rmsnorm(mix, vec_ref[1:2, :])
    f = _rmsnorm(x, vec_ref[2:3, :]).astype(BF16)
    acc = None
    for lo in range(0, D_FF, FF_CHUNK):
        gate = jnp.dot(f, wg_ref[:, lo:lo + FF_CHUNK], preferred_element_type=F32)
        up = jnp.dot(f, wu_ref[:, lo:lo + FF_CHUNK], preferred_element_type=F32)
        hid = (gate * _sigmoid(gate) * up).astype(BF16)
        part = jnp.dot(hid, wd_ref[lo:lo + FF_CHUNK, :], preferred_element_type=F32)
        acc = part if acc is None else acc + part
        if lru_tiles and lo == 0:
            mixer_ahead()
    x2 = x + _rmsnorm(acc, vec_ref[3:4, :])
    gate = _sigmoid(jnp.dot(x2.astype(BF16), pg_ref[...], preferred_element_type=F32))
    emb = jnp.dot(p_ref[...].astype(BF16), pw_ref[...], preferred_element_type=F32)
    y_ref[...] = x2 + gate * emb


def _post(x, oa, ob, oc, p, wo, wg, wu, wd, pg, pw, vecs, layer):
    m = x.shape[0]
    tm = min(ROW_TILE, m)
    row = lambda w_: pl.BlockSpec((tm, w_), lambda i: (i, 0))
    wspec = lambda s: _layer_spec(s, layer, 1, pipeline_mode=pl.Buffered(1))
    return pl.pallas_call(
        functools.partial(_post_kernel, 0),
        grid=(m // tm,),
        in_specs=[row(D_MODEL), row(512), row(256), row(256),
                  pl.BlockSpec((None, tm, PLE_DIM), lambda i: (layer, i, 0)),
                  wspec((D_MODEL, D_MODEL)), wspec((D_MODEL, D_FF)), wspec((D_MODEL, D_FF)),
                  wspec((D_FF, D_MODEL)), wspec((D_MODEL, D_MODEL)), wspec((PLE_DIM, D_MODEL)),
                  wspec((4, D_MODEL))],
        out_specs=row(D_MODEL),
        out_shape=jax.ShapeDtypeStruct((m, D_MODEL), F32),
        compiler_params=_cparams(("arbitrary",)),
        name="post",
    )(x, oa, ob, oc, p, wo, wg, wu, wd, pg, pw, vecs)


def _post_with_lru(x, oa, ob, lr, cv0, h0, t, p, wo, wg, wu, wd, pg, pw, vecs, conv_w, w_gates, lru_vecs, layer):
    m = x.shape[0]
    tm = ROW_TILE
    tiles = t // tm
    nsteps = m // tm
    row = lambda w_: pl.BlockSpec((tm, w_), lambda i: (i, 0))
    wspec = lambda s: _layer_spec(s, layer, 1, pipeline_mode=pl.Buffered(1))
    lspec = lambda s: _layer_spec(s, layer, 1)
    ahead = lambda i: jnp.minimum(i + 1, nsteps - 1)
    seq_state = lambda r: pl.BlockSpec((None, r, LRU_WIDTH), lambda i: (ahead(i) // tiles, 0, 0))
    return pl.pallas_call(
        functools.partial(_post_kernel, tiles),
        grid=(nsteps,),
        in_specs=[row(D_MODEL), row(512), row(256),
                  pl.BlockSpec((tm, 2 * LRU_WIDTH), lambda i: (ahead(i), 0)),
                  pl.BlockSpec((tm, 2 * LRU_WIDTH), lambda i: (0, 0)),
                  seq_state(8), seq_state(1),
                  pl.BlockSpec((None, tm, PLE_DIM), lambda i: (layer, i, 0)),
                  wspec((D_MODEL, D_MODEL)), wspec((D_MODEL, D_FF)), wspec((D_MODEL, D_FF)),
                  wspec((D_FF, D_MODEL)), wspec((D_MODEL, D_MODEL)), wspec((PLE_DIM, D_MODEL)),
                  wspec((4, D_MODEL)),
                  lspec((CONV_W, LRU_WIDTH)), lspec((LRU_WIDTH, 2 * LRU_WIDTH)), lspec((4, LRU_WIDTH))],
        out_specs=[row(D_MODEL), seq_state(1)],
        out_shape=[jax.ShapeDtypeStruct((m, D_MODEL), F32),
                   jax.ShapeDtypeStruct((m // t, 1, LRU_WIDTH), F32)],
        scratch_shapes=[pltpu.VMEM((tm, LRU_WIDTH), F32), pltpu.VMEM((tm + 8, LRU_WIDTH), F32),
                        pltpu.VMEM((1, LRU_WIDTH), F32)],
        compiler_params=_cparams(("arbitrary",)),
        name="post_lru",
    )(x, oa, ob, lr, lr, cv0, h0, p, wo, wg, wu, wd, pg, pw, vecs, conv_w, w_gates, lru_vecs)


def _block_diag(w):
    nl, nb, n, _ = w.shape
    eye = jnp.eye(nb, dtype=w.dtype)
    return (eye[None, :, None, :, None] * w[:, :, :, None, :]).reshape(nl, nb * n, nb * n)


def _post_weights(wts):
    return (wts['w_out'], wts['ffn_w_gate'], wts['ffn_w_up'], wts['ffn_w_down'], wts['ple_gate_w'], wts['ple_w'],
            wts['post_vecs'])


def _layer_prompt(x2, b, t, p, zeros, wkv_all, wts, layer):
    m = b * t
    q, kv, rw, lr = _in_proj(x2, wts['norm_mix_pre'], wts['w_in'], wts['b_in'], layer)
    kv = kv.reshape(b, t, 256)
    rw = rw.reshape(b, t, RWKV_PROJ)
    sh0, st0, cv0, h0 = zeros
    o_a = _attn_prompt(q.reshape(b, t, 512), kv, wts['attn_sinks'], layer)
    nk = kv[:, t - WINDOW:, 0:KV_WIDTH].reshape(b, WINDOW, N_KV_HEADS, HEAD_DIM)
    nv = kv[:, t - WINDOW:, KV_WIDTH:].reshape(b, WINDOW, N_KV_HEADS, HEAD_DIM)
    o_b, wkv_all = _rwkv(rw, sh0, st0, wts['rwkv_mu'], wts['rwkv_lora'], wts['rwkv_g_up'],
                         wts['rwkv_vecs'], wts['ones_blk'], wkv_all, layer)
    nsh = rw[:, t - 1, :]
    nconv = lr.reshape(b, t, 2 * LRU_WIDTH)[:, t - (CONV_W - 1):, 0:LRU_WIDTH]
    x2, nh = _post_with_lru(x2, o_a.reshape(m, 512), o_b.reshape(m, 256), lr, cv0, h0, t, p, *_post_weights(wts),
                            wts['lru_conv_w'], wts['lru_w_gates'], wts['lru_vecs'], layer)
    return x2, (nk, nv, nsh, nconv, nh.reshape(b, LRU_WIDTH)), wkv_all


def _layer_sample(x2, b, t, p, state, outs, wts, layer):
    ck, cv, sh0, st0, cv0, h0 = state
    nk_all, nv_all, wkv_all = outs
    q, kv, rw, lr = _in_proj(x2, wts['norm_mix_pre'], wts['w_in'], wts['b_in'], layer)
    o_a, nk_all, nv_all = _attn_sample(q, kv, ck, cv, nk_all, nv_all, wts['attn_sinks'], t, layer)
    o_b, wkv_all = _rwkv_sample(rw, sh0, st0, wts['rwkv_mu'], wts['rwkv_lora'], wts['rwkv_g_up'],
                                wts['rwkv_vecs'], wts['ones_blk'], wkv_all, t, layer)
    nsh = rw[(t - 1) * b:, :]
    o_c, nh = _lru_sample(lr, cv0, h0, wts['lru_conv_w'], wts['lru_w_gates'], wts['lru_vecs'], t, layer)
    nconv = lr[(t - (CONV_W - 1)) * b:, 0:LRU_WIDTH].reshape(CONV_W - 1, b, LRU_WIDTH)
    x2 = _post(x2, o_a, o_b, o_c, p, *_post_weights(wts), layer)
    return x2, (nsh, nconv, nh), (nk_all, nv_all, wkv_all)


def kernel(x_prompt, x_sample, cache_k, cache_v, state_shift, state_wkv, state_conv, state_lru,
           p_prompt, p_sample, norm_mix_pre, norm_mix_post, norm_ffn_pre, norm_ffn_post,
           w_in, b_in, attn_sinks, rwkv_mu, rwkv_w0, rwkv_w_up, rwkv_a0, rwkv_a_up, rwkv_g_up,
           rwkv_k_k, rwkv_k_a, rwkv_r_k, rwkv_ln_w, rwkv_ln_b, lru_conv_w, lru_conv_b,
           lru_w_a, lru_b_a, lru_w_i, lru_b_i, lru_L, w_out, b_out, ffn_w_gate, ffn_w_up,
           ffn_w_down, ple_w, ple_gate_w):
    nl = DEPTH
    bp, tp_, _ = x_prompt.shape
    bs, ts, _ = x_sample.shape
    head_id = jnp.arange(RWKV_WIDTH) // RWKV_HEAD
    zeros_w = jnp.zeros((nl, 64, RWKV_WIDTH), F32)
    wts = dict(
        norm_mix_pre=norm_mix_pre[:, None, :], w_in=w_in.astype(BF16), b_in=b_in[:, None, :],
        attn_sinks=attn_sinks,
        rwkv_mu=rwkv_mu[:, None, :],
        rwkv_lora=jnp.concatenate([jnp.concatenate([rwkv_w_up, zeros_w], axis=2),
                                   jnp.concatenate([zeros_w, rwkv_a_up], axis=2)], axis=1).astype(BF16),
        rwkv_g_up=rwkv_g_up.astype(BF16),
        rwkv_vecs=jnp.stack([rwkv_w0, rwkv_a0, rwkv_k_k, rwkv_k_a, rwkv_r_k.reshape(nl, RWKV_WIDTH),
                             rwkv_ln_w, rwkv_ln_b, jnp.zeros((nl, RWKV_WIDTH), F32)], axis=1),
        ones_blk=(head_id[:, None] == head_id[None, :]).astype(BF16),
        lru_conv_w=lru_conv_w,
        lru_w_gates=jnp.concatenate([_block_diag(lru_w_a), _block_diag(lru_w_i)], axis=2).astype(BF16),
        lru_vecs=jnp.stack([lru_conv_b, lru_b_a, lru_b_i, lru_L], axis=1),
        w_out=w_out.astype(BF16), ffn_w_gate=ffn_w_gate.astype(BF16), ffn_w_up=ffn_w_up.astype(BF16),
        ffn_w_down=ffn_w_down.astype(BF16), ple_gate_w=ple_gate_w.astype(BF16), ple_w=ple_w.astype(BF16),
        post_vecs=jnp.stack([b_out, norm_mix_post, norm_ffn_pre, norm_ffn_post], axis=1),
    )
    zeros_p = (jnp.zeros((bp, 1, RWKV_PROJ), F32),
               jnp.zeros((bp, RWKV_HEADS, RWKV_HEAD, RWKV_HEAD), F32),
               jnp.zeros((bp, 8, LRU_WIDTH), F32),
               jnp.zeros((bp, 1, LRU_WIDTH), F32))
    st_s = (cache_k.transpose(0, 1, 3, 4, 2), cache_v.transpose(0, 1, 3, 4, 2),
            state_shift,
            state_wkv.transpose(0, 2, 3, 4, 1),
            state_conv.transpose(0, 2, 1, 3),
            state_lru)
    pp = p_prompt.reshape(nl, bp * tp_, PLE_DIM)
    ps = p_sample.transpose(0, 2, 1, 3).reshape(nl, ts * bs, PLE_DIM)

    xp = x_prompt.reshape(bp * tp_, D_MODEL)
    xs = x_sample.transpose(1, 0, 2).reshape(ts * bs, D_MODEL)
    wkv_p = jnp.zeros((nl, bp, RWKV_HEADS, RWKV_HEAD, RWKV_HEAD), F32)
    outs_s = (jnp.zeros((nl, bs, N_KV_HEADS, HEAD_DIM, WINDOW), F32),
              jnp.zeros((nl, bs, N_KV_HEADS, HEAD_DIM, WINDOW), F32),
              jnp.zeros((nl, RWKV_HEADS, RWKV_HEAD, RWKV_HEAD, bs), F32))
    new_p, new_s = [], []
    for i in range(nl):
        xp, sp, wkv_p = _layer_prompt(xp, bp, tp_, pp, zeros_p, wkv_p, wts, i)
        xs, ss, outs_s = _layer_sample(xs, bs, ts, ps, st_s, outs_s, wts, i)
        new_p.append(sp)
        new_s.append(ss)

    def stk(lst, j):
        return jnp.stack([s[j] for s in lst], axis=0)

    nk_s, nv_s, wkv_s = outs_s
    return (xp.reshape(bp, tp_, D_MODEL), xs.reshape(ts, bs, D_MODEL).transpose(1, 0, 2),
            stk(new_p, 0), stk(new_p, 1), stk(new_p, 2), wkv_p, stk(new_p, 3), stk(new_p, 4),
            nk_s.transpose(0, 1, 4, 2, 3), nv_s.transpose(0, 1, 4, 2, 3), stk(new_s, 0),
            wkv_s.transpose(0, 4, 1, 2, 3), stk(new_s, 1).transpose(0, 2, 1, 3), stk(new_s, 2))
```

```python
import functools

import jax
import jax.numpy as jnp
from jax import lax
from jax.experimental import pallas as pl
from jax.experimental.pallas import tpu as pltpu

F32 = jnp.float32
BF16 = jnp.bfloat16

D_MODEL = 1024
DEPTH = 4
HEAD_DIM = 64
ATTN_WIDTH = 512
N_HEADS = 8
N_KV_HEADS = 2
GQA_GROUP = 4
KV_WIDTH = 128
WINDOW = 128
RWKV_WIDTH = 256
RWKV_HEADS = 4
RWKV_HEAD = 64
RWKV_PROJ = 1024
LRU_WIDTH = 256
CONV_W = 4
LRU_C = 8.0
D_FF = 2816
PLE_DIM = 256
RMS_EPS = 1e-6
GN_EPS = 64e-5
IN_COLS = 2304

ROW_TILE = 512
IN_ROW_TILE = 1024
FF_CHUNK = 256
LRU_PIECES = 8
RWKV_CHUNK = 64
RWKV_SEQS = 4
RWKV_SUBCHUNKS = 4
RWKV_STAGGER = 3
LRU_SCAN_BLOCK = 32
ATTN_SAMPLE_SEQS = 8
ATTN_Q_BLOCKS = 8
VMEM_LIMIT = 56 * 1024 * 1024


def _cparams(sem):
    return pltpu.CompilerParams(dimension_semantics=sem, vmem_limit_bytes=VMEM_LIMIT)


def _layer_spec(shape, layer, nidx, **kw):
    zeros = (0,) * len(shape)
    if nidx == 1:
        return pl.BlockSpec((None,) + tuple(shape), lambda i: (layer,) + zeros, **kw)
    return pl.BlockSpec((None,) + tuple(shape), lambda i, j: (layer,) + zeros, **kw)


def _bdot(a, b):
    return jnp.dot(a.astype(BF16), b.astype(BF16), preferred_element_type=F32)


def _block_sums(x, ones_blk):
    return jnp.dot(x.astype(BF16), ones_blk, preferred_element_type=F32)


def _rmsnorm(x, g):
    ms = jnp.mean(x * x, axis=-1, keepdims=True)
    return x * lax.rsqrt(ms + RMS_EPS) * g


def _softplus(x):
    return jnp.maximum(x, 0.0) + jnp.log1p(jnp.exp(-jnp.abs(x)))


def _sigmoid(x):
    return 1.0 / (1.0 + jnp.exp(-x))


def _gelu_tanh(x):
    return 0.5 * x * (1.0 + jnp.tanh(0.7978845608028654 * (x + 0.044715 * (x * x * x))))


def _in_kernel(x_ref, g_ref, w_ref, b_ref, q_ref, kv_ref, rw_ref, lr_ref):
    h = _rmsnorm(x_ref[...], g_ref[...]).astype(BF16)
    for ref, lo, hi in ((q_ref, 0, 512), (kv_ref, 512, 768), (rw_ref, 768, 1792), (lr_ref, 1792, 2304)):
        ref[...] = jnp.dot(h, w_ref[:, lo:hi], preferred_element_type=F32) + b_ref[:, lo:hi]


def _in_proj(x, g, w, b, layer):
    m = x.shape[0]
    tm = min(IN_ROW_TILE, m)
    row = lambda w_: pl.BlockSpec((tm, w_), lambda i: (i, 0))
    return pl.pallas_call(
        _in_kernel,
        grid=(m // tm,),
        in_specs=[row(D_MODEL), _layer_spec((1, D_MODEL), layer, 1),
                  _layer_spec((D_MODEL, IN_COLS), layer, 1), _layer_spec((1, IN_COLS), layer, 1)],
        out_specs=[row(512), row(256), row(1024), row(512)],
        out_shape=[jax.ShapeDtypeStruct((m, n), F32) for n in (512, 256, 1024, 512)],
        compiler_params=_cparams(("arbitrary",)),
        name="in_proj",
    )(x, g, w, b)


def _attn_prompt_kernel(layer, nq, sink_ref, q_ref, kvp_ref, kvc_ref, o_ref):
    j = pl.program_id(1)
    log2e = 1.4426950408889634
    q = q_ref[...] * (HEAD_DIM ** -0.5 * log2e)
    kv = jnp.concatenate([kvp_ref[...], kvc_ref[...]], axis=0)
    kj = lax.broadcasted_iota(jnp.int32, (2 * WINDOW, WINDOW), 0)
    qi = lax.broadcasted_iota(jnp.int32, (2 * WINDOW, WINDOW), 1) + WINDOW
    d = qi - kj
    band = (d >= 0) & (d <= WINDOW)
    first = band & ((j > 0) | (kj >= WINDOW))
    nt = (((1,), (1,)), ((), ()))
    keys = kv[:, 0:KV_WIDTH].astype(BF16)
    v_t = kv[:, KV_WIDTH:2 * KV_WIDTH].T
    ones_rows = (lax.broadcasted_iota(jnp.int32, (8, 2 * WINDOW), 0) == 0).astype(F32)
    scores = {}
    for blk in range(nq):
        qb = q[blk * WINDOW:(blk + 1) * WINDOW]
        for g in range(N_KV_HEADS):
            qg = jnp.concatenate([qb[:, (g * GQA_GROUP + hh) * HEAD_DIM:(g * GQA_GROUP + hh + 1) * HEAD_DIM]
                                  for hh in range(GQA_GROUP)], axis=0).astype(BF16)
            k_ext = keys[blk * WINDOW:(blk + 2) * WINDOW, g * HEAD_DIM:(g + 1) * HEAD_DIM]
            scores[blk, g] = lax.dot_general(k_ext, qg, nt, preferred_element_type=F32)
    probs, sink_terms = {}, {}
    for blk in range(nq):
        mask = first if blk == 0 else band
        for g in range(N_KV_HEADS):
            pg, sg = [], []
            for hh in range(GQA_GROUP):
                s = jnp.where(mask, scores[blk, g][:, hh * WINDOW:(hh + 1) * WINDOW], -1e30)
                sink = sink_ref[layer, g * GQA_GROUP + hh] * log2e
                m = jnp.maximum(jnp.max(s, axis=0, keepdims=True), sink)
                pg.append(jnp.exp2(s - m).astype(BF16))
                sg.append(jnp.exp2(sink - m))
            probs[blk, g] = jnp.concatenate(pg, axis=1)
            sink_terms[blk, g] = jnp.concatenate(sg, axis=1)
    for blk in range(nq):
        outs = []
        for g in range(N_KV_HEADS):
            v_aug = jnp.concatenate([v_t[g * HEAD_DIM:(g + 1) * HEAD_DIM, blk * WINDOW:(blk + 2) * WINDOW],
                                     ones_rows], axis=0).astype(BF16)
            og = jnp.dot(v_aug, probs[blk, g], preferred_element_type=F32)
            den = og[HEAD_DIM:HEAD_DIM + 1, :] + sink_terms[blk, g]
            og = og[0:HEAD_DIM, :] * (1.0 / den)
            outs.extend(og[:, hh * WINDOW:(hh + 1) * WINDOW] for hh in range(GQA_GROUP))
        o_ref[blk * WINDOW:(blk + 1) * WINDOW, :] = jnp.concatenate(outs, axis=0).T


def _attn_prompt(q, kv, sinks, layer):
    b, t, _ = q.shape
    nq = ATTN_Q_BLOCKS
    tq = nq * WINDOW
    return pl.pallas_call(
        functools.partial(_attn_prompt_kernel, layer, nq),
        grid=(b, t // tq),
        in_specs=[
            pl.BlockSpec(memory_space=pltpu.SMEM),
            pl.BlockSpec((None, tq, ATTN_WIDTH), lambda i, j: (i, j, 0)),
            pl.BlockSpec((None, WINDOW, 2 * KV_WIDTH), lambda i, j: (i, jnp.maximum(j * nq - 1, 0), 0)),
            pl.BlockSpec((None, tq, 2 * KV_WIDTH), lambda i, j: (i, j, 0)),
        ],
        out_specs=pl.BlockSpec((None, tq, ATTN_WIDTH), lambda i, j: (i, j, 0)),
        out_shape=jax.ShapeDtypeStruct((b, t, ATTN_WIDTH), F32),
        compiler_params=_cparams(("arbitrary", "arbitrary")),
        name="attn_prompt",
    )(sinks, q, kv, kv)


def _attn_sample_kernel(layer, tn, sink_ref, q_ref, kvn_ref, ck_ref, cv_ref, nk_in_ref, nv_in_ref,
                        o_ref, nk_ref, nv_ref):
    del nk_in_ref, nv_in_ref
    bb = q_ref.shape[0]
    rows = GQA_GROUP * tn
    kvn = kvn_ref[...]
    row = lax.broadcasted_iota(jnp.int32, (1, rows, 1), 1)
    tok = row % tn
    col = lax.broadcasted_iota(jnp.int32, (1, 1, WINDOW), 2)
    cmask = col >= tok
    for g in range(N_KV_HEADS):
        k_t = ck_ref[:, g]
        v_t = cv_ref[:, g]
        qg = q_ref[:, g] * (HEAD_DIM ** -0.5)
        qg_b = qg.astype(BF16).astype(F32)
        sc = jnp.einsum('bqd,bdw->bqw', qg.astype(BF16), k_t.astype(BF16), preferred_element_type=F32)
        sc = jnp.where(cmask, sc, -1e30)
        sink = jnp.zeros((1, rows, 1), F32)
        for hh in range(GQA_GROUP):
            sink = jnp.where(row // tn == hh, sink_ref[layer, g * GQA_GROUP + hh], sink)
        m = jnp.maximum(jnp.max(sc, axis=-1, keepdims=True), sink)
        sn = []
        for jn in range(tn):
            kn = kvn[:, g * tn + jn:g * tn + jn + 1, :].astype(BF16).astype(F32)
            s_j = jnp.sum(qg_b * kn, axis=-1, keepdims=True)
            s_j = jnp.where(tok >= jn, s_j, -1e30)
            sn.append(s_j)
            m = jnp.maximum(m, s_j)
        ec = jnp.exp(sc - m)
        den = jnp.sum(ec, axis=-1, keepdims=True) + jnp.exp(sink - m)
        en = [jnp.exp(s_j - m) for s_j in sn]
        for e_j in en:
            den = den + e_j
        inv = 1.0 / den
        o = jnp.einsum('bqw,bdw->bqd', (ec * inv).astype(BF16), v_t.astype(BF16), preferred_element_type=F32)
        for jn in range(tn):
            vn = kvn[:, (N_KV_HEADS + g) * tn + jn:(N_KV_HEADS + g) * tn + jn + 1, :].astype(BF16).astype(F32)
            o = o + (en[jn] * inv).astype(BF16).astype(F32) * vn
        o_ref[:, g] = o
    new_t = kvn.reshape(bb * 4 * tn, HEAD_DIM).T
    lane = lax.broadcasted_iota(jnp.int32, (HEAD_DIM, WINDOW), 1)
    for c_ref, n_ref, which in ((ck_ref, nk_ref, 0), (cv_ref, nv_ref, 1)):
        for g in range(N_KV_HEADS):
            for b in range(bb):
                src = (b * 2 * N_KV_HEADS + which * N_KV_HEADS + g) * tn
                fresh = pltpu.roll(new_t, (WINDOW - tn - src) % WINDOW, 1)
                kept = pltpu.roll(c_ref[b, g], WINDOW - tn, 1)
                n_ref[b, g] = jnp.where(lane >= WINDOW - tn, fresh, kept)


def _attn_sample(q, kv, ck, cv, nk_all, nv_all, sinks, tn, layer):
    b = q.shape[0] // tn
    bb = ATTN_SAMPLE_SEQS
    assert bb * 4 * tn == WINDOW
    rows = GQA_GROUP * tn
    qh = q.reshape(tn, b, N_KV_HEADS, GQA_GROUP, HEAD_DIM).transpose(1, 2, 3, 0, 4)
    qh = qh.reshape(b, N_KV_HEADS, rows, HEAD_DIM)
    kvn = kv.reshape(tn, b, 2 * N_KV_HEADS, HEAD_DIM).transpose(1, 2, 0, 3).reshape(b, 4 * tn, HEAD_DIM)
    cache = pl.BlockSpec((None, bb, N_KV_HEADS, HEAD_DIM, WINDOW), lambda i: (layer, i, 0, 0, 0))
    heads = pl.BlockSpec((bb, N_KV_HEADS, rows, HEAD_DIM), lambda i: (i, 0, 0, 0))
    o, nk_all, nv_all = pl.pallas_call(
        functools.partial(_attn_sample_kernel, layer, tn),
        grid=(b // bb,),
        in_specs=[
            pl.BlockSpec(memory_space=pltpu.SMEM),
            heads,
            pl.BlockSpec((bb, 4 * tn, HEAD_DIM), lambda i: (i, 0, 0)),
            cache, cache,
            pl.BlockSpec(memory_space=pl.ANY), pl.BlockSpec(memory_space=pl.ANY),
        ],
        out_specs=[heads, cache, cache],
        out_shape=[
            jax.ShapeDtypeStruct((b, N_KV_HEADS, rows, HEAD_DIM), F32),
            jax.ShapeDtypeStruct(nk_all.shape, F32),
            jax.ShapeDtypeStruct(nv_all.shape, F32),
        ],
        input_output_aliases={5: 1, 6: 2},
        compiler_params=_cparams(("arbitrary",)),
        name="attn_sample",
    )(sinks, qh, kvn, ck, cv, nk_all, nv_all)
    o = o.reshape(b, N_KV_HEADS, GQA_GROUP, tn, HEAD_DIM).transpose(3, 0, 1, 2, 4)
    return o.reshape(tn * b, ATTN_WIDTH), nk_all, nv_all


def _tile_rows(x, n):
    return jnp.concatenate([x] * n, axis=0)


def _rwkv_kernel(c, nseq, x_ref, sh0_ref, st0_ref, mu_ref, lora_ref, gup_ref, vec_ref, ones_ref,
                 stacked_ref, o_ref, st_ref, last_ref, state_ref):
    del stacked_ref
    hw = RWKV_WIDTH
    n = RWKV_HEAD
    cw = RWKV_HEADS * c

    @pl.when(pl.program_id(1) == 0)
    def _():
        last_ref[...] = sh0_ref[...]
        for s in range(nseq):
            for h in range(RWKV_HEADS):
                state_ref[s, :, h * n:(h + 1) * n] = st0_ref[s, h]

    row = lax.broadcasted_iota(jnp.int32, (c, 1), 0)
    lane128 = lax.broadcasted_iota(jnp.int32, (c, 128), 1)
    tri = jnp.where(lax.broadcasted_iota(jnp.int32, (c, c), 1) <= lax.broadcasted_iota(jnp.int32, (c, c), 0),
                    1.0, 0.0).astype(BF16)
    tri3 = jnp.concatenate([tri, tri, tri], axis=1)
    head_rows = (lax.broadcasted_iota(jnp.int32, (cw, hw), 0) // c
                 == lax.broadcasted_iota(jnp.int32, (cw, hw), 1) // n)
    t_i = lax.broadcasted_iota(jnp.int32, (c, cw), 0)
    s_i = lax.broadcasted_iota(jnp.int32, (c, cw), 1) % c
    strict = s_i < t_i
    incl = s_i <= t_i
    eye_all = jnp.where(s_i == t_i, 1.0, 0.0)
    blk = (lax.broadcasted_iota(jnp.int32, (cw, cw), 0) // c
           == lax.broadcasted_iota(jnp.int32, (cw, cw), 1) // c)
    sblk = (lax.broadcasted_iota(jnp.int32, (hw, hw), 0) // n
            == lax.broadcasted_iota(jnp.int32, (hw, hw), 1) // n)
    masks = (row, lane128, tri3, head_rows, strict, incl, eye_all, blk, sblk)
    nsub = x_ref.shape[1] // c
    state_owner = [0] * nseq
    shared = {"lora": lora_ref, "gup": gup_ref, "sums": ones_ref}
    chains = {sub: [_rwkv_chunk(c, s, sub, nsub, state_owner, masks, x_ref, mu_ref, vec_ref,
                                o_ref, st_ref, last_ref, state_ref) for s in range(nseq)]
              for sub in range(nsub)}
    answers = {sub: [None] * nseq for sub in range(nsub)}
    rnd = 0
    while chains:
        for sub in sorted(chains):
            if rnd < sub * RWKV_STAGGER:
                continue
            asked = [_advance(ch, ans) for ch, ans in zip(chains[sub], answers[sub])]
            if asked[0] == "done":
                del chains[sub]
                continue
            results = [[] for _ in range(nseq)]
            for i, (kind, _) in enumerate(asked[0] or ()):
                stacked = _bdot(jnp.concatenate([req[i][1] for req in asked], axis=0), shared[kind][...])
                for s in range(nseq):
                    results[s].append(stacked[s * c:(s + 1) * c])
            answers[sub] = [tuple(res) for res in results]
        rnd += 1


def _advance(chain, answer):
    try:
        return next(chain) if answer is None else chain.send(answer)
    except StopIteration:
        return "done"


def _rwkv_chunk(c, s, sub, nsub, state_owner, masks, x_ref, mu_ref, vec_ref, o_ref, st_ref, last_ref, state_ref):
    row, lane128, tri3, head_rows, strict, incl, eye_all, blk, sblk = masks
    hw = RWKV_WIDTH
    n = RWKV_HEAD
    cw = RWKV_HEADS * c
    x = x_ref[s, sub * c:(sub + 1) * c, :]
    before = last_ref[s] if sub == 0 else x_ref[s, sub * c - 1:sub * c, :]
    prev = jnp.where(row == 0, before, pltpu.roll(x, 1, 0))
    xs = x + (prev - x) * mu_ref[...]

    r = xs[:, 0:hw]
    k = xs[:, hw:2 * hw]
    v = xs[:, 2 * hw:3 * hw]
    wa = xs[:, 3 * hw:3 * hw + 128]
    xg = xs[:, 3 * hw + 128:]
    lora_in = jnp.where(lane128 < 64, jnp.tanh(wa), wa)
    w0, a0, k_k, k_a, r_k, ln_w, ln_b = (vec_ref[i:i + 1, :] for i in range(7))
    kk = k * k_k
    lora, g, ss = yield (("lora", lora_in), ("gup", _sigmoid(xg)), ("sums", kk * kk))
    w_log = -_softplus(-(w0 + lora[:, 0:hw])) - 0.5
    logw = -jnp.exp(w_log)
    a = _sigmoid(a0 + lora[:, hw:2 * hw])
    kk = kk * lax.rsqrt(jnp.maximum(ss, 1e-24))
    k2 = k * (1.0 + (a - 1.0) * k_a)
    bv = kk * a

    pieces = []
    rem = logw
    for _ in range(3):
        p = rem.astype(BF16)
        rem = rem - p.astype(F32)
        pieces.append(p)
    cum = jnp.dot(tri3, jnp.concatenate(pieces, axis=0), preferred_element_type=F32)
    (bonus_sum,) = yield (("sums", r * k2 * r_k),)
    e_inc = jnp.exp(cum)
    e_exc = jnp.exp(cum - logw)
    e_inv = jnp.exp(-cum)
    kq = kk * e_exc
    rq = r * e_inc
    kd = k2 * e_inv
    bd = bv * e_inv
    w_end = e_inc[c - 1:c, :]
    kend = kd * w_end
    bend = bd * w_end

    def expand(z):
        return jnp.where(head_rows, _tile_rows(z, RWKV_HEADS), 0.0)

    lhs = jnp.concatenate([kq, rq], axis=0).astype(BF16)
    rhs = jnp.concatenate([expand(bd), expand(kd)], axis=0).astype(BF16)
    nt = (((1,), (1,)), ((), ()))
    prod = lax.dot_general(lhs, rhs, nt, preferred_element_type=F32)
    yield
    a_b = jnp.where(strict, prod[0:c, 0:cw], 0.0)
    a_k = jnp.where(strict, prod[0:c, cw:2 * cw], 0.0)
    p_b = jnp.where(incl, prod[c:2 * c, 0:cw], 0.0)
    p_k = jnp.where(incl, prod[c:2 * c, cw:2 * cw], 0.0)

    def bdiag(z):
        return jnp.where(blk, _tile_rows(z, RWKV_HEADS), 0.0)

    xm = -a_b
    tinv = eye_all + xm
    v_d = expand(v)
    from_v = _bdot(jnp.concatenate([a_k, p_k], axis=0), v_d)
    levels = c.bit_length() - 1
    xm = _bdot(xm, bdiag(xm))
    yield
    for _ in range(1, levels - 1):
        both = _bdot(jnp.concatenate([xm, tinv], axis=0), bdiag(xm))
        yield
        xm = both[0:c]
        tinv = tinv + both[c:2 * c]
    tinv = tinv + _bdot(tinv, bdiag(xm))
    yield
    assert state_owner[s] == sub
    st = state_ref[s]
    st_d = jnp.where(sblk, _tile_rows(st, RWKV_HEADS), 0.0)
    from_state = lax.dot_general(lhs, st_d.astype(BF16), nt, preferred_element_type=F32)
    yield
    u = _bdot(tinv, expand(from_state[0:c] + from_v[0:c]))
    yield
    y = from_state[c:2 * c] + from_v[c:2 * c] - _bdot(p_b, expand(u))

    lhs_s = jnp.concatenate([v, u], axis=0)
    rhs_s = jnp.concatenate([kend, -bend], axis=0)
    tn_dims = (((0,), (0,)), ((), ()))
    gm = lax.dot_general(lhs_s.astype(BF16), rhs_s.astype(BF16), tn_dims, preferred_element_type=F32)
    yield
    gm = jnp.where(sblk, gm, 0.0)
    st_new = st * w_end + (gm[0:n] + gm[n:2 * n] + (gm[2 * n:3 * n] + gm[3 * n:4 * n]))
    state_ref[s] = st_new
    state_owner[s] = sub + 1
    if sub == nsub - 1:
        state_owner[s] = 0
        last_ref[s] = x[c - 1:c, :]
        for h in range(RWKV_HEADS):
            st_ref[s, h] = st_new[:, h * n:(h + 1) * n]

    (y_sum,) = yield (("sums", y),)
    yc = y - y_sum * (1.0 / n)
    (sq_sum,) = yield (("sums", yc * yc),)
    yn = yc * lax.rsqrt(sq_sum * (1.0 / n) + GN_EPS) * ln_w + ln_b
    o_ref[s, sub * c:(sub + 1) * c, :] = (yn + bonus_sum * v) * g


def _rwkv(x, sh0, st0, mu, lora_w, g_up, vecs, ones_blk, stacked, layer):
    b, t, _ = x.shape
    c = RWKV_CHUNK
    tstep = RWKV_SUBCHUNKS * c
    nseq = min(RWKV_SEQS, b)
    wspec = lambda s: _layer_spec(s, layer, 2)
    hstate = (nseq, RWKV_HEADS, RWKV_HEAD, RWKV_HEAD)
    return pl.pallas_call(
        functools.partial(_rwkv_kernel, c, nseq),
        grid=(b // nseq, t // tstep),
        in_specs=[
            pl.BlockSpec((nseq, tstep, RWKV_PROJ), lambda i, j: (i, j, 0)),
            pl.BlockSpec((nseq, 1, RWKV_PROJ), lambda i, j: (i, 0, 0)),
            pl.BlockSpec(hstate, lambda i, j: (i, 0, 0, 0)),
            wspec((1, RWKV_PROJ)), wspec((128, 512)), wspec((128, RWKV_WIDTH)), wspec((8, RWKV_WIDTH)),
            pl.BlockSpec((RWKV_WIDTH, RWKV_WIDTH), lambda i, j: (0, 0)),
            pl.BlockSpec(memory_space=pl.ANY),
        ],
        out_specs=[
            pl.BlockSpec((nseq, tstep, RWKV_WIDTH), lambda i, j: (i, j, 0)),
            pl.BlockSpec((None,) + hstate, lambda i, j: (layer, i, 0, 0, 0)),
        ],
        out_shape=[
            jax.ShapeDtypeStruct((b, t, RWKV_WIDTH), F32),
            jax.ShapeDtypeStruct(stacked.shape, F32),
        ],
        scratch_shapes=[pltpu.VMEM((nseq, 1, RWKV_PROJ), F32), pltpu.VMEM((nseq, RWKV_HEAD, RWKV_WIDTH), F32)],
        input_output_aliases={8: 1},
        compiler_params=_cparams(("arbitrary", "arbitrary")),
        name="rwkv",
    )(x, sh0, st0, mu, lora_w, g_up, vecs, ones_blk, stacked)


def _rwkv_step_kernel(tn, nb, x_ref, sh0_ref, st0_ref, mu_ref, lora_ref, gup_ref, vec_ref, ones_ref,
                      stacked_ref, o_ref, st_ref, vt_ref, nat_ref, yt_ref):
    del stacked_ref
    hw = RWKV_WIDTH
    n = RWKV_HEAD
    h = pl.program_id(0)
    rows = tn * nb

    @pl.when(h == 0)
    def _():
        x = x_ref[...]
        prev = jnp.concatenate([sh0_ref[...], x[0:rows - nb]], axis=0)
        xs = x + (prev - x) * mu_ref[...]
        r = xs[:, 0:hw]
        k = xs[:, hw:2 * hw]
        v = xs[:, 2 * hw:3 * hw]
        wa = xs[:, 3 * hw:3 * hw + 128]
        xg = xs[:, 3 * hw + 128:]
        lane128 = lax.broadcasted_iota(jnp.int32, (rows, 128), 1)
        lora = _bdot(jnp.where(lane128 < 64, jnp.tanh(wa), wa), lora_ref[...])
        w0, a0, k_k, k_a, r_k = (vec_ref[i:i + 1, :] for i in range(5))
        ones_blk = ones_ref[...]
        w_log = -_softplus(-(w0 + lora[:, 0:hw])) - 0.5
        decay = jnp.exp(-jnp.exp(w_log))
        a = _sigmoid(a0 + lora[:, hw:2 * hw])
        kk = k * k_k
        kk = kk * lax.rsqrt(jnp.maximum(_block_sums(kk * kk, ones_blk), 1e-24))
        k2 = k * (1.0 + (a - 1.0) * k_a)
        nat_ref[0] = _bdot(_sigmoid(xg), gup_ref[...])
        nat_ref[1] = v
        nat_ref[2] = _block_sums(r * k2 * r_k, ones_blk)
        for i, z in enumerate((r, decay, k2, v, kk, kk * a)):
            for t in range(tn):
                for half in range(hw // 128):
                    vt_ref[i, t, half * 128:(half + 1) * 128, :] = (
                        z[t * nb:(t + 1) * nb, half * 128:(half + 1) * 128].T)

    base = pl.multiple_of(h * n, n)

    def value_row(vi, carry):
        s = st0_ref[vi]
        for t in range(tn):
            r_t, w_t, k_t, _, kk_t, b_t = (vt_ref[i, t, pl.ds(base, n), :] for i in range(6))
            v_row = vt_ref[3, t, pl.ds(base + vi, 1), :]
            u = jnp.sum(s * kk_t, axis=0, keepdims=True)
            s = s * w_t - u * b_t + v_row * k_t
            yt_ref[t, pl.ds(base + vi, 1), :] = jnp.sum(s * r_t, axis=0, keepdims=True)
        st_ref[vi] = s
        return carry

    lax.fori_loop(0, n, value_row, 0, unroll=4)

    @pl.when(h == RWKV_HEADS - 1)
    def _():
        ones_blk = ones_ref[...]
        ln_w = vec_ref[5:6, :]
        ln_b = vec_ref[6:7, :]
        y = jnp.concatenate(
            [jnp.concatenate([yt_ref[t, half * 128:(half + 1) * 128, :].T for half in range(hw // 128)], axis=1)
             for t in range(tn)], axis=0)
        mean = _block_sums(y, ones_blk) * (1.0 / n)
        yc = y - mean
        var = _block_sums(yc * yc, ones_blk) * (1.0 / n)
        yn = yc * lax.rsqrt(var + GN_EPS) * ln_w + ln_b
        o_ref[...] = (yn + nat_ref[2] * nat_ref[1]) * nat_ref[0]


def _rwkv_sample(x, sh0, st0, mu, lora_w, g_up, vecs, ones_blk, stacked, tn, layer):
    rows = x.shape[0]
    nb = rows // tn
    wspec = lambda s: _layer_spec(s, layer, 1)
    head_state = pl.BlockSpec((None, None, RWKV_HEAD, RWKV_HEAD, nb), lambda h: (layer, h, 0, 0, 0))
    return pl.pallas_call(
        functools.partial(_rwkv_step_kernel, tn, nb),
        grid=(RWKV_HEADS,),
        in_specs=[
            pl.BlockSpec((rows, RWKV_PROJ), lambda h: (0, 0)),
            wspec((nb, RWKV_PROJ)), head_state,
            wspec((1, RWKV_PROJ)), wspec((128, 512)), wspec((128, RWKV_WIDTH)), wspec((8, RWKV_WIDTH)),
            pl.BlockSpec((RWKV_WIDTH, RWKV_WIDTH), lambda h: (0, 0)),
            pl.BlockSpec(memory_space=pl.ANY),
        ],
        out_specs=[pl.BlockSpec((rows, RWKV_WIDTH), lambda h: (0, 0)), head_state],
        out_shape=[jax.ShapeDtypeStruct((rows, RWKV_WIDTH), F32), jax.ShapeDtypeStruct(stacked.shape, F32)],
        scratch_shapes=[pltpu.VMEM((6, tn, RWKV_WIDTH, nb), F32), pltpu.VMEM((3, rows, RWKV_WIDTH), F32),
                        pltpu.VMEM((tn, RWKV_WIDTH, nb), F32)],
        input_output_aliases={8: 1},
        compiler_params=_cparams(("arbitrary",)),
        name="rwkv_step",
    )(x, sh0, st0, mu, lora_w, g_up, vecs, ones_blk, stacked)


def _lru_gates(xc, wg_ref, vec_ref):
    w = LRU_WIDTH
    _, b_a, b_i, lam = (vec_ref[i:i + 1, :] for i in range(4))
    gates = _bdot(xc, wg_ref[...])
    r = _sigmoid(gates[:, 0:w] + b_a)
    i = _sigmoid(gates[:, w:2 * w] + b_i)
    log_a = LRU_C * r * (-_softplus(-lam))
    a = jnp.exp(log_a)
    th = jnp.tanh(log_a)
    u = jnp.sqrt(-2.0 * th / (1.0 - th)) * (i * xc)
    return a, u


def _lru_tile_steps(x_ref, carry, ext_ref, o_ref, cw_ref, wg_ref, vec_ref):
    tc = x_ref.shape[0]
    w = LRU_WIDTH
    blk = LRU_SCAN_BLOCK
    gr = tc // LRU_PIECES
    ext_ref[8:8 + tc, :] = x_ref[:, 0:w]
    xc = vec_ref[0:1, :] + ext_ref[8:8 + tc, :] * cw_ref[CONV_W - 1:CONV_W, :]
    for j in range(CONV_W - 1):
        xc = xc + ext_ref[pl.ds(8 - (CONV_W - 1) + j, tc), :] * cw_ref[j:j + 1, :]
    ext_ref[0:8, :] = ext_ref[tc:tc + 8, :]
    gates = _bdot(xc, wg_ref[...])
    yield
    _, b_a, b_i, lam = (vec_ref[i:i + 1, :] for i in range(4))
    log_lam = -_softplus(-lam)
    pos = lax.broadcasted_iota(jnp.int32, (gr, 1), 0) % blk
    for g in range(LRU_PIECES):
        rows = slice(g * gr, (g + 1) * gr)
        xc_g = xc[rows]
        r = _sigmoid(gates[rows, 0:w] + b_a)
        i = _sigmoid(gates[rows, w:2 * w] + b_i)
        log_a = LRU_C * r * log_lam
        a = jnp.exp(log_a)
        th = jnp.tanh(log_a)
        u = jnp.sqrt(-2.0 * th / (1.0 - th)) * (i * xc_g)
        span = 1
        while span < 8:
            ok = pos >= span
            a_s = pltpu.roll(a, span, 0)
            u_s = pltpu.roll(u, span, 0)
            u = jnp.where(ok, a * u_s + u, u)
            a = jnp.where(ok, a * a_s, a)
            span *= 2
        a = a.reshape(gr // blk, blk, w)
        u = u.reshape(gr // blk, blk, w)
        while span < blk:
            u = jnp.concatenate([u[:, :span], a[:, span:] * u[:, :blk - span] + u[:, span:]], axis=1)
            a = jnp.concatenate([a[:, :span], a[:, span:] * a[:, :blk - span]], axis=1)
            span *= 2
        gelu = _gelu_tanh(x_ref[rows, w:2 * w])
        for b in range(gr // blk):
            h = a[b] * carry + u[b]
            carry = h[blk - 1:blk, :]
            o_ref[g * gr + b * blk:g * gr + (b + 1) * blk, :] = h * gelu[b * blk:(b + 1) * blk]
        yield
    return carry


def _step(steps, result):
    if not result:
        try:
            next(steps)
        except StopIteration as done:
            result.append(done.value)


def _run_to_end(steps, result=None):
    result = [] if result is None else result
    while not result:
        _step(steps, result)
    return result[0]


def _lru_step_kernel(tn, nb, x_ref, cv0_ref, h0_ref, cw_ref, wg_ref, vec_ref, o_ref, h_ref):
    w = LRU_WIDTH
    rows = tn * nb
    xb = x_ref[:, 0:w]
    gb = x_ref[:, w:2 * w]
    ext = jnp.concatenate([cv0_ref[j] for j in range(CONV_W - 1)] + [xb], axis=0)
    xc = vec_ref[0:1, :]
    for j in range(CONV_W):
        xc = xc + ext[j * nb:j * nb + rows] * cw_ref[j:j + 1, :]
    a, u = _lru_gates(xc, wg_ref, vec_ref)
    gelu = _gelu_tanh(gb)
    h = h0_ref[...]
    for t in range(tn):
        h = a[t * nb:(t + 1) * nb] * h + u[t * nb:(t + 1) * nb]
        o_ref[t * nb:(t + 1) * nb, :] = h * gelu[t * nb:(t + 1) * nb]
    h_ref[...] = h


def _lru_sample(x, cv0, h0, conv_w, w_gates, vecs, tn, layer):
    rows = x.shape[0]
    nb = rows // tn
    wspec = lambda s: _layer_spec(s, layer, 1)
    return pl.pallas_call(
        functools.partial(_lru_step_kernel, tn, nb),
        grid=(1,),
        in_specs=[pl.BlockSpec((rows, 2 * LRU_WIDTH), lambda i: (0, 0)),
                  wspec((CONV_W - 1, nb, LRU_WIDTH)), wspec((nb, LRU_WIDTH)),
                  wspec((CONV_W, LRU_WIDTH)), wspec((LRU_WIDTH, 2 * LRU_WIDTH)), wspec((4, LRU_WIDTH))],
        out_specs=[pl.BlockSpec((rows, LRU_WIDTH), lambda i: (0, 0)), pl.BlockSpec((nb, LRU_WIDTH), lambda i: (0, 0))],
        out_shape=[jax.ShapeDtypeStruct((rows, LRU_WIDTH), F32), jax.ShapeDtypeStruct((nb, LRU_WIDTH), F32)],
        compiler_params=_cparams(("arbitrary",)),
        name="lru_step",
    )(x, cv0, h0, conv_w, w_gates, vecs)


def _post_kernel(lru_tiles, *refs):
    if lru_tiles:
        (x_ref, oa_ref, ob_ref, lr_next_ref, lr_first_ref, cv0_ref, h0_ref, p_ref, wo_ref, wg_ref, wu_ref, wd_ref,
         pg_ref, pw_ref, vec_ref, lcw_ref, lwg_ref, lvec_ref, y_ref, h_ref, oc_ref, ext_ref, hc_ref) = refs
        i = pl.program_id(0)

        @pl.when(i == 0)
        def _():
            ext_ref[0:8, :] = cv0_ref[...]
            hc_ref[...] = _run_to_end(_lru_tile_steps(lr_first_ref, h0_ref[...], ext_ref, oc_ref,
                                                      lcw_ref, lwg_ref, lvec_ref))
    else:
        (x_ref, oa_ref, ob_ref, oc_ref, p_ref, wo_ref, wg_ref, wu_ref, wd_ref, pg_ref, pw_ref,
         vec_ref, y_ref) = refs
    mix = (jnp.dot(oa_ref[...].astype(BF16), wo_ref[0:512, :], preferred_element_type=F32)
           + jnp.dot(ob_ref[...].astype(BF16), wo_ref[512:768, :], preferred_element_type=F32)
           + jnp.dot(oc_ref[...].astype(BF16), wo_ref[768:1024, :], preferred_element_type=F32)
           + vec_ref[0:1, :])
    mixer_ahead, mixer_state = None, []
    if lru_tiles:
        opens = (i + 1) % lru_tiles == 0
        ext_ref[0:8, :] = jnp.where(opens, cv0_ref[...], ext_ref[0:8, :])
        carry_in = jnp.where(opens, h0_ref[...], hc_ref[...])
        mixer_ahead = _lru_tile_steps(lr_next_ref, carry_in, ext_ref, oc_ref, lcw_ref, lwg_ref, lvec_ref)

    x = x_ref[...] + _rmsnorm(mix, vec_ref[1:2, :])
    f = _rmsnorm(x, vec_ref[2:3, :]).astype(BF16)
    acc = None
    for lo in range(0, D_FF, FF_CHUNK):
        gate = jnp.dot(f, wg_ref[:, lo:lo + FF_CHUNK], preferred_element_type=F32)
        up = jnp.dot(f, wu_ref[:, lo:lo + FF_CHUNK], preferred_element_type=F32)
        hid = (gate * _sigmoid(gate) * up).astype(BF16)
        part = jnp.dot(hid, wd_ref[lo:lo + FF_CHUNK, :], preferred_element_type=F32)
        acc = part if acc is None else acc + part
        if mixer_ahead is not None:
            _step(mixer_ahead, mixer_state)
    if mixer_ahead is not None:
        carry = _run_to_end(mixer_ahead, mixer_state)
        carry = jnp.where(i + 1 < pl.num_programs(0), carry, hc_ref[...])
        hc_ref[...] = carry
        h_ref[...] = carry
    x2 = x + _rmsnorm(acc, vec_ref[3:4, :])
    gate = _sigmoid(jnp.dot(x2.astype(BF16), pg_ref[...], preferred_element_type=F32))
    emb = jnp.dot(p_ref[...].astype(BF16), pw_ref[...], preferred_element_type=F32)
    y_ref[...] = x2 + gate * emb


def _post(x, oa, ob, oc, p, wo, wg, wu, wd, pg, pw, vecs, layer):
    m = x.shape[0]
    tm = min(ROW_TILE, m)
    row = lambda w_: pl.BlockSpec((tm, w_), lambda i: (i, 0))
    wspec = lambda s: _layer_spec(s, layer, 1, pipeline_mode=pl.Buffered(1))
    return pl.pallas_call(
        functools.partial(_post_kernel, 0),
        grid=(m // tm,),
        in_specs=[row(D_MODEL), row(512), row(256), row(256),
                  pl.BlockSpec((None, tm, PLE_DIM), lambda i: (layer, i, 0)),
                  wspec((D_MODEL, D_MODEL)), wspec((D_MODEL, D_FF)), wspec((D_MODEL, D_FF)),
                  wspec((D_FF, D_MODEL)), wspec((D_MODEL, D_MODEL)), wspec((PLE_DIM, D_MODEL)),
                  wspec((4, D_MODEL))],
        out_specs=row(D_MODEL),
        out_shape=jax.ShapeDtypeStruct((m, D_MODEL), F32),
        compiler_params=_cparams(("arbitrary",)),
        name="post",
    )(x, oa, ob, oc, p, wo, wg, wu, wd, pg, pw, vecs)


def _post_with_lru(x, oa, ob, lr, cv0, h0, t, p, wo, wg, wu, wd, pg, pw, vecs, conv_w, w_gates, lru_vecs, layer):
    m = x.shape[0]
    tm = ROW_TILE
    tiles = t // tm
    nsteps = m // tm
    row = lambda w_: pl.BlockSpec((tm, w_), lambda i: (i, 0))
    wspec = lambda s: _layer_spec(s, layer, 1, pipeline_mode=pl.Buffered(1))
    lspec = lambda s: _layer_spec(s, layer, 1)
    ahead = lambda i: jnp.minimum(i + 1, nsteps - 1)
    seq_state = lambda r: pl.BlockSpec((None, r, LRU_WIDTH), lambda i: (ahead(i) // tiles, 0, 0))
    return pl.pallas_call(
        functools.partial(_post_kernel, tiles),
        grid=(nsteps,),
        in_specs=[row(D_MODEL), row(512), row(256),
                  pl.BlockSpec((tm, 2 * LRU_WIDTH), lambda i: (ahead(i), 0)),
                  pl.BlockSpec((tm, 2 * LRU_WIDTH), lambda i: (0, 0)),
                  seq_state(8), seq_state(1),
                  pl.BlockSpec((None, tm, PLE_DIM), lambda i: (layer, i, 0)),
                  wspec((D_MODEL, D_MODEL)), wspec((D_MODEL, D_FF)), wspec((D_MODEL, D_FF)),
                  wspec((D_FF, D_MODEL)), wspec((D_MODEL, D_MODEL)), wspec((PLE_DIM, D_MODEL)),
                  wspec((4, D_MODEL)),
                  lspec((CONV_W, LRU_WIDTH)), lspec((LRU_WIDTH, 2 * LRU_WIDTH)), lspec((4, LRU_WIDTH))],
        out_specs=[row(D_MODEL), seq_state(1)],
        out_shape=[jax.ShapeDtypeStruct((m, D_MODEL), F32),
                   jax.ShapeDtypeStruct((m // t, 1, LRU_WIDTH), F32)],
        scratch_shapes=[pltpu.VMEM((tm, LRU_WIDTH), F32), pltpu.VMEM((tm + 8, LRU_WIDTH), F32),
                        pltpu.VMEM((1, LRU_WIDTH), F32)],
        compiler_params=_cparams(("arbitrary",)),
        name="post_lru",
    )(x, oa, ob, lr, lr, cv0, h0, p, wo, wg, wu, wd, pg, pw, vecs, conv_w, w_gates, lru_vecs)


def _block_diag(w):
    nl, nb, n, _ = w.shape
    eye = jnp.eye(nb, dtype=w.dtype)
    return (eye[None, :, None, :, None] * w[:, :, :, None, :]).reshape(nl, nb * n, nb * n)


def _post_weights(wts):
    return (wts['w_out'], wts['ffn_w_gate'], wts['ffn_w_up'], wts['ffn_w_down'], wts['ple_gate_w'], wts['ple_w'],
            wts['post_vecs'])


def _layer_prompt(x2, b, t, p, zeros, wkv_all, wts, layer):
    m = b * t
    q, kv, rw, lr = _in_proj(x2, wts['norm_mix_pre'], wts['w_in'], wts['b_in'], layer)
    kv = kv.reshape(b, t, 256)
    rw = rw.reshape(b, t, RWKV_PROJ)
    sh0, st0, cv0, h0 = zeros
    o_a = _attn_prompt(q.reshape(b, t, 512), kv, wts['attn_sinks'], layer)
    nk = kv[:, t - WINDOW:, 0:KV_WIDTH].reshape(b, WINDOW, N_KV_HEADS, HEAD_DIM)
    nv = kv[:, t - WINDOW:, KV_WIDTH:].reshape(b, WINDOW, N_KV_HEADS, HEAD_DIM)
    o_b, wkv_all = _rwkv(rw, sh0, st0, wts['rwkv_mu'], wts['rwkv_lora'], wts['rwkv_g_up'],
                         wts['rwkv_vecs'], wts['ones_blk'], wkv_all, layer)
    nsh = rw[:, t - 1, :]
    nconv = lr.reshape(b, t, 2 * LRU_WIDTH)[:, t - (CONV_W - 1):, 0:LRU_WIDTH]
    x2, nh = _post_with_lru(x2, o_a.reshape(m, 512), o_b.reshape(m, 256), lr, cv0, h0, t, p, *_post_weights(wts),
                            wts['lru_conv_w'], wts['lru_w_gates'], wts['lru_vecs'], layer)
    return x2, (nk, nv, nsh, nconv, nh.reshape(b, LRU_WIDTH)), wkv_all


def _layer_sample(x2, b, t, p, state, outs, wts, layer):
    ck, cv, sh0, st0, cv0, h0 = state
    nk_all, nv_all, wkv_all = outs
    q, kv, rw, lr = _in_proj(x2, wts['norm_mix_pre'], wts['w_in'], wts['b_in'], layer)
    o_a, nk_all, nv_all = _attn_sample(q, kv, ck, cv, nk_all, nv_all, wts['attn_sinks'], t, layer)
    o_b, wkv_all = _rwkv_sample(rw, sh0, st0, wts['rwkv_mu'], wts['rwkv_lora'], wts['rwkv_g_up'],
                                wts['rwkv_vecs'], wts['ones_blk'], wkv_all, t, layer)
    nsh = rw[(t - 1) * b:, :]
    o_c, nh = _lru_sample(lr, cv0, h0, wts['lru_conv_w'], wts['lru_w_gates'], wts['lru_vecs'], t, layer)
    nconv = lr[(t - (CONV_W - 1)) * b:, 0:LRU_WIDTH].reshape(CONV_W - 1, b, LRU_WIDTH)
    x2 = _post(x2, o_a, o_b, o_c, p, *_post_weights(wts), layer)
    return x2, (nsh, nconv, nh), (nk_all, nv_all, wkv_all)


def kernel(x_prompt, x_sample, cache_k, cache_v, state_shift, state_wkv, state_conv, state_lru,
           p_prompt, p_sample, norm_mix_pre, norm_mix_post, norm_ffn_pre, norm_ffn_post,
           w_in, b_in, attn_sinks, rwkv_mu, rwkv_w0, rwkv_w_up, rwkv_a0, rwkv_a_up, rwkv_g_up,
           rwkv_k_k, rwkv_k_a, rwkv_r_k, rwkv_ln_w, rwkv_ln_b, lru_conv_w, lru_conv_b,
           lru_w_a, lru_b_a, lru_w_i, lru_b_i, lru_L, w_out, b_out, ffn_w_gate, ffn_w_up,
           ffn_w_down, ple_w, ple_gate_w):
    nl = DEPTH
    bp, tp_, _ = x_prompt.shape
    bs, ts, _ = x_sample.shape
    head_id = jnp.arange(RWKV_WIDTH) // RWKV_HEAD
    zeros_w = jnp.zeros((nl, 64, RWKV_WIDTH), F32)
    wts = dict(
        norm_mix_pre=norm_mix_pre[:, None, :], w_in=w_in.astype(BF16), b_in=b_in[:, None, :],
        attn_sinks=attn_sinks,
        rwkv_mu=rwkv_mu[:, None, :],
        rwkv_lora=jnp.concatenate([jnp.concatenate([rwkv_w_up, zeros_w], axis=2),
                                   jnp.concatenate([zeros_w, rwkv_a_up], axis=2)], axis=1).astype(BF16),
        rwkv_g_up=rwkv_g_up.astype(BF16),
        rwkv_vecs=jnp.stack([rwkv_w0, rwkv_a0, rwkv_k_k, rwkv_k_a, rwkv_r_k.reshape(nl, RWKV_WIDTH),
                             rwkv_ln_w, rwkv_ln_b, jnp.zeros((nl, RWKV_WIDTH), F32)], axis=1),
        ones_blk=(head_id[:, None] == head_id[None, :]).astype(BF16),
        lru_conv_w=lru_conv_w,
        lru_w_gates=jnp.concatenate([_block_diag(lru_w_a), _block_diag(lru_w_i)], axis=2).astype(BF16),
        lru_vecs=jnp.stack([lru_conv_b, lru_b_a, lru_b_i, lru_L], axis=1),
        w_out=w_out.astype(BF16), ffn_w_gate=ffn_w_gate.astype(BF16), ffn_w_up=ffn_w_up.astype(BF16),
        ffn_w_down=ffn_w_down.astype(BF16), ple_gate_w=ple_gate_w.astype(BF16), ple_w=ple_w.astype(BF16),
        post_vecs=jnp.stack([b_out, norm_mix_post, norm_ffn_pre, norm_ffn_post], axis=1),
    )
    zeros_p = (jnp.zeros((bp, 1, RWKV_PROJ), F32),
               jnp.zeros((bp, RWKV_HEADS, RWKV_HEAD, RWKV_HEAD), F32),
               jnp.zeros((bp, 8, LRU_WIDTH), F32),
               jnp.zeros((bp, 1, LRU_WIDTH), F32))
    st_s = (cache_k.transpose(0, 1, 3, 4, 2), cache_v.transpose(0, 1, 3, 4, 2),
            state_shift,
            state_wkv.transpose(0, 2, 3, 4, 1),
            state_conv.transpose(0, 2, 1, 3),
            state_lru)
    pp = p_prompt.reshape(nl, bp * tp_, PLE_DIM)
    ps = p_sample.transpose(0, 2, 1, 3).reshape(nl, ts * bs, PLE_DIM)

    xp = x_prompt.reshape(bp * tp_, D_MODEL)
    xs = x_sample.transpose(1, 0, 2).reshape(ts * bs, D_MODEL)
    wkv_p = jnp.zeros((nl, bp, RWKV_HEADS, RWKV_HEAD, RWKV_HEAD), F32)
    outs_s = (jnp.zeros((nl, bs, N_KV_HEADS, HEAD_DIM, WINDOW), F32),
              jnp.zeros((nl, bs, N_KV_HEADS, HEAD_DIM, WINDOW), F32),
              jnp.zeros((nl, RWKV_HEADS, RWKV_HEAD, RWKV_HEAD, bs), F32))
    new_p, new_s = [], []
    for i in range(nl):
        xp, sp, wkv_p = _layer_prompt(xp, bp, tp_, pp, zeros_p, wkv_p, wts, i)
        xs, ss, outs_s = _layer_sample(xs, bs, ts, ps, st_s, outs_s, wts, i)
        new_p.append(sp)
        new_s.append(ss)

    def stk(lst, j):
        return jnp.stack([s[j] for s in lst], axis=0)

    nk_s, nv_s, wkv_s = outs_s
    return (xp.reshape(bp, tp_, D_MODEL), xs.reshape(ts, bs, D_MODEL).transpose(1, 0, 2),
            stk(new_p, 0), stk(new_p, 1), stk(new_p, 2), wkv_p, stk(new_p, 3), stk(new_p, 4),
            nk_s.transpose(0, 1, 4, 2, 3), nv_s.transpose(0, 1, 4, 2, 3), stk(new_s, 0),
            wkv_s.transpose(0, 4, 1, 2, 3), stk(new_s, 1).transpose(0, 2, 1, 3), stk(new_s, 2))
```

```python
import functools

import jax
import jax.numpy as jnp
from jax import lax
from jax.experimental import pallas as pl
from jax.experimental.pallas import tpu as pltpu

F32 = jnp.float32
BF16 = jnp.bfloat16

D_MODEL = 1024
DEPTH = 4
HEAD_DIM = 64
ATTN_WIDTH = 512
N_HEADS = 8
N_KV_HEADS = 2
GQA_GROUP = 4
KV_WIDTH = 128
WINDOW = 128
RWKV_WIDTH = 256
RWKV_HEADS = 4
RWKV_HEAD = 64
RWKV_PROJ = 1024
LRU_WIDTH = 256
CONV_W = 4
LRU_C = 8.0
D_FF = 2816
PLE_DIM = 256
RMS_EPS = 1e-6
GN_EPS = 64e-5
IN_COLS = 2304

ROW_TILE = 512
IN_ROW_TILE = 1024
FF_CHUNK = 256
LRU_PIECES = 8
RWKV_CHUNK = 64
RWKV_SEQS = 4
RWKV_SUBCHUNKS = 4
RWKV_STAGGER = 3
LRU_SCAN_BLOCK = 32
ATTN_SAMPLE_SEQS = 8
ATTN_Q_BLOCKS = 8
VMEM_LIMIT = 56 * 1024 * 1024


def _cparams(sem):
    return pltpu.CompilerParams(dimension_semantics=sem, vmem_limit_bytes=VMEM_LIMIT)


def _layer_spec(shape, layer, nidx, **kw):
    zeros = (0,) * len(shape)
    if nidx == 1:
        return pl.BlockSpec((None,) + tuple(shape), lambda i: (layer,) + zeros, **kw)
    return pl.BlockSpec((None,) + tuple(shape), lambda i, j: (layer,) + zeros, **kw)


def _bdot(a, b):
    return jnp.dot(a.astype(BF16), b.astype(BF16), preferred_element_type=F32)


def _block_sums(x, ones_blk):
    return jnp.dot(x.astype(BF16), ones_blk, preferred_element_type=F32)


def _rmsnorm(x, g):
    ms = jnp.mean(x * x, axis=-1, keepdims=True)
    return x * lax.rsqrt(ms + RMS_EPS) * g


def _softplus(x):
    return jnp.maximum(x, 0.0) + jnp.log1p(jnp.exp(-jnp.abs(x)))


def _sigmoid(x):
    return 1.0 / (1.0 + jnp.exp(-x))


def _gelu_tanh(x):
    return 0.5 * x * (1.0 + jnp.tanh(0.7978845608028654 * (x + 0.044715 * (x * x * x))))


def _in_kernel(x_ref, g_ref, w_ref, b_ref, q_ref, kv_ref, rw_ref, lr_ref):
    h = _rmsnorm(x_ref[...], g_ref[...]).astype(BF16)
    for ref, lo, hi in ((q_ref, 0, 512), (kv_ref, 512, 768), (rw_ref, 768, 1792), (lr_ref, 1792, 2304)):
        ref[...] = jnp.dot(h, w_ref[:, lo:hi], preferred_element_type=F32) + b_ref[:, lo:hi]


def _in_proj(x, g, w, b, layer):
    m = x.shape[0]
    tm = min(IN_ROW_TILE, m)
    row = lambda w_: pl.BlockSpec((tm, w_), lambda i: (i, 0))
    return pl.pallas_call(
        _in_kernel,
        grid=(m // tm,),
        in_specs=[row(D_MODEL), _layer_spec((1, D_MODEL), layer, 1),
                  _layer_spec((D_MODEL, IN_COLS), 0, 1), _layer_spec((1, IN_COLS), layer, 1)],
        out_specs=[row(512), row(256), row(1024), row(512)],
        out_shape=[jax.ShapeDtypeStruct((m, n), F32) for n in (512, 256, 1024, 512)],
        compiler_params=_cparams(("arbitrary",)),
        name="in_proj",
    )(x, g, w, b)


def _attn_prompt_kernel(layer, nq, sink_ref, q_ref, kvp_ref, kvc_ref, o_ref):
    j = pl.program_id(1)
    log2e = 1.4426950408889634
    q = q_ref[...] * (HEAD_DIM ** -0.5 * log2e)
    kv = jnp.concatenate([kvp_ref[...], kvc_ref[...]], axis=0)
    kj = lax.broadcasted_iota(jnp.int32, (2 * WINDOW, WINDOW), 0)
    qi = lax.broadcasted_iota(jnp.int32, (2 * WINDOW, WINDOW), 1) + WINDOW
    d = qi - kj
    band = (d >= 0) & (d <= WINDOW)
    first = band & ((j > 0) | (kj >= WINDOW))
    nt = (((1,), (1,)), ((), ()))
    keys = kv[:, 0:KV_WIDTH].astype(BF16)
    v_t = kv[:, KV_WIDTH:2 * KV_WIDTH].T
    ones_rows = (lax.broadcasted_iota(jnp.int32, (8, 2 * WINDOW), 0) == 0).astype(F32)
    scores = {}
    for blk in range(nq):
        qb = q[blk * WINDOW:(blk + 1) * WINDOW]
        for g in range(N_KV_HEADS):
            qg = jnp.concatenate([qb[:, (g * GQA_GROUP + hh) * HEAD_DIM:(g * GQA_GROUP + hh + 1) * HEAD_DIM]
                                  for hh in range(GQA_GROUP)], axis=0).astype(BF16)
            k_ext = keys[blk * WINDOW:(blk + 2) * WINDOW, g * HEAD_DIM:(g + 1) * HEAD_DIM]
            scores[blk, g] = lax.dot_general(k_ext, qg, nt, preferred_element_type=F32)
    probs, sink_terms = {}, {}
    for blk in range(nq):
        mask = first if blk == 0 else band
        for g in range(N_KV_HEADS):
            pg, sg = [], []
            for hh in range(GQA_GROUP):
                s = jnp.where(mask, scores[blk, g][:, hh * WINDOW:(hh + 1) * WINDOW], -1e30)
                sink = sink_ref[layer, g * GQA_GROUP + hh] * log2e
                m = jnp.maximum(jnp.max(s, axis=0, keepdims=True), sink)
                pg.append(jnp.exp2(s - m).astype(BF16))
                sg.append(jnp.exp2(sink - m))
            probs[blk, g] = jnp.concatenate(pg, axis=1)
            sink_terms[blk, g] = jnp.concatenate(sg, axis=1)
    for blk in range(nq):
        outs = []
        for g in range(N_KV_HEADS):
            v_aug = jnp.concatenate([v_t[g * HEAD_DIM:(g + 1) * HEAD_DIM, blk * WINDOW:(blk + 2) * WINDOW],
                                     ones_rows], axis=0).astype(BF16)
            og = jnp.dot(v_aug, probs[blk, g], preferred_element_type=F32)
            den = og[HEAD_DIM:HEAD_DIM + 1, :] + sink_terms[blk, g]
            og = og[0:HEAD_DIM, :] * (1.0 / den)
            outs.extend(og[:, hh * WINDOW:(hh + 1) * WINDOW] for hh in range(GQA_GROUP))
        o_ref[blk * WINDOW:(blk + 1) * WINDOW, :] = jnp.concatenate(outs, axis=0).T


def _attn_prompt(q, kv, sinks, layer):
    b, t, _ = q.shape
    nq = ATTN_Q_BLOCKS
    tq = nq * WINDOW
    return pl.pallas_call(
        functools.partial(_attn_prompt_kernel, layer, nq),
        grid=(b, t // tq),
        in_specs=[
            pl.BlockSpec(memory_space=pltpu.SMEM),
            pl.BlockSpec((None, tq, ATTN_WIDTH), lambda i, j: (i, j, 0)),
            pl.BlockSpec((None, WINDOW, 2 * KV_WIDTH), lambda i, j: (i, jnp.maximum(j * nq - 1, 0), 0)),
            pl.BlockSpec((None, tq, 2 * KV_WIDTH), lambda i, j: (i, j, 0)),
        ],
        out_specs=pl.BlockSpec((None, tq, ATTN_WIDTH), lambda i, j: (i, j, 0)),
        out_shape=jax.ShapeDtypeStruct((b, t, ATTN_WIDTH), F32),
        compiler_params=_cparams(("arbitrary", "arbitrary")),
        name="attn_prompt",
    )(sinks, q, kv, kv)


def _attn_sample_kernel(layer, tn, sink_ref, q_ref, kvn_ref, ck_ref, cv_ref, nk_in_ref, nv_in_ref,
                        o_ref, nk_ref, nv_ref):
    del nk_in_ref, nv_in_ref
    bb = q_ref.shape[0]
    rows = GQA_GROUP * tn
    kvn = kvn_ref[...]
    row = lax.broadcasted_iota(jnp.int32, (1, rows, 1), 1)
    tok = row % tn
    col = lax.broadcasted_iota(jnp.int32, (1, 1, WINDOW), 2)
    cmask = col >= tok
    for g in range(N_KV_HEADS):
        k_t = ck_ref[:, g]
        v_t = cv_ref[:, g]
        qg = q_ref[:, g] * (HEAD_DIM ** -0.5)
        qg_b = qg.astype(BF16).astype(F32)
        sc = jnp.einsum('bqd,bdw->bqw', qg.astype(BF16), k_t.astype(BF16), preferred_element_type=F32)
        sc = jnp.where(cmask, sc, -1e30)
        sink = jnp.zeros((1, rows, 1), F32)
        for hh in range(GQA_GROUP):
            sink = jnp.where(row // tn == hh, sink_ref[layer, g * GQA_GROUP + hh], sink)
        m = jnp.maximum(jnp.max(sc, axis=-1, keepdims=True), sink)
        sn = []
        for jn in range(tn):
            kn = kvn[:, g * tn + jn:g * tn + jn + 1, :].astype(BF16).astype(F32)
            s_j = jnp.sum(qg_b * kn, axis=-1, keepdims=True)
            s_j = jnp.where(tok >= jn, s_j, -1e30)
            sn.append(s_j)
            m = jnp.maximum(m, s_j)
        ec = jnp.exp(sc - m)
        den = jnp.sum(ec, axis=-1, keepdims=True) + jnp.exp(sink - m)
        en = [jnp.exp(s_j - m) for s_j in sn]
        for e_j in en:
            den = den + e_j
        inv = 1.0 / den
        o = jnp.einsum('bqw,bdw->bqd', (ec * inv).astype(BF16), v_t.astype(BF16), preferred_element_type=F32)
        for jn in range(tn):
            vn = kvn[:, (N_KV_HEADS + g) * tn + jn:(N_KV_HEADS + g) * tn + jn + 1, :].astype(BF16).astype(F32)
            o = o + (en[jn] * inv).astype(BF16).astype(F32) * vn
        o_ref[:, g] = o
    new_t = kvn.reshape(bb * 4 * tn, HEAD_DIM).T
    lane = lax.broadcasted_iota(jnp.int32, (HEAD_DIM, WINDOW), 1)
    for c_ref, n_ref, which in ((ck_ref, nk_ref, 0), (cv_ref, nv_ref, 1)):
        for g in range(N_KV_HEADS):
            for b in range(bb):
                src = (b * 2 * N_KV_HEADS + which * N_KV_HEADS + g) * tn
                fresh = pltpu.roll(new_t, (WINDOW - tn - src) % WINDOW, 1)
                kept = pltpu.roll(c_ref[b, g], WINDOW - tn, 1)
                n_ref[b, g] = jnp.where(lane >= WINDOW - tn, fresh, kept)


def _attn_sample(q, kv, ck, cv, nk_all, nv_all, sinks, tn, layer):
    b = q.shape[0] // tn
    bb = ATTN_SAMPLE_SEQS
    assert bb * 4 * tn == WINDOW
    rows = GQA_GROUP * tn
    qh = q.reshape(tn, b, N_KV_HEADS, GQA_GROUP, HEAD_DIM).transpose(1, 2, 3, 0, 4)
    qh = qh.reshape(b, N_KV_HEADS, rows, HEAD_DIM)
    kvn = kv.reshape(tn, b, 2 * N_KV_HEADS, HEAD_DIM).transpose(1, 2, 0, 3).reshape(b, 4 * tn, HEAD_DIM)
    cache = pl.BlockSpec((None, bb, N_KV_HEADS, HEAD_DIM, WINDOW), lambda i: (layer, i, 0, 0, 0))
    heads = pl.BlockSpec((bb, N_KV_HEADS, rows, HEAD_DIM), lambda i: (i, 0, 0, 0))
    o, nk_all, nv_all = pl.pallas_call(
        functools.partial(_attn_sample_kernel, layer, tn),
        grid=(b // bb,),
        in_specs=[
            pl.BlockSpec(memory_space=pltpu.SMEM),
            heads,
            pl.BlockSpec((bb, 4 * tn, HEAD_DIM), lambda i: (i, 0, 0)),
            cache, cache,
            pl.BlockSpec(memory_space=pl.ANY), pl.BlockSpec(memory_space=pl.ANY),
        ],
        out_specs=[heads, cache, cache],
        out_shape=[
            jax.ShapeDtypeStruct((b, N_KV_HEADS, rows, HEAD_DIM), F32),
            jax.ShapeDtypeStruct(nk_all.shape, F32),
            jax.ShapeDtypeStruct(nv_all.shape, F32),
        ],
        input_output_aliases={5: 1, 6: 2},
        compiler_params=_cparams(("arbitrary",)),
        name="attn_sample",
    )(sinks, qh, kvn, ck, cv, nk_all, nv_all)
    o = o.reshape(b, N_KV_HEADS, GQA_GROUP, tn, HEAD_DIM).transpose(3, 0, 1, 2, 4)
    return o.reshape(tn * b, ATTN_WIDTH), nk_all, nv_all


def _tile_rows(x, n):
    return jnp.concatenate([x] * n, axis=0)


def _rwkv_kernel(c, nseq, x_ref, sh0_ref, st0_ref, mu_ref, lora_ref, gup_ref, vec_ref, ones_ref,
                 stacked_ref, o_ref, st_ref, last_ref, state_ref):
    del stacked_ref
    hw = RWKV_WIDTH
    n = RWKV_HEAD
    cw = RWKV_HEADS * c

    @pl.when(pl.program_id(1) == 0)
    def _():
        last_ref[...] = sh0_ref[...]
        for s in range(nseq):
            for h in range(RWKV_HEADS):
                state_ref[s, :, h * n:(h + 1) * n] = st0_ref[s, h]

    row = lax.broadcasted_iota(jnp.int32, (c, 1), 0)
    lane128 = lax.broadcasted_iota(jnp.int32, (c, 128), 1)
    tri = jnp.where(lax.broadcasted_iota(jnp.int32, (c, c), 1) <= lax.broadcasted_iota(jnp.int32, (c, c), 0),
                    1.0, 0.0).astype(BF16)
    tri3 = jnp.concatenate([tri, tri, tri], axis=1)
    head_rows = (lax.broadcasted_iota(jnp.int32, (cw, hw), 0) // c
                 == lax.broadcasted_iota(jnp.int32, (cw, hw), 1) // n)
    t_i = lax.broadcasted_iota(jnp.int32, (c, cw), 0)
    s_i = lax.broadcasted_iota(jnp.int32, (c, cw), 1) % c
    strict = s_i < t_i
    incl = s_i <= t_i
    eye_all = jnp.where(s_i == t_i, 1.0, 0.0)
    blk = (lax.broadcasted_iota(jnp.int32, (cw, cw), 0) // c
           == lax.broadcasted_iota(jnp.int32, (cw, cw), 1) // c)
    sblk = (lax.broadcasted_iota(jnp.int32, (hw, hw), 0) // n
            == lax.broadcasted_iota(jnp.int32, (hw, hw), 1) // n)
    masks = (row, lane128, tri3, head_rows, strict, incl, eye_all, blk, sblk)
    nsub = x_ref.shape[1] // c
    state_owner = [0] * nseq
    shared = {"lora": lora_ref, "gup": gup_ref, "sums": ones_ref}
    chains = {sub: [_rwkv_chunk(c, s, sub, nsub, state_owner, masks, x_ref, mu_ref, vec_ref,
                                o_ref, st_ref, last_ref, state_ref) for s in range(nseq)]
              for sub in range(nsub)}
    answers = {sub: [None] * nseq for sub in range(nsub)}
    rnd = 0
    while chains:
        for sub in sorted(chains):
            if rnd < sub * RWKV_STAGGER:
                continue
            asked = [_advance(ch, ans) for ch, ans in zip(chains[sub], answers[sub])]
            if asked[0] == "done":
                del chains[sub]
                continue
            results = [[] for _ in range(nseq)]
            for i, (kind, _) in enumerate(asked[0] or ()):
                stacked = _bdot(jnp.concatenate([req[i][1] for req in asked], axis=0), shared[kind][...])
                for s in range(nseq):
                    results[s].append(stacked[s * c:(s + 1) * c])
            answers[sub] = [tuple(res) for res in results]
        rnd += 1


def _advance(chain, answer):
    try:
        return next(chain) if answer is None else chain.send(answer)
    except StopIteration:
        return "done"


def _rwkv_chunk(c, s, sub, nsub, state_owner, masks, x_ref, mu_ref, vec_ref, o_ref, st_ref, last_ref, state_ref):
    row, lane128, tri3, head_rows, strict, incl, eye_all, blk, sblk = masks
    hw = RWKV_WIDTH
    n = RWKV_HEAD
    cw = RWKV_HEADS * c
    x = x_ref[s, sub * c:(sub + 1) * c, :]
    before = last_ref[s] if sub == 0 else x_ref[s, sub * c - 1:sub * c, :]
    prev = jnp.where(row == 0, before, pltpu.roll(x, 1, 0))
    xs = x + (prev - x) * mu_ref[...]

    r = xs[:, 0:hw]
    k = xs[:, hw:2 * hw]
    v = xs[:, 2 * hw:3 * hw]
    wa = xs[:, 3 * hw:3 * hw + 128]
    xg = xs[:, 3 * hw + 128:]
    lora_in = jnp.where(lane128 < 64, jnp.tanh(wa), wa)
    w0, a0, k_k, k_a, r_k, ln_w, ln_b = (vec_ref[i:i + 1, :] for i in range(7))
    kk = k * k_k
    lora, g, ss = yield (("lora", lora_in), ("gup", _sigmoid(xg)), ("sums", kk * kk))
    w_log = -_softplus(-(w0 + lora[:, 0:hw])) - 0.5
    logw = -jnp.exp(w_log)
    a = _sigmoid(a0 + lora[:, hw:2 * hw])
    kk = kk * lax.rsqrt(jnp.maximum(ss, 1e-24))
    k2 = k * (1.0 + (a - 1.0) * k_a)
    bv = kk * a

    pieces = []
    rem = logw
    for _ in range(3):
        p = rem.astype(BF16)
        rem = rem - p.astype(F32)
        pieces.append(p)
    cum = jnp.dot(tri3, jnp.concatenate(pieces, axis=0), preferred_element_type=F32)
    (bonus_sum,) = yield (("sums", r * k2 * r_k),)
    e_inc = jnp.exp(cum)
    e_exc = jnp.exp(cum - logw)
    e_inv = jnp.exp(-cum)
    kq = kk * e_exc
    rq = r * e_inc
    kd = k2 * e_inv
    bd = bv * e_inv
    w_end = e_inc[c - 1:c, :]
    kend = kd * w_end
    bend = bd * w_end

    def expand(z):
        return jnp.where(head_rows, _tile_rows(z, RWKV_HEADS), 0.0)

    lhs = jnp.concatenate([kq, rq], axis=0).astype(BF16)
    rhs = jnp.concatenate([expand(bd), expand(kd)], axis=0).astype(BF16)
    nt = (((1,), (1,)), ((), ()))
    prod = lax.dot_general(lhs, rhs, nt, preferred_element_type=F32)
    yield
    a_b = jnp.where(strict, prod[0:c, 0:cw], 0.0)
    a_k = jnp.where(strict, prod[0:c, cw:2 * cw], 0.0)
    p_b = jnp.where(incl, prod[c:2 * c, 0:cw], 0.0)
    p_k = jnp.where(incl, prod[c:2 * c, cw:2 * cw], 0.0)

    def bdiag(z):
        return jnp.where(blk, _tile_rows(z, RWKV_HEADS), 0.0)

    xm = -a_b
    tinv = eye_all + xm
    v_d = expand(v)
    from_v = _bdot(jnp.concatenate([a_k, p_k], axis=0), v_d)
    levels = c.bit_length() - 1
    xm = _bdot(xm, bdiag(xm))
    yield
    for _ in range(1, levels - 1):
        both = _bdot(jnp.concatenate([xm, tinv], axis=0), bdiag(xm))
        yield
        xm = both[0:c]
        tinv = tinv + both[c:2 * c]
    tinv = tinv + _bdot(tinv, bdiag(xm))
    yield
    assert state_owner[s] == sub
    st = state_ref[s]
    st_d = jnp.where(sblk, _tile_rows(st, RWKV_HEADS), 0.0)
    from_state = lax.dot_general(lhs, st_d.astype(BF16), nt, preferred_element_type=F32)
    yield
    u = _bdot(tinv, expand(from_state[0:c] + from_v[0:c]))
    yield
    y = from_state[c:2 * c] + from_v[c:2 * c] - _bdot(p_b, expand(u))

    lhs_s = jnp.concatenate([v, u], axis=0)
    rhs_s = jnp.concatenate([kend, -bend], axis=0)
    tn_dims = (((0,), (0,)), ((), ()))
    gm = lax.dot_general(lhs_s.astype(BF16), rhs_s.astype(BF16), tn_dims, preferred_element_type=F32)
    yield
    gm = jnp.where(sblk, gm, 0.0)
    st_new = st * w_end + (gm[0:n] + gm[n:2 * n] + (gm[2 * n:3 * n] + gm[3 * n:4 * n]))
    state_ref[s] = st_new
    state_owner[s] = sub + 1
    if sub == nsub - 1:
        state_owner[s] = 0
        last_ref[s] = x[c - 1:c, :]
        for h in range(RWKV_HEADS):
            st_ref[s, h] = st_new[:, h * n:(h + 1) * n]

    (y_sum,) = yield (("sums", y),)
    yc = y - y_sum * (1.0 / n)
    (sq_sum,) = yield (("sums", yc * yc),)
    yn = yc * lax.rsqrt(sq_sum * (1.0 / n) + GN_EPS) * ln_w + ln_b
    o_ref[s, sub * c:(sub + 1) * c, :] = (yn + bonus_sum * v) * g


def _rwkv(x, sh0, st0, mu, lora_w, g_up, vecs, ones_blk, stacked, layer):
    b, t, _ = x.shape
    c = RWKV_CHUNK
    tstep = RWKV_SUBCHUNKS * c
    nseq = min(RWKV_SEQS, b)
    wspec = lambda s: _layer_spec(s, layer, 2)
    hstate = (nseq, RWKV_HEADS, RWKV_HEAD, RWKV_HEAD)
    return pl.pallas_call(
        functools.partial(_rwkv_kernel, c, nseq),
        grid=(b // nseq, t // tstep),
        in_specs=[
            pl.BlockSpec((nseq, tstep, RWKV_PROJ), lambda i, j: (i, j, 0)),
            pl.BlockSpec((nseq, 1, RWKV_PROJ), lambda i, j: (i, 0, 0)),
            pl.BlockSpec(hstate, lambda i, j: (i, 0, 0, 0)),
            wspec((1, RWKV_PROJ)), wspec((128, 512)), wspec((128, RWKV_WIDTH)), wspec((8, RWKV_WIDTH)),
            pl.BlockSpec((RWKV_WIDTH, RWKV_WIDTH), lambda i, j: (0, 0)),
            pl.BlockSpec(memory_space=pl.ANY),
        ],
        out_specs=[
            pl.BlockSpec((nseq, tstep, RWKV_WIDTH), lambda i, j: (i, j, 0)),
            pl.BlockSpec((None,) + hstate, lambda i, j: (layer, i, 0, 0, 0)),
        ],
        out_shape=[
            jax.ShapeDtypeStruct((b, t, RWKV_WIDTH), F32),
            jax.ShapeDtypeStruct(stacked.shape, F32),
        ],
        scratch_shapes=[pltpu.VMEM((nseq, 1, RWKV_PROJ), F32), pltpu.VMEM((nseq, RWKV_HEAD, RWKV_WIDTH), F32)],
        input_output_aliases={8: 1},
        compiler_params=_cparams(("arbitrary", "arbitrary")),
        name="rwkv",
    )(x, sh0, st0, mu, lora_w, g_up, vecs, ones_blk, stacked)


def _rwkv_step_kernel(tn, nb, x_ref, sh0_ref, st0_ref, mu_ref, lora_ref, gup_ref, vec_ref, ones_ref,
                      stacked_ref, o_ref, st_ref, vt_ref, nat_ref, yt_ref):
    del stacked_ref
    hw = RWKV_WIDTH
    n = RWKV_HEAD
    h = pl.program_id(0)
    rows = tn * nb

    @pl.when(h == 0)
    def _():
        x = x_ref[...]
        prev = jnp.concatenate([sh0_ref[...], x[0:rows - nb]], axis=0)
        xs = x + (prev - x) * mu_ref[...]
        r = xs[:, 0:hw]
        k = xs[:, hw:2 * hw]
        v = xs[:, 2 * hw:3 * hw]
        wa = xs[:, 3 * hw:3 * hw + 128]
        xg = xs[:, 3 * hw + 128:]
        lane128 = lax.broadcasted_iota(jnp.int32, (rows, 128), 1)
        lora = _bdot(jnp.where(lane128 < 64, jnp.tanh(wa), wa), lora_ref[...])
        w0, a0, k_k, k_a, r_k = (vec_ref[i:i + 1, :] for i in range(5))
        ones_blk = ones_ref[...]
        w_log = -_softplus(-(w0 + lora[:, 0:hw])) - 0.5
        decay = jnp.exp(-jnp.exp(w_log))
        a = _sigmoid(a0 + lora[:, hw:2 * hw])
        kk = k * k_k
        kk = kk * lax.rsqrt(jnp.maximum(_block_sums(kk * kk, ones_blk), 1e-24))
        k2 = k * (1.0 + (a - 1.0) * k_a)
        nat_ref[0] = _bdot(_sigmoid(xg), gup_ref[...])
        nat_ref[1] = v
        nat_ref[2] = _block_sums(r * k2 * r_k, ones_blk)
        for i, z in enumerate((r, decay, k2, v, kk, kk * a)):
            for t in range(tn):
                for half in range(hw // 128):
                    vt_ref[i, t, half * 128:(half + 1) * 128, :] = (
                        z[t * nb:(t + 1) * nb, half * 128:(half + 1) * 128].T)

    base = pl.multiple_of(h * n, n)

    def value_row(vi, carry):
        s = st0_ref[vi]
        for t in range(tn):
            r_t, w_t, k_t, _, kk_t, b_t = (vt_ref[i, t, pl.ds(base, n), :] for i in range(6))
            v_row = vt_ref[3, t, pl.ds(base + vi, 1), :]
            u = jnp.sum(s * kk_t, axis=0, keepdims=True)
            s = s * w_t - u * b_t + v_row * k_t
            yt_ref[t, pl.ds(base + vi, 1), :] = jnp.sum(s * r_t, axis=0, keepdims=True)
        st_ref[vi] = s
        return carry

    lax.fori_loop(0, n, value_row, 0, unroll=4)

    @pl.when(h == RWKV_HEADS - 1)
    def _():
        ones_blk = ones_ref[...]
        ln_w = vec_ref[5:6, :]
        ln_b = vec_ref[6:7, :]
        y = jnp.concatenate(
            [jnp.concatenate([yt_ref[t, half * 128:(half + 1) * 128, :].T for half in range(hw // 128)], axis=1)
             for t in range(tn)], axis=0)
        mean = _block_sums(y, ones_blk) * (1.0 / n)
        yc = y - mean
        var = _block_sums(yc * yc, ones_blk) * (1.0 / n)
        yn = yc * lax.rsqrt(var + GN_EPS) * ln_w + ln_b
        o_ref[...] = (yn + nat_ref[2] * nat_ref[1]) * nat_ref[0]


def _rwkv_sample(x, sh0, st0, mu, lora_w, g_up, vecs, ones_blk, stacked, tn, layer):
    rows = x.shape[0]
    nb = rows // tn
    wspec = lambda s: _layer_spec(s, layer, 1)
    head_state = pl.BlockSpec((None, None, RWKV_HEAD, RWKV_HEAD, nb), lambda h: (layer, h, 0, 0, 0))
    return pl.pallas_call(
        functools.partial(_rwkv_step_kernel, tn, nb),
        grid=(RWKV_HEADS,),
        in_specs=[
            pl.BlockSpec((rows, RWKV_PROJ), lambda h: (0, 0)),
            wspec((nb, RWKV_PROJ)), head_state,
            wspec((1, RWKV_PROJ)), wspec((128, 512)), wspec((128, RWKV_WIDTH)), wspec((8, RWKV_WIDTH)),
            pl.BlockSpec((RWKV_WIDTH, RWKV_WIDTH), lambda h: (0, 0)),
            pl.BlockSpec(memory_space=pl.ANY),
        ],
        out_specs=[pl.BlockSpec((rows, RWKV_WIDTH), lambda h: (0, 0)), head_state],
        out_shape=[jax.ShapeDtypeStruct((rows, RWKV_WIDTH), F32), jax.ShapeDtypeStruct(stacked.shape, F32)],
        scratch_shapes=[pltpu.VMEM((6, tn, RWKV_WIDTH, nb), F32), pltpu.VMEM((3, rows, RWKV_WIDTH), F32),
                        pltpu.VMEM((tn, RWKV_WIDTH, nb), F32)],
        input_output_aliases={8: 1},
        compiler_params=_cparams(("arbitrary",)),
        name="rwkv_step",
    )(x, sh0, st0, mu, lora_w, g_up, vecs, ones_blk, stacked)


def _lru_gates(xc, wg_ref, vec_ref):
    w = LRU_WIDTH
    _, b_a, b_i, lam = (vec_ref[i:i + 1, :] for i in range(4))
    gates = _bdot(xc, wg_ref[...])
    r = _sigmoid(gates[:, 0:w] + b_a)
    i = _sigmoid(gates[:, w:2 * w] + b_i)
    log_a = LRU_C * r * (-_softplus(-lam))
    a = jnp.exp(log_a)
    th = jnp.tanh(log_a)
    u = jnp.sqrt(-2.0 * th / (1.0 - th)) * (i * xc)
    return a, u


def _lru_tile_steps(x_ref, carry, ext_ref, o_ref, cw_ref, wg_ref, vec_ref):
    tc = x_ref.shape[0]
    w = LRU_WIDTH
    blk = LRU_SCAN_BLOCK
    gr = tc // LRU_PIECES
    ext_ref[8:8 + tc, :] = x_ref[:, 0:w]
    xc = vec_ref[0:1, :] + ext_ref[8:8 + tc, :] * cw_ref[CONV_W - 1:CONV_W, :]
    for j in range(CONV_W - 1):
        xc = xc + ext_ref[pl.ds(8 - (CONV_W - 1) + j, tc), :] * cw_ref[j:j + 1, :]
    ext_ref[0:8, :] = ext_ref[tc:tc + 8, :]
    gates = _bdot(xc, wg_ref[...])
    yield
    _, b_a, b_i, lam = (vec_ref[i:i + 1, :] for i in range(4))
    log_lam = -_softplus(-lam)
    pos = lax.broadcasted_iota(jnp.int32, (gr, 1), 0) % blk
    for g in range(LRU_PIECES):
        rows = slice(g * gr, (g + 1) * gr)
        xc_g = xc[rows]
        r = _sigmoid(gates[rows, 0:w] + b_a)
        i = _sigmoid(gates[rows, w:2 * w] + b_i)
        log_a = LRU_C * r * log_lam
        a = jnp.exp(log_a)
        th = jnp.tanh(log_a)
        u = jnp.sqrt(-2.0 * th / (1.0 - th)) * (i * xc_g)
        span = 1
        while span < 8:
            ok = pos >= span
            a_s = pltpu.roll(a, span, 0)
            u_s = pltpu.roll(u, span, 0)
            u = jnp.where(ok, a * u_s + u, u)
            a = jnp.where(ok, a * a_s, a)
            span *= 2
        a = a.reshape(gr // blk, blk, w)
        u = u.reshape(gr // blk, blk, w)
        while span < blk:
            u = jnp.concatenate([u[:, :span], a[:, span:] * u[:, :blk - span] + u[:, span:]], axis=1)
            a = jnp.concatenate([a[:, :span], a[:, span:] * a[:, :blk - span]], axis=1)
            span *= 2
        gelu = _gelu_tanh(x_ref[rows, w:2 * w])
        for b in range(gr // blk):
            h = a[b] * carry + u[b]
            carry = h[blk - 1:blk, :]
            o_ref[g * gr + b * blk:g * gr + (b + 1) * blk, :] = h * gelu[b * blk:(b + 1) * blk]
        yield
    return carry


def _step(steps, result):
    if not result:
        try:
            next(steps)
        except StopIteration as done:
            result.append(done.value)


def _run_to_end(steps, result=None):
    result = [] if result is None else result
    while not result:
        _step(steps, result)
    return result[0]


def _lru_step_kernel(tn, nb, x_ref, cv0_ref, h0_ref, cw_ref, wg_ref, vec_ref, o_ref, h_ref):
    w = LRU_WIDTH
    rows = tn * nb
    xb = x_ref[:, 0:w]
    gb = x_ref[:, w:2 * w]
    ext = jnp.concatenate([cv0_ref[j] for j in range(CONV_W - 1)] + [xb], axis=0)
    xc = vec_ref[0:1, :]
    for j in range(CONV_W):
        xc = xc + ext[j * nb:j * nb + rows] * cw_ref[j:j + 1, :]
    a, u = _lru_gates(xc, wg_ref, vec_ref)
    gelu = _gelu_tanh(gb)
    h = h0_ref[...]
    for t in range(tn):
        h = a[t * nb:(t + 1) * nb] * h + u[t * nb:(t + 1) * nb]
        o_ref[t * nb:(t + 1) * nb, :] = h * gelu[t * nb:(t + 1) * nb]
    h_ref[...] = h


def _lru_sample(x, cv0, h0, conv_w, w_gates, vecs, tn, layer):
    rows = x.shape[0]
    nb = rows // tn
    wspec = lambda s: _layer_spec(s, layer, 1)
    return pl.pallas_call(
        functools.partial(_lru_step_kernel, tn, nb),
        grid=(1,),
        in_specs=[pl.BlockSpec((rows, 2 * LRU_WIDTH), lambda i: (0, 0)),
                  wspec((CONV_W - 1, nb, LRU_WIDTH)), wspec((nb, LRU_WIDTH)),
                  wspec((CONV_W, LRU_WIDTH)), wspec((LRU_WIDTH, 2 * LRU_WIDTH)), wspec((4, LRU_WIDTH))],
        out_specs=[pl.BlockSpec((rows, LRU_WIDTH), lambda i: (0, 0)), pl.BlockSpec((nb, LRU_WIDTH), lambda i: (0, 0))],
        out_shape=[jax.ShapeDtypeStruct((rows, LRU_WIDTH), F32), jax.ShapeDtypeStruct((nb, LRU_WIDTH), F32)],
        compiler_params=_cparams(("arbitrary",)),
        name="lru_step",
    )(x, cv0, h0, conv_w, w_gates, vecs)


def _post_kernel(lru_tiles, ncast, *refs):
    casts = []
    if lru_tiles:
        (x_ref, oa_ref, ob_ref, lr_next_ref, lr_first_ref, cv0_ref, h0_ref, p_ref, wo_ref, wg_ref, wu_ref, wd_ref,
         pg_ref, pw_ref, vec_ref, lcw_ref, lwg_ref, lvec_ref) = refs[:18]
        y_ref, h_ref = refs[18 + ncast:20 + ncast]
        casts = list(zip(refs[18:18 + ncast], refs[20 + ncast:20 + 2 * ncast]))
        oc_ref, ext_ref, hc_ref = refs[20 + 2 * ncast:]
        i = pl.program_id(0)

        @pl.when(i == 0)
        def _():
            ext_ref[0:8, :] = cv0_ref[...]
            hc_ref[...] = _run_to_end(_lru_tile_steps(lr_first_ref, h0_ref[...], ext_ref, oc_ref,
                                                      lcw_ref, lwg_ref, lvec_ref))
    else:
        (x_ref, oa_ref, ob_ref, oc_ref, p_ref, wo_ref, wg_ref, wu_ref, wd_ref, pg_ref, pw_ref,
         vec_ref, y_ref) = refs
    mix = (jnp.dot(oa_ref[...].astype(BF16), wo_ref[0:512, :], preferred_element_type=F32)
           + jnp.dot(ob_ref[...].astype(BF16), wo_ref[512:768, :], preferred_element_type=F32)
           + jnp.dot(oc_ref[...].astype(BF16), wo_ref[768:1024, :], preferred_element_type=F32)
           + vec_ref[0:1, :])
    mixer_ahead, mixer_state = None, []
    if lru_tiles:
        opens = (i + 1) % lru_tiles == 0
        ext_ref[0:8, :] = jnp.where(opens, cv0_ref[...], ext_ref[0:8, :])
        carry_in = jnp.where(opens, h0_ref[...], hc_ref[...])
        mixer_ahead = _lru_tile_steps(lr_next_ref, carry_in, ext_ref, oc_ref, lcw_ref, lwg_ref, lvec_ref)

    x = x_ref[...] + _rmsnorm(mix, vec_ref[1:2, :])
    f = _rmsnorm(x, vec_ref[2:3, :]).astype(BF16)
    acc = None
    for lo in range(0, D_FF, FF_CHUNK):
        gate = jnp.dot(f, wg_ref[:, lo:lo + FF_CHUNK], preferred_element_type=F32)
        up = jnp.dot(f, wu_ref[:, lo:lo + FF_CHUNK], preferred_element_type=F32)
        hid = (gate * _sigmoid(gate) * up).astype(BF16)
        part = jnp.dot(hid, wd_ref[lo:lo + FF_CHUNK, :], preferred_element_type=F32)
        acc = part if acc is None else acc + part
        if mixer_ahead is not None:
            _step(mixer_ahead, mixer_state)
        if casts:
            src, dst = casts.pop()
            dst[...] = src[...].astype(BF16)
    assert not casts
    if mixer_ahead is not None:
        carry = _run_to_end(mixer_ahead, mixer_state)
        carry = jnp.where(i + 1 < pl.num_programs(0), carry, hc_ref[...])
        hc_ref[...] = carry
        h_ref[...] = carry
    x2 = x + _rmsnorm(acc, vec_ref[3:4, :])
    gate = _sigmoid(jnp.dot(x2.astype(BF16), pg_ref[...], preferred_element_type=F32))
    emb = jnp.dot(p_ref[...].astype(BF16), pw_ref[...], preferred_element_type=F32)
    y_ref[...] = x2 + gate * emb


def _post_weight_specs(layer):
    own = lambda s: _layer_spec(s, 0, 1, pipeline_mode=pl.Buffered(1))
    stacked = lambda s: _layer_spec(s, layer, 1, pipeline_mode=pl.Buffered(1))
    return [own((D_MODEL, D_MODEL)), own((D_MODEL, D_FF)), own((D_MODEL, D_FF)), own((D_FF, D_MODEL)),
            own((D_MODEL, D_MODEL)), stacked((PLE_DIM, D_MODEL)), stacked((4, D_MODEL))]


def _post(x, oa, ob, oc, p, wo, wg, wu, wd, pg, pw, vecs, layer):
    m = x.shape[0]
    tm = min(ROW_TILE, m)
    row = lambda w_: pl.BlockSpec((tm, w_), lambda i: (i, 0))
    return pl.pallas_call(
        functools.partial(_post_kernel, 0, 0),
        grid=(m // tm,),
        in_specs=[row(D_MODEL), row(512), row(256), row(256),
                  pl.BlockSpec((None, tm, PLE_DIM), lambda i: (layer, i, 0))] + _post_weight_specs(layer),
        out_specs=row(D_MODEL),
        out_shape=jax.ShapeDtypeStruct((m, D_MODEL), F32),
        compiler_params=_cparams(("arbitrary",)),
        name="post",
    )(x, oa, ob, oc, p, wo, wg, wu, wd, pg, pw, vecs)


def _post_with_lru(x, oa, ob, lr, cv0, h0, t, p, wo, wg, wu, wd, pg, pw, vecs, conv_w, w_gates, lru_vecs, layer,
                   cast_next):
    m = x.shape[0]
    tm = ROW_TILE
    tiles = t // tm
    nsteps = m // tm
    row = lambda w_: pl.BlockSpec((tm, w_), lambda i: (i, 0))
    lspec = lambda s: _layer_spec(s, layer, 1)
    ahead = lambda i: jnp.minimum(i + 1, nsteps - 1)
    seq_state = lambda r: pl.BlockSpec((None, r, LRU_WIDTH), lambda i: (ahead(i) // tiles, 0, 0))
    cast_in, cast_out, cast_shapes = [], [], []
    for wgt in cast_next:
        _, rows, cols = wgt.shape
        slices = max(s for s in range(1, nsteps + 1)
                     if nsteps % s == 0 and rows % s == 0 and (rows // s) % 16 == 0)
        blk = rows // slices
        rep = nsteps // slices
        cast_in.append(pl.BlockSpec((None, blk, cols), lambda i, rep=rep: (layer + 1, i // rep, 0)))
        cast_out.append(pl.BlockSpec((None, blk, cols), lambda i, rep=rep: (0, i // rep, 0)))
        cast_shapes.append(jax.ShapeDtypeStruct((1, rows, cols), BF16))
    return pl.pallas_call(
        functools.partial(_post_kernel, tiles, len(cast_next)),
        grid=(nsteps,),
        in_specs=[row(D_MODEL), row(512), row(256),
                  pl.BlockSpec((tm, 2 * LRU_WIDTH), lambda i: (ahead(i), 0)),
                  pl.BlockSpec((tm, 2 * LRU_WIDTH), lambda i: (0, 0)),
                  seq_state(8), seq_state(1),
                  pl.BlockSpec((None, tm, PLE_DIM), lambda i: (layer, i, 0))] + _post_weight_specs(layer) + [
                  lspec((CONV_W, LRU_WIDTH)), lspec((LRU_WIDTH, 2 * LRU_WIDTH)), lspec((4, LRU_WIDTH))] + cast_in,
        out_specs=[row(D_MODEL), seq_state(1)] + cast_out,
        out_shape=[jax.ShapeDtypeStruct((m, D_MODEL), F32),
                   jax.ShapeDtypeStruct((m // t, 1, LRU_WIDTH), F32)] + cast_shapes,
        scratch_shapes=[pltpu.VMEM((tm, LRU_WIDTH), F32), pltpu.VMEM((tm + 8, LRU_WIDTH), F32),
                        pltpu.VMEM((1, LRU_WIDTH), F32)],
        compiler_params=_cparams(("arbitrary",)),
        name="post_lru",
    )(x, oa, ob, lr, lr, cv0, h0, p, wo, wg, wu, wd, pg, pw, vecs, conv_w, w_gates, lru_vecs, *cast_next)


def _block_diag(w):
    nl, nb, n, _ = w.shape
    eye = jnp.eye(nb, dtype=w.dtype)
    return (eye[None, :, None, :, None] * w[:, :, :, None, :]).reshape(nl, nb * n, nb * n)


BIG_WEIGHTS = ('w_in', 'w_out', 'ffn_w_gate', 'ffn_w_up', 'ffn_w_down', 'ple_gate_w')


def _post_weights(wts, big):
    return (big['w_out'], big['ffn_w_gate'], big['ffn_w_up'], big['ffn_w_down'], big['ple_gate_w'], wts['ple_w'],
            wts['post_vecs'])


def _layer_prompt(x2, b, t, p, zeros, wkv_all, wts, big, raw_big, layer):
    m = b * t
    q, kv, rw, lr = _in_proj(x2, wts['norm_mix_pre'], big['w_in'], wts['b_in'], layer)
    kv = kv.reshape(b, t, 256)
    rw = rw.reshape(b, t, RWKV_PROJ)
    sh0, st0, cv0, h0 = zeros
    o_a = _attn_prompt(q.reshape(b, t, 512), kv, wts['attn_sinks'], layer)
    nk = kv[:, t - WINDOW:, 0:KV_WIDTH].reshape(b, WINDOW, N_KV_HEADS, HEAD_DIM)
    nv = kv[:, t - WINDOW:, KV_WIDTH:].reshape(b, WINDOW, N_KV_HEADS, HEAD_DIM)
    o_b, wkv_all = _rwkv(rw, sh0, st0, wts['rwkv_mu'], wts['rwkv_lora'], wts['rwkv_g_up'],
                         wts['rwkv_vecs'], wts['ones_blk'], wkv_all, layer)
    nsh = rw[:, t - 1, :]
    nconv = lr.reshape(b, t, 2 * LRU_WIDTH)[:, t - (CONV_W - 1):, 0:LRU_WIDTH]
    cast_next = [raw_big[name] for name in BIG_WEIGHTS] if layer + 1 < DEPTH else []
    x2, nh, *cast = _post_with_lru(x2, o_a.reshape(m, 512), o_b.reshape(m, 256), lr, cv0, h0, t, p,
                                   *_post_weights(wts, big), wts['lru_conv_w'], wts['lru_w_gates'],
                                   wts['lru_vecs'], layer, cast_next)
    next_big = dict(zip(BIG_WEIGHTS, cast))
    return x2, (nk, nv, nsh, nconv, nh.reshape(b, LRU_WIDTH)), wkv_all, next_big


def _layer_sample(x2, b, t, p, state, outs, wts, big, layer):
    ck, cv, sh0, st0, cv0, h0 = state
    nk_all, nv_all, wkv_all = outs
    q, kv, rw, lr = _in_proj(x2, wts['norm_mix_pre'], big['w_in'], wts['b_in'], layer)
    o_a, nk_all, nv_all = _attn_sample(q, kv, ck, cv, nk_all, nv_all, wts['attn_sinks'], t, layer)
    o_b, wkv_all = _rwkv_sample(rw, sh0, st0, wts['rwkv_mu'], wts['rwkv_lora'], wts['rwkv_g_up'],
                                wts['rwkv_vecs'], wts['ones_blk'], wkv_all, t, layer)
    nsh = rw[(t - 1) * b:, :]
    o_c, nh = _lru_sample(lr, cv0, h0, wts['lru_conv_w'], wts['lru_w_gates'], wts['lru_vecs'], t, layer)
    nconv = lr[(t - (CONV_W - 1)) * b:, 0:LRU_WIDTH].reshape(CONV_W - 1, b, LRU_WIDTH)
    x2 = _post(x2, o_a, o_b, o_c, p, *_post_weights(wts, big), layer)
    return x2, (nsh, nconv, nh), (nk_all, nv_all, wkv_all)


def kernel(x_prompt, x_sample, cache_k, cache_v, state_shift, state_wkv, state_conv, state_lru,
           p_prompt, p_sample, norm_mix_pre, norm_mix_post, norm_ffn_pre, norm_ffn_post,
           w_in, b_in, attn_sinks, rwkv_mu, rwkv_w0, rwkv_w_up, rwkv_a0, rwkv_a_up, rwkv_g_up,
           rwkv_k_k, rwkv_k_a, rwkv_r_k, rwkv_ln_w, rwkv_ln_b, lru_conv_w, lru_conv_b,
           lru_w_a, lru_b_a, lru_w_i, lru_b_i, lru_L, w_out, b_out, ffn_w_gate, ffn_w_up,
           ffn_w_down, ple_w, ple_gate_w):
    nl = DEPTH
    bp, tp_, _ = x_prompt.shape
    bs, ts, _ = x_sample.shape
    head_id = jnp.arange(RWKV_WIDTH) // RWKV_HEAD
    zeros_w = jnp.zeros((nl, 64, RWKV_WIDTH), F32)
    wts = dict(
        norm_mix_pre=norm_mix_pre[:, None, :], b_in=b_in[:, None, :],
        attn_sinks=attn_sinks,
        rwkv_mu=rwkv_mu[:, None, :],
        rwkv_lora=jnp.concatenate([jnp.concatenate([rwkv_w_up, zeros_w], axis=2),
                                   jnp.concatenate([zeros_w, rwkv_a_up], axis=2)], axis=1).astype(BF16),
        rwkv_g_up=rwkv_g_up.astype(BF16),
        rwkv_vecs=jnp.stack([rwkv_w0, rwkv_a0, rwkv_k_k, rwkv_k_a, rwkv_r_k.reshape(nl, RWKV_WIDTH),
                             rwkv_ln_w, rwkv_ln_b, jnp.zeros((nl, RWKV_WIDTH), F32)], axis=1),
        ones_blk=(head_id[:, None] == head_id[None, :]).astype(BF16),
        lru_conv_w=lru_conv_w,
        lru_w_gates=jnp.concatenate([_block_diag(lru_w_a), _block_diag(lru_w_i)], axis=2).astype(BF16),
        lru_vecs=jnp.stack([lru_conv_b, lru_b_a, lru_b_i, lru_L], axis=1),
        ple_w=ple_w.astype(BF16),
        post_vecs=jnp.stack([b_out, norm_mix_post, norm_ffn_pre, norm_ffn_post], axis=1),
    )
    raw_big = dict(w_in=w_in, w_out=w_out, ffn_w_gate=ffn_w_gate, ffn_w_up=ffn_w_up, ffn_w_down=ffn_w_down,
                   ple_gate_w=ple_gate_w)
    big = {name: raw_big[name][0:1].astype(BF16) for name in BIG_WEIGHTS}
    zeros_p = (jnp.zeros((bp, 1, RWKV_PROJ), F32),
               jnp.zeros((bp, RWKV_HEADS, RWKV_HEAD, RWKV_HEAD), F32),
               jnp.zeros((bp, 8, LRU_WIDTH), F32),
               jnp.zeros((bp, 1, LRU_WIDTH), F32))
    st_s = (cache_k.transpose(0, 1, 3, 4, 2), cache_v.transpose(0, 1, 3, 4, 2),
            state_shift,
            state_wkv.transpose(0, 2, 3, 4, 1),
            state_conv.transpose(0, 2, 1, 3),
            state_lru)
    pp = p_prompt.reshape(nl, bp * tp_, PLE_DIM)
    ps = p_sample.transpose(0, 2, 1, 3).reshape(nl, ts * bs, PLE_DIM)

    xp = x_prompt.reshape(bp * tp_, D_MODEL)
    xs = x_sample.transpose(1, 0, 2).reshape(ts * bs, D_MODEL)
    wkv_p = jnp.zeros((nl, bp, RWKV_HEADS, RWKV_HEAD, RWKV_HEAD), F32)
    outs_s = (jnp.zeros((nl, bs, N_KV_HEADS, HEAD_DIM, WINDOW), F32),
              jnp.zeros((nl, bs, N_KV_HEADS, HEAD_DIM, WINDOW), F32),
              jnp.zeros((nl, RWKV_HEADS, RWKV_HEAD, RWKV_HEAD, bs), F32))
    new_p, new_s = [], []
    for i in range(nl):
        xp, sp, wkv_p, next_big = _layer_prompt(xp, bp, tp_, pp, zeros_p, wkv_p, wts, big, raw_big, i)
        xs, ss, outs_s = _layer_sample(xs, bs, ts, ps, st_s, outs_s, wts, big, i)
        big = next_big
        new_p.append(sp)
        new_s.append(ss)

    def stk(lst, j):
        return jnp.stack([s[j] for s in lst], axis=0)

    nk_s, nv_s, wkv_s = outs_s
    return (xp.reshape(bp, tp_, D_MODEL), xs.reshape(ts, bs, D_MODEL).transpose(1, 0, 2),
            stk(new_p, 0), stk(new_p, 1), stk(new_p, 2), wkv_p, stk(new_p, 3), stk(new_p, 4),
            nk_s.transpose(0, 1, 4, 2, 3), nv_s.transpose(0, 1, 4, 2, 3), stk(new_s, 0),
            wkv_s.transpose(0, 4, 1, 2, 3), stk(new_s, 1).transpose(0, 2, 1, 3), stk(new_s, 2))
```

```python
import functools

import jax
import jax.numpy as jnp
from jax import lax
from jax.experimental import pallas as pl
from jax.experimental.pallas import tpu as pltpu

F32 = jnp.float32
BF16 = jnp.bfloat16

D_MODEL = 1024
DEPTH = 4
HEAD_DIM = 64
ATTN_WIDTH = 512
N_HEADS = 8
N_KV_HEADS = 2
GQA_GROUP = 4
KV_WIDTH = 128
WINDOW = 128
RWKV_WIDTH = 256
RWKV_HEADS = 4
RWKV_HEAD = 64
RWKV_PROJ = 1024
LRU_WIDTH = 256
CONV_W = 4
LRU_C = 8.0
D_FF = 2816
PLE_DIM = 256
RMS_EPS = 1e-6
GN_EPS = 64e-5
IN_COLS = 2304

ROW_TILE = 512
IN_ROW_TILE = 1024
FF_CHUNK = 256
LRU_PIECES = 8
RWKV_CHUNK = 64
RWKV_SEQS = 4
RWKV_SUBCHUNKS = 4
RWKV_STAGGER = 3
LRU_SCAN_BLOCK = 32
ATTN_SAMPLE_SEQS = 8
ATTN_Q_BLOCKS = 16
VMEM_LIMIT = 56 * 1024 * 1024


def _cparams(sem):
    return pltpu.CompilerParams(dimension_semantics=sem, vmem_limit_bytes=VMEM_LIMIT)


def _layer_spec(shape, layer, nidx, **kw):
    zeros = (0,) * len(shape)
    if nidx == 1:
        return pl.BlockSpec((None,) + tuple(shape), lambda i: (layer,) + zeros, **kw)
    return pl.BlockSpec((None,) + tuple(shape), lambda i, j: (layer,) + zeros, **kw)


def _bdot(a, b):
    return jnp.dot(a.astype(BF16), b.astype(BF16), preferred_element_type=F32)


def _block_sums(x, ones_blk):
    return jnp.dot(x.astype(BF16), ones_blk, preferred_element_type=F32)


def _rmsnorm(x, g):
    ms = jnp.mean(x * x, axis=-1, keepdims=True)
    return x * lax.rsqrt(ms + RMS_EPS) * g


def _softplus(x):
    return jnp.maximum(x, 0.0) + jnp.log1p(jnp.exp(-jnp.abs(x)))


def _sigmoid(x):
    return 1.0 / (1.0 + jnp.exp(-x))


def _gelu_tanh(x):
    return 0.5 * x * (1.0 + jnp.tanh(0.7978845608028654 * (x + 0.044715 * (x * x * x))))


def _in_kernel(x_ref, g_ref, w_ref, b_ref, q_ref, kv_ref, rw_ref, lr_ref):
    h = _rmsnorm(x_ref[...], g_ref[...]).astype(BF16)
    for ref, lo, hi in ((q_ref, 0, 512), (kv_ref, 512, 768), (rw_ref, 768, 1792), (lr_ref, 1792, 2304)):
        ref[...] = jnp.dot(h, w_ref[:, lo:hi], preferred_element_type=F32) + b_ref[:, lo:hi]


def _in_proj(x, g, w, b, layer):
    m = x.shape[0]
    tm = min(IN_ROW_TILE, m)
    row = lambda w_: pl.BlockSpec((tm, w_), lambda i: (i, 0))
    return pl.pallas_call(
        _in_kernel,
        grid=(m // tm,),
        in_specs=[row(D_MODEL), _layer_spec((1, D_MODEL), layer, 1),
                  _layer_spec((D_MODEL, IN_COLS), 0, 1), _layer_spec((1, IN_COLS), layer, 1)],
        out_specs=[row(512), row(256), row(1024), row(512)],
        out_shape=[jax.ShapeDtypeStruct((m, n), F32) for n in (512, 256, 1024, 512)],
        compiler_params=_cparams(("arbitrary",)),
        name="in_proj",
    )(x, g, w, b)


def _attn_prompt_kernel(layer, nq, sink_ref, q_ref, kvp_ref, kvc_ref, o_ref):
    j = pl.program_id(1)
    log2e = 1.4426950408889634
    q = q_ref[...] * (HEAD_DIM ** -0.5 * log2e)
    kv = jnp.concatenate([kvp_ref[...], kvc_ref[...]], axis=0)
    kj = lax.broadcasted_iota(jnp.int32, (2 * WINDOW, WINDOW), 0)
    qi = lax.broadcasted_iota(jnp.int32, (2 * WINDOW, WINDOW), 1) + WINDOW
    d = qi - kj
    band = (d >= 0) & (d <= WINDOW)
    first = band & ((j > 0) | (kj >= WINDOW))
    nt = (((1,), (1,)), ((), ()))
    keys = kv[:, 0:KV_WIDTH].astype(BF16)
    v_t = kv[:, KV_WIDTH:2 * KV_WIDTH].T
    ones_rows = (lax.broadcasted_iota(jnp.int32, (8, 2 * WINDOW), 0) == 0).astype(F32)
    scores = {}
    for blk in range(nq):
        qb = q[blk * WINDOW:(blk + 1) * WINDOW]
        for g in range(N_KV_HEADS):
            qg = jnp.concatenate([qb[:, (g * GQA_GROUP + hh) * HEAD_DIM:(g * GQA_GROUP + hh + 1) * HEAD_DIM]
                                  for hh in range(GQA_GROUP)], axis=0).astype(BF16)
            k_ext = keys[blk * WINDOW:(blk + 2) * WINDOW, g * HEAD_DIM:(g + 1) * HEAD_DIM]
            scores[blk, g] = lax.dot_general(k_ext, qg, nt, preferred_element_type=F32)
    probs, sink_terms = {}, {}
    for blk in range(nq):
        mask = first if blk == 0 else band
        for g in range(N_KV_HEADS):
            pg, sg = [], []
            for hh in range(GQA_GROUP):
                s = jnp.where(mask, scores[blk, g][:, hh * WINDOW:(hh + 1) * WINDOW], -1e30)
                sink = sink_ref[layer, g * GQA_GROUP + hh] * log2e
                m = jnp.maximum(jnp.max(s, axis=0, keepdims=True), sink)
                pg.append(jnp.exp2(s - m).astype(BF16))
                sg.append(jnp.exp2(sink - m))
            probs[blk, g] = jnp.concatenate(pg, axis=1)
            sink_terms[blk, g] = jnp.concatenate(sg, axis=1)
    for blk in range(nq):
        outs = []
        for g in range(N_KV_HEADS):
            v_aug = jnp.concatenate([v_t[g * HEAD_DIM:(g + 1) * HEAD_DIM, blk * WINDOW:(blk + 2) * WINDOW],
                                     ones_rows], axis=0).astype(BF16)
            og = jnp.dot(v_aug, probs[blk, g], preferred_element_type=F32)
            den = og[HEAD_DIM:HEAD_DIM + 1, :] + sink_terms[blk, g]
            og = og[0:HEAD_DIM, :] * (1.0 / den)
            outs.extend(og[:, hh * WINDOW:(hh + 1) * WINDOW] for hh in range(GQA_GROUP))
        o_ref[blk * WINDOW:(blk + 1) * WINDOW, :] = jnp.concatenate(outs, axis=0).T


def _attn_prompt(q, kv, sinks, layer):
    b, t, _ = q.shape
    nq = ATTN_Q_BLOCKS
    tq = nq * WINDOW
    return pl.pallas_call(
        functools.partial(_attn_prompt_kernel, layer, nq),
        grid=(b, t // tq),
        in_specs=[
            pl.BlockSpec(memory_space=pltpu.SMEM),
            pl.BlockSpec((None, tq, ATTN_WIDTH), lambda i, j: (i, j, 0)),
            pl.BlockSpec((None, WINDOW, 2 * KV_WIDTH), lambda i, j: (i, jnp.maximum(j * nq - 1, 0), 0)),
            pl.BlockSpec((None, tq, 2 * KV_WIDTH), lambda i, j: (i, j, 0)),
        ],
        out_specs=pl.BlockSpec((None, tq, ATTN_WIDTH), lambda i, j: (i, j, 0)),
        out_shape=jax.ShapeDtypeStruct((b, t, ATTN_WIDTH), F32),
        compiler_params=_cparams(("arbitrary", "arbitrary")),
        name="attn_prompt",
    )(sinks, q, kv, kv)


def _attn_sample_kernel(layer, tn, sink_ref, q_ref, kvn_ref, ck_ref, cv_ref, nk_in_ref, nv_in_ref,
                        o_ref, nk_ref, nv_ref):
    del nk_in_ref, nv_in_ref
    bb = q_ref.shape[0]
    rows = GQA_GROUP * tn
    kvn = kvn_ref[...]
    row = lax.broadcasted_iota(jnp.int32, (1, rows, 1), 1)
    tok = row % tn
    col = lax.broadcasted_iota(jnp.int32, (1, 1, WINDOW), 2)
    cmask = col >= tok
    for g in range(N_KV_HEADS):
        k_t = ck_ref[:, g]
        v_t = cv_ref[:, g]
        qg = q_ref[:, g] * (HEAD_DIM ** -0.5)
        qg_b = qg.astype(BF16).astype(F32)
        sc = jnp.einsum('bqd,bdw->bqw', qg.astype(BF16), k_t.astype(BF16), preferred_element_type=F32)
        sc = jnp.where(cmask, sc, -1e30)
        sink = jnp.zeros((1, rows, 1), F32)
        for hh in range(GQA_GROUP):
            sink = jnp.where(row // tn == hh, sink_ref[layer, g * GQA_GROUP + hh], sink)
        m = jnp.maximum(jnp.max(sc, axis=-1, keepdims=True), sink)
        sn = []
        for jn in range(tn):
            kn = kvn[:, g * tn + jn:g * tn + jn + 1, :].astype(BF16).astype(F32)
            s_j = jnp.sum(qg_b * kn, axis=-1, keepdims=True)
            s_j = jnp.where(tok >= jn, s_j, -1e30)
            sn.append(s_j)
            m = jnp.maximum(m, s_j)
        ec = jnp.exp(sc - m)
        den = jnp.sum(ec, axis=-1, keepdims=True) + jnp.exp(sink - m)
        en = [jnp.exp(s_j - m) for s_j in sn]
        for e_j in en:
            den = den + e_j
        inv = 1.0 / den
        o = jnp.einsum('bqw,bdw->bqd', (ec * inv).astype(BF16), v_t.astype(BF16), preferred_element_type=F32)
        for jn in range(tn):
            vn = kvn[:, (N_KV_HEADS + g) * tn + jn:(N_KV_HEADS + g) * tn + jn + 1, :].astype(BF16).astype(F32)
            o = o + (en[jn] * inv).astype(BF16).astype(F32) * vn
        o_ref[:, g] = o
    new_t = kvn.reshape(bb * 4 * tn, HEAD_DIM).T
    lane = lax.broadcasted_iota(jnp.int32, (HEAD_DIM, WINDOW), 1)
    for c_ref, n_ref, which in ((ck_ref, nk_ref, 0), (cv_ref, nv_ref, 1)):
        for g in range(N_KV_HEADS):
            for b in range(bb):
                src = (b * 2 * N_KV_HEADS + which * N_KV_HEADS + g) * tn
                fresh = pltpu.roll(new_t, (WINDOW - tn - src) % WINDOW, 1)
                kept = pltpu.roll(c_ref[b, g], WINDOW - tn, 1)
                n_ref[b, g] = jnp.where(lane >= WINDOW - tn, fresh, kept)


def _attn_sample(q, kv, ck, cv, nk_all, nv_all, sinks, tn, layer):
    b = q.shape[0] // tn
    bb = ATTN_SAMPLE_SEQS
    assert bb * 4 * tn == WINDOW
    rows = GQA_GROUP * tn
    qh = q.reshape(tn, b, N_KV_HEADS, GQA_GROUP, HEAD_DIM).transpose(1, 2, 3, 0, 4)
    qh = qh.reshape(b, N_KV_HEADS, rows, HEAD_DIM)
    kvn = kv.reshape(tn, b, 2 * N_KV_HEADS, HEAD_DIM).transpose(1, 2, 0, 3).reshape(b, 4 * tn, HEAD_DIM)
    cache = pl.BlockSpec((None, bb, N_KV_HEADS, HEAD_DIM, WINDOW), lambda i: (layer, i, 0, 0, 0))
    heads = pl.BlockSpec((bb, N_KV_HEADS, rows, HEAD_DIM), lambda i: (i, 0, 0, 0))
    o, nk_all, nv_all = pl.pallas_call(
        functools.partial(_attn_sample_kernel, layer, tn),
        grid=(b // bb,),
        in_specs=[
            pl.BlockSpec(memory_space=pltpu.SMEM),
            heads,
            pl.BlockSpec((bb, 4 * tn, HEAD_DIM), lambda i: (i, 0, 0)),
            cache, cache,
            pl.BlockSpec(memory_space=pl.ANY), pl.BlockSpec(memory_space=pl.ANY),
        ],
        out_specs=[heads, cache, cache],
        out_shape=[
            jax.ShapeDtypeStruct((b, N_KV_HEADS, rows, HEAD_DIM), F32),
            jax.ShapeDtypeStruct(nk_all.shape, F32),
            jax.ShapeDtypeStruct(nv_all.shape, F32),
        ],
        input_output_aliases={5: 1, 6: 2},
        compiler_params=_cparams(("arbitrary",)),
        name="attn_sample",
    )(sinks, qh, kvn, ck, cv, nk_all, nv_all)
    o = o.reshape(b, N_KV_HEADS, GQA_GROUP, tn, HEAD_DIM).transpose(3, 0, 1, 2, 4)
    return o.reshape(tn * b, ATTN_WIDTH), nk_all, nv_all


def _tile_rows(x, n):
    return jnp.concatenate([x] * n, axis=0)


def _rwkv_kernel(c, nseq, x_ref, sh0_ref, st0_ref, mu_ref, lora_ref, gup_ref, vec_ref, ones_ref,
                 stacked_ref, o_ref, st_ref, last_ref, state_ref):
    del stacked_ref
    hw = RWKV_WIDTH
    n = RWKV_HEAD
    cw = RWKV_HEADS * c

    @pl.when(pl.program_id(1) == 0)
    def _():
        last_ref[...] = sh0_ref[...]
        for s in range(nseq):
            for h in range(RWKV_HEADS):
                state_ref[s, :, h * n:(h + 1) * n] = st0_ref[s, h]

    row = lax.broadcasted_iota(jnp.int32, (c, 1), 0)
    lane128 = lax.broadcasted_iota(jnp.int32, (c, 128), 1)
    tri = jnp.where(lax.broadcasted_iota(jnp.int32, (c, c), 1) <= lax.broadcasted_iota(jnp.int32, (c, c), 0),
                    1.0, 0.0).astype(BF16)
    tri3 = jnp.concatenate([tri, tri, tri], axis=1)
    head_rows = (lax.broadcasted_iota(jnp.int32, (cw, hw), 0) // c
                 == lax.broadcasted_iota(jnp.int32, (cw, hw), 1) // n)
    t_i = lax.broadcasted_iota(jnp.int32, (c, cw), 0)
    s_i = lax.broadcasted_iota(jnp.int32, (c, cw), 1) % c
    strict = s_i < t_i
    incl = s_i <= t_i
    eye_all = jnp.where(s_i == t_i, 1.0, 0.0)
    blk = (lax.broadcasted_iota(jnp.int32, (cw, cw), 0) // c
           == lax.broadcasted_iota(jnp.int32, (cw, cw), 1) // c)
    sblk = (lax.broadcasted_iota(jnp.int32, (hw, hw), 0) // n
            == lax.broadcasted_iota(jnp.int32, (hw, hw), 1) // n)
    masks = (row, lane128, tri3, head_rows, strict, incl, eye_all, blk, sblk)
    nsub = x_ref.shape[1] // c
    state_owner = [0] * nseq
    shared = {"lora": lora_ref, "gup": gup_ref, "sums": ones_ref}
    chains = {sub: [_rwkv_chunk(c, s, sub, nsub, state_owner, masks, x_ref, mu_ref, vec_ref,
                                o_ref, st_ref, last_ref, state_ref) for s in range(nseq)]
              for sub in range(nsub)}
    answers = {sub: [None] * nseq for sub in range(nsub)}
    rnd = 0
    while chains:
        for sub in sorted(chains):
            if rnd < sub * RWKV_STAGGER:
                continue
            asked = [_advance(ch, ans) for ch, ans in zip(chains[sub], answers[sub])]
            if asked[0] == "done":
                del chains[sub]
                continue
            results = [[] for _ in range(nseq)]
            for i, (kind, _) in enumerate(asked[0] or ()):
                stacked = _bdot(jnp.concatenate([req[i][1] for req in asked], axis=0), shared[kind][...])
                for s in range(nseq):
                    results[s].append(stacked[s * c:(s + 1) * c])
            answers[sub] = [tuple(res) for res in results]
        rnd += 1


def _advance(chain, answer):
    try:
        return next(chain) if answer is None else chain.send(answer)
    except StopIteration:
        return "done"


def _rwkv_chunk(c, s, sub, nsub, state_owner, masks, x_ref, mu_ref, vec_ref, o_ref, st_ref, last_ref, state_ref):
    row, lane128, tri3, head_rows, strict, incl, eye_all, blk, sblk = masks
    hw = RWKV_WIDTH
    n = RWKV_HEAD
    cw = RWKV_HEADS * c
    x = x_ref[s, sub * c:(sub + 1) * c, :]
    before = last_ref[s] if sub == 0 else x_ref[s, sub * c - 1:sub * c, :]
    prev = jnp.where(row == 0, before, pltpu.roll(x, 1, 0))
    xs = x + (prev - x) * mu_ref[...]

    r = xs[:, 0:hw]
    k = xs[:, hw:2 * hw]
    v = xs[:, 2 * hw:3 * hw]
    wa = xs[:, 3 * hw:3 * hw + 128]
    xg = xs[:, 3 * hw + 128:]
    lora_in = jnp.where(lane128 < 64, jnp.tanh(wa), wa)
    w0, a0, k_k, k_a, r_k, ln_w, ln_b = (vec_ref[i:i + 1, :] for i in range(7))
    kk = k * k_k
    lora, g, ss = yield (("lora", lora_in), ("gup", _sigmoid(xg)), ("sums", kk * kk))
    w_log = -_softplus(-(w0 + lora[:, 0:hw])) - 0.5
    logw = -jnp.exp(w_log)
    a = _sigmoid(a0 + lora[:, hw:2 * hw])
    kk = kk * lax.rsqrt(jnp.maximum(ss, 1e-24))
    k2 = k * (1.0 + (a - 1.0) * k_a)
    bv = kk * a

    pieces = []
    rem = logw
    for _ in range(3):
        p = rem.astype(BF16)
        rem = rem - p.astype(F32)
        pieces.append(p)
    cum = jnp.dot(tri3, jnp.concatenate(pieces, axis=0), preferred_element_type=F32)
    (bonus_sum,) = yield (("sums", r * k2 * r_k),)
    e_inc = jnp.exp(cum)
    e_exc = jnp.exp(cum - logw)
    e_inv = jnp.exp(-cum)
    kq = kk * e_exc
    rq = r * e_inc
    kd = k2 * e_inv
    bd = bv * e_inv
    w_end = e_inc[c - 1:c, :]
    kend = kd * w_end
    bend = bd * w_end

    def expand(z):
        return jnp.where(head_rows, _tile_rows(z, RWKV_HEADS), 0.0)

    lhs = jnp.concatenate([kq, rq], axis=0).astype(BF16)
    rhs = jnp.concatenate([expand(bd), expand(kd)], axis=0).astype(BF16)
    nt = (((1,), (1,)), ((), ()))
    prod = lax.dot_general(lhs, rhs, nt, preferred_element_type=F32)
    yield
    a_b = jnp.where(strict, prod[0:c, 0:cw], 0.0)
    a_k = jnp.where(strict, prod[0:c, cw:2 * cw], 0.0)
    p_b = jnp.where(incl, prod[c:2 * c, 0:cw], 0.0)
    p_k = jnp.where(incl, prod[c:2 * c, cw:2 * cw], 0.0)

    def bdiag(z):
        return jnp.where(blk, _tile_rows(z, RWKV_HEADS), 0.0)

    xm = -a_b
    tinv = eye_all + xm
    v_d = expand(v)
    from_v = _bdot(jnp.concatenate([a_k, p_k], axis=0), v_d)
    levels = c.bit_length() - 1
    xm = _bdot(xm, bdiag(xm))
    yield
    for _ in range(1, levels - 1):
        both = _bdot(jnp.concatenate([xm, tinv], axis=0), bdiag(xm))
        yield
        xm = both[0:c]
        tinv = tinv + both[c:2 * c]
    tinv = tinv + _bdot(tinv, bdiag(xm))
    yield
    assert state_owner[s] == sub
    st = state_ref[s]
    st_d = jnp.where(sblk, _tile_rows(st, RWKV_HEADS), 0.0)
    from_state = lax.dot_general(lhs, st_d.astype(BF16), nt, preferred_element_type=F32)
    yield
    u = _bdot(tinv, expand(from_state[0:c] + from_v[0:c]))
    yield
    y = from_state[c:2 * c] + from_v[c:2 * c] - _bdot(p_b, expand(u))

    lhs_s = jnp.concatenate([v, u], axis=0)
    rhs_s = jnp.concatenate([kend, -bend], axis=0)
    tn_dims = (((0,), (0,)), ((), ()))
    gm = lax.dot_general(lhs_s.astype(BF16), rhs_s.astype(BF16), tn_dims, preferred_element_type=F32)
    yield
    gm = jnp.where(sblk, gm, 0.0)
    st_new = st * w_end + (gm[0:n] + gm[n:2 * n] + (gm[2 * n:3 * n] + gm[3 * n:4 * n]))
    state_ref[s] = st_new
    state_owner[s] = sub + 1
    if sub == nsub - 1:
        state_owner[s] = 0
        last_ref[s] = x[c - 1:c, :]
        for h in range(RWKV_HEADS):
            st_ref[s, h] = st_new[:, h * n:(h + 1) * n]

    (y_sum,) = yield (("sums", y),)
    yc = y - y_sum * (1.0 / n)
    (sq_sum,) = yield (("sums", yc * yc),)
    yn = yc * lax.rsqrt(sq_sum * (1.0 / n) + GN_EPS) * ln_w + ln_b
    o_ref[s, sub * c:(sub + 1) * c, :] = (yn + bonus_sum * v) * g


def _rwkv(x, sh0, st0, mu, lora_w, g_up, vecs, ones_blk, stacked, layer):
    b, t, _ = x.shape
    c = RWKV_CHUNK
    tstep = RWKV_SUBCHUNKS * c
    nseq = min(RWKV_SEQS, b)
    wspec = lambda s: _layer_spec(s, layer, 2)
    hstate = (nseq, RWKV_HEADS, RWKV_HEAD, RWKV_HEAD)
    return pl.pallas_call(
        functools.partial(_rwkv_kernel, c, nseq),
        grid=(b // nseq, t // tstep),
        in_specs=[
            pl.BlockSpec((nseq, tstep, RWKV_PROJ), lambda i, j: (i, j, 0)),
            pl.BlockSpec((nseq, 1, RWKV_PROJ), lambda i, j: (i, 0, 0)),
            pl.BlockSpec(hstate, lambda i, j: (i, 0, 0, 0)),
            wspec((1, RWKV_PROJ)), wspec((128, 512)), wspec((128, RWKV_WIDTH)), wspec((8, RWKV_WIDTH)),
            pl.BlockSpec((RWKV_WIDTH, RWKV_WIDTH), lambda i, j: (0, 0)),
            pl.BlockSpec(memory_space=pl.ANY),
        ],
        out_specs=[
            pl.BlockSpec((nseq, tstep, RWKV_WIDTH), lambda i, j: (i, j, 0)),
            pl.BlockSpec((None,) + hstate, lambda i, j: (layer, i, 0, 0, 0)),
        ],
        out_shape=[
            jax.ShapeDtypeStruct((b, t, RWKV_WIDTH), F32),
            jax.ShapeDtypeStruct(stacked.shape, F32),
        ],
        scratch_shapes=[pltpu.VMEM((nseq, 1, RWKV_PROJ), F32), pltpu.VMEM((nseq, RWKV_HEAD, RWKV_WIDTH), F32)],
        input_output_aliases={8: 1},
        compiler_params=_cparams(("arbitrary", "arbitrary")),
        name="rwkv",
    )(x, sh0, st0, mu, lora_w, g_up, vecs, ones_blk, stacked)


def _rwkv_step_kernel(tn, nb, x_ref, sh0_ref, st0_ref, mu_ref, lora_ref, gup_ref, vec_ref, ones_ref,
                      stacked_ref, o_ref, st_ref, vt_ref, nat_ref, yt_ref):
    del stacked_ref
    hw = RWKV_WIDTH
    n = RWKV_HEAD
    h = pl.program_id(0)
    rows = tn * nb

    @pl.when(h == 0)
    def _():
        x = x_ref[...]
        prev = jnp.concatenate([sh0_ref[...], x[0:rows - nb]], axis=0)
        xs = x + (prev - x) * mu_ref[...]
        r = xs[:, 0:hw]
        k = xs[:, hw:2 * hw]
        v = xs[:, 2 * hw:3 * hw]
        wa = xs[:, 3 * hw:3 * hw + 128]
        xg = xs[:, 3 * hw + 128:]
        lane128 = lax.broadcasted_iota(jnp.int32, (rows, 128), 1)
        lora = _bdot(jnp.where(lane128 < 64, jnp.tanh(wa), wa), lora_ref[...])
        w0, a0, k_k, k_a, r_k = (vec_ref[i:i + 1, :] for i in range(5))
        ones_blk = ones_ref[...]
        w_log = -_softplus(-(w0 + lora[:, 0:hw])) - 0.5
        decay = jnp.exp(-jnp.exp(w_log))
        a = _sigmoid(a0 + lora[:, hw:2 * hw])
        kk = k * k_k
        kk = kk * lax.rsqrt(jnp.maximum(_block_sums(kk * kk, ones_blk), 1e-24))
        k2 = k * (1.0 + (a - 1.0) * k_a)
        nat_ref[0] = _bdot(_sigmoid(xg), gup_ref[...])
        nat_ref[1] = v
        nat_ref[2] = _block_sums(r * k2 * r_k, ones_blk)
        for i, z in enumerate((r, decay, k2, v, kk, kk * a)):
            for t in range(tn):
                for half in range(hw // 128):
                    vt_ref[i, t, half * 128:(half + 1) * 128, :] = (
                        z[t * nb:(t + 1) * nb, half * 128:(half + 1) * 128].T)

    base = pl.multiple_of(h * n, n)

    def value_row(vi, carry):
        s = st0_ref[vi]
        for t in range(tn):
            r_t, w_t, k_t, _, kk_t, b_t = (vt_ref[i, t, pl.ds(base, n), :] for i in range(6))
            v_row = vt_ref[3, t, pl.ds(base + vi, 1), :]
            u = jnp.sum(s * kk_t, axis=0, keepdims=True)
            s = s * w_t - u * b_t + v_row * k_t
            yt_ref[t, pl.ds(base + vi, 1), :] = jnp.sum(s * r_t, axis=0, keepdims=True)
        st_ref[vi] = s
        return carry

    lax.fori_loop(0, n, value_row, 0, unroll=4)

    @pl.when(h == RWKV_HEADS - 1)
    def _():
        ones_blk = ones_ref[...]
        ln_w = vec_ref[5:6, :]
        ln_b = vec_ref[6:7, :]
        y = jnp.concatenate(
            [jnp.concatenate([yt_ref[t, half * 128:(half + 1) * 128, :].T for half in range(hw // 128)], axis=1)
             for t in range(tn)], axis=0)
        mean = _block_sums(y, ones_blk) * (1.0 / n)
        yc = y - mean
        var = _block_sums(yc * yc, ones_blk) * (1.0 / n)
        yn = yc * lax.rsqrt(var + GN_EPS) * ln_w + ln_b
        o_ref[...] = (yn + nat_ref[2] * nat_ref[1]) * nat_ref[0]


def _rwkv_sample(x, sh0, st0, mu, lora_w, g_up, vecs, ones_blk, stacked, tn, layer):
    rows = x.shape[0]
    nb = rows // tn
    wspec = lambda s: _layer_spec(s, layer, 1)
    head_state = pl.BlockSpec((None, None, RWKV_HEAD, RWKV_HEAD, nb), lambda h: (layer, h, 0, 0, 0))
    return pl.pallas_call(
        functools.partial(_rwkv_step_kernel, tn, nb),
        grid=(RWKV_HEADS,),
        in_specs=[
            pl.BlockSpec((rows, RWKV_PROJ), lambda h: (0, 0)),
            wspec((nb, RWKV_PROJ)), head_state,
            wspec((1, RWKV_PROJ)), wspec((128, 512)), wspec((128, RWKV_WIDTH)), wspec((8, RWKV_WIDTH)),
            pl.BlockSpec((RWKV_WIDTH, RWKV_WIDTH), lambda h: (0, 0)),
            pl.BlockSpec(memory_space=pl.ANY),
        ],
        out_specs=[pl.BlockSpec((rows, RWKV_WIDTH), lambda h: (0, 0)), head_state],
        out_shape=[jax.ShapeDtypeStruct((rows, RWKV_WIDTH), F32), jax.ShapeDtypeStruct(stacked.shape, F32)],
        scratch_shapes=[pltpu.VMEM((6, tn, RWKV_WIDTH, nb), F32), pltpu.VMEM((3, rows, RWKV_WIDTH), F32),
                        pltpu.VMEM((tn, RWKV_WIDTH, nb), F32)],
        input_output_aliases={8: 1},
        compiler_params=_cparams(("arbitrary",)),
        name="rwkv_step",
    )(x, sh0, st0, mu, lora_w, g_up, vecs, ones_blk, stacked)


def _lru_gates(xc, wg_ref, vec_ref):
    w = LRU_WIDTH
    _, b_a, b_i, lam = (vec_ref[i:i + 1, :] for i in range(4))
    gates = _bdot(xc, wg_ref[...])
    r = _sigmoid(gates[:, 0:w] + b_a)
    i = _sigmoid(gates[:, w:2 * w] + b_i)
    log_a = LRU_C * r * (-_softplus(-lam))
    a = jnp.exp(log_a)
    th = jnp.tanh(log_a)
    u = jnp.sqrt(-2.0 * th / (1.0 - th)) * (i * xc)
    return a, u


def _lru_tile_steps(x_ref, carry, ext_ref, o_ref, cw_ref, wg_ref, vec_ref):
    tc = x_ref.shape[0]
    w = LRU_WIDTH
    blk = LRU_SCAN_BLOCK
    gr = tc // LRU_PIECES
    ext_ref[8:8 + tc, :] = x_ref[:, 0:w]
    xc = vec_ref[0:1, :] + ext_ref[8:8 + tc, :] * cw_ref[CONV_W - 1:CONV_W, :]
    for j in range(CONV_W - 1):
        xc = xc + ext_ref[pl.ds(8 - (CONV_W - 1) + j, tc), :] * cw_ref[j:j + 1, :]
    ext_ref[0:8, :] = ext_ref[tc:tc + 8, :]
    gates = _bdot(xc, wg_ref[...])
    yield
    _, b_a, b_i, lam = (vec_ref[i:i + 1, :] for i in range(4))
    log_lam = -_softplus(-lam)
    pos = lax.broadcasted_iota(jnp.int32, (gr, 1), 0) % blk
    for g in range(LRU_PIECES):
        rows = slice(g * gr, (g + 1) * gr)
        xc_g = xc[rows]
        r = _sigmoid(gates[rows, 0:w] + b_a)
        i = _sigmoid(gates[rows, w:2 * w] + b_i)
        log_a = LRU_C * r * log_lam
        a = jnp.exp(log_a)
        th = jnp.tanh(log_a)
        u = jnp.sqrt(-2.0 * th / (1.0 - th)) * (i * xc_g)
        span = 1
        while span < 8:
            ok = pos >= span
            a_s = pltpu.roll(a, span, 0)
            u_s = pltpu.roll(u, span, 0)
            u = jnp.where(ok, a * u_s + u, u)
            a = jnp.where(ok, a * a_s, a)
            span *= 2
        a = a.reshape(gr // blk, blk, w)
        u = u.reshape(gr // blk, blk, w)
        while span < blk:
            u = jnp.concatenate([u[:, :span], a[:, span:] * u[:, :blk - span] + u[:, span:]], axis=1)
            a = jnp.concatenate([a[:, :span], a[:, span:] * a[:, :blk - span]], axis=1)
            span *= 2
        gelu = _gelu_tanh(x_ref[rows, w:2 * w])
        for b in range(gr // blk):
            h = a[b] * carry + u[b]
            carry = h[blk - 1:blk, :]
            o_ref[g * gr + b * blk:g * gr + (b + 1) * blk, :] = h * gelu[b * blk:(b + 1) * blk]
        yield
    return carry


def _step(steps, result):
    if not result:
        try:
            next(steps)
        except StopIteration as done:
            result.append(done.value)


def _run_to_end(steps, result=None):
    result = [] if result is None else result
    while not result:
        _step(steps, result)
    return result[0]


def _lru_step_kernel(tn, nb, x_ref, cv0_ref, h0_ref, cw_ref, wg_ref, vec_ref, o_ref, h_ref):
    w = LRU_WIDTH
    rows = tn * nb
    xb = x_ref[:, 0:w]
    gb = x_ref[:, w:2 * w]
    ext = jnp.concatenate([cv0_ref[j] for j in range(CONV_W - 1)] + [xb], axis=0)
    xc = vec_ref[0:1, :]
    for j in range(CONV_W):
        xc = xc + ext[j * nb:j * nb + rows] * cw_ref[j:j + 1, :]
    a, u = _lru_gates(xc, wg_ref, vec_ref)
    gelu = _gelu_tanh(gb)
    h = h0_ref[...]
    for t in range(tn):
        h = a[t * nb:(t + 1) * nb] * h + u[t * nb:(t + 1) * nb]
        o_ref[t * nb:(t + 1) * nb, :] = h * gelu[t * nb:(t + 1) * nb]
    h_ref[...] = h


def _lru_sample(x, cv0, h0, conv_w, w_gates, vecs, tn, layer):
    rows = x.shape[0]
    nb = rows // tn
    wspec = lambda s: _layer_spec(s, layer, 1)
    return pl.pallas_call(
        functools.partial(_lru_step_kernel, tn, nb),
        grid=(1,),
        in_specs=[pl.BlockSpec((rows, 2 * LRU_WIDTH), lambda i: (0, 0)),
                  wspec((CONV_W - 1, nb, LRU_WIDTH)), wspec((nb, LRU_WIDTH)),
                  wspec((CONV_W, LRU_WIDTH)), wspec((LRU_WIDTH, 2 * LRU_WIDTH)), wspec((4, LRU_WIDTH))],
        out_specs=[pl.BlockSpec((rows, LRU_WIDTH), lambda i: (0, 0)), pl.BlockSpec((nb, LRU_WIDTH), lambda i: (0, 0))],
        out_shape=[jax.ShapeDtypeStruct((rows, LRU_WIDTH), F32), jax.ShapeDtypeStruct((nb, LRU_WIDTH), F32)],
        compiler_params=_cparams(("arbitrary",)),
        name="lru_step",
    )(x, cv0, h0, conv_w, w_gates, vecs)


def _post_kernel(lru_tiles, ncast, *refs):
    casts = []
    if lru_tiles:
        (x_ref, oa_ref, ob_ref, lr_next_ref, lr_first_ref, cv0_ref, h0_ref, p_ref, wo_ref, wg_ref, wu_ref, wd_ref,
         pg_ref, pw_ref, vec_ref, lcw_ref, lwg_ref, lvec_ref) = refs[:18]
        y_ref, h_ref = refs[18 + ncast:20 + ncast]
        casts = list(zip(refs[18:18 + ncast], refs[20 + ncast:20 + 2 * ncast]))
        oc_ref, ext_ref, hc_ref = refs[20 + 2 * ncast:]
        i = pl.program_id(0)

        @pl.when(i == 0)
        def _():
            ext_ref[0:8, :] = cv0_ref[...]
            hc_ref[...] = _run_to_end(_lru_tile_steps(lr_first_ref, h0_ref[...], ext_ref, oc_ref,
                                                      lcw_ref, lwg_ref, lvec_ref))
    else:
        (x_ref, oa_ref, ob_ref, oc_ref, p_ref, wo_ref, wg_ref, wu_ref, wd_ref, pg_ref, pw_ref,
         vec_ref, y_ref) = refs
    mix = (jnp.dot(oa_ref[...].astype(BF16), wo_ref[0:512, :], preferred_element_type=F32)
           + jnp.dot(ob_ref[...].astype(BF16), wo_ref[512:768, :], preferred_element_type=F32)
           + jnp.dot(oc_ref[...].astype(BF16), wo_ref[768:1024, :], preferred_element_type=F32)
           + vec_ref[0:1, :])
    mixer_ahead, mixer_state = None, []
    if lru_tiles:
        opens = (i + 1) % lru_tiles == 0
        ext_ref[0:8, :] = jnp.where(opens, cv0_ref[...], ext_ref[0:8, :])
        carry_in = jnp.where(opens, h0_ref[...], hc_ref[...])
        mixer_ahead = _lru_tile_steps(lr_next_ref, carry_in, ext_ref, oc_ref, lcw_ref, lwg_ref, lvec_ref)

    x = x_ref[...] + _rmsnorm(mix, vec_ref[1:2, :])
    f = _rmsnorm(x, vec_ref[2:3, :]).astype(BF16)
    acc = None
    for lo in range(0, D_FF, FF_CHUNK):
        gate = jnp.dot(f, wg_ref[:, lo:lo + FF_CHUNK], preferred_element_type=F32)
        up = jnp.dot(f, wu_ref[:, lo:lo + FF_CHUNK], preferred_element_type=F32)
        hid = (gate * _sigmoid(gate) * up).astype(BF16)
        part = jnp.dot(hid, wd_ref[lo:lo + FF_CHUNK, :], preferred_element_type=F32)
        acc = part if acc is None else acc + part
        if mixer_ahead is not None:
            for _ in range(-(-(LRU_PIECES + 1) // (D_FF // FF_CHUNK))):
                _step(mixer_ahead, mixer_state)
        if casts:
            src, dst = casts.pop()
            dst[...] = src[...].astype(BF16)
    assert not casts
    if mixer_ahead is not None:
        carry = _run_to_end(mixer_ahead, mixer_state)
        carry = jnp.where(i + 1 < pl.num_programs(0), carry, hc_ref[...])
        hc_ref[...] = carry
        h_ref[...] = carry
    x2 = x + _rmsnorm(acc, vec_ref[3:4, :])
    gate = _sigmoid(jnp.dot(x2.astype(BF16), pg_ref[...], preferred_element_type=F32))
    emb = jnp.dot(p_ref[...].astype(BF16), pw_ref[...], preferred_element_type=F32)
    y_ref[...] = x2 + gate * emb


def _post_weight_specs(layer):
    own = lambda s: _layer_spec(s, 0, 1, pipeline_mode=pl.Buffered(1))
    stacked = lambda s: _layer_spec(s, layer, 1, pipeline_mode=pl.Buffered(1))
    return [own((D_MODEL, D_MODEL)), own((D_MODEL, D_FF)), own((D_MODEL, D_FF)), own((D_FF, D_MODEL)),
            own((D_MODEL, D_MODEL)), stacked((PLE_DIM, D_MODEL)), stacked((4, D_MODEL))]


def _post(x, oa, ob, oc, p, wo, wg, wu, wd, pg, pw, vecs, layer):
    m = x.shape[0]
    tm = min(ROW_TILE, m)
    row = lambda w_: pl.BlockSpec((tm, w_), lambda i: (i, 0))
    return pl.pallas_call(
        functools.partial(_post_kernel, 0, 0),
        grid=(m // tm,),
        in_specs=[row(D_MODEL), row(512), row(256), row(256),
                  pl.BlockSpec((None, tm, PLE_DIM), lambda i: (layer, i, 0))] + _post_weight_specs(layer),
        out_specs=row(D_MODEL),
        out_shape=jax.ShapeDtypeStruct((m, D_MODEL), F32),
        compiler_params=_cparams(("arbitrary",)),
        name="post",
    )(x, oa, ob, oc, p, wo, wg, wu, wd, pg, pw, vecs)


def _post_with_lru(x, oa, ob, lr, cv0, h0, t, p, wo, wg, wu, wd, pg, pw, vecs, conv_w, w_gates, lru_vecs, layer,
                   cast_next):
    m = x.shape[0]
    tm = ROW_TILE
    tiles = t // tm
    nsteps = m // tm
    row = lambda w_: pl.BlockSpec((tm, w_), lambda i: (i, 0))
    lspec = lambda s: _layer_spec(s, layer, 1)
    ahead = lambda i: jnp.minimum(i + 1, nsteps - 1)
    seq_state = lambda r: pl.BlockSpec((None, r, LRU_WIDTH), lambda i: (ahead(i) // tiles, 0, 0))
    cast_in, cast_out, cast_shapes = [], [], []
    for wgt in cast_next:
        _, rows, cols = wgt.shape
        slices = max(s for s in range(1, nsteps + 1)
                     if nsteps % s == 0 and rows % s == 0 and (rows // s) % 16 == 0)
        blk = rows // slices
        rep = nsteps // slices
        cast_in.append(pl.BlockSpec((None, blk, cols), lambda i, rep=rep: (layer + 1, i // rep, 0)))
        cast_out.append(pl.BlockSpec((None, blk, cols), lambda i, rep=rep: (0, i // rep, 0)))
        cast_shapes.append(jax.ShapeDtypeStruct((1, rows, cols), BF16))
    return pl.pallas_call(
        functools.partial(_post_kernel, tiles, len(cast_next)),
        grid=(nsteps,),
        in_specs=[row(D_MODEL), row(512), row(256),
                  pl.BlockSpec((tm, 2 * LRU_WIDTH), lambda i: (ahead(i), 0)),
                  pl.BlockSpec((tm, 2 * LRU_WIDTH), lambda i: (0, 0)),
                  seq_state(8), seq_state(1),
                  pl.BlockSpec((None, tm, PLE_DIM), lambda i: (layer, i, 0))] + _post_weight_specs(layer) + [
                  lspec((CONV_W, LRU_WIDTH)), lspec((LRU_WIDTH, 2 * LRU_WIDTH)), lspec((4, LRU_WIDTH))] + cast_in,
        out_specs=[row(D_MODEL), seq_state(1)] + cast_out,
        out_shape=[jax.ShapeDtypeStruct((m, D_MODEL), F32),
                   jax.ShapeDtypeStruct((m // t, 1, LRU_WIDTH), F32)] + cast_shapes,
        scratch_shapes=[pltpu.VMEM((tm, LRU_WIDTH), F32), pltpu.VMEM((tm + 8, LRU_WIDTH), F32),
                        pltpu.VMEM((1, LRU_WIDTH), F32)],
        compiler_params=_cparams(("arbitrary",)),
        name="post_lru",
    )(x, oa, ob, lr, lr, cv0, h0, p, wo, wg, wu, wd, pg, pw, vecs, conv_w, w_gates, lru_vecs, *cast_next)


def _block_diag(w):
    nl, nb, n, _ = w.shape
    eye = jnp.eye(nb, dtype=w.dtype)
    return (eye[None, :, None, :, None] * w[:, :, :, None, :]).reshape(nl, nb * n, nb * n)


BIG_WEIGHTS = ('w_in', 'w_out', 'ffn_w_gate', 'ffn_w_up', 'ffn_w_down', 'ple_gate_w')
CAST_STEPS = 8


def _cast_kernel(*refs):
    half = len(refs) // 2
    for src, dst in zip(refs[:half], refs[half:]):
        dst[...] = src[...].astype(BF16)


def _cast_first_layer(weights):
    in_specs, out_specs, shapes = [], [], []
    for wgt in weights:
        _, rows, cols = wgt.shape
        blk = rows // CAST_STEPS
        assert rows % CAST_STEPS == 0 and blk % 16 == 0
        spec = pl.BlockSpec((None, blk, cols), lambda i: (0, i, 0))
        in_specs.append(spec)
        out_specs.append(spec)
        shapes.append(jax.ShapeDtypeStruct((1, rows, cols), BF16))
    return pl.pallas_call(
        _cast_kernel,
        grid=(CAST_STEPS,),
        in_specs=in_specs,
        out_specs=out_specs,
        out_shape=shapes,
        compiler_params=_cparams(("arbitrary",)),
        name="cast_weights",
    )(*weights)


def _post_weights(wts, big):
    return (big['w_out'], big['ffn_w_gate'], big['ffn_w_up'], big['ffn_w_down'], big['ple_gate_w'], wts['ple_w'],
            wts['post_vecs'])


def _layer_prompt(x2, b, t, p, zeros, wkv_all, wts, big, raw_big, layer):
    m = b * t
    q, kv, rw, lr = _in_proj(x2, wts['norm_mix_pre'], big['w_in'], wts['b_in'], layer)
    kv = kv.reshape(b, t, 256)
    rw = rw.reshape(b, t, RWKV_PROJ)
    sh0, st0, cv0, h0 = zeros
    o_a = _attn_prompt(q.reshape(b, t, 512), kv, wts['attn_sinks'], layer)
    nk = kv[:, t - WINDOW:, 0:KV_WIDTH].reshape(b, WINDOW, N_KV_HEADS, HEAD_DIM)
    nv = kv[:, t - WINDOW:, KV_WIDTH:].reshape(b, WINDOW, N_KV_HEADS, HEAD_DIM)
    o_b, wkv_all = _rwkv(rw, sh0, st0, wts['rwkv_mu'], wts['rwkv_lora'], wts['rwkv_g_up'],
                         wts['rwkv_vecs'], wts['ones_blk'], wkv_all, layer)
    nsh = rw[:, t - 1, :]
    nconv = lr.reshape(b, t, 2 * LRU_WIDTH)[:, t - (CONV_W - 1):, 0:LRU_WIDTH]
    cast_next = [raw_big[name] for name in BIG_WEIGHTS] if layer + 1 < DEPTH else []
    x2, nh, *cast = _post_with_lru(x2, o_a.reshape(m, 512), o_b.reshape(m, 256), lr, cv0, h0, t, p,
                                   *_post_weights(wts, big), wts['lru_conv_w'], wts['lru_w_gates'],
                                   wts['lru_vecs'], layer, cast_next)
    next_big = dict(zip(BIG_WEIGHTS, cast))
    return x2, (nk, nv, nsh, nconv, nh.reshape(b, LRU_WIDTH)), wkv_all, next_big


def _layer_sample(x2, b, t, p, state, outs, wts, big, layer):
    ck, cv, sh0, st0, cv0, h0 = state
    nk_all, nv_all, wkv_all = outs
    q, kv, rw, lr = _in_proj(x2, wts['norm_mix_pre'], big['w_in'], wts['b_in'], layer)
    o_a, nk_all, nv_all = _attn_sample(q, kv, ck, cv, nk_all, nv_all, wts['attn_sinks'], t, layer)
    o_b, wkv_all = _rwkv_sample(rw, sh0, st0, wts['rwkv_mu'], wts['rwkv_lora'], wts['rwkv_g_up'],
                                wts['rwkv_vecs'], wts['ones_blk'], wkv_all, t, layer)
    nsh = rw[(t - 1) * b:, :]
    o_c, nh = _lru_sample(lr, cv0, h0, wts['lru_conv_w'], wts['lru_w_gates'], wts['lru_vecs'], t, layer)
    nconv = lr[(t - (CONV_W - 1)) * b:, 0:LRU_WIDTH].reshape(CONV_W - 1, b, LRU_WIDTH)
    x2 = _post(x2, o_a, o_b, o_c, p, *_post_weights(wts, big), layer)
    return x2, (nsh, nconv, nh), (nk_all, nv_all, wkv_all)


def kernel(x_prompt, x_sample, cache_k, cache_v, state_shift, state_wkv, state_conv, state_lru,
           p_prompt, p_sample, norm_mix_pre, norm_mix_post, norm_ffn_pre, norm_ffn_post,
           w_in, b_in, attn_sinks, rwkv_mu, rwkv_w0, rwkv_w_up, rwkv_a0, rwkv_a_up, rwkv_g_up,
           rwkv_k_k, rwkv_k_a, rwkv_r_k, rwkv_ln_w, rwkv_ln_b, lru_conv_w, lru_conv_b,
           lru_w_a, lru_b_a, lru_w_i, lru_b_i, lru_L, w_out, b_out, ffn_w_gate, ffn_w_up,
           ffn_w_down, ple_w, ple_gate_w):
    nl = DEPTH
    bp, tp_, _ = x_prompt.shape
    bs, ts, _ = x_sample.shape
    head_id = jnp.arange(RWKV_WIDTH) // RWKV_HEAD
    zeros_w = jnp.zeros((nl, 64, RWKV_WIDTH), F32)
    wts = dict(
        norm_mix_pre=norm_mix_pre[:, None, :], b_in=b_in[:, None, :],
        attn_sinks=attn_sinks,
        rwkv_mu=rwkv_mu[:, None, :],
        rwkv_lora=jnp.concatenate([jnp.concatenate([rwkv_w_up, zeros_w], axis=2),
                                   jnp.concatenate([zeros_w, rwkv_a_up], axis=2)], axis=1).astype(BF16),
        rwkv_g_up=rwkv_g_up.astype(BF16),
        rwkv_vecs=jnp.stack([rwkv_w0, rwkv_a0, rwkv_k_k, rwkv_k_a, rwkv_r_k.reshape(nl, RWKV_WIDTH),
                             rwkv_ln_w, rwkv_ln_b, jnp.zeros((nl, RWKV_WIDTH), F32)], axis=1),
        ones_blk=(head_id[:, None] == head_id[None, :]).astype(BF16),
        lru_conv_w=lru_conv_w,
        lru_w_gates=jnp.concatenate([_block_diag(lru_w_a), _block_diag(lru_w_i)], axis=2).astype(BF16),
        lru_vecs=jnp.stack([lru_conv_b, lru_b_a, lru_b_i, lru_L], axis=1),
        ple_w=ple_w.astype(BF16),
        post_vecs=jnp.stack([b_out, norm_mix_post, norm_ffn_pre, norm_ffn_post], axis=1),
    )
    raw_big = dict(w_in=w_in, w_out=w_out, ffn_w_gate=ffn_w_gate, ffn_w_up=ffn_w_up, ffn_w_down=ffn_w_down,
                   ple_gate_w=ple_gate_w)
    big = dict(zip(BIG_WEIGHTS, _cast_first_layer([raw_big[name] for name in BIG_WEIGHTS])))
    zeros_p = (jnp.zeros((bp, 1, RWKV_PROJ), F32),
               jnp.zeros((bp, RWKV_HEADS, RWKV_HEAD, RWKV_HEAD), F32),
               jnp.zeros((bp, 8, LRU_WIDTH), F32),
               jnp.zeros((bp, 1, LRU_WIDTH), F32))
    st_s = (cache_k.transpose(0, 1, 3, 4, 2), cache_v.transpose(0, 1, 3, 4, 2),
            state_shift,
            state_wkv.transpose(0, 2, 3, 4, 1),
            state_conv.transpose(0, 2, 1, 3),
            state_lru)
    pp = p_prompt.reshape(nl, bp * tp_, PLE_DIM)
    ps = p_sample.transpose(0, 2, 1, 3).reshape(nl, ts * bs, PLE_DIM)

    xp = x_prompt.reshape(bp * tp_, D_MODEL)
    xs = x_sample.transpose(1, 0, 2).reshape(ts * bs, D_MODEL)
    wkv_p = jnp.zeros((nl, bp, RWKV_HEADS, RWKV_HEAD, RWKV_HEAD), F32)
    outs_s = (jnp.zeros((nl, bs, N_KV_HEADS, HEAD_DIM, WINDOW), F32),
              jnp.zeros((nl, bs, N_KV_HEADS, HEAD_DIM, WINDOW), F32),
              jnp.zeros((nl, RWKV_HEADS, RWKV_HEAD, RWKV_HEAD, bs), F32))
    new_p, new_s = [], []
    for i in range(nl):
        xp, sp, wkv_p, next_big = _layer_prompt(xp, bp, tp_, pp, zeros_p, wkv_p, wts, big, raw_big, i)
        xs, ss, outs_s = _layer_sample(xs, bs, ts, ps, st_s, outs_s, wts, big, i)
        big = next_big
        new_p.append(sp)
        new_s.append(ss)

    def stk(lst, j):
        return jnp.stack([s[j] for s in lst], axis=0)

    nk_s, nv_s, wkv_s = outs_s
    return (xp.reshape(bp, tp_, D_MODEL), xs.reshape(ts, bs, D_MODEL).transpose(1, 0, 2),
            stk(new_p, 0), stk(new_p, 1), stk(new_p, 2), wkv_p, stk(new_p, 3), stk(new_p, 4),
            nk_s.transpose(0, 1, 4, 2, 3), nv_s.transpose(0, 1, 4, 2, 3), stk(new_s, 0),
            wkv_s.transpose(0, 4, 1, 2, 3), stk(new_s, 1).transpose(0, 2, 1, 3), stk(new_s, 2))
```

```python
import functools

import jax
import jax.numpy as jnp
from jax import lax
from jax.experimental import pallas as pl
from jax.experimental.pallas import tpu as pltpu

F32 = jnp.float32
BF16 = jnp.bfloat16

D_MODEL = 1024
DEPTH = 4
HEAD_DIM = 64
ATTN_WIDTH = 512
N_HEADS = 8
N_KV_HEADS = 2
GQA_GROUP = 4
KV_WIDTH = 128
WINDOW = 128
RWKV_WIDTH = 256
RWKV_HEADS = 4
RWKV_HEAD = 64
RWKV_PROJ = 1024
LRU_WIDTH = 256
CONV_W = 4
LRU_C = 8.0
D_FF = 2816
PLE_DIM = 256
RMS_EPS = 1e-6
GN_EPS = 64e-5
IN_COLS = 2304

ROW_TILE = 512
IN_ROW_TILE = 1024
FF_CHUNK = 256
LRU_PIECES = 8
RWKV_CHUNK = 64
RWKV_SEQS = 4
RWKV_SUBCHUNKS = 4
RWKV_STAGGER = 3
LRU_SCAN_BLOCK = 32
ATTN_SAMPLE_SEQS = 8
ATTN_Q_BLOCKS = 16
VMEM_LIMIT = 56 * 1024 * 1024


def _cparams(sem):
    return pltpu.CompilerParams(dimension_semantics=sem, vmem_limit_bytes=VMEM_LIMIT)


def _layer_spec(shape, layer, nidx, **kw):
    zeros = (0,) * len(shape)
    if nidx == 1:
        return pl.BlockSpec((None,) + tuple(shape), lambda i: (layer,) + zeros, **kw)
    return pl.BlockSpec((None,) + tuple(shape), lambda i, j: (layer,) + zeros, **kw)


def _bdot(a, b):
    return jnp.dot(a.astype(BF16), b.astype(BF16), preferred_element_type=F32)


def _block_sums(x, ones_blk):
    return jnp.dot(x.astype(BF16), ones_blk, preferred_element_type=F32)


def _rmsnorm(x, g):
    ms = jnp.mean(x * x, axis=-1, keepdims=True)
    return x * lax.rsqrt(ms + RMS_EPS) * g


def _softplus(x):
    return jnp.maximum(x, 0.0) + jnp.log1p(jnp.exp(-jnp.abs(x)))


def _sigmoid(x):
    return 1.0 / (1.0 + jnp.exp(-x))


def _gelu_tanh(x):
    return 0.5 * x * (1.0 + jnp.tanh(0.7978845608028654 * (x + 0.044715 * (x * x * x))))


def _in_kernel(x_ref, g_ref, w_ref, b_ref, q_ref, kv_ref, rw_ref, lr_ref):
    h = _rmsnorm(x_ref[...], g_ref[...]).astype(BF16)
    for ref, lo, hi in ((q_ref, 0, 512), (kv_ref, 512, 768), (rw_ref, 768, 1792), (lr_ref, 1792, 2304)):
        ref[...] = jnp.dot(h, w_ref[:, lo:hi], preferred_element_type=F32) + b_ref[:, lo:hi]


def _in_proj(x, g, w, b, layer):
    m = x.shape[0]
    tm = min(IN_ROW_TILE, m)
    row = lambda w_: pl.BlockSpec((tm, w_), lambda i: (i, 0))
    return pl.pallas_call(
        _in_kernel,
        grid=(m // tm,),
        in_specs=[row(D_MODEL), _layer_spec((1, D_MODEL), layer, 1),
                  _layer_spec((D_MODEL, IN_COLS), 0, 1), _layer_spec((1, IN_COLS), layer, 1)],
        out_specs=[row(512), row(256), row(1024), row(512)],
        out_shape=[jax.ShapeDtypeStruct((m, n), F32) for n in (512, 256, 1024, 512)],
        compiler_params=_cparams(("arbitrary",)),
        name="in_proj",
    )(x, g, w, b)


def _attn_prompt_kernel(layer, nq, sink_ref, q_ref, kvp_ref, kvc_ref, o_ref):
    j = pl.program_id(1)
    log2e = 1.4426950408889634
    q = q_ref[...] * (HEAD_DIM ** -0.5 * log2e)
    kv = jnp.concatenate([kvp_ref[...], kvc_ref[...]], axis=0)
    kj = lax.broadcasted_iota(jnp.int32, (2 * WINDOW, WINDOW), 0)
    qi = lax.broadcasted_iota(jnp.int32, (2 * WINDOW, WINDOW), 1) + WINDOW
    d = qi - kj
    band = (d >= 0) & (d <= WINDOW)
    first = band & ((j > 0) | (kj >= WINDOW))
    nt = (((1,), (1,)), ((), ()))
    keys = kv[:, 0:KV_WIDTH].astype(BF16)
    v_t = kv[:, KV_WIDTH:2 * KV_WIDTH].T
    ones_rows = (lax.broadcasted_iota(jnp.int32, (8, 2 * WINDOW), 0) == 0).astype(F32)
    scores = {}
    for blk in range(nq):
        qb = q[blk * WINDOW:(blk + 1) * WINDOW]
        for g in range(N_KV_HEADS):
            qg = jnp.concatenate([qb[:, (g * GQA_GROUP + hh) * HEAD_DIM:(g * GQA_GROUP + hh + 1) * HEAD_DIM]
                                  for hh in range(GQA_GROUP)], axis=0).astype(BF16)
            k_ext = keys[blk * WINDOW:(blk + 2) * WINDOW, g * HEAD_DIM:(g + 1) * HEAD_DIM]
            scores[blk, g] = lax.dot_general(k_ext, qg, nt, preferred_element_type=F32)
    probs, sink_terms = {}, {}
    for blk in range(nq):
        mask = first if blk == 0 else band
        for g in range(N_KV_HEADS):
            pg, sg = [], []
            for hh in range(GQA_GROUP):
                s = jnp.where(mask, scores[blk, g][:, hh * WINDOW:(hh + 1) * WINDOW], -1e30)
                sink = sink_ref[layer, g * GQA_GROUP + hh] * log2e
                m = jnp.maximum(jnp.max(s, axis=0, keepdims=True), sink)
                pg.append(jnp.exp2(s - m).astype(BF16))
                sg.append(jnp.exp2(sink - m))
            probs[blk, g] = jnp.concatenate(pg, axis=1)
            sink_terms[blk, g] = jnp.concatenate(sg, axis=1)
    for blk in range(nq):
        outs = []
        for g in range(N_KV_HEADS):
            v_aug = jnp.concatenate([v_t[g * HEAD_DIM:(g + 1) * HEAD_DIM, blk * WINDOW:(blk + 2) * WINDOW],
                                     ones_rows], axis=0).astype(BF16)
            og = jnp.dot(v_aug, probs[blk, g], preferred_element_type=F32)
            den = og[HEAD_DIM:HEAD_DIM + 1, :] + sink_terms[blk, g]
            og = og[0:HEAD_DIM, :] * (1.0 / den)
            outs.extend(og[:, hh * WINDOW:(hh + 1) * WINDOW] for hh in range(GQA_GROUP))
        o_ref[blk * WINDOW:(blk + 1) * WINDOW, :] = jnp.concatenate(outs, axis=0).T


def _attn_prompt(q, kv, sinks, layer):
    b, t, _ = q.shape
    nq = ATTN_Q_BLOCKS
    tq = nq * WINDOW
    return pl.pallas_call(
        functools.partial(_attn_prompt_kernel, layer, nq),
        grid=(b, t // tq),
        in_specs=[
            pl.BlockSpec(memory_space=pltpu.SMEM),
            pl.BlockSpec((None, tq, ATTN_WIDTH), lambda i, j: (i, j, 0)),
            pl.BlockSpec((None, WINDOW, 2 * KV_WIDTH), lambda i, j: (i, jnp.maximum(j * nq - 1, 0), 0)),
            pl.BlockSpec((None, tq, 2 * KV_WIDTH), lambda i, j: (i, j, 0)),
        ],
        out_specs=pl.BlockSpec((None, tq, ATTN_WIDTH), lambda i, j: (i, j, 0)),
        out_shape=jax.ShapeDtypeStruct((b, t, ATTN_WIDTH), F32),
        compiler_params=_cparams(("arbitrary", "arbitrary")),
        name="attn_prompt",
    )(sinks, q, kv, kv)


def _attn_sample_kernel(layer, tn, sink_ref, q_ref, kvn_ref, ck_ref, cv_ref, nk_in_ref, nv_in_ref,
                        o_ref, nk_ref, nv_ref):
    del nk_in_ref, nv_in_ref
    bb = q_ref.shape[0]
    rows = GQA_GROUP * tn
    kvn = kvn_ref[...]
    row = lax.broadcasted_iota(jnp.int32, (1, rows, 1), 1)
    tok = row % tn
    col = lax.broadcasted_iota(jnp.int32, (1, 1, WINDOW), 2)
    cmask = col >= tok
    for g in range(N_KV_HEADS):
        k_t = ck_ref[:, g]
        v_t = cv_ref[:, g]
        qg = q_ref[:, g] * (HEAD_DIM ** -0.5)
        qg_b = qg.astype(BF16).astype(F32)
        sc = jnp.einsum('bqd,bdw->bqw', qg.astype(BF16), k_t.astype(BF16), preferred_element_type=F32)
        sc = jnp.where(cmask, sc, -1e30)
        sink = jnp.zeros((1, rows, 1), F32)
        for hh in range(GQA_GROUP):
            sink = jnp.where(row // tn == hh, sink_ref[layer, g * GQA_GROUP + hh], sink)
        m = jnp.maximum(jnp.max(sc, axis=-1, keepdims=True), sink)
        sn = []
        for jn in range(tn):
            kn = kvn[:, g * tn + jn:g * tn + jn + 1, :].astype(BF16).astype(F32)
            s_j = jnp.sum(qg_b * kn, axis=-1, keepdims=True)
            s_j = jnp.where(tok >= jn, s_j, -1e30)
            sn.append(s_j)
            m = jnp.maximum(m, s_j)
        ec = jnp.exp(sc - m)
        den = jnp.sum(ec, axis=-1, keepdims=True) + jnp.exp(sink - m)
        en = [jnp.exp(s_j - m) for s_j in sn]
        for e_j in en:
            den = den + e_j
        inv = 1.0 / den
        o = jnp.einsum('bqw,bdw->bqd', (ec * inv).astype(BF16), v_t.astype(BF16), preferred_element_type=F32)
        for jn in range(tn):
            vn = kvn[:, (N_KV_HEADS + g) * tn + jn:(N_KV_HEADS + g) * tn + jn + 1, :].astype(BF16).astype(F32)
            o = o + (en[jn] * inv).astype(BF16).astype(F32) * vn
        o_ref[:, g] = o
    new_t = kvn.reshape(bb * 4 * tn, HEAD_DIM).T
    lane = lax.broadcasted_iota(jnp.int32, (HEAD_DIM, WINDOW), 1)
    for c_ref, n_ref, which in ((ck_ref, nk_ref, 0), (cv_ref, nv_ref, 1)):
        for g in range(N_KV_HEADS):
            for b in range(bb):
                src = (b * 2 * N_KV_HEADS + which * N_KV_HEADS + g) * tn
                fresh = pltpu.roll(new_t, (WINDOW - tn - src) % WINDOW, 1)
                kept = pltpu.roll(c_ref[b, g], WINDOW - tn, 1)
                n_ref[b, g] = jnp.where(lane >= WINDOW - tn, fresh, kept)


def _attn_sample(q, kv, ck, cv, nk_all, nv_all, sinks, tn, layer):
    b = q.shape[0] // tn
    bb = ATTN_SAMPLE_SEQS
    assert bb * 4 * tn == WINDOW
    rows = GQA_GROUP * tn
    qh = q.reshape(tn, b, N_KV_HEADS, GQA_GROUP, HEAD_DIM).transpose(1, 2, 3, 0, 4)
    qh = qh.reshape(b, N_KV_HEADS, rows, HEAD_DIM)
    kvn = kv.reshape(tn, b, 2 * N_KV_HEADS, HEAD_DIM).transpose(1, 2, 0, 3).reshape(b, 4 * tn, HEAD_DIM)
    cache = pl.BlockSpec((None, bb, N_KV_HEADS, HEAD_DIM, WINDOW), lambda i: (layer, i, 0, 0, 0))
    heads = pl.BlockSpec((bb, N_KV_HEADS, rows, HEAD_DIM), lambda i: (i, 0, 0, 0))
    o, nk_all, nv_all = pl.pallas_call(
        functools.partial(_attn_sample_kernel, layer, tn),
        grid=(b // bb,),
        in_specs=[
            pl.BlockSpec(memory_space=pltpu.SMEM),
            heads,
            pl.BlockSpec((bb, 4 * tn, HEAD_DIM), lambda i: (i, 0, 0)),
            cache, cache,
            pl.BlockSpec(memory_space=pl.ANY), pl.BlockSpec(memory_space=pl.ANY),
        ],
        out_specs=[heads, cache, cache],
        out_shape=[
            jax.ShapeDtypeStruct((b, N_KV_HEADS, rows, HEAD_DIM), F32),
            jax.ShapeDtypeStruct(nk_all.shape, F32),
            jax.ShapeDtypeStruct(nv_all.shape, F32),
        ],
        input_output_aliases={5: 1, 6: 2},
        compiler_params=_cparams(("arbitrary",)),
        name="attn_sample",
    )(sinks, qh, kvn, ck, cv, nk_all, nv_all)
    o = o.reshape(b, N_KV_HEADS, GQA_GROUP, tn, HEAD_DIM).transpose(3, 0, 1, 2, 4)
    return o.reshape(tn * b, ATTN_WIDTH), nk_all, nv_all


def _tile_rows(x, n):
    return jnp.concatenate([x] * n, axis=0)


def _rwkv_kernel(c, nseq, x_ref, sh0_ref, st0_ref, mu_ref, lora_ref, gup_ref, vec_ref, ones_ref,
                 stacked_ref, o_ref, st_ref, last_ref, state_ref):
    del stacked_ref
    hw = RWKV_WIDTH
    n = RWKV_HEAD
    cw = RWKV_HEADS * c

    @pl.when(pl.program_id(1) == 0)
    def _():
        last_ref[...] = sh0_ref[...]
        for s in range(nseq):
            for h in range(RWKV_HEADS):
                state_ref[s, :, h * n:(h + 1) * n] = st0_ref[s, h]

    row = lax.broadcasted_iota(jnp.int32, (c, 1), 0)
    lane128 = lax.broadcasted_iota(jnp.int32, (c, 128), 1)
    tri = jnp.where(lax.broadcasted_iota(jnp.int32, (c, c), 1) <= lax.broadcasted_iota(jnp.int32, (c, c), 0),
                    1.0, 0.0).astype(BF16)
    tri3 = jnp.concatenate([tri, tri, tri], axis=1)
    head_rows = (lax.broadcasted_iota(jnp.int32, (cw, hw), 0) // c
                 == lax.broadcasted_iota(jnp.int32, (cw, hw), 1) // n)
    t_i = lax.broadcasted_iota(jnp.int32, (c, cw), 0)
    s_i = lax.broadcasted_iota(jnp.int32, (c, cw), 1) % c
    strict = s_i < t_i
    incl = s_i <= t_i
    eye_all = jnp.where(s_i == t_i, 1.0, 0.0)
    blk = (lax.broadcasted_iota(jnp.int32, (cw, cw), 0) // c
           == lax.broadcasted_iota(jnp.int32, (cw, cw), 1) // c)
    sblk = (lax.broadcasted_iota(jnp.int32, (hw, hw), 0) // n
            == lax.broadcasted_iota(jnp.int32, (hw, hw), 1) // n)
    masks = (row, lane128, tri3, head_rows, strict, incl, eye_all, blk, sblk)
    nsub = x_ref.shape[1] // c
    state_owner = [0] * nseq
    shared = {"lora": lora_ref, "gup": gup_ref, "sums": ones_ref}
    chains = {sub: [_rwkv_chunk(c, s, sub, nsub, state_owner, masks, x_ref, mu_ref, vec_ref,
                                o_ref, st_ref, last_ref, state_ref) for s in range(nseq)]
              for sub in range(nsub)}
    answers = {sub: [None] * nseq for sub in range(nsub)}
    rnd = 0
    while chains:
        for sub in sorted(chains):
            if rnd < sub * RWKV_STAGGER:
                continue
            asked = [_advance(ch, ans) for ch, ans in zip(chains[sub], answers[sub])]
            if asked[0] == "done":
                del chains[sub]
                continue
            results = [[] for _ in range(nseq)]
            for i, (kind, _) in enumerate(asked[0] or ()):
                stacked = _bdot(jnp.concatenate([req[i][1] for req in asked], axis=0), shared[kind][...])
                for s in range(nseq):
                    results[s].append(stacked[s * c:(s + 1) * c])
            answers[sub] = [tuple(res) for res in results]
        rnd += 1


def _advance(chain, answer):
    try:
        return next(chain) if answer is None else chain.send(answer)
    except StopIteration:
        return "done"


def _rwkv_chunk(c, s, sub, nsub, state_owner, masks, x_ref, mu_ref, vec_ref, o_ref, st_ref, last_ref, state_ref):
    row, lane128, tri3, head_rows, strict, incl, eye_all, blk, sblk = masks
    hw = RWKV_WIDTH
    n = RWKV_HEAD
    cw = RWKV_HEADS * c
    x = x_ref[s, sub * c:(sub + 1) * c, :]
    before = last_ref[s] if sub == 0 else x_ref[s, sub * c - 1:sub * c, :]
    prev = jnp.where(row == 0, before, pltpu.roll(x, 1, 0))
    xs = x + (prev - x) * mu_ref[...]

    r = xs[:, 0:hw]
    k = xs[:, hw:2 * hw]
    v = xs[:, 2 * hw:3 * hw]
    wa = xs[:, 3 * hw:3 * hw + 128]
    xg = xs[:, 3 * hw + 128:]
    lora_in = jnp.where(lane128 < 64, jnp.tanh(wa), wa)
    w0, a0, k_k, k_a, r_k, ln_w, ln_b = (vec_ref[i:i + 1, :] for i in range(7))
    kk = k * k_k
    lora, g, ss = yield (("lora", lora_in), ("gup", _sigmoid(xg)), ("sums", kk * kk))
    w_log = -_softplus(-(w0 + lora[:, 0:hw])) - 0.5
    logw = -jnp.exp(w_log)
    a = _sigmoid(a0 + lora[:, hw:2 * hw])
    kk = kk * lax.rsqrt(jnp.maximum(ss, 1e-24))
    k2 = k * (1.0 + (a - 1.0) * k_a)
    bv = kk * a

    pieces = []
    rem = logw
    for _ in range(3):
        p = rem.astype(BF16)
        rem = rem - p.astype(F32)
        pieces.append(p)
    cum = jnp.dot(tri3, jnp.concatenate(pieces, axis=0), preferred_element_type=F32)
    (bonus_sum,) = yield (("sums", r * k2 * r_k),)
    e_inc = jnp.exp(cum)
    e_exc = jnp.exp(cum - logw)
    e_inv = jnp.exp(-cum)
    kq = kk * e_exc
    rq = r * e_inc
    kd = k2 * e_inv
    bd = bv * e_inv
    w_end = e_inc[c - 1:c, :]
    kend = kd * w_end
    bend = bd * w_end

    def expand(z):
        return jnp.where(head_rows, _tile_rows(z, RWKV_HEADS), 0.0)

    lhs = jnp.concatenate([kq, rq], axis=0).astype(BF16)
    rhs = jnp.concatenate([expand(bd), expand(kd)], axis=0).astype(BF16)
    nt = (((1,), (1,)), ((), ()))
    prod = lax.dot_general(lhs, rhs, nt, preferred_element_type=F32)
    yield
    a_b = jnp.where(strict, prod[0:c, 0:cw], 0.0)
    a_k = jnp.where(strict, prod[0:c, cw:2 * cw], 0.0)
    p_b = jnp.where(incl, prod[c:2 * c, 0:cw], 0.0)
    p_k = jnp.where(incl, prod[c:2 * c, cw:2 * cw], 0.0)

    def bdiag(z):
        return jnp.where(blk, _tile_rows(z, RWKV_HEADS), 0.0)

    xm = -a_b
    tinv = eye_all + xm
    v_d = expand(v)
    from_v = _bdot(jnp.concatenate([a_k, p_k], axis=0), v_d)
    levels = c.bit_length() - 1
    xm = _bdot(xm, bdiag(xm))
    yield
    for _ in range(1, levels - 1):
        both = _bdot(jnp.concatenate([xm, tinv], axis=0), bdiag(xm))
        yield
        xm = both[0:c]
        tinv = tinv + both[c:2 * c]
    tinv = tinv + _bdot(tinv, bdiag(xm))
    yield
    assert state_owner[s] == sub
    st = state_ref[s]
    st_d = jnp.where(sblk, _tile_rows(st, RWKV_HEADS), 0.0)
    from_state = lax.dot_general(lhs, st_d.astype(BF16), nt, preferred_element_type=F32)
    yield
    u = _bdot(tinv, expand(from_state[0:c] + from_v[0:c]))
    yield
    y = from_state[c:2 * c] + from_v[c:2 * c] - _bdot(p_b, expand(u))

    lhs_s = jnp.concatenate([v, u], axis=0)
    rhs_s = jnp.concatenate([kend, -bend], axis=0)
    tn_dims = (((0,), (0,)), ((), ()))
    gm = lax.dot_general(lhs_s.astype(BF16), rhs_s.astype(BF16), tn_dims, preferred_element_type=F32)
    yield
    gm = jnp.where(sblk, gm, 0.0)
    st_new = st * w_end + (gm[0:n] + gm[n:2 * n] + (gm[2 * n:3 * n] + gm[3 * n:4 * n]))
    state_ref[s] = st_new
    state_owner[s] = sub + 1
    if sub == nsub - 1:
        state_owner[s] = 0
        last_ref[s] = x[c - 1:c, :]
        for h in range(RWKV_HEADS):
            st_ref[s, h] = st_new[:, h * n:(h + 1) * n]

    (y_sum,) = yield (("sums", y),)
    yc = y - y_sum * (1.0 / n)
    (sq_sum,) = yield (("sums", yc * yc),)
    yn = yc * lax.rsqrt(sq_sum * (1.0 / n) + GN_EPS) * ln_w + ln_b
    o_ref[s, sub * c:(sub + 1) * c, :] = (yn + bonus_sum * v) * g


def _rwkv(x, sh0, st0, mu, lora_w, g_up, vecs, ones_blk, stacked, layer):
    b, t, _ = x.shape
    c = RWKV_CHUNK
    tstep = RWKV_SUBCHUNKS * c
    nseq = min(RWKV_SEQS, b)
    wspec = lambda s: _layer_spec(s, layer, 2)
    hstate = (nseq, RWKV_HEADS, RWKV_HEAD, RWKV_HEAD)
    return pl.pallas_call(
        functools.partial(_rwkv_kernel, c, nseq),
        grid=(b // nseq, t // tstep),
        in_specs=[
            pl.BlockSpec((nseq, tstep, RWKV_PROJ), lambda i, j: (i, j, 0)),
            pl.BlockSpec((nseq, 1, RWKV_PROJ), lambda i, j: (i, 0, 0)),
            pl.BlockSpec(hstate, lambda i, j: (i, 0, 0, 0)),
            wspec((1, RWKV_PROJ)), wspec((128, 512)), wspec((128, RWKV_WIDTH)), wspec((8, RWKV_WIDTH)),
            pl.BlockSpec((RWKV_WIDTH, RWKV_WIDTH), lambda i, j: (0, 0)),
            pl.BlockSpec(memory_space=pl.ANY),
        ],
        out_specs=[
            pl.BlockSpec((nseq, tstep, RWKV_WIDTH), lambda i, j: (i, j, 0)),
            pl.BlockSpec((None,) + hstate, lambda i, j: (layer, i, 0, 0, 0)),
        ],
        out_shape=[
            jax.ShapeDtypeStruct((b, t, RWKV_WIDTH), F32),
            jax.ShapeDtypeStruct(stacked.shape, F32),
        ],
        scratch_shapes=[pltpu.VMEM((nseq, 1, RWKV_PROJ), F32), pltpu.VMEM((nseq, RWKV_HEAD, RWKV_WIDTH), F32)],
        input_output_aliases={8: 1},
        compiler_params=_cparams(("arbitrary", "arbitrary")),
        name="rwkv",
    )(x, sh0, st0, mu, lora_w, g_up, vecs, ones_blk, stacked)


def _rwkv_step_kernel(tn, nb, x_ref, sh0_ref, st0_ref, mu_ref, lora_ref, gup_ref, vec_ref, ones_ref,
                      stacked_ref, o_ref, st_ref, vt_ref, nat_ref, yt_ref):
    del stacked_ref
    hw = RWKV_WIDTH
    n = RWKV_HEAD
    h = pl.program_id(0)
    rows = tn * nb

    @pl.when(h == 0)
    def _():
        x = x_ref[...]
        prev = jnp.concatenate([sh0_ref[...], x[0:rows - nb]], axis=0)
        xs = x + (prev - x) * mu_ref[...]
        r = xs[:, 0:hw]
        k = xs[:, hw:2 * hw]
        v = xs[:, 2 * hw:3 * hw]
        wa = xs[:, 3 * hw:3 * hw + 128]
        xg = xs[:, 3 * hw + 128:]
        lane128 = lax.broadcasted_iota(jnp.int32, (rows, 128), 1)
        lora = _bdot(jnp.where(lane128 < 64, jnp.tanh(wa), wa), lora_ref[...])
        w0, a0, k_k, k_a, r_k = (vec_ref[i:i + 1, :] for i in range(5))
        ones_blk = ones_ref[...]
        w_log = -_softplus(-(w0 + lora[:, 0:hw])) - 0.5
        decay = jnp.exp(-jnp.exp(w_log))
        a = _sigmoid(a0 + lora[:, hw:2 * hw])
        kk = k * k_k
        kk = kk * lax.rsqrt(jnp.maximum(_block_sums(kk * kk, ones_blk), 1e-24))
        k2 = k * (1.0 + (a - 1.0) * k_a)
        nat_ref[0] = _bdot(_sigmoid(xg), gup_ref[...])
        nat_ref[1] = v
        nat_ref[2] = _block_sums(r * k2 * r_k, ones_blk)
        for i, z in enumerate((r, decay, k2, v, kk, kk * a)):
            for t in range(tn):
                for half in range(hw // 128):
                    vt_ref[i, t, half * 128:(half + 1) * 128, :] = (
                        z[t * nb:(t + 1) * nb, half * 128:(half + 1) * 128].T)

    base = pl.multiple_of(h * n, n)

    def value_row(vi, carry):
        s = st0_ref[vi]
        for t in range(tn):
            r_t, w_t, k_t, _, kk_t, b_t = (vt_ref[i, t, pl.ds(base, n), :] for i in range(6))
            v_row = vt_ref[3, t, pl.ds(base + vi, 1), :]
            u = jnp.sum(s * kk_t, axis=0, keepdims=True)
            s = s * w_t - u * b_t + v_row * k_t
            yt_ref[t, pl.ds(base + vi, 1), :] = jnp.sum(s * r_t, axis=0, keepdims=True)
        st_ref[vi] = s
        return carry

    lax.fori_loop(0, n, value_row, 0, unroll=4)

    @pl.when(h == RWKV_HEADS - 1)
    def _():
        ones_blk = ones_ref[...]
        ln_w = vec_ref[5:6, :]
        ln_b = vec_ref[6:7, :]
        y = jnp.concatenate(
            [jnp.concatenate([yt_ref[t, half * 128:(half + 1) * 128, :].T for half in range(hw // 128)], axis=1)
             for t in range(tn)], axis=0)
        mean = _block_sums(y, ones_blk) * (1.0 / n)
        yc = y - mean
        var = _block_sums(yc * yc, ones_blk) * (1.0 / n)
        yn = yc * lax.rsqrt(var + GN_EPS) * ln_w + ln_b
        o_ref[...] = (yn + nat_ref[2] * nat_ref[1]) * nat_ref[0]


def _rwkv_sample(x, sh0, st0, mu, lora_w, g_up, vecs, ones_blk, stacked, tn, layer):
    rows = x.shape[0]
    nb = rows // tn
    wspec = lambda s: _layer_spec(s, layer, 1)
    head_state = pl.BlockSpec((None, None, RWKV_HEAD, RWKV_HEAD, nb), lambda h: (layer, h, 0, 0, 0))
    return pl.pallas_call(
        functools.partial(_rwkv_step_kernel, tn, nb),
        grid=(RWKV_HEADS,),
        in_specs=[
            pl.BlockSpec((rows, RWKV_PROJ), lambda h: (0, 0)),
            wspec((nb, RWKV_PROJ)), head_state,
            wspec((1, RWKV_PROJ)), wspec((128, 512)), wspec((128, RWKV_WIDTH)), wspec((8, RWKV_WIDTH)),
            pl.BlockSpec((RWKV_WIDTH, RWKV_WIDTH), lambda h: (0, 0)),
            pl.BlockSpec(memory_space=pl.ANY),
        ],
        out_specs=[pl.BlockSpec((rows, RWKV_WIDTH), lambda h: (0, 0)), head_state],
        out_shape=[jax.ShapeDtypeStruct((rows, RWKV_WIDTH), F32), jax.ShapeDtypeStruct(stacked.shape, F32)],
        scratch_shapes=[pltpu.VMEM((6, tn, RWKV_WIDTH, nb), F32), pltpu.VMEM((3, rows, RWKV_WIDTH), F32),
                        pltpu.VMEM((tn, RWKV_WIDTH, nb), F32)],
        input_output_aliases={8: 1},
        compiler_params=_cparams(("arbitrary",)),
        name="rwkv_step",
    )(x, sh0, st0, mu, lora_w, g_up, vecs, ones_blk, stacked)


def _lru_gates(xc, wg_ref, vec_ref):
    w = LRU_WIDTH
    _, b_a, b_i, lam = (vec_ref[i:i + 1, :] for i in range(4))
    gates = _bdot(xc, wg_ref[...])
    r = _sigmoid(gates[:, 0:w] + b_a)
    i = _sigmoid(gates[:, w:2 * w] + b_i)
    log_a = LRU_C * r * (-_softplus(-lam))
    a = jnp.exp(log_a)
    th = jnp.tanh(log_a)
    u = jnp.sqrt(-2.0 * th / (1.0 - th)) * (i * xc)
    return a, u


def _lru_tile_steps(x_ref, carry, ext_ref, o_ref, cw_ref, wg_ref, vec_ref):
    tc = x_ref.shape[0]
    w = LRU_WIDTH
    blk = LRU_SCAN_BLOCK
    gr = tc // LRU_PIECES
    ext_ref[8:8 + tc, :] = x_ref[:, 0:w]
    xc = vec_ref[0:1, :] + ext_ref[8:8 + tc, :] * cw_ref[CONV_W - 1:CONV_W, :]
    for j in range(CONV_W - 1):
        xc = xc + ext_ref[pl.ds(8 - (CONV_W - 1) + j, tc), :] * cw_ref[j:j + 1, :]
    ext_ref[0:8, :] = ext_ref[tc:tc + 8, :]
    gates = _bdot(xc, wg_ref[...])
    yield
    _, b_a, b_i, lam = (vec_ref[i:i + 1, :] for i in range(4))
    log_lam = -_softplus(-lam)
    pos = lax.broadcasted_iota(jnp.int32, (gr, 1), 0) % blk
    for g in range(LRU_PIECES):
        rows = slice(g * gr, (g + 1) * gr)
        xc_g = xc[rows]
        r = _sigmoid(gates[rows, 0:w] + b_a)
        i = _sigmoid(gates[rows, w:2 * w] + b_i)
        log_a = LRU_C * r * log_lam
        a = jnp.exp(log_a)
        th = jnp.tanh(log_a)
        u = jnp.sqrt(-2.0 * th / (1.0 - th)) * (i * xc_g)
        span = 1
        while span < 8:
            ok = pos >= span
            a_s = pltpu.roll(a, span, 0)
            u_s = pltpu.roll(u, span, 0)
            u = jnp.where(ok, a * u_s + u, u)
            a = jnp.where(ok, a * a_s, a)
            span *= 2
        a = a.reshape(gr // blk, blk, w)
        u = u.reshape(gr // blk, blk, w)
        while span < blk:
            u = jnp.concatenate([u[:, :span], a[:, span:] * u[:, :blk - span] + u[:, span:]], axis=1)
            a = jnp.concatenate([a[:, :span], a[:, span:] * a[:, :blk - span]], axis=1)
            span *= 2
        gelu = _gelu_tanh(x_ref[rows, w:2 * w])
        for b in range(gr // blk):
            h = a[b] * carry + u[b]
            carry = h[blk - 1:blk, :]
            o_ref[g * gr + b * blk:g * gr + (b + 1) * blk, :] = h * gelu[b * blk:(b + 1) * blk]
        yield
    return carry


def _step(steps, result):
    if not result:
        try:
            next(steps)
        except StopIteration as done:
            result.append(done.value)


def _run_to_end(steps, result=None):
    result = [] if result is None else result
    while not result:
        _step(steps, result)
    return result[0]


def _lru_step_kernel(tn, nb, x_ref, cv0_ref, h0_ref, cw_ref, wg_ref, vec_ref, o_ref, h_ref):
    w = LRU_WIDTH
    rows = tn * nb
    xb = x_ref[:, 0:w]
    gb = x_ref[:, w:2 * w]
    ext = jnp.concatenate([cv0_ref[j] for j in range(CONV_W - 1)] + [xb], axis=0)
    xc = vec_ref[0:1, :]
    for j in range(CONV_W):
        xc = xc + ext[j * nb:j * nb + rows] * cw_ref[j:j + 1, :]
    a, u = _lru_gates(xc, wg_ref, vec_ref)
    gelu = _gelu_tanh(gb)
    h = h0_ref[...]
    for t in range(tn):
        h = a[t * nb:(t + 1) * nb] * h + u[t * nb:(t + 1) * nb]
        o_ref[t * nb:(t + 1) * nb, :] = h * gelu[t * nb:(t + 1) * nb]
    h_ref[...] = h


def _lru_sample(x, cv0, h0, conv_w, w_gates, vecs, tn, layer):
    rows = x.shape[0]
    nb = rows // tn
    wspec = lambda s: _layer_spec(s, layer, 1)
    return pl.pallas_call(
        functools.partial(_lru_step_kernel, tn, nb),
        grid=(1,),
        in_specs=[pl.BlockSpec((rows, 2 * LRU_WIDTH), lambda i: (0, 0)),
                  wspec((CONV_W - 1, nb, LRU_WIDTH)), wspec((nb, LRU_WIDTH)),
                  wspec((CONV_W, LRU_WIDTH)), wspec((LRU_WIDTH, 2 * LRU_WIDTH)), wspec((4, LRU_WIDTH))],
        out_specs=[pl.BlockSpec((rows, LRU_WIDTH), lambda i: (0, 0)), pl.BlockSpec((nb, LRU_WIDTH), lambda i: (0, 0))],
        out_shape=[jax.ShapeDtypeStruct((rows, LRU_WIDTH), F32), jax.ShapeDtypeStruct((nb, LRU_WIDTH), F32)],
        compiler_params=_cparams(("arbitrary",)),
        name="lru_step",
    )(x, cv0, h0, conv_w, w_gates, vecs)


def _post_kernel(lru_tiles, ncast, *refs):
    casts = []
    if lru_tiles:
        (x_ref, oa_ref, ob_ref, lr_next_ref, lr_first_ref, cv0_ref, h0_ref, p_ref, wo_ref, wg_ref, wu_ref, wd_ref,
         pg_ref, pw_ref, vec_ref, lcw_ref, lwg_ref, lvec_ref) = refs[:18]
        y_ref, h_ref = refs[18 + ncast:20 + ncast]
        casts = list(zip(refs[18:18 + ncast], refs[20 + ncast:20 + 2 * ncast]))
        oc_ref, ext_ref, hc_ref = refs[20 + 2 * ncast:]
        i = pl.program_id(0)

        @pl.when(i == 0)
        def _():
            ext_ref[0:8, :] = cv0_ref[...]
            hc_ref[...] = _run_to_end(_lru_tile_steps(lr_first_ref, h0_ref[...], ext_ref, oc_ref,
                                                      lcw_ref, lwg_ref, lvec_ref))
    else:
        (x_ref, oa_ref, ob_ref, oc_ref, p_ref, wo_ref, wg_ref, wu_ref, wd_ref, pg_ref, pw_ref,
         vec_ref, y_ref) = refs
    mix = (jnp.dot(oa_ref[...].astype(BF16), wo_ref[0:512, :], preferred_element_type=F32)
           + jnp.dot(ob_ref[...].astype(BF16), wo_ref[512:768, :], preferred_element_type=F32)
           + jnp.dot(oc_ref[...].astype(BF16), wo_ref[768:1024, :], preferred_element_type=F32)
           + vec_ref[0:1, :])
    mixer_ahead, mixer_state = None, []
    if lru_tiles:
        opens = (i + 1) % lru_tiles == 0
        ext_ref[0:8, :] = jnp.where(opens, cv0_ref[...], ext_ref[0:8, :])
        carry_in = jnp.where(opens, h0_ref[...], hc_ref[...])
        mixer_ahead = _lru_tile_steps(lr_next_ref, carry_in, ext_ref, oc_ref, lcw_ref, lwg_ref, lvec_ref)

    emb = jnp.dot(p_ref[...].astype(BF16), pw_ref[...], preferred_element_type=F32)
    x = x_ref[...] + _rmsnorm(mix, vec_ref[1:2, :])
    f = _rmsnorm(x, vec_ref[2:3, :]).astype(BF16)
    acc = None
    for lo in range(0, D_FF, FF_CHUNK):
        gate = jnp.dot(f, wg_ref[:, lo:lo + FF_CHUNK], preferred_element_type=F32)
        up = jnp.dot(f, wu_ref[:, lo:lo + FF_CHUNK], preferred_element_type=F32)
        hid = (gate * _sigmoid(gate) * up).astype(BF16)
        part = jnp.dot(hid, wd_ref[lo:lo + FF_CHUNK, :], preferred_element_type=F32)
        acc = part if acc is None else acc + part
        if mixer_ahead is not None:
            for _ in range(-(-(LRU_PIECES + 1) // (D_FF // FF_CHUNK))):
                _step(mixer_ahead, mixer_state)
        if casts:
            src, dst = casts.pop()
            dst[...] = src[...].astype(BF16)
    assert not casts
    if mixer_ahead is not None:
        carry = _run_to_end(mixer_ahead, mixer_state)
        carry = jnp.where(i + 1 < pl.num_programs(0), carry, hc_ref[...])
        hc_ref[...] = carry
        h_ref[...] = carry
    x2 = x + _rmsnorm(acc, vec_ref[3:4, :])
    gate = _sigmoid(jnp.dot(x2.astype(BF16), pg_ref[...], preferred_element_type=F32))
    y_ref[...] = x2 + gate * emb


def _post_weight_specs(layer):
    own = lambda s: _layer_spec(s, 0, 1, pipeline_mode=pl.Buffered(1))
    stacked = lambda s: _layer_spec(s, layer, 1, pipeline_mode=pl.Buffered(1))
    return [own((D_MODEL, D_MODEL)), own((D_MODEL, D_FF)), own((D_MODEL, D_FF)), own((D_FF, D_MODEL)),
            own((D_MODEL, D_MODEL)), stacked((PLE_DIM, D_MODEL)), stacked((4, D_MODEL))]


def _post(x, oa, ob, oc, p, wo, wg, wu, wd, pg, pw, vecs, layer):
    m = x.shape[0]
    tm = min(ROW_TILE, m)
    row = lambda w_: pl.BlockSpec((tm, w_), lambda i: (i, 0))
    return pl.pallas_call(
        functools.partial(_post_kernel, 0, 0),
        grid=(m // tm,),
        in_specs=[row(D_MODEL), row(512), row(256), row(256),
                  pl.BlockSpec((None, tm, PLE_DIM), lambda i: (layer, i, 0))] + _post_weight_specs(layer),
        out_specs=row(D_MODEL),
        out_shape=jax.ShapeDtypeStruct((m, D_MODEL), F32),
        compiler_params=_cparams(("arbitrary",)),
        name="post",
    )(x, oa, ob, oc, p, wo, wg, wu, wd, pg, pw, vecs)


def _post_with_lru(x, oa, ob, lr, cv0, h0, t, p, wo, wg, wu, wd, pg, pw, vecs, conv_w, w_gates, lru_vecs, layer,
                   cast_next):
    m = x.shape[0]
    tm = ROW_TILE
    tiles = t // tm
    nsteps = m // tm
    row = lambda w_: pl.BlockSpec((tm, w_), lambda i: (i, 0))
    lspec = lambda s: _layer_spec(s, layer, 1)
    ahead = lambda i: jnp.minimum(i + 1, nsteps - 1)
    seq_state = lambda r: pl.BlockSpec((None, r, LRU_WIDTH), lambda i: (ahead(i) // tiles, 0, 0))
    cast_in, cast_out, cast_shapes = [], [], []
    for wgt in cast_next:
        _, rows, cols = wgt.shape
        slices = max(s for s in range(1, nsteps + 1)
                     if nsteps % s == 0 and rows % s == 0 and (rows // s) % 16 == 0)
        blk = rows // slices
        rep = nsteps // slices
        cast_in.append(pl.BlockSpec((None, blk, cols), lambda i, rep=rep: (layer + 1, i // rep, 0)))
        cast_out.append(pl.BlockSpec((None, blk, cols), lambda i, rep=rep: (0, i // rep, 0)))
        cast_shapes.append(jax.ShapeDtypeStruct((1, rows, cols), BF16))
    return pl.pallas_call(
        functools.partial(_post_kernel, tiles, len(cast_next)),
        grid=(nsteps,),
        in_specs=[row(D_MODEL), row(512), row(256),
                  pl.BlockSpec((tm, 2 * LRU_WIDTH), lambda i: (ahead(i), 0)),
                  pl.BlockSpec((tm, 2 * LRU_WIDTH), lambda i: (0, 0)),
                  seq_state(8), seq_state(1),
                  pl.BlockSpec((None, tm, PLE_DIM), lambda i: (layer, i, 0))] + _post_weight_specs(layer) + [
                  lspec((CONV_W, LRU_WIDTH)), lspec((LRU_WIDTH, 2 * LRU_WIDTH)), lspec((4, LRU_WIDTH))] + cast_in,
        out_specs=[row(D_MODEL), seq_state(1)] + cast_out,
        out_shape=[jax.ShapeDtypeStruct((m, D_MODEL), F32),
                   jax.ShapeDtypeStruct((m // t, 1, LRU_WIDTH), F32)] + cast_shapes,
        scratch_shapes=[pltpu.VMEM((tm, LRU_WIDTH), F32), pltpu.VMEM((tm + 8, LRU_WIDTH), F32),
                        pltpu.VMEM((1, LRU_WIDTH), F32)],
        compiler_params=_cparams(("arbitrary",)),
        name="post_lru",
    )(x, oa, ob, lr, lr, cv0, h0, p, wo, wg, wu, wd, pg, pw, vecs, conv_w, w_gates, lru_vecs, *cast_next)


def _block_diag(w):
    nl, nb, n, _ = w.shape
    eye = jnp.eye(nb, dtype=w.dtype)
    return (eye[None, :, None, :, None] * w[:, :, :, None, :]).reshape(nl, nb * n, nb * n)


BIG_WEIGHTS = ('w_in', 'w_out', 'ffn_w_gate', 'ffn_w_up', 'ffn_w_down', 'ple_gate_w')
CAST_STEPS = 8


def _cast_kernel(*refs):
    half = len(refs) // 2
    for src, dst in zip(refs[:half], refs[half:]):
        dst[...] = src[...].astype(BF16)


def _cast_first_layer(weights):
    in_specs, out_specs, shapes = [], [], []
    for wgt in weights:
        _, rows, cols = wgt.shape
        blk = rows // CAST_STEPS
        assert rows % CAST_STEPS == 0 and blk % 16 == 0
        spec = pl.BlockSpec((None, blk, cols), lambda i: (0, i, 0))
        in_specs.append(spec)
        out_specs.append(spec)
        shapes.append(jax.ShapeDtypeStruct((1, rows, cols), BF16))
    return pl.pallas_call(
        _cast_kernel,
        grid=(CAST_STEPS,),
        in_specs=in_specs,
        out_specs=out_specs,
        out_shape=shapes,
        compiler_params=_cparams(("arbitrary",)),
        name="cast_weights",
    )(*weights)


def _post_weights(wts, big):
    return (big['w_out'], big['ffn_w_gate'], big['ffn_w_up'], big['ffn_w_down'], big['ple_gate_w'], wts['ple_w'],
            wts['post_vecs'])


def _layer_prompt(x2, b, t, p, zeros, wkv_all, wts, big, raw_big, layer):
    m = b * t
    q, kv, rw, lr = _in_proj(x2, wts['norm_mix_pre'], big['w_in'], wts['b_in'], layer)
    kv = kv.reshape(b, t, 256)
    rw = rw.reshape(b, t, RWKV_PROJ)
    sh0, st0, cv0, h0 = zeros
    o_a = _attn_prompt(q.reshape(b, t, 512), kv, wts['attn_sinks'], layer)
    nk = kv[:, t - WINDOW:, 0:KV_WIDTH].reshape(b, WINDOW, N_KV_HEADS, HEAD_DIM)
    nv = kv[:, t - WINDOW:, KV_WIDTH:].reshape(b, WINDOW, N_KV_HEADS, HEAD_DIM)
    o_b, wkv_all = _rwkv(rw, sh0, st0, wts['rwkv_mu'], wts['rwkv_lora'], wts['rwkv_g_up'],
                         wts['rwkv_vecs'], wts['ones_blk'], wkv_all, layer)
    nsh = rw[:, t - 1, :]
    nconv = lr.reshape(b, t, 2 * LRU_WIDTH)[:, t - (CONV_W - 1):, 0:LRU_WIDTH]
    cast_next = [raw_big[name] for name in BIG_WEIGHTS] if layer + 1 < DEPTH else []
    x2, nh, *cast = _post_with_lru(x2, o_a.reshape(m, 512), o_b.reshape(m, 256), lr, cv0, h0, t, p,
                                   *_post_weights(wts, big), wts['lru_conv_w'], wts['lru_w_gates'],
                                   wts['lru_vecs'], layer, cast_next)
    next_big = dict(zip(BIG_WEIGHTS, cast))
    return x2, (nk, nv, nsh, nconv, nh.reshape(b, LRU_WIDTH)), wkv_all, next_big


def _layer_sample(x2, b, t, p, state, outs, wts, big, layer):
    ck, cv, sh0, st0, cv0, h0 = state
    nk_all, nv_all, wkv_all = outs
    q, kv, rw, lr = _in_proj(x2, wts['norm_mix_pre'], big['w_in'], wts['b_in'], layer)
    o_a, nk_all, nv_all = _attn_sample(q, kv, ck, cv, nk_all, nv_all, wts['attn_sinks'], t, layer)
    o_b, wkv_all = _rwkv_sample(rw, sh0, st0, wts['rwkv_mu'], wts['rwkv_lora'], wts['rwkv_g_up'],
                                wts['rwkv_vecs'], wts['ones_blk'], wkv_all, t, layer)
    nsh = rw[(t - 1) * b:, :]
    o_c, nh = _lru_sample(lr, cv0, h0, wts['lru_conv_w'], wts['lru_w_gates'], wts['lru_vecs'], t, layer)
    nconv = lr[(t - (CONV_W - 1)) * b:, 0:LRU_WIDTH].reshape(CONV_W - 1, b, LRU_WIDTH)
    x2 = _post(x2, o_a, o_b, o_c, p, *_post_weights(wts, big), layer)
    return x2, (nsh, nconv, nh), (nk_all, nv_all, wkv_all)


def kernel(x_prompt, x_sample, cache_k, cache_v, state_shift, state_wkv, state_conv, state_lru,
           p_prompt, p_sample, norm_mix_pre, norm_mix_post, norm_ffn_pre, norm_ffn_post,
           w_in, b_in, attn_sinks, rwkv_mu, rwkv_w0, rwkv_w_up, rwkv_a0, rwkv_a_up, rwkv_g_up,
           rwkv_k_k, rwkv_k_a, rwkv_r_k, rwkv_ln_w, rwkv_ln_b, lru_conv_w, lru_conv_b,
           lru_w_a, lru_b_a, lru_w_i, lru_b_i, lru_L, w_out, b_out, ffn_w_gate, ffn_w_up,
           ffn_w_down, ple_w, ple_gate_w):
    nl = DEPTH
    bp, tp_, _ = x_prompt.shape
    bs, ts, _ = x_sample.shape
    head_id = jnp.arange(RWKV_WIDTH) // RWKV_HEAD
    zeros_w = jnp.zeros((nl, 64, RWKV_WIDTH), F32)
    wts = dict(
        norm_mix_pre=norm_mix_pre[:, None, :], b_in=b_in[:, None, :],
        attn_sinks=attn_sinks,
        rwkv_mu=rwkv_mu[:, None, :],
        rwkv_lora=jnp.concatenate([jnp.concatenate([rwkv_w_up, zeros_w], axis=2),
                                   jnp.concatenate([zeros_w, rwkv_a_up], axis=2)], axis=1).astype(BF16),
        rwkv_g_up=rwkv_g_up.astype(BF16),
        rwkv_vecs=jnp.stack([rwkv_w0, rwkv_a0, rwkv_k_k, rwkv_k_a, rwkv_r_k.reshape(nl, RWKV_WIDTH),
                             rwkv_ln_w, rwkv_ln_b, jnp.zeros((nl, RWKV_WIDTH), F32)], axis=1),
        ones_blk=(head_id[:, None] == head_id[None, :]).astype(BF16),
        lru_conv_w=lru_conv_w,
        lru_w_gates=jnp.concatenate([_block_diag(lru_w_a), _block_diag(lru_w_i)], axis=2).astype(BF16),
        lru_vecs=jnp.stack([lru_conv_b, lru_b_a, lru_b_i, lru_L], axis=1),
        ple_w=ple_w.astype(BF16),
        post_vecs=jnp.stack([b_out, norm_mix_post, norm_ffn_pre, norm_ffn_post], axis=1),
    )
    raw_big = dict(w_in=w_in, w_out=w_out, ffn_w_gate=ffn_w_gate, ffn_w_up=ffn_w_up, ffn_w_down=ffn_w_down,
                   ple_gate_w=ple_gate_w)
    big = dict(zip(BIG_WEIGHTS, _cast_first_layer([raw_big[name] for name in BIG_WEIGHTS])))
    zeros_p = (jnp.zeros((bp, 1, RWKV_PROJ), F32),
               jnp.zeros((bp, RWKV_HEADS, RWKV_HEAD, RWKV_HEAD), F32),
               jnp.zeros((bp, 8, LRU_WIDTH), F32),
               jnp.zeros((bp, 1, LRU_WIDTH), F32))
    st_s = (cache_k.transpose(0, 1, 3, 4, 2), cache_v.transpose(0, 1, 3, 4, 2),
            state_shift,
            state_wkv.transpose(0, 2, 3, 4, 1),
            state_conv.transpose(0, 2, 1, 3),
            state_lru)
    pp = p_prompt.reshape(nl, bp * tp_, PLE_DIM)
    ps = p_sample.transpose(0, 2, 1, 3).reshape(nl, ts * bs, PLE_DIM)

    xp = x_prompt.reshape(bp * tp_, D_MODEL)
    xs = x_sample.transpose(1, 0, 2).reshape(ts * bs, D_MODEL)
    wkv_p = jnp.zeros((nl, bp, RWKV_HEADS, RWKV_HEAD, RWKV_HEAD), F32)
    outs_s = (jnp.zeros((nl, bs, N_KV_HEADS, HEAD_DIM, WINDOW), F32),
              jnp.zeros((nl, bs, N_KV_HEADS, HEAD_DIM, WINDOW), F32),
              jnp.zeros((nl, RWKV_HEADS, RWKV_HEAD, RWKV_HEAD, bs), F32))
    new_p, new_s = [], []
    for i in range(nl):
        xp, sp, wkv_p, next_big = _layer_prompt(xp, bp, tp_, pp, zeros_p, wkv_p, wts, big, raw_big, i)
        xs, ss, outs_s = _layer_sample(xs, bs, ts, ps, st_s, outs_s, wts, big, i)
        big = next_big
        new_p.append(sp)
        new_s.append(ss)

    def stk(lst, j):
        return jnp.stack([s[j] for s in lst], axis=0)

    nk_s, nv_s, wkv_s = outs_s
    return (xp.reshape(bp, tp_, D_MODEL), xs.reshape(ts, bs, D_MODEL).transpose(1, 0, 2),
            stk(new_p, 0), stk(new_p, 1), stk(new_p, 2), wkv_p, stk(new_p, 3), stk(new_p, 4),
            nk_s.transpose(0, 1, 4, 2, 3), nv_s.transpose(0, 1, 4, 2, 3), stk(new_s, 0),
            wkv_s.transpose(0, 4, 1, 2, 3), stk(new_s, 1).transpose(0, 2, 1, 3), stk(new_s, 2))
```
